```python
import jax
import jax.numpy as jnp
from jax import lax
import numpy as np

D_MODEL = 1024
BATCH = 16
SEQ = 2048
DEPTH = 2
DEC_BATCH = 2
DEC_SEQ = 16384
PAST_LEN = 128

EPS = 1e-6
ROPE_THETA = 500000.0
NEG_INF = -1e30

A_HEADS = 8
A_HEAD_DIM = 64
A_ROPE_DIM = A_HEAD_DIM // 4
A_PATTERNS = ((128, 1), (512, 4), (2048, 16))
A_WIDTH = A_HEADS * A_HEAD_DIM

B_HEADS = 4
B_KEY_DIM = 128
B_VAL_DIM = 256
B_GATE_RANK = 16
B_GATE_TAU = 16.0
B_CHUNK = 64
B_KW = B_HEADS * B_KEY_DIM
B_VW = B_HEADS * B_VAL_DIM

C_HEADS = 8
C_NOPE_DIM = 64
C_ROPE_DIM = 32
C_V_DIM = 64
C_QK_DIM = C_NOPE_DIM + C_ROPE_DIM
C_Q_RANK = 384
C_KV_RANK = 256
C_Q_BLOCK = 128
C_WIDTH = C_HEADS * C_V_DIM

N_EXPERTS = 16
N_GROUPS = 4
EXPERTS_PER_GROUP = N_EXPERTS // N_GROUPS
TOP_K = 2
GROUP_SCORE_K = 2
D_EXPERT = 512

N_MOD = 6
IN_SIZES = (A_WIDTH, A_WIDTH, A_WIDTH, B_KW, B_KW, B_VW, B_VW, B_GATE_RANK, B_GATE_RANK, C_Q_RANK, C_KV_RANK, C_ROPE_DIM, 3 * D_MODEL)
D_IN = sum(IN_SIZES)
IN_OFFSETS = tuple(int(o) for o in np.cumsum(IN_SIZES)[:-1])

kernel_name = 'hybrid_dilated_gla_mla_moe_encoder'


def rms_norm(x, gain):
    xf = x.astype(jnp.float32)
    y = xf * lax.rsqrt(jnp.mean(xf * xf, axis=-1, keepdims=True) + EPS)
    return (y * gain.astype(jnp.float32)).astype(x.dtype)


def rope(x, pos, rot_dim):
    half = rot_dim // 2
    inv_freq = ROPE_THETA ** (-jnp.arange(half, dtype=jnp.float32) / half)
    ang = pos.astype(jnp.float32)[:, None] * inv_freq[None, :]
    cos = jnp.cos(ang)[:, None, :]
    sin = jnp.sin(ang)[:, None, :]
    xf = x.astype(jnp.float32)
    x1 = xf[..., :half]
    x2 = xf[..., half:rot_dim]
    out = jnp.concatenate([x1 * cos - x2 * sin, x2 * cos + x1 * sin, xf[..., rot_dim:]], axis=-1)
    return out.astype(x.dtype)


def band_attention(q, k, v, length, half):
    n, L, H, Dh = q.shape
    nb = L // half
    qb = q.reshape(n, nb, half, H, Dh)
    pad = ((0, 0), (half, half), (0, 0), (0, 0))
    kp = jnp.pad(k, pad).reshape(n, nb + 2, half, H, Dh)
    vp = jnp.pad(v, pad).reshape(n, nb + 2, half, H, Dh)
    kw = jnp.concatenate([kp[:, :-2], kp[:, 1:-1], kp[:, 2:]], axis=2)
    vw = jnp.concatenate([vp[:, :-2], vp[:, 1:-1], vp[:, 2:]], axis=2).astype(jnp.float32)
    logits = jnp.einsum('nbqhd,nbkhd->nbhqk', qb, kw, preferred_element_type=jnp.float32) * (Dh ** -0.5)
    blk = jnp.arange(nb)[:, None] * half
    qpos = blk + jnp.arange(half)[None, :]
    kpos = blk - half + jnp.arange(3 * half)[None, :]
    rel = kpos[:, None, :] - qpos[:, :, None]
    valid = (jnp.abs(rel) <= half) & (kpos[:, None, :] >= 0) & (kpos[:, None, :] < length)
    logits = jnp.where(valid[None, :, None], logits, NEG_INF)
    m = jnp.max(logits, axis=-1, keepdims=True)
    p = jnp.exp(logits - m)
    s = jnp.sum(p, axis=-1, keepdims=True)
    o = jnp.einsum('nbhqk,nbkhd->nbqhd', p / s, vw).reshape(n, L, H, Dh)
    lse = (m + jnp.log(s))[..., 0].transpose(0, 1, 3, 2).reshape(n, L, H)
    return o, lse


def dilated_attention(q, k, v):
    B, S, H, Dh = q.shape
    outs, lses = [], []
    for window, dil in A_PATTERNS:
        half = window // (2 * dil)
        L = S // dil
        Lp = -(-L // half) * half
        def to_strided(t):
            t = t.reshape(B, L, dil, H, Dh).transpose(0, 2, 1, 3, 4).reshape(B * dil, L, H, Dh)
            return jnp.pad(t, ((0, 0), (0, Lp - L), (0, 0), (0, 0)))
        o, lse = band_attention(to_strided(q), to_strided(k), to_strided(v), L, half)
        outs.append(o[:, :L].reshape(B, dil, L, H, Dh).transpose(0, 2, 1, 3, 4).reshape(B, S, H, Dh))
        lses.append(lse[:, :L].reshape(B, dil, L, H).transpose(0, 2, 1, 3).reshape(B, S, H))
    w = jax.nn.softmax(jnp.stack(lses, axis=-1), axis=-1)
    o = jnp.einsum('pbshd,bshp->bshd', jnp.stack(outs), w)
    return o.astype(q.dtype)


def gla_scan(q, k, v, log_a, inclusive):
    B, S, H, Dk = q.shape
    Dv = v.shape[-1]
    n = S // B_CHUNK
    q, k, log_a = (t.reshape(B, n, B_CHUNK, H, Dk) for t in (q, k, log_a))
    v = v.reshape(B, n, B_CHUNK, H, Dv)
    b = jnp.cumsum(log_a, axis=2)
    b_last = b[:, :, -1:]
    q_dec = q * jnp.exp(b)
    k_inv = k * jnp.exp(-b)
    k_end = k * jnp.exp(b_last - b)
    idx = jnp.arange(B_CHUNK)
    mask = (idx[:, None] >= idx[None, :]) if inclusive else (idx[:, None] > idx[None, :])
    att = jnp.where(mask, jnp.einsum('bnqhk,bnshk->bnhqs', q_dec, k_inv), 0.0)
    o_intra = jnp.einsum('bnhqs,bnshv->bnqhv', att, v)

    def step(state, xs):
        qd, ke, vc, dl = xs
        o = jnp.einsum('bqhk,bhkv->bqhv', qd, state)
        state = state * jnp.exp(dl)[..., None] + jnp.einsum('bshk,bshv->bhkv', ke, vc)
        return state, o

    xs = (jnp.moveaxis(q_dec, 1, 0), jnp.moveaxis(k_end, 1, 0), jnp.moveaxis(v, 1, 0), jnp.moveaxis(b_last[:, :, 0], 1, 0))
    state0 = jnp.zeros((B, H, Dk, Dv), jnp.float32)
    _, o_inter = lax.scan(step, state0, xs)
    return (o_intra + jnp.moveaxis(o_inter, 0, 1)).reshape(B, S, H, Dv)


def gla_mixer(q, k, v, r, zf, zb, w_af, b_af, w_ab, b_ab, norm_gain):
    B, S, _ = q.shape
    f32 = jnp.float32
    heads = lambda t, d: t.astype(f32).reshape(B, S, B_HEADS, d)
    qh = heads(q, B_KEY_DIM) * (B_KEY_DIM ** -0.5)
    kh = heads(k, B_KEY_DIM)
    vh = heads(v, B_VAL_DIM)
    la_f = heads(jax.nn.log_sigmoid((zf @ w_af + b_af).astype(f32)) / B_GATE_TAU, B_KEY_DIM)
    la_b = heads(jax.nn.log_sigmoid((zb @ w_ab + b_ab).astype(f32)) / B_GATE_TAU, B_KEY_DIM)
    flip = lambda t: t[:, ::-1]
    o = gla_scan(qh, kh, vh, la_f, True) + flip(gla_scan(flip(qh), flip(kh), flip(vh), flip(la_b), False))
    o = rms_norm(o, norm_gain).reshape(B, S, B_VW) * jax.nn.silu(r.astype(f32))
    return o.astype(q.dtype)


def mla_mixer(cq, ckv, kr, q_norm, w_uq, kv_norm, w_ukv, pos):
    B, S, _ = cq.shape
    q = (rms_norm(cq, q_norm) @ w_uq).reshape(B, S, C_HEADS, C_QK_DIM)
    q_nope = q[..., :C_NOPE_DIM]
    q_rope = rope(q[..., C_NOPE_DIM:], pos, C_ROPE_DIM)
    kv = (rms_norm(ckv, kv_norm) @ w_ukv).reshape(B, S, C_HEADS, C_NOPE_DIM + C_V_DIM)
    k_nope = kv[..., :C_NOPE_DIM]
    v = kv[..., C_NOPE_DIM:].astype(jnp.float32)
    k_rope = rope(kr[:, :, None, :], pos, C_ROPE_DIM)[:, :, 0]
    scale = C_QK_DIM ** -0.5
    nq = S // C_Q_BLOCK
    qn_blocks = jnp.moveaxis(q_nope.reshape(B, nq, C_Q_BLOCK, C_HEADS, C_NOPE_DIM), 1, 0)
    qr_blocks = jnp.moveaxis(q_rope.reshape(B, nq, C_Q_BLOCK, C_HEADS, C_ROPE_DIM), 1, 0)

    def block(args):
        qn, qr = args
        logits = (jnp.einsum('bqhd,bkhd->bhqk', qn, k_nope, preferred_element_type=jnp.float32)
                  + jnp.einsum('bqhd,bkd->bhqk', qr, k_rope, preferred_element_type=jnp.float32)) * scale
        p = jax.nn.softmax(logits, axis=-1)
        return jnp.einsum('bhqk,bkhd->bqhd', p, v)

    o = lax.map(block, (qn_blocks, qr_blocks))
    return jnp.moveaxis(o, 0, 1).reshape(B, S, C_WIDTH).astype(cq.dtype)


def mixer_sublayer(h, pos, w_in, w_af, b_af, w_ab, b_ab, gla_norm, q_norm, w_uq, kv_norm, w_ukv, w_pa, w_pb, w_pc, w_out):
    B, S, _ = h.shape
    (a_q, a_k, a_v, b_q, b_k, b_v, b_r, b_zf, b_zb, c_cq, c_ckv, c_kr, gates) = jnp.split(h @ w_in, IN_OFFSETS, axis=-1)
    a_heads = lambda t: t.reshape(B, S, A_HEADS, A_HEAD_DIM)
    y_a = dilated_attention(rope(a_heads(a_q), pos, A_ROPE_DIM), rope(a_heads(a_k), pos, A_ROPE_DIM), a_heads(a_v)).reshape(B, S, A_WIDTH)
    y_b = gla_mixer(b_q, b_k, b_v, b_r, b_zf, b_zb, w_af, b_af, w_ab, b_ab, gla_norm)
    y_c = mla_mixer(c_cq, c_ckv, c_kr, q_norm, w_uq, kv_norm, w_ukv, pos)
    g_a, g_b, g_c = jnp.split(jax.nn.sigmoid(gates), 3, axis=-1)
    merged = g_a * (y_a @ w_pa) + g_b * (y_b @ w_pb) + g_c * (y_c @ w_pc)
    return merged @ w_out


def moe_sublayer(h, w_router, b_router, w_gate, w_up, w_down):
    B, S, D = h.shape
    t = h.reshape(B * S, D)
    s = jax.nn.sigmoid((t @ w_router).astype(jnp.float32))
    sel = (s + b_router.astype(jnp.float32)).reshape(-1, N_GROUPS, EXPERTS_PER_GROUP)
    group_score = jnp.sum(lax.top_k(sel, GROUP_SCORE_K)[0], axis=-1)
    g_idx = jnp.argmax(group_score, axis=-1)
    in_group = jnp.take_along_axis(sel, g_idx[:, None, None], axis=1)[:, 0]
    _, e_local = lax.top_k(in_group, TOP_K)
    e_idx = g_idx[:, None] * EXPERTS_PER_GROUP + e_local
    s_sel = jnp.take_along_axis(s, e_idx, axis=1)
    w = s_sel / jnp.sum(s_sel, axis=-1, keepdims=True)
    combine = jnp.sum(jax.nn.one_hot(e_idx, N_EXPERTS, dtype=jnp.float32) * w[..., None], axis=1)
    out = jnp.zeros((B * S, D), jnp.float32)
    for e in range(N_EXPERTS):
        a = jax.nn.silu(t @ w_gate[e]) * (t @ w_up[e])
        out = out + combine[:, e:e + 1] * (a @ w_down[e]).astype(jnp.float32)
    return out.astype(h.dtype).reshape(B, S, D)


def encoder_trunk(x, c, norm_mix, norm_moe, w_mod, b_mod, w_in, w_gla_af, b_gla_af, w_gla_ab, b_gla_ab, gla_norm,
                  mla_q_norm, w_mla_uq, mla_kv_norm, w_mla_ukv, w_proj_a, w_proj_b, w_proj_c, w_out,
                  w_router, b_router, w_exp_gate, w_exp_up, w_exp_down, final_norm):
    S = x.shape[1]
    pos = jnp.arange(S)
    c_act = jax.nn.silu(c)
    for l in range(DEPTH):
        mod = (c_act @ w_mod[l] + b_mod[l])[:, None, :]
        shift1, scale1, gate1, shift2, scale2, gate2 = jnp.split(mod, N_MOD, axis=-1)
        h = rms_norm(x, norm_mix[l]) * (1.0 + scale1) + shift1
        x = x + gate1 * mixer_sublayer(h, pos, w_in[l], w_gla_af[l], b_gla_af[l], w_gla_ab[l], b_gla_ab[l], gla_norm[l],
                                       mla_q_norm[l], w_mla_uq[l], mla_kv_norm[l], w_mla_ukv[l],
                                       w_proj_a[l], w_proj_b[l], w_proj_c[l], w_out[l])
        h = rms_norm(x, norm_moe[l]) * (1.0 + scale2) + shift2
        x = x + gate2 * moe_sublayer(h, w_router, b_router, w_exp_gate[l], w_exp_up[l], w_exp_down[l])
    return rms_norm(x, final_norm)


def setup_inputs(seed: int = 0) -> dict:
    key = jax.random.key(seed)
    ks = iter(jax.random.split(key, 32))
    def nrm(shape, scale):
        return scale * jax.random.normal(next(ks), shape, jnp.float32)
    def gain(shape):
        return 1.0 + 0.05 * jax.random.normal(next(ks), shape, jnp.float32)
    L, D = DEPTH, D_MODEL
    return {
        'x_prompt': nrm((BATCH, SEQ, D), 1.0),
        'x_sample': nrm((DEC_BATCH, DEC_SEQ, D), 1.0),
        'c_prompt': nrm((BATCH, D), 1.0),
        'c_sample': nrm((DEC_BATCH, D), 1.0),
        'norm_mix': gain((L, D)),
        'norm_moe': gain((L, D)),
        'w_mod': nrm((L, D, N_MOD * D), 0.5 * D ** -0.5),
        'b_mod': nrm((L, N_MOD * D), 0.02),
        'w_in': nrm((L, D, D_IN), D ** -0.5),
        'w_gla_af': nrm((L, B_GATE_RANK, B_KW), B_GATE_RANK ** -0.5),
        'b_gla_af': nrm((L, B_KW), 0.1),
        'w_gla_ab': nrm((L, B_GATE_RANK, B_KW), B_GATE_RANK ** -0.5),
        'b_gla_ab': nrm((L, B_KW), 0.1),
        'gla_norm': gain((L, B_VAL_DIM)),
        'mla_q_norm': gain((L, C_Q_RANK)),
        'w_mla_uq': nrm((L, C_Q_RANK, C_HEADS * C_QK_DIM), C_Q_RANK ** -0.5),
        'mla_kv_norm': gain((L, C_KV_RANK)),
        'w_mla_ukv': nrm((L, C_KV_RANK, C_HEADS * (C_NOPE_DIM + C_V_DIM)), C_KV_RANK ** -0.5),
        'w_proj_a': nrm((L, A_WIDTH, D), A_WIDTH ** -0.5),
        'w_proj_b': nrm((L, B_VW, D), B_VW ** -0.5),
        'w_proj_c': nrm((L, C_WIDTH, D), C_WIDTH ** -0.5),
        'w_out': nrm((L, D, D), D ** -0.5),
        'w_router': nrm((D, N_EXPERTS), D ** -0.5),
        'b_router': nrm((N_EXPERTS,), 0.01),
        'w_exp_gate': nrm((L, N_EXPERTS, D, D_EXPERT), D ** -0.5),
        'w_exp_up': nrm((L, N_EXPERTS, D, D_EXPERT), D ** -0.5),
        'w_exp_down': nrm((L, N_EXPERTS, D_EXPERT, D), D_EXPERT ** -0.5),
        'final_norm': gain((D,)),
    }


def reference(x_prompt, x_sample, c_prompt, c_sample, norm_mix, norm_moe, w_mod, b_mod, w_in,
              w_gla_af, b_gla_af, w_gla_ab, b_gla_ab, gla_norm, mla_q_norm, w_mla_uq, mla_kv_norm, w_mla_ukv,
              w_proj_a, w_proj_b, w_proj_c, w_out, w_router, b_router, w_exp_gate, w_exp_up, w_exp_down, final_norm):
    weights = (norm_mix, norm_moe, w_mod, b_mod, w_in, w_gla_af, b_gla_af, w_gla_ab, b_gla_ab, gla_norm,
               mla_q_norm, w_mla_uq, mla_kv_norm, w_mla_ukv, w_proj_a, w_proj_b, w_proj_c, w_out,
               w_router, b_router, w_exp_gate, w_exp_up, w_exp_down, final_norm)
    y_prompt = encoder_trunk(x_prompt, c_prompt, *weights)
    y_sample = encoder_trunk(x_sample, c_sample, *weights)
    return (y_prompt, y_sample)
```

```python
import functools

import jax
import jax.numpy as jnp
from jax import lax
from jax.experimental import pallas as pl
from jax.experimental.pallas import tpu as pltpu

F32 = jnp.float32
BF16 = jnp.bfloat16

D_MODEL = 1024
DEPTH = 2
EPS = 1e-6
ROPE_THETA = 500000.0
NEG_INF = -1e30

A_HEADS = 8
A_HEAD_DIM = 64
A_ROPE_DIM = A_HEAD_DIM // 4
A_PATTERNS = ((128, 1), (512, 4), (2048, 16))
A_WIDTH = A_HEADS * A_HEAD_DIM
A_HALF = 64

B_HEADS = 4
B_KEY_DIM = 128
B_VAL_DIM = 256
B_GATE_RANK = 16
B_GATE_TAU = 16.0
B_CHUNK = 64
B_KW = B_HEADS * B_KEY_DIM
B_VW = B_HEADS * B_VAL_DIM

C_HEADS = 8
C_NOPE_DIM = 64
C_ROPE_DIM = 32
C_V_DIM = 64
C_QK_DIM = C_NOPE_DIM + C_ROPE_DIM
C_Q_RANK = 384
C_KV_RANK = 256
C_HEAD_PAD = 128
C_PAD_WIDTH = C_HEADS * C_HEAD_PAD

N_EXPERTS = 16
N_GROUPS = 4
EXPERTS_PER_GROUP = N_EXPERTS // N_GROUPS
TOP_K = 2
GROUP_SCORE_K = 2
D_EXPERT = 512
N_MOD = 6

IN_SIZES = (A_WIDTH, A_WIDTH, A_WIDTH, B_KW, B_KW, B_VW, B_VW, B_GATE_RANK, B_GATE_RANK,
            C_Q_RANK, C_KV_RANK, C_ROPE_DIM, 3 * D_MODEL)

LANES = 128

OFF_GATES = 0
OFF_BV = 3072
OFF_BR = 4096
OFF_AQ = 5120
OFF_AK = 5632
OFF_AV = 6144
OFF_BQ = 6656
OFF_BK = 7168
OFF_CKV = 7680
OFF_BZ = 7936
OFF_CQ = 8064
OFF_KR = 8448
NPAD = 8704
KR_LANE = 64

VMEM_LIMIT = 56 * 1024 * 1024


def _cparams(sem):
    return pltpu.CompilerParams(dimension_semantics=sem, vmem_limit_bytes=VMEM_LIMIT)


class Groups:
    def __init__(self, b1, s1, b2, s2):
        self.b1, self.s1, self.b2, self.s2 = b1, s1, b2, s2
        self.t1, self.t2 = b1 * s1, b2 * s2
        self.t = self.t1 + self.t2
        self.nb = b1 + b2

    def batch_of_block(self, i, tm):
        n1 = self.t1 // tm
        return jnp.where(i < n1, i // (self.s1 // tm), self.b1 + (i - n1) // (self.s2 // tm))

    def pos_block(self, i, tm):
        n1 = self.t1 // tm
        return jnp.where(i < n1, i % (self.s1 // tm), (i - n1) % (self.s2 // tm))


def _dot(a, b):
    return jnp.dot(a, b, preferred_element_type=F32)


def _dot_nt(a, b):
    return lax.dot_general(a, b, (((1,), (1,)), ((), ())), preferred_element_type=F32)


def _dot_tn(a, b):
    return lax.dot_general(a, b, (((0,), (0,)), ((), ())), preferred_element_type=F32)


def _split3(a):
    hi = a.astype(BF16)
    r1 = a - hi.astype(F32)
    mid = r1.astype(BF16)
    lo = (r1 - mid.astype(F32)).astype(BF16)
    return hi, mid, lo


def _rope(y, ct, s1, s2, half):
    return y * ct + pltpu.roll(y, LANES - half, 1) * s1 + pltpu.roll(y, half, 1) * s2


def _sigmoid(x):
    return 1.0 / (1.0 + jnp.exp(-x))


def _mod_kernel(c_ref, w_ref, b_ref, o_ref):
    c = c_ref[...]
    ca = c * _sigmoid(c)
    ch, cm, _ = _split3(ca)
    wh, wm, _ = _split3(w_ref[0])
    o_ref[0] = _dot(ch, wh) + _dot(cm, wh) + _dot(ch, wm) + b_ref[0]


def modulation(c_pad, w_mod, b_mod):
    nbp, d = c_pad.shape
    depth, _, n = w_mod.shape
    tn = 512
    return pl.pallas_call(
        _mod_kernel,
        grid=(depth, n // tn),
        in_specs=[pl.BlockSpec((nbp, d), lambda l, j: (0, 0)),
                  pl.BlockSpec((1, d, tn), lambda l, j: (l, 0, j)),
                  pl.BlockSpec((1, 1, tn), lambda l, j: (l, 0, j))],
        out_specs=pl.BlockSpec((1, nbp, tn), lambda l, j: (l, 0, j)),
        out_shape=jax.ShapeDtypeStruct((depth, nbp, n), F32),
        compiler_params=_cparams(("arbitrary", "arbitrary")),
    )(c_pad, w_mod, b_mod.reshape(depth, 1, n))


def _inproj_kernel(x_ref, mul_ref, sh_ref, w_ref, ct_ref, s1_ref, s2_ref, o_ref, h_scr, *, tn, rope_lo, rope_hi):
    j = pl.program_id(1)

    @pl.when(j == 0)
    def _():
        x = x_ref[...]
        ms = jnp.mean(x * x, axis=-1, keepdims=True)
        h = x * lax.rsqrt(ms + EPS) * mul_ref[0] + sh_ref[0]
        h_scr[...] = h.astype(BF16)

    y = _dot(h_scr[...], w_ref[...])
    is_rope = (j >= rope_lo) & (j < rope_hi)

    @pl.when(is_rope)
    def _():
        ct, s1, s2 = ct_ref[...], s1_ref[...], s2_ref[...]
        for c in range(tn // LANES):
            sl = slice(c * LANES, (c + 1) * LANES)
            o_ref[:, sl] = _rope(y[:, sl], ct, s1, s2, A_ROPE_DIM // 2).astype(BF16)

    @pl.when(jnp.logical_not(is_rope))
    def _():
        o_ref[...] = y.astype(BF16)


def in_projection(g, x, mul, shift, w_pad, tabs, tm):
    t, d = x.shape
    tn = 512
    ct, s1, s2 = tabs
    bmap = lambda i, j: (g.batch_of_block(i, tm), 0, 0)
    pmap = lambda i, j: (g.pos_block(i, tm), 0)
    kern = functools.partial(_inproj_kernel, tn=tn, rope_lo=OFF_AQ // tn, rope_hi=OFF_AV // tn)
    return pl.pallas_call(
        kern,
        grid=(t // tm, NPAD // tn),
        in_specs=[pl.BlockSpec((tm, d), lambda i, j: (i, 0)),
                  pl.BlockSpec((1, 1, d), bmap),
                  pl.BlockSpec((1, 1, d), bmap),
                  pl.BlockSpec((d, tn), lambda i, j: (0, j)),
                  pl.BlockSpec((tm, LANES), pmap),
                  pl.BlockSpec((tm, LANES), pmap),
                  pl.BlockSpec((tm, LANES), pmap)],
        out_specs=pl.BlockSpec((tm, tn), lambda i, j: (i, j)),
        out_shape=jax.ShapeDtypeStruct((t, NPAD), BF16),
        scratch_shapes=[pltpu.VMEM((tm, d), BF16)],
        compiler_params=_cparams(("arbitrary", "arbitrary")),
    )(x, mul, shift, w_pad, ct, s1, s2)


def _band_kernel(q_ref, kp_ref, km_ref, kn_ref, vp_ref, vm_ref, vn_ref, o_ref, lse_ref, *, tq, rows1, l1, l2):
    i = pl.program_id(1)
    row0 = i * tq
    in1 = row0 < rows1
    seq_len = jnp.where(in1, l1, l2)
    pos0 = jnp.where(in1, row0 % l1, (row0 - rows1) % l2)
    tk = tq + 2 * A_HALF
    q = q_ref[...]
    k = jnp.concatenate([kp_ref[...], km_ref[...], kn_ref[...]], axis=0)
    v = jnp.concatenate([vp_ref[...], vm_ref[...], vn_ref[...]], axis=0)
    qi = lax.broadcasted_iota(jnp.int32, (tq, tk), 0)
    kj = lax.broadcasted_iota(jnp.int32, (tq, tk), 1)
    rel = kj - A_HALF - qi
    kpos = pos0 - A_HALF + kj
    valid = (jnp.abs(rel) <= A_HALF) & (kpos >= 0) & (kpos < seq_len)
    lo = lax.broadcasted_iota(jnp.int32, (tq, LANES), 1) < A_HEAD_DIM
    scale = A_HEAD_DIM ** -0.5
    for p in range(A_WIDTH // LANES):
        sl = slice(p * LANES, (p + 1) * LANES)
        qp, kp, vp = q[:, sl], k[:, sl], v[:, sl]
        outs, lses = [], []
        for hh in range(2):
            qm = jnp.where(lo if hh == 0 else jnp.logical_not(lo), qp, jnp.zeros_like(qp))
            s = _dot_nt(qm, kp) * scale
            s = jnp.where(valid, s, NEG_INF)
            m = jnp.max(s, axis=1, keepdims=True)
            e = jnp.exp(s - m)
            l = jnp.sum(e, axis=1, keepdims=True)
            outs.append(_dot(e.astype(BF16), vp) / l)
            lses.append(m + jnp.log(l))
        o_ref[:, sl] = jnp.where(lo, outs[0], outs[1]).astype(BF16)
        lse_ref[:, sl] = jnp.where(lo, lses[0], lses[1])


def band_attention(g, y, dil):
    t = g.t
    rows = t // dil
    tq = 128
    yd = y.reshape(rows, dil * NPAD)
    cpb = NPAD // A_WIDTH
    sub = tq // A_HALF
    nsub = rows // A_HALF
    qcol, kcol, vcol = OFF_AQ // A_WIDTH, OFF_AK // A_WIDTH, OFF_AV // A_WIDTH

    def main(col):
        return pl.BlockSpec((tq, A_WIDTH), lambda r, i: (i, r * cpb + col))

    def prev(col):
        return pl.BlockSpec((A_HALF, A_WIDTH), lambda r, i: (jnp.maximum(i * sub - 1, 0), r * cpb + col))

    def nxt(col):
        return pl.BlockSpec((A_HALF, A_WIDTH), lambda r, i: (jnp.minimum((i + 1) * sub, nsub - 1), r * cpb + col))

    kern = functools.partial(_band_kernel, tq=tq, rows1=g.t1 // dil, l1=g.s1 // dil, l2=g.s2 // dil)
    o, lse = pl.pallas_call(
        kern,
        grid=(dil, rows // tq),
        in_specs=[main(qcol), prev(kcol), main(kcol), nxt(kcol), prev(vcol), main(vcol), nxt(vcol)],
        out_specs=[pl.BlockSpec((tq, A_WIDTH), lambda r, i: (i, r)),
                   pl.BlockSpec((tq, A_WIDTH), lambda r, i: (i, r))],
        out_shape=[jax.ShapeDtypeStruct((rows, dil * A_WIDTH), BF16),
                   jax.ShapeDtypeStruct((rows, dil * A_WIDTH), F32)],
        compiler_params=_cparams(("arbitrary", "arbitrary")),
    )(yd, yd, yd, yd, yd, yd, yd)
    return o.reshape(t, A_WIDTH), lse.reshape(t, A_WIDTH)


def _gla_kernel(*refs, reverse, final, tc, nblk, t1, s1, s2):
    if final:
        q_ref, k_ref, v_ref, z_ref, wa_ref, ba_ref, ob_ref, r_ref, gain_ref, o_ref, st_scr = refs
    else:
        q_ref, k_ref, v_ref, z_ref, wa_ref, ba_ref, o_ref, st_scr = refs
    i = pl.program_id(1)
    blk = (nblk - 1 - i) if reverse else i
    row0 = blk * tc
    in1 = row0 < t1
    pos0 = jnp.where(in1, row0 % s1, (row0 - t1) % s2)
    slen = jnp.where(in1, s1, s2)
    start = (pos0 + tc == slen) if reverse else (pos0 == 0)

    @pl.when(start)
    def _():
        st_scr[...] = jnp.zeros_like(st_scr)

    zl = _dot(z_ref[...], wa_ref[0]) + ba_ref[0]
    la = (jnp.minimum(zl, 0.0) - jnp.log(1.0 + jnp.exp(-jnp.abs(zl)))) * (1.0 / B_GATE_TAU)
    ri = lax.broadcasted_iota(jnp.int32, (tc, tc), 0)
    ci = lax.broadcasted_iota(jnp.int32, (tc, tc), 1)
    same = (ri // B_CHUNK) == (ci // B_CHUNK)
    tri = jnp.where(same & ((ci >= ri) if reverse else (ci <= ri)), 1.0, 0.0).astype(BF16)
    hi, mid, lo = _split3(la)
    bc = _dot(tri, hi) + _dot(tri, mid) + _dot(tri, lo)

    q = q_ref[...].astype(F32) * (B_KEY_DIM ** -0.5)
    k = k_ref[...].astype(F32)
    v = v_ref[...]
    eb = jnp.exp(bc)
    q_dec = (q * eb).astype(BF16)
    k_inv = (k * jnp.exp(-bc)).astype(BF16)
    qi = lax.broadcasted_iota(jnp.int32, (B_CHUNK, B_CHUNK), 0)
    si = lax.broadcasted_iota(jnp.int32, (B_CHUNK, B_CHUNK), 1)
    mask = (si > qi) if reverse else (si <= qi)
    st = st_scr[...]
    nch = tc // B_CHUNK
    outs = [None] * nch
    for c in (range(nch - 1, -1, -1) if reverse else range(nch)):
        sl = slice(c * B_CHUNK, (c + 1) * B_CHUNK)
        tot = bc[c * B_CHUNK:c * B_CHUNK + 1] if reverse else bc[(c + 1) * B_CHUNK - 1:(c + 1) * B_CHUNK]
        k_end = (k[sl] * jnp.exp(tot - bc[sl])).astype(BF16)
        att = jnp.where(mask, _dot_nt(q_dec[sl], k_inv[sl]), 0.0).astype(BF16)
        outs[c] = _dot(att, v[sl]) + _dot_nt(q_dec[sl], st.astype(BF16))
        st = st * jnp.exp(tot) + _dot_tn(v[sl], k_end)
    st_scr[...] = st
    o = jnp.concatenate(outs, axis=0)
    if final:
        o = o + ob_ref[...]
        on = o * lax.rsqrt(jnp.mean(o * o, axis=-1, keepdims=True) + EPS) * gain_ref[...]
        r = r_ref[...].astype(F32)
        o_ref[...] = (on * (r * _sigmoid(r))).astype(BF16)
    else:
        o_ref[...] = o


def gla_pass(g, y, wa, ba, reverse, o_back=None, gain=None):
    t = g.t
    tc = 256
    nblk = t // tc
    final = o_back is not None
    rowmap = (lambda i: nblk - 1 - i) if reverse else (lambda i: i)
    kq, kk = OFF_BQ // B_KEY_DIM, OFF_BK // B_KEY_DIM
    kv, kr, kz = OFF_BV // B_VAL_DIM, OFF_BR // B_VAL_DIM, OFF_BZ // LANES
    in_specs = [pl.BlockSpec((tc, B_KEY_DIM), lambda h, i: (rowmap(i), kq + h)),
                pl.BlockSpec((tc, B_KEY_DIM), lambda h, i: (rowmap(i), kk + h)),
                pl.BlockSpec((tc, B_VAL_DIM), lambda h, i: (rowmap(i), kv + h)),
                pl.BlockSpec((tc, LANES), lambda h, i: (rowmap(i), kz)),
                pl.BlockSpec((1, LANES, B_KEY_DIM), lambda h, i: (h, 0, 0)),
                pl.BlockSpec((1, 1, B_KEY_DIM), lambda h, i: (h, 0, 0))]
    args = [y, y, y, y, wa, ba]
    if final:
        in_specs += [pl.BlockSpec((tc, B_VAL_DIM), lambda h, i: (rowmap(i), h)),
                     pl.BlockSpec((tc, B_VAL_DIM), lambda h, i: (rowmap(i), kr + h)),
                     pl.BlockSpec((1, B_VAL_DIM), lambda h, i: (0, 0))]
        args += [o_back, y, gain]
    kern = functools.partial(_gla_kernel, reverse=reverse, final=final, tc=tc, nblk=nblk, t1=g.t1, s1=g.s1, s2=g.s2)
    return pl.pallas_call(
        kern,
        grid=(B_HEADS, nblk),
        in_specs=in_specs,
        out_specs=pl.BlockSpec((tc, B_VAL_DIM), lambda h, i: (rowmap(i), h)),
        out_shape=jax.ShapeDtypeStruct((t, B_VW), BF16 if final else F32),
        scratch_shapes=[pltpu.VMEM((B_VAL_DIM, B_KEY_DIM), F32)],
        compiler_params=_cparams(("arbitrary", "arbitrary")),
    )(*args)


def _mla_prep_kernel(ckv_ref, cq_ref, kr_ref, qn_ref, kvn_ref, wq_ref, wk_ref, wv_ref, ct_ref, s1_ref, s2_ref,
                     q_out, k_out, v_out):
    def norm(ref, gain_ref):
        xf = ref[...].astype(F32)
        return (xf * lax.rsqrt(jnp.mean(xf * xf, axis=-1, keepdims=True) + EPS) * gain_ref[...]).astype(BF16)

    ckv_n = norm(ckv_ref, kvn_ref)
    cq_n = norm(cq_ref, qn_ref)
    q = _dot(cq_n, wq_ref[...])
    kn = _dot(ckv_n, wk_ref[...])
    v = _dot(ckv_n, wv_ref[...])
    ct, s1, s2 = ct_ref[...], s1_ref[...], s2_ref[...]
    half = C_ROPE_DIM // 2
    kr_rot = _rope(kr_ref[...].astype(F32), ct, s1, s2, half)
    ones_lane = lax.broadcasted_iota(jnp.int32, kr_rot.shape, 1) == C_V_DIM
    scale = C_QK_DIM ** -0.5
    for h in range(C_HEADS):
        sl = slice(h * C_HEAD_PAD, (h + 1) * C_HEAD_PAD)
        q_out[:, sl] = (_rope(q[:, sl], ct, s1, s2, half) * scale).astype(BF16)
        k_out[:, sl] = (kn[:, sl] + kr_rot).astype(BF16)
        v_out[:, sl] = jnp.where(ones_lane, 1.0, v[:, sl]).astype(BF16)


def mla_prep(g, y, qn, kvn, wq, wk, wv, tabs, tm):
    t = g.t
    ct, s1, s2 = tabs
    pmap = lambda i: (g.pos_block(i, tm), 0)
    full = lambda a: pl.BlockSpec(a.shape, lambda i: (0,) * a.ndim)
    out = jax.ShapeDtypeStruct((t, C_PAD_WIDTH), BF16)
    ospec = pl.BlockSpec((tm, C_PAD_WIDTH), lambda i: (i, 0))
    return pl.pallas_call(
        _mla_prep_kernel,
        grid=(t // tm,),
        in_specs=[pl.BlockSpec((tm, C_KV_RANK), lambda i: (i, OFF_CKV // C_KV_RANK)),
                  pl.BlockSpec((tm, C_Q_RANK), lambda i: (i, OFF_CQ // C_Q_RANK)),
                  pl.BlockSpec((tm, LANES), lambda i: (i, OFF_KR // LANES)),
                  full(qn), full(kvn), full(wq), full(wk), full(wv),
                  pl.BlockSpec((tm, LANES), pmap), pl.BlockSpec((tm, LANES), pmap), pl.BlockSpec((tm, LANES), pmap)],
        out_specs=[ospec, ospec, ospec],
        out_shape=[out, out, out],
        compiler_params=_cparams(("arbitrary",)),
    )(y, y, y, qn, kvn, wq, wk, wv, ct, s1, s2)


def _flash_kernel(q_ref, k_ref, v_ref, o_ref, *, tk, nk):
    q = q_ref[...]
    tq = q.shape[0]

    def body(j, carry):
        m, acc = carry
        off = pl.multiple_of(j * tk, tk)
        s = _dot_nt(q, k_ref[pl.ds(off, tk), :])
        m_new = jnp.maximum(m, jnp.max(s, axis=1, keepdims=True))
        p = jnp.exp(s - m_new)
        acc = acc * jnp.exp(m - m_new) + _dot(p.astype(BF16), v_ref[pl.ds(off, tk), :])
        return m_new, acc

    m0 = jnp.full((tq, 1), NEG_INF, F32)
    acc0 = jnp.zeros((tq, C_HEAD_PAD), F32)
    _, acc = lax.fori_loop(0, nk, body, (m0, acc0))
    o_ref[...] = (acc / acc[:, C_V_DIM:C_V_DIM + 1]).astype(BF16)


def mla_flash(q, k, v, row_off, nseq, s, tq, tk):
    t = q.shape[0]
    qb0 = row_off // tq
    sb0 = row_off // s
    nq = s // tq
    kern = functools.partial(_flash_kernel, tk=tk, nk=s // tk)
    return pl.pallas_call(
        kern,
        grid=(nseq, C_HEADS, nq),
        in_specs=[pl.BlockSpec((tq, C_HEAD_PAD), lambda b, h, i: (qb0 + b * nq + i, h)),
                  pl.BlockSpec((s, C_HEAD_PAD), lambda b, h, i: (sb0 + b, h)),
                  pl.BlockSpec((s, C_HEAD_PAD), lambda b, h, i: (sb0 + b, h))],
        out_specs=pl.BlockSpec((tq, C_HEAD_PAD), lambda b, h, i: (b * nq + i, h)),
        out_shape=jax.ShapeDtypeStruct((nseq * s, C_PAD_WIDTH), BF16),
        compiler_params=_cparams(("arbitrary", "arbitrary", "arbitrary")),
    )(q, k, v)


def _proj_kernel(o1_ref, o2_ref, o3_ref, l1_ref, l2_ref, l3_ref, yb_ref, yc_ref, ga_ref, gb_ref, gc_ref,
                 x_ref, g1_ref, mul2_ref, sh2_ref, wpa_ref, wpb_ref, wpc_ref, wout_ref, wrh_ref, wrm_ref,
                 x_out, h_out, s_out):
    l1, l2, l3 = l1_ref[...], l2_ref[...], l3_ref[...]
    m = jnp.maximum(jnp.maximum(l1, l2), l3)
    e1, e2, e3 = jnp.exp(l1 - m), jnp.exp(l2 - m), jnp.exp(l3 - m)
    ya = (e1 * o1_ref[...].astype(F32) + e2 * o2_ref[...].astype(F32) + e3 * o3_ref[...].astype(F32)) / (e1 + e2 + e3)
    sig = lambda ref: _sigmoid(ref[...].astype(F32))
    merged = (sig(ga_ref) * _dot(ya.astype(BF16), wpa_ref[...])
              + sig(gb_ref) * _dot(yb_ref[...], wpb_ref[...])
              + sig(gc_ref) * _dot(yc_ref[...], wpc_ref[...]))
    out = _dot(merged.astype(BF16), wout_ref[...])
    x = x_ref[...] + g1_ref[0] * out
    x_out[...] = x
    h = x * lax.rsqrt(jnp.mean(x * x, axis=-1, keepdims=True) + EPS) * mul2_ref[0] + sh2_ref[0]
    h_out[...] = h.astype(BF16)
    hh, hm, _ = _split3(h)
    s_out[...] = _sigmoid(_dot(hh, wrh_ref[...]) + _dot(hm, wrh_ref[...]) + _dot(hh, wrm_ref[...]))


def proj_merge(g, o_list, lse_list, yb, yc, y, x, gate1, mul2, sh2, wpa, wpb, wpc, wout, wrh, wrm, tm):
    t, d = x.shape
    bmap = lambda i: (g.batch_of_block(i, tm), 0, 0)
    row = lambda w, col=0: pl.BlockSpec((tm, w), lambda i: (i, col))
    full = lambda a: pl.BlockSpec(a.shape, lambda i: (0,) * a.ndim)
    gcol = OFF_GATES // d
    return pl.pallas_call(
        _proj_kernel,
        grid=(t // tm,),
        in_specs=[row(A_WIDTH)] * 6 + [row(B_VW), row(C_PAD_WIDTH), row(d, gcol), row(d, gcol + 1), row(d, gcol + 2),
                                      row(d), pl.BlockSpec((1, 1, d), bmap), pl.BlockSpec((1, 1, d), bmap),
                                      pl.BlockSpec((1, 1, d), bmap),
                                      full(wpa), full(wpb), full(wpc), full(wout), full(wrh), full(wrm)],
        out_specs=[row(d), row(d), row(LANES)],
        out_shape=[jax.ShapeDtypeStruct((t, d), F32), jax.ShapeDtypeStruct((t, d), BF16),
                   jax.ShapeDtypeStruct((t, LANES), F32)],
        compiler_params=_cparams(("arbitrary",)),
    )(*o_list, *lse_list, yb, yc, y, y, y, x, gate1, mul2, sh2, wpa, wpb, wpc, wout, wrh, wrm)


def _moe_kernel(h_ref, comb_ref, wg_ref, wu_ref, wd_ref, x_ref, g2_ref, fn_ref, o_ref, acc_scr, *, final):
    e = pl.program_id(1)

    @pl.when(e == 0)
    def _():
        acc_scr[...] = jnp.zeros_like(acc_scr)

    h = h_ref[...]
    gate = _dot(h, wg_ref[0])
    a = gate * _sigmoid(gate) * _dot(h, wu_ref[0])
    yv = _dot(a.astype(BF16), wd_ref[0])
    comb = comb_ref[...]
    lane = lax.broadcasted_iota(jnp.int32, comb.shape, 1)
    ce = jnp.sum(jnp.where(lane == e, comb, 0.0), axis=1, keepdims=True)
    acc_scr[...] += ce * yv

    @pl.when(e == N_EXPERTS - 1)
    def _():
        x = x_ref[...] + g2_ref[0] * acc_scr[...]
        if final:
            x = x * lax.rsqrt(jnp.mean(x * x, axis=-1, keepdims=True) + EPS) * fn_ref[...]
        o_ref[...] = x


def moe_dense(g, h, comb, wg, wu, wd, x, gate2, fnorm, final, tm):
    t, d = x.shape
    bmap = lambda i, e: (g.batch_of_block(i, tm), 0, 0)
    kern = functools.partial(_moe_kernel, final=final)
    return pl.pallas_call(
        kern,
        grid=(t // tm, N_EXPERTS),
        in_specs=[pl.BlockSpec((tm, d), lambda i, e: (i, 0)),
                  pl.BlockSpec((tm, LANES), lambda i, e: (i, 0)),
                  pl.BlockSpec((1, d, D_EXPERT), lambda i, e: (e, 0, 0)),
                  pl.BlockSpec((1, d, D_EXPERT), lambda i, e: (e, 0, 0)),
                  pl.BlockSpec((1, D_EXPERT, d), lambda i, e: (e, 0, 0)),
                  pl.BlockSpec((tm, d), lambda i, e: (i, 0)),
                  pl.BlockSpec((1, 1, d), bmap),
                  pl.BlockSpec((1, d), lambda i, e: (0, 0))],
        out_specs=pl.BlockSpec((tm, d), lambda i, e: (i, 0)),
        out_shape=jax.ShapeDtypeStruct((t, d), F32),
        scratch_shapes=[pltpu.VMEM((tm, d), F32)],
        compiler_params=_cparams(("arbitrary", "arbitrary")),
    )(h, comb, wg, wu, wd, x, gate2, fnorm)


def routing_weights(s, b_router):
    s = s[:, :N_EXPERTS]
    sel = (s + b_router.astype(F32)).reshape(-1, N_GROUPS, EXPERTS_PER_GROUP)
    group_score = jnp.sum(lax.top_k(sel, GROUP_SCORE_K)[0], axis=-1)
    g_idx = jnp.argmax(group_score, axis=-1)
    in_group = jnp.take_along_axis(sel, g_idx[:, None, None], axis=1)[:, 0]
    _, e_local = lax.top_k(in_group, TOP_K)
    e_idx = g_idx[:, None] * EXPERTS_PER_GROUP + e_local
    s_sel = jnp.take_along_axis(s, e_idx, axis=1)
    w = s_sel / jnp.sum(s_sel, axis=-1, keepdims=True)
    comb = jnp.sum(jax.nn.one_hot(e_idx, LANES, dtype=F32) * w[..., None], axis=1)
    return comb


def _rope_tables(smax, rot_dim, period, lane_off):
    half = rot_dim // 2
    inv_freq = ROPE_THETA ** (-jnp.arange(half, dtype=F32) / half)
    ang = jnp.arange(smax).astype(F32)[:, None] * inv_freq[None, :]
    cos, sin = jnp.cos(ang), jnp.sin(ang)
    gl = (jnp.arange(LANES) % period) - lane_off
    first = (gl >= 0) & (gl < half)
    second = (gl >= half) & (gl < rot_dim)
    j = jnp.clip(jnp.where(first, gl, gl - half), 0, half - 1)
    cl, sn = cos[:, j], sin[:, j]
    ct = jnp.where((first | second)[None, :], cl, 1.0)
    s1 = jnp.where(first[None, :], -sn, 0.0)
    s2 = jnp.where(second[None, :], sn, 0.0)
    return ct, s1, s2


def _pad_w_in(w):
    d = w.shape[0]
    parts, off = [], 0
    for sz in IN_SIZES:
        parts.append(w[:, off:off + sz])
        off += sz
    a_q, a_k, a_v, b_q, b_k, b_v, b_r, b_zf, b_zb, c_cq, c_ckv, c_kr, gates = parts
    z = lambda n: jnp.zeros((d, n), w.dtype)
    bz = jnp.concatenate([b_zf, b_zb, z(LANES - 2 * B_GATE_RANK)], axis=1)
    kr = jnp.concatenate([z(KR_LANE), c_kr, z(LANES - KR_LANE - C_ROPE_DIM)], axis=1)
    out = jnp.concatenate([gates, b_v, b_r, a_q, a_k, a_v, b_q, b_k, c_ckv, bz, c_cq, kr, z(LANES)], axis=1)
    assert out.shape[1] == NPAD
    return out.astype(BF16)


def _pad_heads_cols(w, real, take_lo, take_hi):
    kdim = w.shape[0]
    wh = w.reshape(kdim, C_HEADS, real)[:, :, take_lo:take_hi]
    wh = jnp.pad(wh, ((0, 0), (0, 0), (0, C_HEAD_PAD - (take_hi - take_lo))))
    return wh.reshape(kdim, C_PAD_WIDTH).astype(BF16)


def _gate_weights(w_a, b_a, row_off):
    wa = jnp.zeros((LANES, B_KW), F32).at[row_off:row_off + B_GATE_RANK].set(w_a)
    wa = wa.reshape(LANES, B_HEADS, B_KEY_DIM).transpose(1, 0, 2).astype(BF16)
    return wa, b_a.reshape(B_HEADS, 1, B_KEY_DIM).astype(F32)


def kernel(x_prompt, x_sample, c_prompt, c_sample, norm_mix, norm_moe, w_mod, b_mod, w_in, w_gla_af, b_gla_af,
           w_gla_ab, b_gla_ab, gla_norm, mla_q_norm, w_mla_uq, mla_kv_norm, w_mla_ukv, w_proj_a, w_proj_b, w_proj_c,
           w_out, w_router, b_router, w_exp_gate, w_exp_up, w_exp_down, final_norm):
    b1, s1, d = x_prompt.shape
    b2, s2, _ = x_sample.shape
    g = Groups(b1, s1, b2, s2)
    depth = w_in.shape[0]
    tm_in = min(1024, s1, s2)
    tm_prep = min(512, s1, s2)
    tm_proj = min(512, s1, s2)
    tm_moe = min(1024, s1, s2)

    x = jnp.concatenate([x_prompt.reshape(g.t1, d), x_sample.reshape(g.t2, d)], axis=0)
    c = jnp.concatenate([c_prompt, c_sample], axis=0)
    nbp = -(-g.nb // 8) * 8
    c_pad = jnp.pad(c, ((0, nbp - g.nb), (0, 0)))
    mod = modulation(c_pad, w_mod, b_mod)[:, :g.nb]

    smax = max(s1, s2)
    tabs_a = _rope_tables(smax, A_ROPE_DIM, A_HEAD_DIM, 0)
    tabs_c = _rope_tables(smax, C_ROPE_DIM, C_HEAD_PAD, KR_LANE)

    wr_hi = jnp.pad(w_router, ((0, 0), (0, LANES - N_EXPERTS)))
    wrh = wr_hi.astype(BF16)
    wrm = (wr_hi - wrh.astype(F32)).astype(BF16)
    fnorm = final_norm.reshape(1, d)

    for l in range(depth):
        sh1, sc1, gt1, sh2, sc2, gt2 = [m.reshape(g.nb, 1, d) for m in jnp.split(mod[l], N_MOD, axis=-1)]
        mul1 = norm_mix[l][None, None, :] * (1.0 + sc1)
        mul2 = norm_moe[l][None, None, :] * (1.0 + sc2)

        y = in_projection(g, x, mul1, sh1, _pad_w_in(w_in[l]), tabs_a, tm_in)

        o_list, lse_list = [], []
        for _, dil in A_PATTERNS:
            o, lse = band_attention(g, y, dil)
            o_list.append(o)
            lse_list.append(lse)

        waf, baf = _gate_weights(w_gla_af[l], b_gla_af[l], 0)
        wab, bab = _gate_weights(w_gla_ab[l], b_gla_ab[l], B_GATE_RANK)
        o_back = gla_pass(g, y, wab, bab, reverse=True)
        yb = gla_pass(g, y, waf, baf, reverse=False, o_back=o_back, gain=gla_norm[l].reshape(1, B_VAL_DIM))

        wq = jnp.concatenate(
            [w_mla_uq[l].reshape(C_Q_RANK, C_HEADS, C_QK_DIM),
             jnp.zeros((C_Q_RANK, C_HEADS, C_HEAD_PAD - C_QK_DIM), F32)], axis=2
        ).reshape(C_Q_RANK, C_PAD_WIDTH).astype(BF16)
        wk = _pad_heads_cols(w_mla_ukv[l], C_NOPE_DIM + C_V_DIM, 0, C_NOPE_DIM)
        wv = _pad_heads_cols(w_mla_ukv[l], C_NOPE_DIM + C_V_DIM, C_NOPE_DIM, C_NOPE_DIM + C_V_DIM)
        qm, km, vm = mla_prep(g, y, mla_q_norm[l].reshape(1, C_Q_RANK), mla_kv_norm[l].reshape(1, C_KV_RANK),
                              wq, wk, wv, tabs_c, tm_prep)
        yc = jnp.concatenate([mla_flash(qm, km, vm, 0, b1, s1, min(256, s1), min(512, s1)),
                              mla_flash(qm, km, vm, g.t1, b2, s2, min(256, s2), min(512, s2))], axis=0)

        wpc = jnp.pad(w_proj_c[l].reshape(C_HEADS, C_V_DIM, d), ((0, 0), (0, C_HEAD_PAD - C_V_DIM), (0, 0)))
        wpc = wpc.reshape(C_PAD_WIDTH, d).astype(BF16)
        x, h2, s = proj_merge(g, o_list, lse_list, yb, yc, y, x, gt1, mul2, sh2,
                              w_proj_a[l].astype(BF16), w_proj_b[l].astype(BF16), wpc, w_out[l].astype(BF16),
                              wrh, wrm, tm_proj)

        comb = routing_weights(s, b_router)
        x = moe_dense(g, h2, comb, w_exp_gate[l].astype(BF16), w_exp_up[l].astype(BF16),
                      w_exp_down[l].astype(BF16), x, gt2, fnorm, l == depth - 1, tm_moe)

    return x[:g.t1].reshape(b1, s1, d), x[g.t1:].reshape(b2, s2, d)
```

```python
import functools

import jax
import jax.numpy as jnp
from jax import lax
from jax.experimental import pallas as pl
from jax.experimental.pallas import tpu as pltpu

F32 = jnp.float32
BF16 = jnp.bfloat16

D_MODEL = 1024
DEPTH = 2
EPS = 1e-6
ROPE_THETA = 500000.0
NEG_INF = -1e30

A_HEADS = 8
A_HEAD_DIM = 64
A_ROPE_DIM = A_HEAD_DIM // 4
A_PATTERNS = ((128, 1), (512, 4), (2048, 16))
A_WIDTH = A_HEADS * A_HEAD_DIM
A_HALF = 64

B_HEADS = 4
B_KEY_DIM = 128
B_VAL_DIM = 256
B_GATE_RANK = 16
B_GATE_TAU = 16.0
B_CHUNK = 64
B_KW = B_HEADS * B_KEY_DIM
B_VW = B_HEADS * B_VAL_DIM

C_HEADS = 8
C_NOPE_DIM = 64
C_ROPE_DIM = 32
C_V_DIM = 64
C_QK_DIM = C_NOPE_DIM + C_ROPE_DIM
C_Q_RANK = 384
C_KV_RANK = 256
C_HEAD_PAD = 128
C_PAD_WIDTH = C_HEADS * C_HEAD_PAD

N_EXPERTS = 16
N_GROUPS = 4
EXPERTS_PER_GROUP = N_EXPERTS // N_GROUPS
TOP_K = 2
GROUP_SCORE_K = 2
D_EXPERT = 512
N_MOD = 6

IN_SIZES = (A_WIDTH, A_WIDTH, A_WIDTH, B_KW, B_KW, B_VW, B_VW, B_GATE_RANK, B_GATE_RANK,
            C_Q_RANK, C_KV_RANK, C_ROPE_DIM, 3 * D_MODEL)

LANES = 128
LOG2E = 1.4426950408889634

OFF_GATES = 0
OFF_BV = 3072
OFF_BR = 4096
OFF_AQ = 5120
OFF_AK = 5632
OFF_AV = 6144
OFF_BQ = 6656
OFF_BK = 7168
OFF_CKV = 7680
OFF_BZ = 7936
OFF_CQ = 8064
OFF_KR = 8448
NPAD = 8704
KR_LANE = 64

VMEM_LIMIT = 56 * 1024 * 1024


def _cparams(sem):
    return pltpu.CompilerParams(dimension_semantics=sem, vmem_limit_bytes=VMEM_LIMIT)


class Groups:
    def __init__(self, b1, s1, b2, s2):
        self.b1, self.s1, self.b2, self.s2 = b1, s1, b2, s2
        self.t1, self.t2 = b1 * s1, b2 * s2
        self.t = self.t1 + self.t2
        self.nb = b1 + b2

    def batch_of_block(self, i, tm):
        n1 = self.t1 // tm
        return jnp.where(i < n1, i // (self.s1 // tm), self.b1 + (i - n1) // (self.s2 // tm))

    def pos_block(self, i, tm):
        n1 = self.t1 // tm
        return jnp.where(i < n1, i % (self.s1 // tm), (i - n1) % (self.s2 // tm))


def _dot(a, b):
    return jnp.dot(a, b, preferred_element_type=F32)


def _dot_nt(a, b):
    return lax.dot_general(a, b, (((1,), (1,)), ((), ())), preferred_element_type=F32)


def _dot_tn(a, b):
    return lax.dot_general(a, b, (((0,), (0,)), ((), ())), preferred_element_type=F32)


def _split3(a):
    hi = a.astype(BF16)
    r1 = a - hi.astype(F32)
    mid = r1.astype(BF16)
    lo = (r1 - mid.astype(F32)).astype(BF16)
    return hi, mid, lo


def _rope(y, ct, s1, s2, half):
    return y * ct + pltpu.roll(y, LANES - half, 1) * s1 + pltpu.roll(y, half, 1) * s2


def _sigmoid(x):
    return 1.0 / (1.0 + jnp.exp(-x))


def _mod_kernel(c_ref, w_ref, b_ref, o_ref):
    c = c_ref[...]
    ca = c * _sigmoid(c)
    ch, cm, _ = _split3(ca)
    wh, wm, _ = _split3(w_ref[0])
    o_ref[0] = _dot(ch, wh) + _dot(cm, wh) + _dot(ch, wm) + b_ref[0]


def modulation(c_pad, w_mod, b_mod):
    nbp, d = c_pad.shape
    depth, _, n = w_mod.shape
    tn = 512
    return pl.pallas_call(
        _mod_kernel,
        grid=(depth, n // tn),
        in_specs=[pl.BlockSpec((nbp, d), lambda l, j: (0, 0)),
                  pl.BlockSpec((1, d, tn), lambda l, j: (l, 0, j)),
                  pl.BlockSpec((1, 1, tn), lambda l, j: (l, 0, j))],
        out_specs=pl.BlockSpec((1, nbp, tn), lambda l, j: (l, 0, j)),
        out_shape=jax.ShapeDtypeStruct((depth, nbp, n), F32),
        compiler_params=_cparams(("arbitrary", "arbitrary")),
        name="modulation",
    )(c_pad, w_mod, b_mod.reshape(depth, 1, n))


def _inproj_kernel(x_ref, mul_ref, sh_ref, w_ref, ct_ref, s1_ref, s2_ref, o_ref, od4_ref, od16_ref, h_scr, y_scr,
                   *, tm, tn, jq, jv):
    j = pl.program_id(1)

    @pl.when(j == 0)
    def _():
        x = x_ref[...]
        ms = jnp.mean(x * x, axis=-1, keepdims=True)
        h = x * lax.rsqrt(ms + EPS) * mul_ref[0] + sh_ref[0]
        h_scr[...] = h.astype(BF16)

    y = _dot(h_scr[...], w_ref[...])
    is_attn = (j >= jq) & (j <= jv)

    @pl.when(jnp.logical_not(is_attn))
    def _():
        o_ref[...] = y.astype(BF16)

    @pl.when(is_attn)
    def _():
        @pl.when(j < jv)
        def _():
            ct, s1, s2 = ct_ref[...], s1_ref[...], s2_ref[...]
            for c in range(tn // LANES):
                y_scr[c] = _rope(y[:, c * LANES:(c + 1) * LANES], ct, s1, s2, A_ROPE_DIM // 2)

        @pl.when(j == jv)
        def _():
            for c in range(tn // LANES):
                y_scr[c] = y[:, c * LANES:(c + 1) * LANES]

        for c in range(tn // LANES):
            o_ref[:, c * LANES:(c + 1) * LANES] = y_scr[c].astype(BF16)
            for dil, ref in ((4, od4_ref), (16, od16_ref)):
                for r in range(dil):
                    ref[:, r * tn + c * LANES:r * tn + (c + 1) * LANES] = (
                        y_scr[c, pl.ds(r, tm // dil, stride=dil), :].astype(BF16))


def in_projection(g, x, mul, shift, w_pad, tabs, tm):
    t, d = x.shape
    tn = A_WIDTH
    ct, s1, s2 = tabs
    jq, jv = OFF_AQ // tn, OFF_AV // tn
    bmap = lambda i, j: (g.batch_of_block(i, tm), 0, 0)
    pmap = lambda i, j: (g.pos_block(i, tm), 0)
    dmap = lambda i, j: (jnp.clip(j - jq, 0, jv - jq), i, 0)
    kern = functools.partial(_inproj_kernel, tm=tm, tn=tn, jq=jq, jv=jv)
    return pl.pallas_call(
        kern,
        grid=(t // tm, NPAD // tn),
        in_specs=[pl.BlockSpec((tm, d), lambda i, j: (i, 0)),
                  pl.BlockSpec((1, 1, d), bmap),
                  pl.BlockSpec((1, 1, d), bmap),
                  pl.BlockSpec((d, tn), lambda i, j: (0, j)),
                  pl.BlockSpec((tm, LANES), pmap),
                  pl.BlockSpec((tm, LANES), pmap),
                  pl.BlockSpec((tm, LANES), pmap)],
        out_specs=[pl.BlockSpec((tm, tn), lambda i, j: (i, j)),
                   pl.BlockSpec((None, tm // 4, 4 * tn), dmap),
                   pl.BlockSpec((None, tm // 16, 16 * tn), dmap)],
        out_shape=[jax.ShapeDtypeStruct((t, NPAD), BF16),
                   jax.ShapeDtypeStruct((3, t // 4, 4 * tn), BF16),
                   jax.ShapeDtypeStruct((3, t // 16, 16 * tn), BF16)],
        scratch_shapes=[pltpu.VMEM((tm, d), BF16), pltpu.VMEM((tn // LANES, tm, LANES), F32)],
        compiler_params=_cparams(("arbitrary", "arbitrary")),
        name="in_projection",
    )(x, mul, shift, w_pad, ct, s1, s2)


def _band_kernel(q_ref, kp_ref, km_ref, kn_ref, vp_ref, vm_ref, vn_ref, o_ref, lse_ref, *, tq, rows1, l1, l2):
    i = pl.program_id(1)
    row0 = i * tq
    in1 = row0 < rows1
    seq_len = jnp.where(in1, l1, l2)
    pos0 = jnp.where(in1, row0 % l1, (row0 - rows1) % l2)
    tk = tq + 2 * A_HALF
    q = q_ref[...]
    k = jnp.concatenate([kp_ref[...], km_ref[...], kn_ref[...]], axis=0)
    v = jnp.concatenate([vp_ref[...], vm_ref[...], vn_ref[...]], axis=0)
    qi = lax.broadcasted_iota(jnp.int32, (tq, tk), 0)
    kj = lax.broadcasted_iota(jnp.int32, (tq, tk), 1)
    rel = kj - A_HALF - qi
    kpos = pos0 - A_HALF + kj
    valid = (jnp.abs(rel) <= A_HALF) & (kpos >= 0) & (kpos < seq_len)
    lo = lax.broadcasted_iota(jnp.int32, (tq, LANES), 1) < A_HEAD_DIM
    scale = A_HEAD_DIM ** -0.5
    for p in range(A_WIDTH // LANES):
        sl = slice(p * LANES, (p + 1) * LANES)
        qp, kp, vp = q[:, sl], k[:, sl], v[:, sl]
        outs, lses = [], []
        for hh in range(2):
            qm = jnp.where(lo if hh == 0 else jnp.logical_not(lo), qp, jnp.zeros_like(qp))
            s = _dot_nt(qm, kp) * scale
            s = jnp.where(valid, s, NEG_INF)
            m = jnp.max(s, axis=1, keepdims=True)
            e = jnp.exp(s - m)
            l = jnp.sum(e, axis=1, keepdims=True)
            outs.append(_dot(e.astype(BF16), vp) / l)
            lses.append(m + jnp.log(l))
        o_ref[:, sl] = jnp.where(lo, outs[0], outs[1]).astype(BF16)
        lse_ref[:, sl] = jnp.where(lo, lses[0], lses[1])


def band_attention(g, qkv, dil, cols, cpb):
    rows = g.t // dil
    tq = 128
    sub = tq // A_HALF
    nsub = rows // A_HALF

    def main(c):
        ld, col = cols[c]
        return pl.BlockSpec((None, tq, A_WIDTH), lambda r, i: (ld, i, r * cpb + col))

    def prev(c):
        ld, col = cols[c]
        return pl.BlockSpec((None, A_HALF, A_WIDTH), lambda r, i: (ld, jnp.maximum(i * sub - 1, 0), r * cpb + col))

    def nxt(c):
        ld, col = cols[c]
        return pl.BlockSpec((None, A_HALF, A_WIDTH),
                            lambda r, i: (ld, jnp.minimum((i + 1) * sub, nsub - 1), r * cpb + col))

    kern = functools.partial(_band_kernel, tq=tq, rows1=g.t1 // dil, l1=g.s1 // dil, l2=g.s2 // dil)
    return pl.pallas_call(
        kern,
        grid=(dil, rows // tq),
        in_specs=[main(0), prev(1), main(1), nxt(1), prev(2), main(2), nxt(2)],
        out_specs=[pl.BlockSpec((tq, A_WIDTH), lambda r, i: (i, r)),
                   pl.BlockSpec((tq, A_WIDTH), lambda r, i: (i, r))],
        out_shape=[jax.ShapeDtypeStruct((rows, dil * A_WIDTH), BF16),
                   jax.ShapeDtypeStruct((rows, dil * A_WIDTH), F32)],
        compiler_params=_cparams(("arbitrary", "arbitrary")),
        name="band_attention_d%d" % dil,
    )(qkv, qkv, qkv, qkv, qkv, qkv, qkv)


def _gla_kernel(*refs, reverse, final, tc, nblk, t1, s1, s2):
    if final:
        q_ref, k_ref, v_ref, z_ref, wa_ref, ba_ref, ob_ref, r_ref, gain_ref, o_ref, st_scr = refs
    else:
        q_ref, k_ref, v_ref, z_ref, wa_ref, ba_ref, o_ref, st_scr = refs
    i = pl.program_id(1)
    blk = (nblk - 1 - i) if reverse else i
    row0 = blk * tc
    in1 = row0 < t1
    pos0 = jnp.where(in1, row0 % s1, (row0 - t1) % s2)
    slen = jnp.where(in1, s1, s2)
    start = (pos0 + tc == slen) if reverse else (pos0 == 0)

    @pl.when(start)
    def _():
        st_scr[...] = jnp.zeros_like(st_scr)

    zl = _dot(z_ref[...], wa_ref[0]) + ba_ref[0]
    la = (jnp.minimum(zl, 0.0) - jnp.log(1.0 + jnp.exp(-jnp.abs(zl)))) * (1.0 / B_GATE_TAU)
    ri = lax.broadcasted_iota(jnp.int32, (tc, tc), 0)
    ci = lax.broadcasted_iota(jnp.int32, (tc, tc), 1)
    same = (ri // B_CHUNK) == (ci // B_CHUNK)
    tri = jnp.where(same & ((ci >= ri) if reverse else (ci <= ri)), 1.0, 0.0).astype(BF16)
    hi, mid, lo = _split3(la)
    bc = _dot(tri, hi) + _dot(tri, mid) + _dot(tri, lo)

    q = q_ref[...].astype(F32) * (B_KEY_DIM ** -0.5)
    k = k_ref[...].astype(F32)
    v = v_ref[...]
    eb = jnp.exp(bc)
    q_dec = (q * eb).astype(BF16)
    k_inv = (k * jnp.exp(-bc)).astype(BF16)
    qi = lax.broadcasted_iota(jnp.int32, (B_CHUNK, B_CHUNK), 0)
    si = lax.broadcasted_iota(jnp.int32, (B_CHUNK, B_CHUNK), 1)
    mask = (si > qi) if reverse else (si <= qi)
    st = st_scr[...]
    nch = tc // B_CHUNK
    outs = [None] * nch
    for c in (range(nch - 1, -1, -1) if reverse else range(nch)):
        sl = slice(c * B_CHUNK, (c + 1) * B_CHUNK)
        tot = bc[c * B_CHUNK:c * B_CHUNK + 1] if reverse else bc[(c + 1) * B_CHUNK - 1:(c + 1) * B_CHUNK]
        k_end = (k[sl] * jnp.exp(tot - bc[sl])).astype(BF16)
        att = jnp.where(mask, _dot_nt(q_dec[sl], k_inv[sl]), 0.0).astype(BF16)
        outs[c] = _dot(att, v[sl]) + _dot_nt(q_dec[sl], st.astype(BF16))
        st = st * jnp.exp(tot) + _dot_tn(v[sl], k_end)
    st_scr[...] = st
    o = jnp.concatenate(outs, axis=0)
    if final:
        o = o + ob_ref[...]
        on = o * lax.rsqrt(jnp.mean(o * o, axis=-1, keepdims=True) + EPS) * gain_ref[...]
        r = r_ref[...].astype(F32)
        o_ref[...] = (on * (r * _sigmoid(r))).astype(BF16)
    else:
        o_ref[...] = o


def gla_pass(g, y, wa, ba, reverse, o_back=None, gain=None):
    t = g.t
    tc = 256
    nblk = t // tc
    final = o_back is not None
    rowmap = (lambda i: nblk - 1 - i) if reverse else (lambda i: i)
    kq, kk = OFF_BQ // B_KEY_DIM, OFF_BK // B_KEY_DIM
    kv, kr, kz = OFF_BV // B_VAL_DIM, OFF_BR // B_VAL_DIM, OFF_BZ // LANES
    in_specs = [pl.BlockSpec((tc, B_KEY_DIM), lambda h, i: (rowmap(i), kq + h)),
                pl.BlockSpec((tc, B_KEY_DIM), lambda h, i: (rowmap(i), kk + h)),
                pl.BlockSpec((tc, B_VAL_DIM), lambda h, i: (rowmap(i), kv + h)),
                pl.BlockSpec((tc, LANES), lambda h, i: (rowmap(i), kz)),
                pl.BlockSpec((1, LANES, B_KEY_DIM), lambda h, i: (h, 0, 0)),
                pl.BlockSpec((1, 1, B_KEY_DIM), lambda h, i: (h, 0, 0))]
    args = [y, y, y, y, wa, ba]
    if final:
        in_specs += [pl.BlockSpec((tc, B_VAL_DIM), lambda h, i: (rowmap(i), h)),
                     pl.BlockSpec((tc, B_VAL_DIM), lambda h, i: (rowmap(i), kr + h)),
                     pl.BlockSpec((1, B_VAL_DIM), lambda h, i: (0, 0))]
        args += [o_back, y, gain]
    kern = functools.partial(_gla_kernel, reverse=reverse, final=final, tc=tc, nblk=nblk, t1=g.t1, s1=g.s1, s2=g.s2)
    return pl.pallas_call(
        kern,
        grid=(B_HEADS, nblk),
        in_specs=in_specs,
        out_specs=pl.BlockSpec((tc, B_VAL_DIM), lambda h, i: (rowmap(i), h)),
        out_shape=jax.ShapeDtypeStruct((t, B_VW), BF16 if final else F32),
        scratch_shapes=[pltpu.VMEM((B_VAL_DIM, B_KEY_DIM), F32)],
        compiler_params=_cparams(("arbitrary", "arbitrary")),
        name="gla_forward_final" if final else "gla_backward",
    )(*args)


def _mla_prep_kernel(ckv_ref, cq_ref, kr_ref, qn_ref, kvn_ref, wq_ref, wk_ref, wv_ref, ct_ref, s1_ref, s2_ref,
                     q_out, k_out, v_out):
    def norm(ref, gain_ref):
        xf = ref[...].astype(F32)
        return (xf * lax.rsqrt(jnp.mean(xf * xf, axis=-1, keepdims=True) + EPS) * gain_ref[...]).astype(BF16)

    ckv_n = norm(ckv_ref, kvn_ref)
    cq_n = norm(cq_ref, qn_ref)
    q = _dot(cq_n, wq_ref[...])
    kn = _dot(ckv_n, wk_ref[...])
    v = _dot(ckv_n, wv_ref[...])
    ct, s1, s2 = ct_ref[...], s1_ref[...], s2_ref[...]
    half = C_ROPE_DIM // 2
    kr_rot = _rope(kr_ref[...].astype(F32), ct, s1, s2, half)
    ones_lane = lax.broadcasted_iota(jnp.int32, kr_rot.shape, 1) == C_V_DIM
    scale = C_QK_DIM ** -0.5 * LOG2E
    for h in range(C_HEADS):
        sl = slice(h * C_HEAD_PAD, (h + 1) * C_HEAD_PAD)
        q_out[:, sl] = (_rope(q[:, sl], ct, s1, s2, half) * scale).astype(BF16)
        k_out[:, sl] = (kn[:, sl] + kr_rot).astype(BF16)
        v_out[:, sl] = jnp.where(ones_lane, 1.0, v[:, sl]).astype(BF16)


def mla_prep(g, y, qn, kvn, wq, wk, wv, tabs, tm):
    t = g.t
    ct, s1, s2 = tabs
    pmap = lambda i: (g.pos_block(i, tm), 0)
    full = lambda a: pl.BlockSpec(a.shape, lambda i: (0,) * a.ndim)
    out = jax.ShapeDtypeStruct((t, C_PAD_WIDTH), BF16)
    ospec = pl.BlockSpec((tm, C_PAD_WIDTH), lambda i: (i, 0))
    return pl.pallas_call(
        _mla_prep_kernel,
        grid=(t // tm,),
        in_specs=[pl.BlockSpec((tm, C_KV_RANK), lambda i: (i, OFF_CKV // C_KV_RANK)),
                  pl.BlockSpec((tm, C_Q_RANK), lambda i: (i, OFF_CQ // C_Q_RANK)),
                  pl.BlockSpec((tm, LANES), lambda i: (i, OFF_KR // LANES)),
                  full(qn), full(kvn), full(wq), full(wk), full(wv),
                  pl.BlockSpec((tm, LANES), pmap), pl.BlockSpec((tm, LANES), pmap), pl.BlockSpec((tm, LANES), pmap)],
        out_specs=[ospec, ospec, ospec],
        out_shape=[out, out, out],
        compiler_params=_cparams(("arbitrary",)),
        name="mla_prep",
    )(y, y, y, qn, kvn, wq, wk, wv, ct, s1, s2)


def _flash_kernel(q_ref, k_ref, v_ref, o_ref, *, tk, nk, unroll):
    q = q_ref[...]
    tq = q.shape[0]

    def body(j, carry):
        m, acc = carry
        off = pl.multiple_of(j * tk, tk)
        s = _dot_nt(q, k_ref[pl.ds(off, tk), :])
        m_new = jnp.maximum(m, jnp.max(s, axis=1, keepdims=True))
        p = jnp.exp2(s - m_new)
        acc = acc * jnp.exp2(m - m_new) + _dot(p.astype(BF16), v_ref[pl.ds(off, tk), :])
        return m_new, acc

    m0 = jnp.full((tq, 1), NEG_INF, F32)
    acc0 = jnp.zeros((tq, C_HEAD_PAD), F32)
    _, acc = lax.fori_loop(0, nk, body, (m0, acc0), unroll=unroll)
    o_ref[...] = (acc / acc[:, C_V_DIM:C_V_DIM + 1]).astype(BF16)


def mla_flash(q, k, v, row_off, nseq, s, tq, tk):
    assert row_off % s == 0 and row_off % tq == 0
    qb0 = row_off // tq
    sb0 = row_off // s
    nq = s // tq
    nk = s // tk
    kern = functools.partial(_flash_kernel, tk=tk, nk=nk, unroll=min(2, nk))
    return pl.pallas_call(
        kern,
        grid=(nseq, C_HEADS, nq),
        in_specs=[pl.BlockSpec((tq, C_HEAD_PAD), lambda b, h, i: (qb0 + b * nq + i, h)),
                  pl.BlockSpec((s, C_HEAD_PAD), lambda b, h, i: (sb0 + b, h)),
                  pl.BlockSpec((s, C_HEAD_PAD), lambda b, h, i: (sb0 + b, h))],
        out_specs=pl.BlockSpec((tq, C_HEAD_PAD), lambda b, h, i: (b * nq + i, h)),
        out_shape=jax.ShapeDtypeStruct((nseq * s, C_PAD_WIDTH), BF16),
        compiler_params=_cparams(("arbitrary", "arbitrary", "arbitrary")),
        name="mla_flash_s%d" % s,
    )(q, k, v)


def _route(s, bias):
    lane = lax.broadcasted_iota(jnp.int32, s.shape, 1)
    pos = lane % EXPERTS_PER_GROUP
    grp = lane // EXPERTS_PER_GROUP
    sel = s + bias
    rank = jnp.zeros(s.shape, jnp.int32)
    for k in range(1, EXPERTS_PER_GROUP):
        below = pltpu.roll(sel, k, 1)
        above = pltpu.roll(sel, LANES - k, 1)
        rank += jnp.where((pos >= k) & (below >= sel), 1, 0)
        rank += jnp.where((pos + k < EXPERTS_PER_GROUP) & (above > sel), 1, 0)
    top = rank < GROUP_SCORE_K
    contrib = jnp.where(top, sel, 0.0)
    score = contrib
    for k in range(1, EXPERTS_PER_GROUP):
        score += jnp.where(pos >= k, pltpu.roll(contrib, k, 1), 0.0)
        score += jnp.where(pos + k < EXPERTS_PER_GROUP, pltpu.roll(contrib, LANES - k, 1), 0.0)
    best = lane < N_EXPERTS
    for k in range(1, N_GROUPS):
        earlier = pltpu.roll(score, k * EXPERTS_PER_GROUP, 1)
        later = pltpu.roll(score, LANES - k * EXPERTS_PER_GROUP, 1)
        best &= jnp.logical_not((grp >= k) & (earlier >= score))
        best &= jnp.logical_not((grp + k < N_GROUPS) & (later > score))
    chosen = best & (rank < TOP_K)
    total = jnp.sum(jnp.where(chosen, s, 0.0), axis=1, keepdims=True)
    return jnp.where(chosen, s / total, 0.0)


def _proj_kernel(o1_ref, o2_ref, o3_ref, l1_ref, l2_ref, l3_ref, yb_ref, yc1_ref, yc2_ref, ga_ref, gb_ref, gc_ref,
                 x_ref, g1_ref, mul2_ref, sh2_ref, wpa_ref, wpb_ref, wpc_ref, wout_ref, wrh_ref, wrm_ref, br_ref,
                 x_out, h_out, comb_out, o2_scr, l2_scr, o3_scr, l3_scr, *, tm, n1):
    for dil, o_ref, l_ref, o_scr, l_scr in ((A_PATTERNS[1][1], o2_ref, l2_ref, o2_scr, l2_scr),
                                           (A_PATTERNS[2][1], o3_ref, l3_ref, o3_scr, l3_scr)):
        for r in range(dil):
            for c in range(A_WIDTH // LANES):
                sl = slice(r * A_WIDTH + c * LANES, r * A_WIDTH + (c + 1) * LANES)
                o_scr[c, pl.ds(r, tm // dil, stride=dil), :] = o_ref[:, sl].astype(F32)
                l_scr[c, pl.ds(r, tm // dil, stride=dil), :] = l_ref[:, sl]
    slabs = lambda scr: jnp.concatenate([scr[c] for c in range(A_WIDTH // LANES)], axis=1)
    l1, l2, l3 = l1_ref[...], slabs(l2_scr), slabs(l3_scr)
    m = jnp.maximum(jnp.maximum(l1, l2), l3)
    e1, e2, e3 = jnp.exp(l1 - m), jnp.exp(l2 - m), jnp.exp(l3 - m)
    ya = (e1 * o1_ref[...].astype(F32) + e2 * slabs(o2_scr) + e3 * slabs(o3_scr)) / (e1 + e2 + e3)
    sig = lambda ref: _sigmoid(ref[...].astype(F32))
    yc = jnp.where(pl.program_id(0) < n1, yc1_ref[...], yc2_ref[...])
    merged = (sig(ga_ref) * _dot(ya.astype(BF16), wpa_ref[...])
              + sig(gb_ref) * _dot(yb_ref[...], wpb_ref[...])
              + sig(gc_ref) * _dot(yc, wpc_ref[...]))
    out = _dot(merged.astype(BF16), wout_ref[...])
    x = x_ref[...] + g1_ref[0] * out
    x_out[...] = x
    h = x * lax.rsqrt(jnp.mean(x * x, axis=-1, keepdims=True) + EPS) * mul2_ref[0] + sh2_ref[0]
    h_out[...] = h.astype(BF16)
    hh, hm, _ = _split3(h)
    s = _sigmoid(_dot(hh, wrh_ref[...]) + _dot(hm, wrh_ref[...]) + _dot(hh, wrm_ref[...]))
    comb_out[...] = _route(s, br_ref[...])


def proj_merge(g, o_list, lse_list, yb, yc1, yc2, y, x, gate1, mul2, sh2, wpa, wpb, wpc, wout, wrh, wrm, br, tm):
    t, d = x.shape
    n1 = g.t1 // tm
    n2 = g.t2 // tm
    bmap = lambda i: (g.batch_of_block(i, tm), 0, 0)
    row = lambda w, col=0: pl.BlockSpec((tm, w), lambda i: (i, col))
    dil_row = lambda dil: pl.BlockSpec((tm // dil, dil * A_WIDTH), lambda i: (i, 0))
    full = lambda a: pl.BlockSpec(a.shape, lambda i: (0,) * a.ndim)
    gcol = OFF_GATES // d
    d2, d3 = A_PATTERNS[1][1], A_PATTERNS[2][1]
    attn_specs = [row(A_WIDTH), dil_row(d2), dil_row(d3)]
    return pl.pallas_call(
        functools.partial(_proj_kernel, tm=tm, n1=n1),
        grid=(t // tm,),
        in_specs=attn_specs + attn_specs + [
            row(B_VW),
            pl.BlockSpec((tm, C_PAD_WIDTH), lambda i: (jnp.minimum(i, n1 - 1), 0)),
            pl.BlockSpec((tm, C_PAD_WIDTH), lambda i: (jnp.clip(i - n1, 0, n2 - 1), 0)),
            row(d, gcol), row(d, gcol + 1), row(d, gcol + 2),
            row(d), pl.BlockSpec((1, 1, d), bmap), pl.BlockSpec((1, 1, d), bmap), pl.BlockSpec((1, 1, d), bmap),
            full(wpa), full(wpb), full(wpc), full(wout), full(wrh), full(wrm), full(br)],
        out_specs=[row(d), row(d), row(LANES)],
        out_shape=[jax.ShapeDtypeStruct((t, d), F32), jax.ShapeDtypeStruct((t, d), BF16),
                   jax.ShapeDtypeStruct((t, LANES), F32)],
        scratch_shapes=[pltpu.VMEM((A_WIDTH // LANES, tm, LANES), F32)] * 4,
        compiler_params=_cparams(("arbitrary",)),
        name="proj_merge_route",
    )(*o_list, *lse_list, yb, yc1, yc2, y, y, y, x, gate1, mul2, sh2, wpa, wpb, wpc, wout, wrh, wrm, br)


def _moe_kernel(h_ref, comb_ref, wg_ref, wu_ref, wd_ref, x_ref, g2_ref, fn_ref, o_ref, acc_scr, *, final):
    e = pl.program_id(1)

    @pl.when(e == 0)
    def _():
        acc_scr[...] = jnp.zeros_like(acc_scr)

    h = h_ref[...]
    gate = _dot(h, wg_ref[0])
    a = gate * _sigmoid(gate) * _dot(h, wu_ref[0])
    yv = _dot(a.astype(BF16), wd_ref[0])
    comb = comb_ref[...]
    lane = lax.broadcasted_iota(jnp.int32, comb.shape, 1)
    ce = jnp.sum(jnp.where(lane == e, comb, 0.0), axis=1, keepdims=True)
    acc_scr[...] += ce * yv

    @pl.when(e == N_EXPERTS - 1)
    def _():
        x = x_ref[...] + g2_ref[0] * acc_scr[...]
        if final:
            x = x * lax.rsqrt(jnp.mean(x * x, axis=-1, keepdims=True) + EPS) * fn_ref[...]
        o_ref[...] = x


def moe_dense(g, h, comb, wg, wu, wd, x, gate2, fnorm, final, tm):
    t, d = x.shape
    bmap = lambda i, e: (g.batch_of_block(i, tm), 0, 0)
    kern = functools.partial(_moe_kernel, final=final)
    return pl.pallas_call(
        kern,
        grid=(t // tm, N_EXPERTS),
        in_specs=[pl.BlockSpec((tm, d), lambda i, e: (i, 0)),
                  pl.BlockSpec((tm, LANES), lambda i, e: (i, 0)),
                  pl.BlockSpec((1, d, D_EXPERT), lambda i, e: (e, 0, 0)),
                  pl.BlockSpec((1, d, D_EXPERT), lambda i, e: (e, 0, 0)),
                  pl.BlockSpec((1, D_EXPERT, d), lambda i, e: (e, 0, 0)),
                  pl.BlockSpec((tm, d), lambda i, e: (i, 0)),
                  pl.BlockSpec((1, 1, d), bmap),
                  pl.BlockSpec((1, d), lambda i, e: (0, 0))],
        out_specs=pl.BlockSpec((tm, d), lambda i, e: (i, 0)),
        out_shape=jax.ShapeDtypeStruct((t, d), F32),
        scratch_shapes=[pltpu.VMEM((tm, d), F32)],
        compiler_params=_cparams(("arbitrary", "arbitrary")),
        name="moe_dense",
    )(h, comb, wg, wu, wd, x, gate2, fnorm)


def _rope_tables(smax, rot_dim, period, lane_off):
    half = rot_dim // 2
    inv_freq = ROPE_THETA ** (-jnp.arange(half, dtype=F32) / half)
    ang = jnp.arange(smax).astype(F32)[:, None] * inv_freq[None, :]
    cos, sin = jnp.cos(ang), jnp.sin(ang)
    gl = (jnp.arange(LANES) % period) - lane_off
    first = (gl >= 0) & (gl < half)
    second = (gl >= half) & (gl < rot_dim)
    j = jnp.clip(jnp.where(first, gl, gl - half), 0, half - 1)
    cl, sn = cos[:, j], sin[:, j]
    ct = jnp.where((first | second)[None, :], cl, 1.0)
    s1 = jnp.where(first[None, :], -sn, 0.0)
    s2 = jnp.where(second[None, :], sn, 0.0)
    return ct, s1, s2


def _pad_w_in(w):
    d = w.shape[0]
    parts, off = [], 0
    for sz in IN_SIZES:
        parts.append(w[:, off:off + sz])
        off += sz
    a_q, a_k, a_v, b_q, b_k, b_v, b_r, b_zf, b_zb, c_cq, c_ckv, c_kr, gates = parts
    z = lambda n: jnp.zeros((d, n), w.dtype)
    bz = jnp.concatenate([b_zf, b_zb, z(LANES - 2 * B_GATE_RANK)], axis=1)
    kr = jnp.concatenate([z(KR_LANE), c_kr, z(LANES - KR_LANE - C_ROPE_DIM)], axis=1)
    out = jnp.concatenate([gates, b_v, b_r, a_q, a_k, a_v, b_q, b_k, c_ckv, bz, c_cq, kr, z(LANES)], axis=1)
    assert out.shape[1] == NPAD
    return out.astype(BF16)


def _pad_heads_cols(w, real, take_lo, take_hi):
    kdim = w.shape[0]
    wh = w.reshape(kdim, C_HEADS, real)[:, :, take_lo:take_hi]
    wh = jnp.pad(wh, ((0, 0), (0, 0), (0, C_HEAD_PAD - (take_hi - take_lo))))
    return wh.reshape(kdim, C_PAD_WIDTH).astype(BF16)


def _gate_weights(w_a, b_a, row_off):
    wa = jnp.zeros((LANES, B_KW), F32).at[row_off:row_off + B_GATE_RANK].set(w_a)
    wa = wa.reshape(LANES, B_HEADS, B_KEY_DIM).transpose(1, 0, 2).astype(BF16)
    return wa, b_a.reshape(B_HEADS, 1, B_KEY_DIM).astype(F32)


def kernel(x_prompt, x_sample, c_prompt, c_sample, norm_mix, norm_moe, w_mod, b_mod, w_in, w_gla_af, b_gla_af,
           w_gla_ab, b_gla_ab, gla_norm, mla_q_norm, w_mla_uq, mla_kv_norm, w_mla_ukv, w_proj_a, w_proj_b, w_proj_c,
           w_out, w_router, b_router, w_exp_gate, w_exp_up, w_exp_down, final_norm):
    b1, s1, d = x_prompt.shape
    b2, s2, _ = x_sample.shape
    g = Groups(b1, s1, b2, s2)
    depth = w_in.shape[0]
    tm_in = min(1024, s1, s2)
    tm_prep = min(512, s1, s2)
    tm_proj = min(512, s1, s2)
    tm_moe = min(1024, s1, s2)

    x = jnp.concatenate([x_prompt.reshape(g.t1, d), x_sample.reshape(g.t2, d)], axis=0)
    c = jnp.concatenate([c_prompt, c_sample], axis=0)
    nbp = -(-g.nb // 8) * 8
    c_pad = jnp.pad(c, ((0, nbp - g.nb), (0, 0)))
    mod = modulation(c_pad, w_mod, b_mod)[:, :g.nb]

    smax = max(s1, s2)
    tabs_a = _rope_tables(smax, A_ROPE_DIM, A_HEAD_DIM, 0)
    tabs_c = _rope_tables(smax, C_ROPE_DIM, C_HEAD_PAD, KR_LANE)

    wr_hi = jnp.pad(w_router, ((0, 0), (0, LANES - N_EXPERTS)))
    wrh = wr_hi.astype(BF16)
    wrm = (wr_hi - wrh.astype(F32)).astype(BF16)
    fnorm = final_norm.reshape(1, d)
    br = jnp.pad(b_router.astype(F32), (0, LANES - N_EXPERTS)).reshape(1, LANES)

    for l in range(depth):
        sh1, sc1, gt1, sh2, sc2, gt2 = [m.reshape(g.nb, 1, d) for m in jnp.split(mod[l], N_MOD, axis=-1)]
        mul1 = norm_mix[l][None, None, :] * (1.0 + sc1)
        mul2 = norm_moe[l][None, None, :] * (1.0 + sc2)

        y, yd4, yd16 = in_projection(g, x, mul1, sh1, _pad_w_in(w_in[l]), tabs_a, tm_in)

        nat_cols = tuple((0, off // A_WIDTH) for off in (OFF_AQ, OFF_AK, OFF_AV))
        dil_cols = ((0, 0), (1, 0), (2, 0))
        o_list, lse_list = [], []
        for (_, dil), src in zip(A_PATTERNS, (y.reshape(1, g.t, NPAD), yd4, yd16)):
            o, lse = band_attention(g, src, dil, nat_cols if dil == 1 else dil_cols, NPAD // A_WIDTH if dil == 1 else 1)
            o_list.append(o)
            lse_list.append(lse)

        waf, baf = _gate_weights(w_gla_af[l], b_gla_af[l], 0)
        wab, bab = _gate_weights(w_gla_ab[l], b_gla_ab[l], B_GATE_RANK)
        o_back = gla_pass(g, y, wab, bab, reverse=True)
        yb = gla_pass(g, y, waf, baf, reverse=False, o_back=o_back, gain=gla_norm[l].reshape(1, B_VAL_DIM))

        wq = jnp.concatenate(
            [w_mla_uq[l].reshape(C_Q_RANK, C_HEADS, C_QK_DIM),
             jnp.zeros((C_Q_RANK, C_HEADS, C_HEAD_PAD - C_QK_DIM), F32)], axis=2
        ).reshape(C_Q_RANK, C_PAD_WIDTH).astype(BF16)
        wk = _pad_heads_cols(w_mla_ukv[l], C_NOPE_DIM + C_V_DIM, 0, C_NOPE_DIM)
        wv = _pad_heads_cols(w_mla_ukv[l], C_NOPE_DIM + C_V_DIM, C_NOPE_DIM, C_NOPE_DIM + C_V_DIM)
        qm, km, vm = mla_prep(g, y, mla_q_norm[l].reshape(1, C_Q_RANK), mla_kv_norm[l].reshape(1, C_KV_RANK),
                              wq, wk, wv, tabs_c, tm_prep)
        yc1 = mla_flash(qm, km, vm, 0, b1, s1, min(512, s1), min(512, s1))
        yc2 = mla_flash(qm, km, vm, g.t1, b2, s2, min(512, s2), min(512, s2))

        wpc = jnp.pad(w_proj_c[l].reshape(C_HEADS, C_V_DIM, d), ((0, 0), (0, C_HEAD_PAD - C_V_DIM), (0, 0)))
        wpc = wpc.reshape(C_PAD_WIDTH, d).astype(BF16)
        x, h2, comb = proj_merge(g, o_list, lse_list, yb, yc1, yc2, y, x, gt1, mul2, sh2,
                                 w_proj_a[l].astype(BF16), w_proj_b[l].astype(BF16), wpc, w_out[l].astype(BF16),
                                 wrh, wrm, br, tm_proj)

        x = moe_dense(g, h2, comb, w_exp_gate[l].astype(BF16), w_exp_up[l].astype(BF16),
                      w_exp_down[l].astype(BF16), x, gt2, fnorm, l == depth - 1, tm_moe)

    return x[:g.t1].reshape(b1, s1, d), x[g.t1:].reshape(b2, s2, d)
```

```python
import functools

import jax
import jax.numpy as jnp
from jax import lax
from jax.experimental import pallas as pl
from jax.experimental.pallas import tpu as pltpu

F32 = jnp.float32
BF16 = jnp.bfloat16

D_MODEL = 1024
DEPTH = 2
EPS = 1e-6
ROPE_THETA = 500000.0
NEG_INF = -1e30

A_HEADS = 8
A_HEAD_DIM = 64
A_ROPE_DIM = A_HEAD_DIM // 4
A_PATTERNS = ((128, 1), (512, 4), (2048, 16))
A_WIDTH = A_HEADS * A_HEAD_DIM
A_HALF = 64

B_HEADS = 4
B_KEY_DIM = 128
B_VAL_DIM = 256
B_GATE_RANK = 16
B_GATE_TAU = 16.0
B_CHUNK = 64
B_KW = B_HEADS * B_KEY_DIM
B_VW = B_HEADS * B_VAL_DIM

C_HEADS = 8
C_NOPE_DIM = 64
C_ROPE_DIM = 32
C_V_DIM = 64
C_QK_DIM = C_NOPE_DIM + C_ROPE_DIM
C_Q_RANK = 384
C_KV_RANK = 256
C_HEAD_PAD = 128
C_PAD_WIDTH = C_HEADS * C_HEAD_PAD

N_EXPERTS = 16
N_GROUPS = 4
EXPERTS_PER_GROUP = N_EXPERTS // N_GROUPS
TOP_K = 2
GROUP_SCORE_K = 2
D_EXPERT = 512
N_MOD = 6

IN_SIZES = (A_WIDTH, A_WIDTH, A_WIDTH, B_KW, B_KW, B_VW, B_VW, B_GATE_RANK, B_GATE_RANK,
            C_Q_RANK, C_KV_RANK, C_ROPE_DIM, 3 * D_MODEL)

LANES = 128
LOG2E = 1.4426950408889634

OFF_GATES = 0
OFF_BV = 3072
OFF_BR = 4096
OFF_AQ = 5120
OFF_AK = 5632
OFF_AV = 6144
OFF_BQ = 6656
OFF_BK = 7168
OFF_CKV = 7680
OFF_BZ = 7936
OFF_CQ = 8064
OFF_KR = 8448
NPAD = 8704
KR_LANE = 64

VMEM_LIMIT = 56 * 1024 * 1024


def _cparams(sem):
    return pltpu.CompilerParams(dimension_semantics=sem, vmem_limit_bytes=VMEM_LIMIT)


class Groups:
    def __init__(self, b1, s1, b2, s2):
        self.b1, self.s1, self.b2, self.s2 = b1, s1, b2, s2
        self.t1, self.t2 = b1 * s1, b2 * s2
        self.t = self.t1 + self.t2
        self.nb = b1 + b2

    def batch_of_block(self, i, tm):
        n1 = self.t1 // tm
        return jnp.where(i < n1, i // (self.s1 // tm), self.b1 + (i - n1) // (self.s2 // tm))

    def pos_block(self, i, tm):
        n1 = self.t1 // tm
        return jnp.where(i < n1, i % (self.s1 // tm), (i - n1) % (self.s2 // tm))


def _dot(a, b):
    return jnp.dot(a, b, preferred_element_type=F32)


def _dot_nt(a, b):
    return lax.dot_general(a, b, (((1,), (1,)), ((), ())), preferred_element_type=F32)


def _dot_tn(a, b):
    return lax.dot_general(a, b, (((0,), (0,)), ((), ())), preferred_element_type=F32)


def _split3(a):
    hi = a.astype(BF16)
    r1 = a - hi.astype(F32)
    mid = r1.astype(BF16)
    lo = (r1 - mid.astype(F32)).astype(BF16)
    return hi, mid, lo


def _rope(y, ct, s1, s2, half):
    return y * ct + pltpu.roll(y, LANES - half, 1) * s1 + pltpu.roll(y, half, 1) * s2


def _sigmoid(x):
    return 0.5 * jnp.tanh(0.5 * x) + 0.5


def _mod_kernel(c_ref, w_ref, b_ref, o_ref):
    c = c_ref[...]
    ca = c * _sigmoid(c)
    ch, cm, _ = _split3(ca)
    wh, wm, _ = _split3(w_ref[0])
    o_ref[0] = _dot(ch, wh) + _dot(cm, wh) + _dot(ch, wm) + b_ref[0]


def modulation(c_pad, w_mod, b_mod):
    nbp, d = c_pad.shape
    depth, _, n = w_mod.shape
    tn = 512
    return pl.pallas_call(
        _mod_kernel,
        grid=(depth, n // tn),
        in_specs=[pl.BlockSpec((nbp, d), lambda l, j: (0, 0)),
                  pl.BlockSpec((1, d, tn), lambda l, j: (l, 0, j)),
                  pl.BlockSpec((1, 1, tn), lambda l, j: (l, 0, j))],
        out_specs=pl.BlockSpec((1, nbp, tn), lambda l, j: (l, 0, j)),
        out_shape=jax.ShapeDtypeStruct((depth, nbp, n), F32),
        compiler_params=_cparams(("arbitrary", "arbitrary")),
        name="modulation",
    )(c_pad, w_mod, b_mod.reshape(depth, 1, n))


def _inproj_kernel(x_ref, mul_ref, sh_ref, w_ref, ct_ref, s1_ref, s2_ref, o_ref, od4_ref, od16_ref,
                   h_scr, y_scr, y4_scr, *, tm, tn, jq, jv):
    j = pl.program_id(1)

    @pl.when(j == 0)
    def _():
        x = x_ref[...]
        ms = jnp.mean(x * x, axis=-1, keepdims=True)
        h = x * lax.rsqrt(ms + EPS) * mul_ref[0] + sh_ref[0]
        h_scr[...] = h.astype(BF16)

    is_attn = (j >= jq) & (j <= jv)

    @pl.when(jnp.logical_not(is_attn))
    def _():
        o_ref[...] = _dot(h_scr[...], w_ref[...]).astype(BF16)

    @pl.when(is_attn)
    def _():
        y = _dot(h_scr[...], w_ref[...])
        ncol = tn // LANES

        @pl.when(j < jv)
        def _():
            ct, s1, s2 = ct_ref[...], s1_ref[...], s2_ref[...]
            for c in range(ncol):
                y_scr[c] = _rope(y[:, c * LANES:(c + 1) * LANES], ct, s1, s2, A_ROPE_DIM // 2)

        @pl.when(j == jv)
        def _():
            for c in range(ncol):
                y_scr[c] = y[:, c * LANES:(c + 1) * LANES]

        n4, n16 = tm // 4, tm // 16
        for c in range(ncol):
            cs = slice(c * LANES, (c + 1) * LANES)
            o_ref[:, cs] = y_scr[c].astype(BF16)
            for r in range(4):
                y4 = y_scr[c, pl.ds(r, n4, stride=4), :]
                y4_scr[c, r * n4:(r + 1) * n4, :] = y4
                od4_ref[:, r * tn + c * LANES:r * tn + (c + 1) * LANES] = y4.astype(BF16)
            for r in range(4):
                for r2 in range(4):
                    col = (r + 4 * r2) * tn + c * LANES
                    od16_ref[:, col:col + LANES] = y4_scr[c, pl.ds(r * n4 + r2, n16, stride=4), :].astype(BF16)


def in_projection(g, x, mul, shift, w_pad, tabs, tm):
    t, d = x.shape
    tn = A_WIDTH
    ct, s1, s2 = tabs
    jq, jv = OFF_AQ // tn, OFF_AV // tn
    bmap = lambda i, j: (g.batch_of_block(i, tm), 0, 0)
    pmap = lambda i, j: (g.pos_block(i, tm), 0)
    dmap = lambda i, j: (jnp.clip(j - jq, 0, jv - jq), i, 0)
    kern = functools.partial(_inproj_kernel, tm=tm, tn=tn, jq=jq, jv=jv)
    return pl.pallas_call(
        kern,
        grid=(t // tm, NPAD // tn),
        in_specs=[pl.BlockSpec((tm, d), lambda i, j: (i, 0)),
                  pl.BlockSpec((1, 1, d), bmap),
                  pl.BlockSpec((1, 1, d), bmap),
                  pl.BlockSpec((d, tn), lambda i, j: (0, j)),
                  pl.BlockSpec((tm, LANES), pmap),
                  pl.BlockSpec((tm, LANES), pmap),
                  pl.BlockSpec((tm, LANES), pmap)],
        out_specs=[pl.BlockSpec((tm, tn), lambda i, j: (i, j)),
                   pl.BlockSpec((None, tm // 4, 4 * tn), dmap),
                   pl.BlockSpec((None, tm // 16, 16 * tn), dmap)],
        out_shape=[jax.ShapeDtypeStruct((t, NPAD), BF16),
                   jax.ShapeDtypeStruct((3, t // 4, 4 * tn), BF16),
                   jax.ShapeDtypeStruct((3, t // 16, 16 * tn), BF16)],
        scratch_shapes=[pltpu.VMEM((tm, d), BF16), pltpu.VMEM((tn // LANES, tm, LANES), F32),
                        pltpu.VMEM((tn // LANES, tm, LANES), F32)],
        compiler_params=_cparams(("arbitrary", "arbitrary")),
        name="in_projection",
    )(x, mul, shift, w_pad, ct, s1, s2)


def _band_kernel(q_ref, kp_ref, km_ref, kn_ref, vp_ref, vm_ref, vn_ref, o_ref, lse_ref, *, tq, rows1, l1, l2):
    i = pl.program_id(1)
    row0 = i * tq
    in1 = row0 < rows1
    seq_len = jnp.where(in1, l1, l2)
    pos0 = jnp.where(in1, row0 % l1, (row0 - rows1) % l2)
    tk = tq + 2 * A_HALF
    q = q_ref[...]
    k = jnp.concatenate([kp_ref[...], km_ref[...], kn_ref[...]], axis=0)
    v = jnp.concatenate([vp_ref[...], vm_ref[...], vn_ref[...]], axis=0)
    qi = lax.broadcasted_iota(jnp.int32, (tq, tk), 0)
    kj = lax.broadcasted_iota(jnp.int32, (tq, tk), 1)
    rel = kj - A_HALF - qi
    kpos = pos0 - A_HALF + kj
    valid = (jnp.abs(rel) <= A_HALF) & (kpos >= 0) & (kpos < seq_len)
    valid2 = jnp.concatenate([valid, valid], axis=0)
    lo = lax.broadcasted_iota(jnp.int32, (tq, LANES), 1) < A_HEAD_DIM
    scale = A_HEAD_DIM ** -0.5
    for p in range(A_WIDTH // LANES):
        sl = slice(p * LANES, (p + 1) * LANES)
        qp, kp, vp = q[:, sl] * scale, k[:, sl], v[:, sl]
        zero = jnp.zeros_like(qp)
        qm = jnp.concatenate([jnp.where(lo, qp, zero), jnp.where(lo, zero, qp)], axis=0)
        s = jnp.where(valid2, _dot_nt(qm, kp), NEG_INF)
        m = jnp.max(s, axis=1, keepdims=True)
        e = jnp.exp(s - m)
        l = jnp.sum(e, axis=1, keepdims=True)
        o2 = _dot(e.astype(BF16), vp) / l
        lse2 = m + jnp.log(l)
        o_ref[:, sl] = jnp.where(lo, o2[:tq], o2[tq:]).astype(BF16)
        lse_ref[:, sl] = jnp.where(lo, lse2[:tq], lse2[tq:])


def band_attention(g, qkv, dil, cols, cpb):
    rows = g.t // dil
    tq = 128
    sub = tq // A_HALF
    nsub = rows // A_HALF

    def main(c):
        ld, col = cols[c]
        return pl.BlockSpec((None, tq, A_WIDTH), lambda r, i: (ld, i, r * cpb + col))

    def prev(c):
        ld, col = cols[c]
        return pl.BlockSpec((None, A_HALF, A_WIDTH), lambda r, i: (ld, jnp.maximum(i * sub - 1, 0), r * cpb + col))

    def nxt(c):
        ld, col = cols[c]
        return pl.BlockSpec((None, A_HALF, A_WIDTH),
                            lambda r, i: (ld, jnp.minimum((i + 1) * sub, nsub - 1), r * cpb + col))

    kern = functools.partial(_band_kernel, tq=tq, rows1=g.t1 // dil, l1=g.s1 // dil, l2=g.s2 // dil)
    return pl.pallas_call(
        kern,
        grid=(dil, rows // tq),
        in_specs=[main(0), prev(1), main(1), nxt(1), prev(2), main(2), nxt(2)],
        out_specs=[pl.BlockSpec((tq, A_WIDTH), lambda r, i: (i, r)),
                   pl.BlockSpec((tq, A_WIDTH), lambda r, i: (i, r))],
        out_shape=[jax.ShapeDtypeStruct((rows, dil * A_WIDTH), BF16),
                   jax.ShapeDtypeStruct((rows, dil * A_WIDTH), F32)],
        compiler_params=_cparams(("arbitrary", "arbitrary")),
        name="band_attention_d%d" % dil,
    )(qkv, qkv, qkv, qkv, qkv, qkv, qkv)


def _gla_kernel(*refs, reverse, final, tc, nblk, t1, s1, s2):
    if final:
        q_ref, k_ref, v_ref, z_ref, wa_ref, ba_ref, tri_ref, ob_ref, r_ref, gain_ref, o_ref, st_scr = refs
    else:
        q_ref, k_ref, v_ref, z_ref, wa_ref, ba_ref, tri_ref, o_ref, st_scr = refs
    i = pl.program_id(0)
    blk = (nblk - 1 - i) if reverse else i
    row0 = blk * tc
    in1 = row0 < t1
    pos0 = jnp.where(in1, row0 % s1, (row0 - t1) % s2)
    slen = jnp.where(in1, s1, s2)
    start = (pos0 + tc == slen) if reverse else (pos0 == 0)

    @pl.when(start)
    def _():
        st_scr[...] = jnp.zeros_like(st_scr)

    zl = _dot(z_ref[...], wa_ref[...]) + ba_ref[...]
    la = (jnp.minimum(zl, 0.0) - jnp.log(1.0 + jnp.exp(-jnp.abs(zl)))) * (1.0 / B_GATE_TAU)
    hi, mid, _ = _split3(la)
    tri = tri_ref[...]
    bc_all = _dot(tri, hi) + _dot(tri, mid)

    qi = lax.broadcasted_iota(jnp.int32, (B_CHUNK, B_CHUNK), 0)
    si = lax.broadcasted_iota(jnp.int32, (B_CHUNK, B_CHUNK), 1)
    mask = (si > qi) if reverse else (si <= qi)
    nch = tc // B_CHUNK
    q_all = q_ref[...].astype(F32) * (B_KEY_DIM ** -0.5)
    k_all = k_ref[...].astype(F32)
    q_dec = (q_all * jnp.exp(bc_all)).astype(BF16)
    k_inv = (k_all * jnp.exp(-bc_all)).astype(BF16)
    st = [st_scr[h] for h in range(B_HEADS)]
    for c in (range(nch - 1, -1, -1) if reverse else range(nch)):
        sl = slice(c * B_CHUNK, (c + 1) * B_CHUNK)
        edge = c * B_CHUNK if reverse else (c + 1) * B_CHUNK - 1
        tot = bc_all[edge:edge + 1]
        k_end = (k_all[sl] * jnp.exp(tot - bc_all[sl])).astype(BF16)
        dec = jnp.exp(tot)
        for h in range(B_HEADS):
            ks = slice(h * B_KEY_DIM, (h + 1) * B_KEY_DIM)
            vs = slice(h * B_VAL_DIM, (h + 1) * B_VAL_DIM)
            v = v_ref[sl, vs]
            att = jnp.where(mask, _dot_nt(q_dec[sl, ks], k_inv[sl, ks]), 0.0).astype(BF16)
            o = _dot(att, v) + _dot_nt(q_dec[sl, ks], st[h].astype(BF16))
            st[h] = st[h] * dec[:, ks] + _dot_tn(v, k_end[:, ks])
            if final:
                o = o + ob_ref[sl, vs]
                on = o * lax.rsqrt(jnp.mean(o * o, axis=-1, keepdims=True) + EPS) * gain_ref[...]
                r = r_ref[sl, vs].astype(F32)
                o_ref[sl, vs] = (on * (r * _sigmoid(r))).astype(BF16)
            else:
                o_ref[sl, vs] = o
    for h in range(B_HEADS):
        st_scr[h] = st[h]


def gla_pass(g, y, wa, ba, reverse, o_back=None, gain=None):
    t = g.t
    tc = 256
    nblk = t // tc
    final = o_back is not None
    rowmap = (lambda i: nblk - 1 - i) if reverse else (lambda i: i)
    row = lambda w, off: pl.BlockSpec((tc, w), lambda i: (rowmap(i), off // w))
    full = lambda a: pl.BlockSpec(a.shape, lambda i: (0,) * a.ndim)
    ri = jnp.arange(tc)[:, None]
    ci = jnp.arange(tc)[None, :]
    tri = ((ri // B_CHUNK == ci // B_CHUNK) & ((ci >= ri) if reverse else (ci <= ri))).astype(BF16)
    in_specs = [row(B_KW, OFF_BQ), row(B_KW, OFF_BK), row(B_VW, OFF_BV), row(LANES, OFF_BZ),
                full(wa), full(ba), full(tri)]
    args = [y, y, y, y, wa, ba, tri]
    if final:
        in_specs += [row(B_VW, 0), row(B_VW, OFF_BR), full(gain)]
        args += [o_back, y, gain]
    kern = functools.partial(_gla_kernel, reverse=reverse, final=final, tc=tc, nblk=nblk, t1=g.t1, s1=g.s1, s2=g.s2)
    return pl.pallas_call(
        kern,
        grid=(nblk,),
        in_specs=in_specs,
        out_specs=row(B_VW, 0),
        out_shape=jax.ShapeDtypeStruct((t, B_VW), BF16 if final else F32),
        scratch_shapes=[pltpu.VMEM((B_HEADS, B_VAL_DIM, B_KEY_DIM), F32)],
        compiler_params=_cparams(("arbitrary",)),
        name="gla_forward_final" if final else "gla_backward",
    )(*args)


def _mla_prep_kernel(ckv_ref, cq_ref, kr_ref, qn_ref, kvn_ref, wq_ref, wk_ref, wv_ref, ct_ref, s1_ref, s2_ref,
                     q_out, k_out, v_out):
    def norm(ref, gain_ref):
        xf = ref[...].astype(F32)
        return (xf * lax.rsqrt(jnp.mean(xf * xf, axis=-1, keepdims=True) + EPS) * gain_ref[...]).astype(BF16)

    ckv_n = norm(ckv_ref, kvn_ref)
    cq_n = norm(cq_ref, qn_ref)
    q = _dot(cq_n, wq_ref[...])
    kn = _dot(ckv_n, wk_ref[...])
    v = _dot(ckv_n, wv_ref[...])
    ct, s1, s2 = ct_ref[...], s1_ref[...], s2_ref[...]
    half = C_ROPE_DIM // 2
    kr_rot = _rope(kr_ref[...].astype(F32), ct, s1, s2, half)
    ones_lane = lax.broadcasted_iota(jnp.int32, kr_rot.shape, 1) == C_V_DIM
    scale = C_QK_DIM ** -0.5 * LOG2E
    for h in range(C_HEADS):
        sl = slice(h * C_HEAD_PAD, (h + 1) * C_HEAD_PAD)
        q_out[:, sl] = (_rope(q[:, sl], ct, s1, s2, half) * scale).astype(BF16)
        k_out[:, sl] = (kn[:, sl] + kr_rot).astype(BF16)
        v_out[:, sl] = jnp.where(ones_lane, 1.0, v[:, sl]).astype(BF16)


def mla_prep(g, y, qn, kvn, wq, wk, wv, tabs, tm):
    t = g.t
    ct, s1, s2 = tabs
    pmap = lambda i: (g.pos_block(i, tm), 0)
    full = lambda a: pl.BlockSpec(a.shape, lambda i: (0,) * a.ndim)
    out = jax.ShapeDtypeStruct((t, C_PAD_WIDTH), BF16)
    ospec = pl.BlockSpec((tm, C_PAD_WIDTH), lambda i: (i, 0))
    return pl.pallas_call(
        _mla_prep_kernel,
        grid=(t // tm,),
        in_specs=[pl.BlockSpec((tm, C_KV_RANK), lambda i: (i, OFF_CKV // C_KV_RANK)),
                  pl.BlockSpec((tm, C_Q_RANK), lambda i: (i, OFF_CQ // C_Q_RANK)),
                  pl.BlockSpec((tm, LANES), lambda i: (i, OFF_KR // LANES)),
                  full(qn), full(kvn), full(wq), full(wk), full(wv),
                  pl.BlockSpec((tm, LANES), pmap), pl.BlockSpec((tm, LANES), pmap), pl.BlockSpec((tm, LANES), pmap)],
        out_specs=[ospec, ospec, ospec],
        out_shape=[out, out, out],
        compiler_params=_cparams(("arbitrary",)),
        name="mla_prep",
    )(y, y, y, qn, kvn, wq, wk, wv, ct, s1, s2)


def _flash_kernel(q_ref, k_ref, v_ref, o_ref, s0_scr, s1_scr, p_scr, m_scr, a_scr, acc_scr, *, tk, nk, rc, unroll):
    tq = q_ref.shape[0]

    def logits(t, scr):
        off = pl.multiple_of(t * tk, tk)
        scr[...] = _dot_nt(q_ref[...], k_ref[pl.ds(off, tk), :])

    def softmax_pv(t, scr):
        off = pl.multiple_of(t * tk, tk)
        for r in range(tq // rc):
            rows = slice(r * rc, (r + 1) * rc)
            s = scr[rows, :]
            m_old = m_scr[rows, :]
            m_new = jnp.maximum(m_old, jnp.max(s, axis=1, keepdims=True))
            p_scr[rows, :] = jnp.exp2(s - jnp.concatenate([m_new] * (tk // LANES), axis=1)).astype(BF16)
            a_scr[rows, :] = jnp.exp2(m_old - m_new)
            m_scr[rows, :] = m_new
        acc_scr[...] = acc_scr[...] * a_scr[...] + _dot(p_scr[...], v_ref[pl.ds(off, tk), :])

    bufs = (s0_scr, s1_scr)

    def group(t0, last):
        for u in range(unroll):
            if not (last and u == unroll - 1):
                logits(t0 + u + 1, bufs[(u + 1) % 2])
            softmax_pv(t0 + u, bufs[u % 2])

    def body(jj, carry):
        group(unroll * jj, False)
        return carry

    m_scr[...] = jnp.full(m_scr.shape, NEG_INF, F32)
    acc_scr[...] = jnp.zeros(acc_scr.shape, F32)
    logits(0, s0_scr)
    lax.fori_loop(0, nk // unroll - 1, body, 0)
    group(nk - unroll, True)
    acc = acc_scr[...]
    o_ref[...] = (acc / acc[:, C_V_DIM:C_V_DIM + 1]).astype(BF16)


def mla_flash(q, k, v, row_off, nseq, s, tq, tk):
    assert row_off % s == 0 and row_off % tq == 0
    qb0 = row_off // tq
    sb0 = row_off // s
    nq = s // tq
    nk = s // tk
    unroll = 4 if nk % 4 == 0 else 2
    assert nk % unroll == 0
    kern = functools.partial(_flash_kernel, tk=tk, nk=nk, rc=64, unroll=unroll)
    return pl.pallas_call(
        kern,
        grid=(nseq, C_HEADS, nq),
        scratch_shapes=[pltpu.VMEM((tq, tk), F32), pltpu.VMEM((tq, tk), F32), pltpu.VMEM((tq, tk), BF16),
                        pltpu.VMEM((tq, LANES), F32), pltpu.VMEM((tq, LANES), F32),
                        pltpu.VMEM((tq, C_HEAD_PAD), F32)],
        in_specs=[pl.BlockSpec((tq, C_HEAD_PAD), lambda b, h, i: (qb0 + b * nq + i, h)),
                  pl.BlockSpec((s, C_HEAD_PAD), lambda b, h, i: (sb0 + b, h)),
                  pl.BlockSpec((s, C_HEAD_PAD), lambda b, h, i: (sb0 + b, h))],
        out_specs=pl.BlockSpec((tq, C_HEAD_PAD), lambda b, h, i: (b * nq + i, h)),
        out_shape=jax.ShapeDtypeStruct((nseq * s, C_PAD_WIDTH), BF16),
        compiler_params=_cparams(("arbitrary", "arbitrary", "arbitrary")),
        name="mla_flash_s%d" % s,
    )(q, k, v)


def _route(s, bias):
    lane = lax.broadcasted_iota(jnp.int32, s.shape, 1)
    pos = lane % EXPERTS_PER_GROUP
    grp = lane // EXPERTS_PER_GROUP
    sel = s + bias
    rank = jnp.zeros(s.shape, jnp.int32)
    for k in range(1, EXPERTS_PER_GROUP):
        below = pltpu.roll(sel, k, 1)
        above = pltpu.roll(sel, LANES - k, 1)
        rank += jnp.where((pos >= k) & (below >= sel), 1, 0)
        rank += jnp.where((pos + k < EXPERTS_PER_GROUP) & (above > sel), 1, 0)
    top = rank < GROUP_SCORE_K
    contrib = jnp.where(top, sel, 0.0)
    score = contrib
    for k in range(1, EXPERTS_PER_GROUP):
        score += jnp.where(pos >= k, pltpu.roll(contrib, k, 1), 0.0)
        score += jnp.where(pos + k < EXPERTS_PER_GROUP, pltpu.roll(contrib, LANES - k, 1), 0.0)
    best = lane < N_EXPERTS
    for k in range(1, N_GROUPS):
        earlier = pltpu.roll(score, k * EXPERTS_PER_GROUP, 1)
        later = pltpu.roll(score, LANES - k * EXPERTS_PER_GROUP, 1)
        best &= jnp.logical_not((grp >= k) & (earlier >= score))
        best &= jnp.logical_not((grp + k < N_GROUPS) & (later > score))
    chosen = best & (rank < TOP_K)
    total = jnp.sum(jnp.where(chosen, s, 0.0), axis=1, keepdims=True)
    return jnp.where(chosen, s / total, 0.0)


def _proj_kernel(o1_ref, o2_ref, o3_ref, l1_ref, l2_ref, l3_ref, yb_ref, yc1_ref, yc2_ref, ga_ref, gb_ref, gc_ref,
                 x_ref, g1_ref, mul2_ref, sh2_ref, wpa_ref, wpb_ref, wpc_ref, wout_ref, wrh_ref, wrm_ref, br_ref,
                 x_out, h_out, comb_out, o2_scr, l2_scr, o3_scr, l3_scr, *, tm, n1):
    for dil, o_ref, l_ref, o_scr, l_scr in ((A_PATTERNS[1][1], o2_ref, l2_ref, o2_scr, l2_scr),
                                           (A_PATTERNS[2][1], o3_ref, l3_ref, o3_scr, l3_scr)):
        for r in range(dil):
            for c in range(A_WIDTH // LANES):
                sl = slice(r * A_WIDTH + c * LANES, r * A_WIDTH + (c + 1) * LANES)
                o_scr[c, pl.ds(r, tm // dil, stride=dil), :] = o_ref[:, sl].astype(F32)
                l_scr[c, pl.ds(r, tm // dil, stride=dil), :] = l_ref[:, sl]
    slabs = lambda scr: jnp.concatenate([scr[c] for c in range(A_WIDTH // LANES)], axis=1)
    l1, l2, l3 = l1_ref[...], slabs(l2_scr), slabs(l3_scr)
    m = jnp.maximum(jnp.maximum(l1, l2), l3)
    e1, e2, e3 = jnp.exp(l1 - m), jnp.exp(l2 - m), jnp.exp(l3 - m)
    ya = (e1 * o1_ref[...].astype(F32) + e2 * slabs(o2_scr) + e3 * slabs(o3_scr)) / (e1 + e2 + e3)
    sig = lambda ref: _sigmoid(ref[...].astype(F32))
    yc = jnp.where(pl.program_id(0) < n1, yc1_ref[...], yc2_ref[...])
    merged = (sig(ga_ref) * _dot(ya.astype(BF16), wpa_ref[...])
              + sig(gb_ref) * _dot(yb_ref[...], wpb_ref[...])
              + sig(gc_ref) * _dot(yc, wpc_ref[...]))
    out = _dot(merged.astype(BF16), wout_ref[...])
    x = x_ref[...] + g1_ref[0] * out
    x_out[...] = x
    h = x * lax.rsqrt(jnp.mean(x * x, axis=-1, keepdims=True) + EPS) * mul2_ref[0] + sh2_ref[0]
    h_out[...] = h.astype(BF16)
    hh, hm, _ = _split3(h)
    s = _sigmoid(_dot(hh, wrh_ref[...]) + _dot(hm, wrh_ref[...]) + _dot(hh, wrm_ref[...]))
    comb_out[...] = _route(s, br_ref[...])


def proj_merge(g, o_list, lse_list, yb, yc1, yc2, y, x, gate1, mul2, sh2, wpa, wpb, wpc, wout, wrh, wrm, br, tm):
    t, d = x.shape
    n1 = g.t1 // tm
    n2 = g.t2 // tm
    bmap = lambda i: (g.batch_of_block(i, tm), 0, 0)
    row = lambda w, col=0: pl.BlockSpec((tm, w), lambda i: (i, col))
    dil_row = lambda dil: pl.BlockSpec((tm // dil, dil * A_WIDTH), lambda i: (i, 0))
    full = lambda a: pl.BlockSpec(a.shape, lambda i: (0,) * a.ndim)
    gcol = OFF_GATES // d
    d2, d3 = A_PATTERNS[1][1], A_PATTERNS[2][1]
    attn_specs = [row(A_WIDTH), dil_row(d2), dil_row(d3)]
    return pl.pallas_call(
        functools.partial(_proj_kernel, tm=tm, n1=n1),
        grid=(t // tm,),
        in_specs=attn_specs + attn_specs + [
            row(B_VW),
            pl.BlockSpec((tm, C_PAD_WIDTH), lambda i: (jnp.minimum(i, n1 - 1), 0)),
            pl.BlockSpec((tm, C_PAD_WIDTH), lambda i: (jnp.clip(i - n1, 0, n2 - 1), 0)),
            row(d, gcol), row(d, gcol + 1), row(d, gcol + 2),
            row(d), pl.BlockSpec((1, 1, d), bmap), pl.BlockSpec((1, 1, d), bmap), pl.BlockSpec((1, 1, d), bmap),
            full(wpa), full(wpb), full(wpc), full(wout), full(wrh), full(wrm), full(br)],
        out_specs=[row(d), row(d), row(LANES)],
        out_shape=[jax.ShapeDtypeStruct((t, d), F32), jax.ShapeDtypeStruct((t, d), BF16),
                   jax.ShapeDtypeStruct((t, LANES), F32)],
        scratch_shapes=[pltpu.VMEM((A_WIDTH // LANES, tm, LANES), F32)] * 4,
        compiler_params=_cparams(("arbitrary",)),
        name="proj_merge_route",
    )(*o_list, *lse_list, yb, yc1, yc2, y, y, y, x, gate1, mul2, sh2, wpa, wpb, wpc, wout, wrh, wrm, br)


def _moe_kernel(h_ref, comb_ref, wg_ref, wu_ref, wd_ref, x_ref, g2_ref, fn_ref, o_ref, acc_scr, *, final):
    e = pl.program_id(1)

    @pl.when(e == 0)
    def _():
        acc_scr[...] = jnp.zeros_like(acc_scr)

    h = h_ref[...]
    gate = _dot(h, wg_ref[0])
    a = gate * _sigmoid(gate) * _dot(h, wu_ref[0])
    yv = _dot(a.astype(BF16), wd_ref[0])
    comb = comb_ref[...]
    lane = lax.broadcasted_iota(jnp.int32, comb.shape, 1)
    ce = jnp.sum(jnp.where(lane == e, comb, 0.0), axis=1, keepdims=True)
    acc_scr[...] += ce * yv

    @pl.when(e == N_EXPERTS - 1)
    def _():
        x = x_ref[...] + g2_ref[0] * acc_scr[...]
        if final:
            x = x * lax.rsqrt(jnp.mean(x * x, axis=-1, keepdims=True) + EPS) * fn_ref[...]
        o_ref[...] = x


def moe_dense(g, h, comb, wg, wu, wd, x, gate2, fnorm, final, tm):
    t, d = x.shape
    bmap = lambda i, e: (g.batch_of_block(i, tm), 0, 0)
    kern = functools.partial(_moe_kernel, final=final)
    return pl.pallas_call(
        kern,
        grid=(t // tm, N_EXPERTS),
        in_specs=[pl.BlockSpec((tm, d), lambda i, e: (i, 0)),
                  pl.BlockSpec((tm, LANES), lambda i, e: (i, 0)),
                  pl.BlockSpec((1, d, D_EXPERT), lambda i, e: (e, 0, 0)),
                  pl.BlockSpec((1, d, D_EXPERT), lambda i, e: (e, 0, 0)),
                  pl.BlockSpec((1, D_EXPERT, d), lambda i, e: (e, 0, 0)),
                  pl.BlockSpec((tm, d), lambda i, e: (i, 0)),
                  pl.BlockSpec((1, 1, d), bmap),
                  pl.BlockSpec((1, d), lambda i, e: (0, 0))],
        out_specs=pl.BlockSpec((tm, d), lambda i, e: (i, 0)),
        out_shape=jax.ShapeDtypeStruct((t, d), F32),
        scratch_shapes=[pltpu.VMEM((tm, d), F32)],
        compiler_params=_cparams(("arbitrary", "arbitrary")),
        name="moe_dense",
    )(h, comb, wg, wu, wd, x, gate2, fnorm)


def _rope_tables(smax, rot_dim, period, lane_off):
    half = rot_dim // 2
    inv_freq = ROPE_THETA ** (-jnp.arange(half, dtype=F32) / half)
    ang = jnp.arange(smax).astype(F32)[:, None] * inv_freq[None, :]
    cos, sin = jnp.cos(ang), jnp.sin(ang)
    gl = (jnp.arange(LANES) % period) - lane_off
    first = (gl >= 0) & (gl < half)
    second = (gl >= half) & (gl < rot_dim)
    j = jnp.clip(jnp.where(first, gl, gl - half), 0, half - 1)
    cl, sn = cos[:, j], sin[:, j]
    ct = jnp.where((first | second)[None, :], cl, 1.0)
    s1 = jnp.where(first[None, :], -sn, 0.0)
    s2 = jnp.where(second[None, :], sn, 0.0)
    return ct, s1, s2


def _pad_w_in(w):
    d = w.shape[0]
    parts, off = [], 0
    for sz in IN_SIZES:
        parts.append(w[:, off:off + sz])
        off += sz
    a_q, a_k, a_v, b_q, b_k, b_v, b_r, b_zf, b_zb, c_cq, c_ckv, c_kr, gates = parts
    z = lambda n: jnp.zeros((d, n), w.dtype)
    bz = jnp.concatenate([b_zf, b_zb, z(LANES - 2 * B_GATE_RANK)], axis=1)
    kr = jnp.concatenate([z(KR_LANE), c_kr, z(LANES - KR_LANE - C_ROPE_DIM)], axis=1)
    out = jnp.concatenate([gates, b_v, b_r, a_q, a_k, a_v, b_q, b_k, c_ckv, bz, c_cq, kr, z(LANES)], axis=1)
    assert out.shape[1] == NPAD
    return out.astype(BF16)


def _pad_heads_cols(w, real, take_lo, take_hi):
    kdim = w.shape[0]
    wh = w.reshape(kdim, C_HEADS, real)[:, :, take_lo:take_hi]
    wh = jnp.pad(wh, ((0, 0), (0, 0), (0, C_HEAD_PAD - (take_hi - take_lo))))
    return wh.reshape(kdim, C_PAD_WIDTH).astype(BF16)


def _gate_weights(w_a, b_a, row_off):
    wa = jnp.zeros((LANES, B_KW), F32).at[row_off:row_off + B_GATE_RANK].set(w_a)
    return wa.astype(BF16), b_a.reshape(1, B_KW).astype(F32)


def kernel(x_prompt, x_sample, c_prompt, c_sample, norm_mix, norm_moe, w_mod, b_mod, w_in, w_gla_af, b_gla_af,
           w_gla_ab, b_gla_ab, gla_norm, mla_q_norm, w_mla_uq, mla_kv_norm, w_mla_ukv, w_proj_a, w_proj_b, w_proj_c,
           w_out, w_router, b_router, w_exp_gate, w_exp_up, w_exp_down, final_norm):
    b1, s1, d = x_prompt.shape
    b2, s2, _ = x_sample.shape
    g = Groups(b1, s1, b2, s2)
    depth = w_in.shape[0]
    tm_in = min(1024, s1, s2)
    tm_prep = min(512, s1, s2)
    tm_proj = min(512, s1, s2)
    tm_moe = min(1024, s1, s2)

    x = jnp.concatenate([x_prompt.reshape(g.t1, d), x_sample.reshape(g.t2, d)], axis=0)
    c = jnp.concatenate([c_prompt, c_sample], axis=0)
    nbp = -(-g.nb // 8) * 8
    c_pad = jnp.pad(c, ((0, nbp - g.nb), (0, 0)))
    mod = modulation(c_pad, w_mod, b_mod)[:, :g.nb]

    smax = max(s1, s2)
    tabs_a = _rope_tables(smax, A_ROPE_DIM, A_HEAD_DIM, 0)
    tabs_c = _rope_tables(smax, C_ROPE_DIM, C_HEAD_PAD, KR_LANE)

    wr_hi = jnp.pad(w_router, ((0, 0), (0, LANES - N_EXPERTS)))
    wrh = wr_hi.astype(BF16)
    wrm = (wr_hi - wrh.astype(F32)).astype(BF16)
    fnorm = final_norm.reshape(1, d)
    br = jnp.pad(b_router.astype(F32), (0, LANES - N_EXPERTS)).reshape(1, LANES)

    for l in range(depth):
        sh1, sc1, gt1, sh2, sc2, gt2 = [m.reshape(g.nb, 1, d) for m in jnp.split(mod[l], N_MOD, axis=-1)]
        mul1 = norm_mix[l][None, None, :] * (1.0 + sc1)
        mul2 = norm_moe[l][None, None, :] * (1.0 + sc2)

        y, yd4, yd16 = in_projection(g, x, mul1, sh1, _pad_w_in(w_in[l]), tabs_a, tm_in)

        nat_cols = tuple((0, off // A_WIDTH) for off in (OFF_AQ, OFF_AK, OFF_AV))
        dil_cols = ((0, 0), (1, 0), (2, 0))
        o_list, lse_list = [], []
        for (_, dil), src in zip(A_PATTERNS, (y.reshape(1, g.t, NPAD), yd4, yd16)):
            o, lse = band_attention(g, src, dil, nat_cols if dil == 1 else dil_cols, NPAD // A_WIDTH if dil == 1 else 1)
            o_list.append(o)
            lse_list.append(lse)

        waf, baf = _gate_weights(w_gla_af[l], b_gla_af[l], 0)
        wab, bab = _gate_weights(w_gla_ab[l], b_gla_ab[l], B_GATE_RANK)
        o_back = gla_pass(g, y, wab, bab, reverse=True)
        yb = gla_pass(g, y, waf, baf, reverse=False, o_back=o_back, gain=gla_norm[l].reshape(1, B_VAL_DIM))

        wq = jnp.concatenate(
            [w_mla_uq[l].reshape(C_Q_RANK, C_HEADS, C_QK_DIM),
             jnp.zeros((C_Q_RANK, C_HEADS, C_HEAD_PAD - C_QK_DIM), F32)], axis=2
        ).reshape(C_Q_RANK, C_PAD_WIDTH).astype(BF16)
        wk = _pad_heads_cols(w_mla_ukv[l], C_NOPE_DIM + C_V_DIM, 0, C_NOPE_DIM)
        wv = _pad_heads_cols(w_mla_ukv[l], C_NOPE_DIM + C_V_DIM, C_NOPE_DIM, C_NOPE_DIM + C_V_DIM)
        qm, km, vm = mla_prep(g, y, mla_q_norm[l].reshape(1, C_Q_RANK), mla_kv_norm[l].reshape(1, C_KV_RANK),
                              wq, wk, wv, tabs_c, tm_prep)
        yc1 = mla_flash(qm, km, vm, 0, b1, s1, min(512, s1), min(512, s1))
        yc2 = mla_flash(qm, km, vm, g.t1, b2, s2, min(512, s2), min(1024, s2))

        wpc = jnp.pad(w_proj_c[l].reshape(C_HEADS, C_V_DIM, d), ((0, 0), (0, C_HEAD_PAD - C_V_DIM), (0, 0)))
        wpc = wpc.reshape(C_PAD_WIDTH, d).astype(BF16)
        x, h2, comb = proj_merge(g, o_list, lse_list, yb, yc1, yc2, y, x, gt1, mul2, sh2,
                                 w_proj_a[l].astype(BF16), w_proj_b[l].astype(BF16), wpc, w_out[l].astype(BF16),
                                 wrh, wrm, br, tm_proj)

        x = moe_dense(g, h2, comb, w_exp_gate[l].astype(BF16), w_exp_up[l].astype(BF16),
                      w_exp_down[l].astype(BF16), x, gt2, fnorm, l == depth - 1, tm_moe)

    return x[:g.t1].reshape(b1, s1, d), x[g.t1:].reshape(b2, s2, d)
```

```python
import functools

import jax
import jax.numpy as jnp
from jax import lax
from jax.experimental import pallas as pl
from jax.experimental.pallas import tpu as pltpu

F32 = jnp.float32
BF16 = jnp.bfloat16

D_MODEL = 1024
DEPTH = 2
EPS = 1e-6
ROPE_THETA = 500000.0
NEG_INF = -1e30

A_HEADS = 8
A_HEAD_DIM = 64
A_ROPE_DIM = A_HEAD_DIM // 4
A_PATTERNS = ((128, 1), (512, 4), (2048, 16))
A_WIDTH = A_HEADS * A_HEAD_DIM
A_HALF = 64

B_HEADS = 4
B_KEY_DIM = 128
B_VAL_DIM = 256
B_GATE_RANK = 16
B_GATE_TAU = 16.0
B_CHUNK = 64
B_KW = B_HEADS * B_KEY_DIM
B_VW = B_HEADS * B_VAL_DIM

C_HEADS = 8
C_NOPE_DIM = 64
C_ROPE_DIM = 32
C_V_DIM = 64
C_QK_DIM = C_NOPE_DIM + C_ROPE_DIM
C_Q_RANK = 384
C_KV_RANK = 256
C_HEAD_PAD = 128
C_PAD_WIDTH = C_HEADS * C_HEAD_PAD

N_EXPERTS = 16
N_GROUPS = 4
EXPERTS_PER_GROUP = N_EXPERTS // N_GROUPS
TOP_K = 2
GROUP_SCORE_K = 2
D_EXPERT = 512
N_MOD = 6

IN_SIZES = (A_WIDTH, A_WIDTH, A_WIDTH, B_KW, B_KW, B_VW, B_VW, B_GATE_RANK, B_GATE_RANK,
            C_Q_RANK, C_KV_RANK, C_ROPE_DIM, 3 * D_MODEL)

LANES = 128
LOG2E = 1.4426950408889634

OFF_GATES = 0
OFF_BV = 3072
OFF_BR = 4096
OFF_AQ = 5120
OFF_AK = 5632
OFF_AV = 6144
OFF_BQ = 6656
OFF_BK = 7168
OFF_CKV = 7680
OFF_BZ = 7936
OFF_CQ = 8064
OFF_KR = 8448
NPAD = 8704
KR_LANE = 64

VMEM_LIMIT = 56 * 1024 * 1024


def _cparams(sem):
    return pltpu.CompilerParams(dimension_semantics=sem, vmem_limit_bytes=VMEM_LIMIT)


class Groups:
    def __init__(self, b1, s1, b2, s2):
        self.b1, self.s1, self.b2, self.s2 = b1, s1, b2, s2
        self.t1, self.t2 = b1 * s1, b2 * s2
        self.t = self.t1 + self.t2
        self.nb = b1 + b2

    def batch_of_block(self, i, tm):
        n1 = self.t1 // tm
        return jnp.where(i < n1, i // (self.s1 // tm), self.b1 + (i - n1) // (self.s2 // tm))

    def pos_block(self, i, tm):
        n1 = self.t1 // tm
        return jnp.where(i < n1, i % (self.s1 // tm), (i - n1) % (self.s2 // tm))


def _dot(a, b):
    return jnp.dot(a, b, preferred_element_type=F32)


def _dot_nt(a, b):
    return lax.dot_general(a, b, (((1,), (1,)), ((), ())), preferred_element_type=F32)


def _dot_tn(a, b):
    return lax.dot_general(a, b, (((0,), (0,)), ((), ())), preferred_element_type=F32)


def _split3(a):
    hi = a.astype(BF16)
    r1 = a - hi.astype(F32)
    mid = r1.astype(BF16)
    lo = (r1 - mid.astype(F32)).astype(BF16)
    return hi, mid, lo


def _rope(y, ct, s1, s2, half):
    return y * ct + pltpu.roll(y, LANES - half, 1) * s1 + pltpu.roll(y, half, 1) * s2


def _sigmoid(x):
    return 0.5 * jnp.tanh(0.5 * x) + 0.5


def _mod_kernel(c_ref, w_ref, b_ref, o_ref):
    c = c_ref[...]
    ca = c * _sigmoid(c)
    ch, cm, _ = _split3(ca)
    wh, wm, _ = _split3(w_ref[0])
    o_ref[0] = _dot(ch, wh) + _dot(cm, wh) + _dot(ch, wm) + b_ref[0]


def modulation(c_pad, w_mod, b_mod):
    nbp, d = c_pad.shape
    depth, _, n = w_mod.shape
    tn = 512
    return pl.pallas_call(
        _mod_kernel,
        grid=(depth, n // tn),
        in_specs=[pl.BlockSpec((nbp, d), lambda l, j: (0, 0)),
                  pl.BlockSpec((1, d, tn), lambda l, j: (l, 0, j)),
                  pl.BlockSpec((1, 1, tn), lambda l, j: (l, 0, j))],
        out_specs=pl.BlockSpec((1, nbp, tn), lambda l, j: (l, 0, j)),
        out_shape=jax.ShapeDtypeStruct((depth, nbp, n), F32),
        compiler_params=_cparams(("arbitrary", "arbitrary")),
        name="modulation",
    )(c_pad, w_mod, b_mod.reshape(depth, 1, n))


def _inproj_kernel(x_ref, mul_ref, sh_ref, w_ref, ct_ref, s1_ref, s2_ref, o_ref, od4_ref, od16_ref,
                   h_scr, y_scr, y4_scr, *, tm, tn, jq, jv):
    j = pl.program_id(1)

    @pl.when(j == 0)
    def _():
        x = x_ref[...]
        ms = jnp.mean(x * x, axis=-1, keepdims=True)
        h = x * lax.rsqrt(ms + EPS) * mul_ref[0] + sh_ref[0]
        h_scr[...] = h.astype(BF16)

    is_attn = (j >= jq) & (j <= jv)

    @pl.when(jnp.logical_not(is_attn))
    def _():
        o_ref[...] = _dot(h_scr[...], w_ref[j]).astype(BF16)

    @pl.when(is_attn)
    def _():
        y = _dot(h_scr[...], w_ref[j])
        ncol = tn // LANES

        @pl.when(j < jv)
        def _():
            ct, s1, s2 = ct_ref[...], s1_ref[...], s2_ref[...]
            for c in range(ncol):
                y_scr[c] = _rope(y[:, c * LANES:(c + 1) * LANES], ct, s1, s2, A_ROPE_DIM // 2)

        @pl.when(j == jv)
        def _():
            for c in range(ncol):
                y_scr[c] = y[:, c * LANES:(c + 1) * LANES]

        n4, n16 = tm // 4, tm // 16
        for c in range(ncol):
            cs = slice(c * LANES, (c + 1) * LANES)
            o_ref[:, cs] = y_scr[c].astype(BF16)
            for r in range(4):
                y4 = y_scr[c, pl.ds(r, n4, stride=4), :]
                y4_scr[c, r * n4:(r + 1) * n4, :] = y4
                od4_ref[:, r * tn + c * LANES:r * tn + (c + 1) * LANES] = y4.astype(BF16)
            for r in range(4):
                for r2 in range(4):
                    col = (r + 4 * r2) * tn + c * LANES
                    od16_ref[:, col:col + LANES] = y4_scr[c, pl.ds(r * n4 + r2, n16, stride=4), :].astype(BF16)


def in_projection(g, x, mul, shift, w_pad, tabs, tm):
    t, d = x.shape
    tn = A_WIDTH
    ct, s1, s2 = tabs
    jq, jv = OFF_AQ // tn, OFF_AV // tn
    bmap = lambda i, j: (g.batch_of_block(i, tm), 0, 0)
    pmap = lambda i, j: (g.pos_block(i, tm), 0)
    dmap = lambda i, j: (jnp.clip(j - jq, 0, jv - jq), i, 0)
    kern = functools.partial(_inproj_kernel, tm=tm, tn=tn, jq=jq, jv=jv)
    return pl.pallas_call(
        kern,
        grid=(t // tm, NPAD // tn),
        in_specs=[pl.BlockSpec((tm, d), lambda i, j: (i, 0)),
                  pl.BlockSpec((1, 1, d), bmap),
                  pl.BlockSpec((1, 1, d), bmap),
                  pl.BlockSpec((NPAD // tn, d, tn), lambda i, j: (0, 0, 0), pipeline_mode=pl.Buffered(1)),
                  pl.BlockSpec((tm, LANES), pmap),
                  pl.BlockSpec((tm, LANES), pmap),
                  pl.BlockSpec((tm, LANES), pmap)],
        out_specs=[pl.BlockSpec((tm, tn), lambda i, j: (i, j)),
                   pl.BlockSpec((None, tm // 4, 4 * tn), dmap),
                   pl.BlockSpec((None, tm // 16, 16 * tn), dmap)],
        out_shape=[jax.ShapeDtypeStruct((t, NPAD), BF16),
                   jax.ShapeDtypeStruct((3, t // 4, 4 * tn), BF16),
                   jax.ShapeDtypeStruct((3, t // 16, 16 * tn), BF16)],
        scratch_shapes=[pltpu.VMEM((tm, d), BF16), pltpu.VMEM((tn // LANES, tm, LANES), F32),
                        pltpu.VMEM((tn // LANES, tm, LANES), F32)],
        compiler_params=_cparams(("arbitrary", "arbitrary")),
        name="in_projection",
    )(x, mul, shift, w_pad.reshape(d, NPAD // tn, tn).transpose(1, 0, 2), ct, s1, s2)


def _band_kernel(q_ref, kp_ref, km_ref, kn_ref, vp_ref, vm_ref, vn_ref, o_ref, lse_ref, *, tb, tq, rows1, l1, l2):
    i = pl.program_id(1)
    tk = tq + 2 * A_HALF
    k = jnp.concatenate([kp_ref[...], km_ref[...], kn_ref[...]], axis=0)
    v = jnp.concatenate([vp_ref[...], vm_ref[...], vn_ref[...]], axis=0)
    qi = lax.broadcasted_iota(jnp.int32, (tq, tk), 0)
    kj = lax.broadcasted_iota(jnp.int32, (tq, tk), 1)
    band = jnp.abs(kj - A_HALF - qi) <= A_HALF
    lo = lax.broadcasted_iota(jnp.int32, (tq, LANES), 1) < A_HEAD_DIM
    scale = A_HEAD_DIM ** -0.5
    for u in range(tb // tq):
        row0 = i * tb + u * tq
        in1 = row0 < rows1
        seq_len = jnp.where(in1, l1, l2)
        pos0 = jnp.where(in1, row0 % l1, (row0 - rows1) % l2)
        kpos = pos0 - A_HALF + kj
        valid = band & (kpos >= 0) & (kpos < seq_len)
        valid2 = jnp.concatenate([valid, valid], axis=0)
        rows = slice(u * tq, (u + 1) * tq)
        krows = slice(u * tq, u * tq + tk)
        for p in range(A_WIDTH // LANES):
            sl = slice(p * LANES, (p + 1) * LANES)
            qp, kp, vp = q_ref[rows, sl] * scale, k[krows, sl], v[krows, sl]
            zero = jnp.zeros_like(qp)
            qm = jnp.concatenate([jnp.where(lo, qp, zero), jnp.where(lo, zero, qp)], axis=0)
            s = jnp.where(valid2, _dot_nt(qm, kp), NEG_INF)
            m = jnp.max(s, axis=1, keepdims=True)
            e = jnp.exp(s - m)
            l = jnp.sum(e, axis=1, keepdims=True)
            o2 = _dot(e.astype(BF16), vp) / l
            lse2 = m + jnp.log(l)
            o_ref[rows, sl] = jnp.where(lo, o2[:tq], o2[tq:]).astype(BF16)
            lse_ref[rows, sl] = jnp.where(lo, lse2[:tq], lse2[tq:])


def band_attention(g, qkv, dil, cols, cpb):
    rows = g.t // dil
    tq = 128
    tb = 256
    sub = tb // A_HALF
    nsub = rows // A_HALF

    def main(c):
        ld, col = cols[c]
        return pl.BlockSpec((None, tb, A_WIDTH), lambda r, i: (ld, i, r * cpb + col))

    def prev(c):
        ld, col = cols[c]
        return pl.BlockSpec((None, A_HALF, A_WIDTH), lambda r, i: (ld, jnp.maximum(i * sub - 1, 0), r * cpb + col))

    def nxt(c):
        ld, col = cols[c]
        return pl.BlockSpec((None, A_HALF, A_WIDTH),
                            lambda r, i: (ld, jnp.minimum((i + 1) * sub, nsub - 1), r * cpb + col))

    assert (g.s1 // dil) % tq == 0 and (g.s2 // dil) % tq == 0 and rows % tb == 0
    kern = functools.partial(_band_kernel, tb=tb, tq=tq, rows1=g.t1 // dil, l1=g.s1 // dil, l2=g.s2 // dil)
    return pl.pallas_call(
        kern,
        grid=(dil, rows // tb),
        in_specs=[main(0), prev(1), main(1), nxt(1), prev(2), main(2), nxt(2)],
        out_specs=[pl.BlockSpec((tb, A_WIDTH), lambda r, i: (i, r)),
                   pl.BlockSpec((tb, A_WIDTH), lambda r, i: (i, r))],
        out_shape=[jax.ShapeDtypeStruct((rows, dil * A_WIDTH), BF16),
                   jax.ShapeDtypeStruct((rows, dil * A_WIDTH), F32)],
        compiler_params=_cparams(("arbitrary", "arbitrary")),
        name="band_attention_d%d" % dil,
    )(qkv, qkv, qkv, qkv, qkv, qkv, qkv)


def _gla_kernel(*refs, reverse, final, tc, nblk, t1, s1, s2):
    if final:
        q_ref, k_ref, v_ref, z_ref, wa_ref, ba_ref, tri_ref, ob_ref, r_ref, gain_ref, o_ref, st_scr = refs
    else:
        q_ref, k_ref, v_ref, z_ref, wa_ref, ba_ref, tri_ref, o_ref, st_scr = refs
    i = pl.program_id(0)
    blk = (nblk - 1 - i) if reverse else i
    row0 = blk * tc
    in1 = row0 < t1
    pos0 = jnp.where(in1, row0 % s1, (row0 - t1) % s2)
    slen = jnp.where(in1, s1, s2)
    start = (pos0 + tc == slen) if reverse else (pos0 == 0)

    @pl.when(start)
    def _():
        st_scr[...] = jnp.zeros_like(st_scr)

    zl = _dot(z_ref[...], wa_ref[...]) + ba_ref[...]
    la = (jnp.minimum(zl, 0.0) - jnp.log(1.0 + jnp.exp(-jnp.abs(zl)))) * (1.0 / B_GATE_TAU)
    hi, mid, _ = _split3(la)
    tri = tri_ref[...]
    bc_all = _dot(tri, hi) + _dot(tri, mid)

    qi = lax.broadcasted_iota(jnp.int32, (B_CHUNK, B_CHUNK), 0)
    si = lax.broadcasted_iota(jnp.int32, (B_CHUNK, B_CHUNK), 1)
    mask = (si > qi) if reverse else (si <= qi)
    nch = tc // B_CHUNK
    q_all = q_ref[...].astype(F32) * (B_KEY_DIM ** -0.5)
    k_all = k_ref[...].astype(F32)
    q_dec = (q_all * jnp.exp(bc_all)).astype(BF16)
    k_inv = (k_all * jnp.exp(-bc_all)).astype(BF16)
    st = [st_scr[h] for h in range(B_HEADS)]
    for c in (range(nch - 1, -1, -1) if reverse else range(nch)):
        sl = slice(c * B_CHUNK, (c + 1) * B_CHUNK)
        edge = c * B_CHUNK if reverse else (c + 1) * B_CHUNK - 1
        tot = bc_all[edge:edge + 1]
        k_end = (k_all[sl] * jnp.exp(tot - bc_all[sl])).astype(BF16)
        dec = jnp.exp(tot)
        for h in range(B_HEADS):
            ks = slice(h * B_KEY_DIM, (h + 1) * B_KEY_DIM)
            vs = slice(h * B_VAL_DIM, (h + 1) * B_VAL_DIM)
            v = v_ref[sl, vs]
            att = jnp.where(mask, _dot_nt(q_dec[sl, ks], k_inv[sl, ks]), 0.0).astype(BF16)
            o = _dot(att, v) + _dot_nt(q_dec[sl, ks], st[h].astype(BF16))
            st[h] = st[h] * dec[:, ks] + _dot_tn(v, k_end[:, ks])
            if final:
                o = o + ob_ref[sl, vs]
                on = o * lax.rsqrt(jnp.mean(o * o, axis=-1, keepdims=True) + EPS) * gain_ref[...]
                r = r_ref[sl, vs].astype(F32)
                o_ref[sl, vs] = (on * (r * _sigmoid(r))).astype(BF16)
            else:
                o_ref[sl, vs] = o
    for h in range(B_HEADS):
        st_scr[h] = st[h]


def gla_pass(g, y, wa, ba, reverse, o_back=None, gain=None):
    t = g.t
    tc = 256
    nblk = t // tc
    final = o_back is not None
    rowmap = (lambda i: nblk - 1 - i) if reverse else (lambda i: i)
    row = lambda w, off: pl.BlockSpec((tc, w), lambda i: (rowmap(i), off // w))
    full = lambda a: pl.BlockSpec(a.shape, lambda i: (0,) * a.ndim)
    ri = jnp.arange(tc)[:, None]
    ci = jnp.arange(tc)[None, :]
    tri = ((ri // B_CHUNK == ci // B_CHUNK) & ((ci >= ri) if reverse else (ci <= ri))).astype(BF16)
    in_specs = [row(B_KW, OFF_BQ), row(B_KW, OFF_BK), row(B_VW, OFF_BV), row(LANES, OFF_BZ),
                full(wa), full(ba), full(tri)]
    args = [y, y, y, y, wa, ba, tri]
    if final:
        in_specs += [row(B_VW, 0), row(B_VW, OFF_BR), full(gain)]
        args += [o_back, y, gain]
    kern = functools.partial(_gla_kernel, reverse=reverse, final=final, tc=tc, nblk=nblk, t1=g.t1, s1=g.s1, s2=g.s2)
    return pl.pallas_call(
        kern,
        grid=(nblk,),
        in_specs=in_specs,
        out_specs=row(B_VW, 0),
        out_shape=jax.ShapeDtypeStruct((t, B_VW), BF16 if final else F32),
        scratch_shapes=[pltpu.VMEM((B_HEADS, B_VAL_DIM, B_KEY_DIM), F32)],
        compiler_params=_cparams(("arbitrary",)),
        name="gla_forward_final" if final else "gla_backward",
    )(*args)


def _mla_prep_kernel(ckv_ref, cq_ref, kr_ref, qn_ref, kvn_ref, wq_ref, wk_ref, wv_ref, ct_ref, s1_ref, s2_ref,
                     q_out, k_out, v_out):
    def norm(ref, gain_ref):
        xf = ref[...].astype(F32)
        return (xf * lax.rsqrt(jnp.mean(xf * xf, axis=-1, keepdims=True) + EPS) * gain_ref[...]).astype(BF16)

    ckv_n = norm(ckv_ref, kvn_ref)
    cq_n = norm(cq_ref, qn_ref)
    q = _dot(cq_n, wq_ref[...])
    kn = _dot(ckv_n, wk_ref[...])
    v = _dot(ckv_n, wv_ref[...])
    ct, s1, s2 = ct_ref[...], s1_ref[...], s2_ref[...]
    half = C_ROPE_DIM // 2
    kr_rot = _rope(kr_ref[...].astype(F32), ct, s1, s2, half)
    ones_lane = lax.broadcasted_iota(jnp.int32, kr_rot.shape, 1) == C_V_DIM
    scale = C_QK_DIM ** -0.5 * LOG2E
    for h in range(C_HEADS):
        sl = slice(h * C_HEAD_PAD, (h + 1) * C_HEAD_PAD)
        q_out[:, sl] = (_rope(q[:, sl], ct, s1, s2, half) * scale).astype(BF16)
        k_out[:, sl] = (kn[:, sl] + kr_rot).astype(BF16)
        v_out[:, sl] = jnp.where(ones_lane, 1.0, v[:, sl]).astype(BF16)


def mla_prep(g, y, qn, kvn, wq, wk, wv, tabs, tm):
    t = g.t
    ct, s1, s2 = tabs
    pmap = lambda i: (g.pos_block(i, tm), 0)
    full = lambda a: pl.BlockSpec(a.shape, lambda i: (0,) * a.ndim)
    out = jax.ShapeDtypeStruct((t, C_PAD_WIDTH), BF16)
    ospec = pl.BlockSpec((tm, C_PAD_WIDTH), lambda i: (i, 0))
    return pl.pallas_call(
        _mla_prep_kernel,
        grid=(t // tm,),
        in_specs=[pl.BlockSpec((tm, C_KV_RANK), lambda i: (i, OFF_CKV // C_KV_RANK)),
                  pl.BlockSpec((tm, C_Q_RANK), lambda i: (i, OFF_CQ // C_Q_RANK)),
                  pl.BlockSpec((tm, LANES), lambda i: (i, OFF_KR // LANES)),
                  full(qn), full(kvn), full(wq), full(wk), full(wv),
                  pl.BlockSpec((tm, LANES), pmap), pl.BlockSpec((tm, LANES), pmap), pl.BlockSpec((tm, LANES), pmap)],
        out_specs=[ospec, ospec, ospec],
        out_shape=[out, out, out],
        compiler_params=_cparams(("arbitrary",)),
        name="mla_prep",
    )(y, y, y, qn, kvn, wq, wk, wv, ct, s1, s2)


def _flash_kernel(q_ref, k_ref, v_ref, o_ref, s0_scr, s1_scr, p_scr, m_scr, a_scr, acc_scr, *, tk, nk, rc, unroll):
    tq = q_ref.shape[0]

    def logits(t, scr):
        off = pl.multiple_of(t * tk, tk)
        scr[...] = _dot_nt(q_ref[...], k_ref[pl.ds(off, tk), :])

    def softmax_pv(t, scr):
        off = pl.multiple_of(t * tk, tk)
        for r in range(tq // rc):
            rows = slice(r * rc, (r + 1) * rc)
            s = scr[rows, :]
            m_old = m_scr[rows, :]
            m_new = jnp.maximum(m_old, jnp.max(s, axis=1, keepdims=True))
            p_scr[rows, :] = jnp.exp2(s - jnp.concatenate([m_new] * (tk // LANES), axis=1)).astype(BF16)
            a_scr[rows, :] = jnp.exp2(m_old - m_new)
            m_scr[rows, :] = m_new
        acc_scr[...] = acc_scr[...] * a_scr[...] + _dot(p_scr[...], v_ref[pl.ds(off, tk), :])

    bufs = (s0_scr, s1_scr)

    def group(t0, last):
        for u in range(unroll):
            if not (last and u == unroll - 1):
                logits(t0 + u + 1, bufs[(u + 1) % 2])
            softmax_pv(t0 + u, bufs[u % 2])

    def body(jj, carry):
        group(unroll * jj, False)
        return carry

    m_scr[...] = jnp.full(m_scr.shape, NEG_INF, F32)
    acc_scr[...] = jnp.zeros(acc_scr.shape, F32)
    logits(0, s0_scr)
    lax.fori_loop(0, nk // unroll - 1, body, 0)
    group(nk - unroll, True)
    acc = acc_scr[...]
    o_ref[...] = (acc / acc[:, C_V_DIM:C_V_DIM + 1]).astype(BF16)


def mla_flash(q, k, v, row_off, nseq, s, tq, tk):
    assert row_off % s == 0 and row_off % tq == 0
    qb0 = row_off // tq
    sb0 = row_off // s
    nq = s // tq
    nk = s // tk
    unroll = 4 if nk % 4 == 0 else 2
    assert nk % unroll == 0
    kern = functools.partial(_flash_kernel, tk=tk, nk=nk, rc=64, unroll=unroll)
    return pl.pallas_call(
        kern,
        grid=(nseq, C_HEADS, nq),
        scratch_shapes=[pltpu.VMEM((tq, tk), F32), pltpu.VMEM((tq, tk), F32), pltpu.VMEM((tq, tk), BF16),
                        pltpu.VMEM((tq, LANES), F32), pltpu.VMEM((tq, LANES), F32),
                        pltpu.VMEM((tq, C_HEAD_PAD), F32)],
        in_specs=[pl.BlockSpec((tq, C_HEAD_PAD), lambda b, h, i: (qb0 + b * nq + i, h)),
                  pl.BlockSpec((s, C_HEAD_PAD), lambda b, h, i: (sb0 + b, h)),
                  pl.BlockSpec((s, C_HEAD_PAD), lambda b, h, i: (sb0 + b, h))],
        out_specs=pl.BlockSpec((tq, C_HEAD_PAD), lambda b, h, i: (b * nq + i, h)),
        out_shape=jax.ShapeDtypeStruct((nseq * s, C_PAD_WIDTH), BF16),
        compiler_params=_cparams(("arbitrary", "arbitrary", "arbitrary")),
        name="mla_flash_s%d" % s,
    )(q, k, v)


def _route(s, bias):
    lane = lax.broadcasted_iota(jnp.int32, s.shape, 1)
    pos = lane % EXPERTS_PER_GROUP
    grp = lane // EXPERTS_PER_GROUP
    sel = s + bias
    rank = jnp.zeros(s.shape, jnp.int32)
    for k in range(1, EXPERTS_PER_GROUP):
        below = pltpu.roll(sel, k, 1)
        above = pltpu.roll(sel, LANES - k, 1)
        rank += jnp.where((pos >= k) & (below >= sel), 1, 0)
        rank += jnp.where((pos + k < EXPERTS_PER_GROUP) & (above > sel), 1, 0)
    top = rank < GROUP_SCORE_K
    contrib = jnp.where(top, sel, 0.0)
    score = contrib
    for k in range(1, EXPERTS_PER_GROUP):
        score += jnp.where(pos >= k, pltpu.roll(contrib, k, 1), 0.0)
        score += jnp.where(pos + k < EXPERTS_PER_GROUP, pltpu.roll(contrib, LANES - k, 1), 0.0)
    best = lane < N_EXPERTS
    for k in range(1, N_GROUPS):
        earlier = pltpu.roll(score, k * EXPERTS_PER_GROUP, 1)
        later = pltpu.roll(score, LANES - k * EXPERTS_PER_GROUP, 1)
        best &= jnp.logical_not((grp >= k) & (earlier >= score))
        best &= jnp.logical_not((grp + k < N_GROUPS) & (later > score))
    chosen = best & (rank < TOP_K)
    total = jnp.sum(jnp.where(chosen, s, 0.0), axis=1, keepdims=True)
    return jnp.where(chosen, s / total, 0.0)


def _proj_kernel(o1_ref, o2_ref, o3_ref, l1_ref, l2_ref, l3_ref, yb_ref, yc1_ref, yc2_ref, ga_ref, gb_ref, gc_ref,
                 x_ref, g1_ref, mul2_ref, sh2_ref, wpa_ref, wpb_ref, wpc_ref, wout_ref, wrh_ref, wrm_ref, br_ref,
                 x_out, h_out, comb_out, o2_scr, l2_scr, o3_scr, l3_scr, *, tm, n1):
    for dil, o_ref, l_ref, o_scr, l_scr in ((A_PATTERNS[1][1], o2_ref, l2_ref, o2_scr, l2_scr),
                                           (A_PATTERNS[2][1], o3_ref, l3_ref, o3_scr, l3_scr)):
        for r in range(dil):
            for c in range(A_WIDTH // LANES):
                sl = slice(r * A_WIDTH + c * LANES, r * A_WIDTH + (c + 1) * LANES)
                o_scr[c, pl.ds(r, tm // dil, stride=dil), :] = o_ref[:, sl].astype(F32)
                l_scr[c, pl.ds(r, tm // dil, stride=dil), :] = l_ref[:, sl]
    slabs = lambda scr: jnp.concatenate([scr[c] for c in range(A_WIDTH // LANES)], axis=1)
    l1, l2, l3 = l1_ref[...], slabs(l2_scr), slabs(l3_scr)
    m = jnp.maximum(jnp.maximum(l1, l2), l3)
    e1, e2, e3 = jnp.exp(l1 - m), jnp.exp(l2 - m), jnp.exp(l3 - m)
    ya = (e1 * o1_ref[...].astype(F32) + e2 * slabs(o2_scr) + e3 * slabs(o3_scr)) / (e1 + e2 + e3)
    sig = lambda ref: _sigmoid(ref[...].astype(F32))
    yc = jnp.where(pl.program_id(0) < n1, yc1_ref[...], yc2_ref[...])
    merged = (sig(ga_ref) * _dot(ya.astype(BF16), wpa_ref[...])
              + sig(gb_ref) * _dot(yb_ref[...], wpb_ref[...])
              + sig(gc_ref) * _dot(yc, wpc_ref[...]))
    out = _dot(merged.astype(BF16), wout_ref[...])
    x = x_ref[...] + g1_ref[0] * out
    x_out[...] = x
    h = x * lax.rsqrt(jnp.mean(x * x, axis=-1, keepdims=True) + EPS) * mul2_ref[0] + sh2_ref[0]
    h_out[...] = h.astype(BF16)
    hh, hm, _ = _split3(h)
    s = _sigmoid(_dot(hh, wrh_ref[...]) + _dot(hm, wrh_ref[...]) + _dot(hh, wrm_ref[...]))
    comb_out[...] = _route(s, br_ref[...])


def proj_merge(g, o_list, lse_list, yb, yc1, yc2, y, x, gate1, mul2, sh2, wpa, wpb, wpc, wout, wrh, wrm, br, tm):
    t, d = x.shape
    n1 = g.t1 // tm
    n2 = g.t2 // tm
    bmap = lambda i: (g.batch_of_block(i, tm), 0, 0)
    row = lambda w, col=0: pl.BlockSpec((tm, w), lambda i: (i, col))
    dil_row = lambda dil: pl.BlockSpec((tm // dil, dil * A_WIDTH), lambda i: (i, 0))
    full = lambda a: pl.BlockSpec(a.shape, lambda i: (0,) * a.ndim)
    gcol = OFF_GATES // d
    d2, d3 = A_PATTERNS[1][1], A_PATTERNS[2][1]
    attn_specs = [row(A_WIDTH), dil_row(d2), dil_row(d3)]
    return pl.pallas_call(
        functools.partial(_proj_kernel, tm=tm, n1=n1),
        grid=(t // tm,),
        in_specs=attn_specs + attn_specs + [
            row(B_VW),
            pl.BlockSpec((tm, C_PAD_WIDTH), lambda i: (jnp.minimum(i, n1 - 1), 0)),
            pl.BlockSpec((tm, C_PAD_WIDTH), lambda i: (jnp.clip(i - n1, 0, n2 - 1), 0)),
            row(d, gcol), row(d, gcol + 1), row(d, gcol + 2),
            row(d), pl.BlockSpec((1, 1, d), bmap), pl.BlockSpec((1, 1, d), bmap), pl.BlockSpec((1, 1, d), bmap),
            full(wpa), full(wpb), full(wpc), full(wout), full(wrh), full(wrm), full(br)],
        out_specs=[row(d), row(d), row(LANES)],
        out_shape=[jax.ShapeDtypeStruct((t, d), F32), jax.ShapeDtypeStruct((t, d), BF16),
                   jax.ShapeDtypeStruct((t, LANES), F32)],
        scratch_shapes=[pltpu.VMEM((A_WIDTH // LANES, tm, LANES), F32)] * 4,
        compiler_params=_cparams(("arbitrary",)),
        name="proj_merge_route",
    )(*o_list, *lse_list, yb, yc1, yc2, y, y, y, x, gate1, mul2, sh2, wpa, wpb, wpc, wout, wrh, wrm, br)


def _moe_kernel(h_ref, comb_ref, wg_ref, wu_ref, wd_ref, x_ref, g2_ref, fn_ref, o_ref, acc_scr, *, final):
    e = pl.program_id(1)

    @pl.when(e == 0)
    def _():
        acc_scr[...] = jnp.zeros_like(acc_scr)

    h = h_ref[...]
    gate = _dot(h, wg_ref[0])
    a = gate * _sigmoid(gate) * _dot(h, wu_ref[0])
    yv = _dot(a.astype(BF16), wd_ref[0])
    comb = comb_ref[...]
    lane = lax.broadcasted_iota(jnp.int32, comb.shape, 1)
    ce = jnp.sum(jnp.where(lane == e, comb, 0.0), axis=1, keepdims=True)
    acc_scr[...] += ce * yv

    @pl.when(e == N_EXPERTS - 1)
    def _():
        x = x_ref[...] + g2_ref[0] * acc_scr[...]
        if final:
            x = x * lax.rsqrt(jnp.mean(x * x, axis=-1, keepdims=True) + EPS) * fn_ref[...]
        o_ref[...] = x


def moe_dense(g, h, comb, wg, wu, wd, x, gate2, fnorm, final, tm):
    t, d = x.shape
    bmap = lambda i, e: (g.batch_of_block(i, tm), 0, 0)
    kern = functools.partial(_moe_kernel, final=final)
    return pl.pallas_call(
        kern,
        grid=(t // tm, N_EXPERTS),
        in_specs=[pl.BlockSpec((tm, d), lambda i, e: (i, 0)),
                  pl.BlockSpec((tm, LANES), lambda i, e: (i, 0)),
                  pl.BlockSpec((1, d, D_EXPERT), lambda i, e: (e, 0, 0)),
                  pl.BlockSpec((1, d, D_EXPERT), lambda i, e: (e, 0, 0)),
                  pl.BlockSpec((1, D_EXPERT, d), lambda i, e: (e, 0, 0)),
                  pl.BlockSpec((tm, d), lambda i, e: (i, 0)),
                  pl.BlockSpec((1, 1, d), bmap),
                  pl.BlockSpec((1, d), lambda i, e: (0, 0))],
        out_specs=pl.BlockSpec((tm, d), lambda i, e: (i, 0)),
        out_shape=jax.ShapeDtypeStruct((t, d), F32),
        scratch_shapes=[pltpu.VMEM((tm, d), F32)],
        compiler_params=_cparams(("arbitrary", "arbitrary")),
        name="moe_dense",
    )(h, comb, wg, wu, wd, x, gate2, fnorm)


def _rope_tables(smax, rot_dim, period, lane_off):
    half = rot_dim // 2
    inv_freq = ROPE_THETA ** (-jnp.arange(half, dtype=F32) / half)
    ang = jnp.arange(smax).astype(F32)[:, None] * inv_freq[None, :]
    cos, sin = jnp.cos(ang), jnp.sin(ang)
    gl = (jnp.arange(LANES) % period) - lane_off
    first = (gl >= 0) & (gl < half)
    second = (gl >= half) & (gl < rot_dim)
    j = jnp.clip(jnp.where(first, gl, gl - half), 0, half - 1)
    cl, sn = cos[:, j], sin[:, j]
    ct = jnp.where((first | second)[None, :], cl, 1.0)
    s1 = jnp.where(first[None, :], -sn, 0.0)
    s2 = jnp.where(second[None, :], sn, 0.0)
    return ct, s1, s2


def _pad_w_in(w):
    d = w.shape[0]
    parts, off = [], 0
    for sz in IN_SIZES:
        parts.append(w[:, off:off + sz])
        off += sz
    a_q, a_k, a_v, b_q, b_k, b_v, b_r, b_zf, b_zb, c_cq, c_ckv, c_kr, gates = parts
    z = lambda n: jnp.zeros((d, n), w.dtype)
    bz = jnp.concatenate([b_zf, b_zb, z(LANES - 2 * B_GATE_RANK)], axis=1)
    kr = jnp.concatenate([z(KR_LANE), c_kr, z(LANES - KR_LANE - C_ROPE_DIM)], axis=1)
    out = jnp.concatenate([gates, b_v, b_r, a_q, a_k, a_v, b_q, b_k, c_ckv, bz, c_cq, kr, z(LANES)], axis=1)
    assert out.shape[1] == NPAD
    return out.astype(BF16)


def _pad_heads_cols(w, real, take_lo, take_hi):
    kdim = w.shape[0]
    wh = w.reshape(kdim, C_HEADS, real)[:, :, take_lo:take_hi]
    wh = jnp.pad(wh, ((0, 0), (0, 0), (0, C_HEAD_PAD - (take_hi - take_lo))))
    return wh.reshape(kdim, C_PAD_WIDTH).astype(BF16)


def _gate_weights(w_a, b_a, row_off):
    wa = jnp.zeros((LANES, B_KW), F32).at[row_off:row_off + B_GATE_RANK].set(w_a)
    return wa.astype(BF16), b_a.reshape(1, B_KW).astype(F32)


def kernel(x_prompt, x_sample, c_prompt, c_sample, norm_mix, norm_moe, w_mod, b_mod, w_in, w_gla_af, b_gla_af,
           w_gla_ab, b_gla_ab, gla_norm, mla_q_norm, w_mla_uq, mla_kv_norm, w_mla_ukv, w_proj_a, w_proj_b, w_proj_c,
           w_out, w_router, b_router, w_exp_gate, w_exp_up, w_exp_down, final_norm):
    b1, s1, d = x_prompt.shape
    b2, s2, _ = x_sample.shape
    g = Groups(b1, s1, b2, s2)
    depth = w_in.shape[0]
    tm_in = min(1024, s1, s2)
    tm_prep = min(512, s1, s2)
    tm_proj = min(512, s1, s2)
    tm_moe = min(1024, s1, s2)

    x = jnp.concatenate([x_prompt.reshape(g.t1, d), x_sample.reshape(g.t2, d)], axis=0)
    c = jnp.concatenate([c_prompt, c_sample], axis=0)
    nbp = -(-g.nb // 8) * 8
    c_pad = jnp.pad(c, ((0, nbp - g.nb), (0, 0)))
    mod = modulation(c_pad, w_mod, b_mod)[:, :g.nb]

    smax = max(s1, s2)
    tabs_a = _rope_tables(smax, A_ROPE_DIM, A_HEAD_DIM, 0)
    tabs_c = _rope_tables(smax, C_ROPE_DIM, C_HEAD_PAD, KR_LANE)

    wr_hi = jnp.pad(w_router, ((0, 0), (0, LANES - N_EXPERTS)))
    wrh = wr_hi.astype(BF16)
    wrm = (wr_hi - wrh.astype(F32)).astype(BF16)
    fnorm = final_norm.reshape(1, d)
    br = jnp.pad(b_router.astype(F32), (0, LANES - N_EXPERTS)).reshape(1, LANES)

    for l in range(depth):
        sh1, sc1, gt1, sh2, sc2, gt2 = [m.reshape(g.nb, 1, d) for m in jnp.split(mod[l], N_MOD, axis=-1)]
        mul1 = norm_mix[l][None, None, :] * (1.0 + sc1)
        mul2 = norm_moe[l][None, None, :] * (1.0 + sc2)

        y, yd4, yd16 = in_projection(g, x, mul1, sh1, _pad_w_in(w_in[l]), tabs_a, tm_in)

        nat_cols = tuple((0, off // A_WIDTH) for off in (OFF_AQ, OFF_AK, OFF_AV))
        dil_cols = ((0, 0), (1, 0), (2, 0))
        o_list, lse_list = [], []
        for (_, dil), src in zip(A_PATTERNS, (y.reshape(1, g.t, NPAD), yd4, yd16)):
            o, lse = band_attention(g, src, dil, nat_cols if dil == 1 else dil_cols, NPAD // A_WIDTH if dil == 1 else 1)
            o_list.append(o)
            lse_list.append(lse)

        waf, baf = _gate_weights(w_gla_af[l], b_gla_af[l], 0)
        wab, bab = _gate_weights(w_gla_ab[l], b_gla_ab[l], B_GATE_RANK)
        o_back = gla_pass(g, y, wab, bab, reverse=True)
        yb = gla_pass(g, y, waf, baf, reverse=False, o_back=o_back, gain=gla_norm[l].reshape(1, B_VAL_DIM))

        wq = jnp.concatenate(
            [w_mla_uq[l].reshape(C_Q_RANK, C_HEADS, C_QK_DIM),
             jnp.zeros((C_Q_RANK, C_HEADS, C_HEAD_PAD - C_QK_DIM), F32)], axis=2
        ).reshape(C_Q_RANK, C_PAD_WIDTH).astype(BF16)
        wk = _pad_heads_cols(w_mla_ukv[l], C_NOPE_DIM + C_V_DIM, 0, C_NOPE_DIM)
        wv = _pad_heads_cols(w_mla_ukv[l], C_NOPE_DIM + C_V_DIM, C_NOPE_DIM, C_NOPE_DIM + C_V_DIM)
        qm, km, vm = mla_prep(g, y, mla_q_norm[l].reshape(1, C_Q_RANK), mla_kv_norm[l].reshape(1, C_KV_RANK),
                              wq, wk, wv, tabs_c, tm_prep)
        yc1 = mla_flash(qm, km, vm, 0, b1, s1, min(512, s1), min(512, s1))
        yc2 = mla_flash(qm, km, vm, g.t1, b2, s2, min(512, s2), min(1024, s2))

        wpc = jnp.pad(w_proj_c[l].reshape(C_HEADS, C_V_DIM, d), ((0, 0), (0, C_HEAD_PAD - C_V_DIM), (0, 0)))
        wpc = wpc.reshape(C_PAD_WIDTH, d).astype(BF16)
        x, h2, comb = proj_merge(g, o_list, lse_list, yb, yc1, yc2, y, x, gt1, mul2, sh2,
                                 w_proj_a[l].astype(BF16), w_proj_b[l].astype(BF16), wpc, w_out[l].astype(BF16),
                                 wrh, wrm, br, tm_proj)

        x = moe_dense(g, h2, comb, w_exp_gate[l].astype(BF16), w_exp_up[l].astype(BF16),
                      w_exp_down[l].astype(BF16), x, gt2, fnorm, l == depth - 1, tm_moe)

    return x[:g.t1].reshape(b1, s1, d), x[g.t1:].reshape(b2, s2, d)
```

```python
import functools

import jax
import jax.numpy as jnp
from jax import lax
from jax.experimental import pallas as pl
from jax.experimental.pallas import tpu as pltpu

F32 = jnp.float32
BF16 = jnp.bfloat16

D_MODEL = 1024
DEPTH = 2
EPS = 1e-6
ROPE_THETA = 500000.0
NEG_INF = -1e30

A_HEADS = 8
A_HEAD_DIM = 64
A_ROPE_DIM = A_HEAD_DIM // 4
A_PATTERNS = ((128, 1), (512, 4), (2048, 16))
A_WIDTH = A_HEADS * A_HEAD_DIM
A_HALF = 64

B_HEADS = 4
B_KEY_DIM = 128
B_VAL_DIM = 256
B_GATE_RANK = 16
B_GATE_TAU = 16.0
B_CHUNK = 64
B_KW = B_HEADS * B_KEY_DIM
B_VW = B_HEADS * B_VAL_DIM

C_HEADS = 8
C_NOPE_DIM = 64
C_ROPE_DIM = 32
C_V_DIM = 64
C_QK_DIM = C_NOPE_DIM + C_ROPE_DIM
C_Q_RANK = 384
C_KV_RANK = 256
C_HEAD_PAD = 128
C_PAD_WIDTH = C_HEADS * C_HEAD_PAD

N_EXPERTS = 16
N_GROUPS = 4
EXPERTS_PER_GROUP = N_EXPERTS // N_GROUPS
TOP_K = 2
GROUP_SCORE_K = 2
D_EXPERT = 512
N_MOD = 6

IN_SIZES = (A_WIDTH, A_WIDTH, A_WIDTH, B_KW, B_KW, B_VW, B_VW, B_GATE_RANK, B_GATE_RANK,
            C_Q_RANK, C_KV_RANK, C_ROPE_DIM, 3 * D_MODEL)

LANES = 128
ROW_TILE = D_MODEL // LANES
LOG2E = 1.4426950408889634

OFF_GATES = 0
OFF_BV = 3072
OFF_BR = 4096
OFF_AQ = 5120
OFF_AK = 5632
OFF_AV = 6144
OFF_BQ = 6656
OFF_BK = 7168
OFF_CKV = 7680
OFF_BZ = 7936
OFF_CQ = 8064
OFF_KR = 8448
NPAD = 8704
KR_LANE = 64

VMEM_LIMIT = 56 * 1024 * 1024


def _cparams(sem):
    return pltpu.CompilerParams(dimension_semantics=sem, vmem_limit_bytes=VMEM_LIMIT)


class Groups:
    def __init__(self, b1, s1, b2, s2):
        self.b1, self.s1, self.b2, self.s2 = b1, s1, b2, s2
        self.t1, self.t2 = b1 * s1, b2 * s2
        self.t = self.t1 + self.t2
        self.nb = b1 + b2

    def batch_of_block(self, i, tm):
        n1 = self.t1 // tm
        return jnp.where(i < n1, i // (self.s1 // tm), self.b1 + (i - n1) // (self.s2 // tm))

    def pos_block(self, i, tm):
        n1 = self.t1 // tm
        return jnp.where(i < n1, i % (self.s1 // tm), (i - n1) % (self.s2 // tm))


def _dot(a, b):
    return jnp.dot(a, b, preferred_element_type=F32)


def _dot_nt(a, b):
    return lax.dot_general(a, b, (((1,), (1,)), ((), ())), preferred_element_type=F32)


def _dot_tn(a, b):
    return lax.dot_general(a, b, (((0,), (0,)), ((), ())), preferred_element_type=F32)


def _split3(a):
    hi = a.astype(BF16)
    r1 = a - hi.astype(F32)
    mid = r1.astype(BF16)
    lo = (r1 - mid.astype(F32)).astype(BF16)
    return hi, mid, lo


def _rope(y, ct, s1, s2, half):
    return y * ct + pltpu.roll(y, LANES - half, 1) * s1 + pltpu.roll(y, half, 1) * s2


def _sigmoid(x):
    return 0.5 * jnp.tanh(0.5 * x) + 0.5


def _mod_kernel(c_ref, w_ref, b_ref, o_ref):
    c = c_ref[...]
    ca = c * _sigmoid(c)
    ch, cm, _ = _split3(ca)
    wh, wm, _ = _split3(w_ref[0])
    o_ref[0] = _dot(ch, wh) + _dot(cm, wh) + _dot(ch, wm) + b_ref[0]


def modulation(c_pad, w_mod, b_mod):
    nbp, d = c_pad.shape
    depth, _, n = w_mod.shape
    tn = 512
    return pl.pallas_call(
        _mod_kernel,
        grid=(depth, n // tn),
        in_specs=[pl.BlockSpec((nbp, d), lambda l, j: (0, 0)),
                  pl.BlockSpec((1, d, tn), lambda l, j: (l, 0, j)),
                  pl.BlockSpec((1, 1, tn), lambda l, j: (l, 0, j))],
        out_specs=pl.BlockSpec((1, nbp, tn), lambda l, j: (l, 0, j)),
        out_shape=jax.ShapeDtypeStruct((depth, nbp, n), F32),
        compiler_params=_cparams(("arbitrary", "arbitrary")),
        name="modulation",
    )(c_pad, w_mod, b_mod.reshape(depth, 1, n))


def _inproj_kernel(x_ref, mul_ref, sh_ref, w_ref, ct_ref, s1_ref, s2_ref, o_ref, od4_ref, od16_ref,
                   h_scr, y_scr, y4_scr, *, tm, tn, jq, jv):
    j = pl.program_id(1)

    @pl.when(j == 0)
    def _():
        x = x_ref[...]
        ms = jnp.mean(x * x, axis=-1, keepdims=True)
        h = x * lax.rsqrt(ms + EPS) * mul_ref[0] + sh_ref[0]
        h_scr[...] = h.astype(BF16)

    is_attn = (j >= jq) & (j <= jv)

    @pl.when(jnp.logical_not(is_attn))
    def _():
        o_ref[...] = _dot(h_scr[...], w_ref[j]).astype(BF16)

    @pl.when(is_attn)
    def _():
        y = _dot(h_scr[...], w_ref[j])
        ncol = tn // LANES

        @pl.when(j < jv)
        def _():
            ct, s1, s2 = ct_ref[...], s1_ref[...], s2_ref[...]
            for c in range(ncol):
                y_scr[c] = _rope(y[:, c * LANES:(c + 1) * LANES], ct, s1, s2, A_ROPE_DIM // 2)

        @pl.when(j == jv)
        def _():
            for c in range(ncol):
                y_scr[c] = y[:, c * LANES:(c + 1) * LANES]

        n4, n16 = tm // 4, tm // 16
        for c in range(ncol):
            cs = slice(c * LANES, (c + 1) * LANES)
            o_ref[:, cs] = y_scr[c].astype(BF16)
            for r in range(4):
                y4 = y_scr[c, pl.ds(r, n4, stride=4), :]
                y4_scr[c, r * n4:(r + 1) * n4, :] = y4
                od4_ref[:, r * tn + c * LANES:r * tn + (c + 1) * LANES] = y4.astype(BF16)
            for r in range(4):
                for r2 in range(4):
                    col = (r + 4 * r2) * tn + c * LANES
                    od16_ref[:, col:col + LANES] = y4_scr[c, pl.ds(r * n4 + r2, n16, stride=4), :].astype(BF16)


def in_projection(g, x, mul, shift, w_pad, tabs, tm):
    t, d = x.shape
    tn = A_WIDTH
    ct, s1, s2 = tabs
    jq, jv = OFF_AQ // tn, OFF_AV // tn
    bmap = lambda i, j: (g.batch_of_block(i, tm), 0, 0)
    pmap = lambda i, j: (g.pos_block(i, tm), 0)
    dmap = lambda i, j: (jnp.clip(j - jq, 0, jv - jq), i, 0)
    kern = functools.partial(_inproj_kernel, tm=tm, tn=tn, jq=jq, jv=jv)
    return pl.pallas_call(
        kern,
        grid=(t // tm, NPAD // tn),
        in_specs=[pl.BlockSpec((tm, d), lambda i, j: (i, 0)),
                  pl.BlockSpec((1, 1, d), bmap),
                  pl.BlockSpec((1, 1, d), bmap),
                  pl.BlockSpec((NPAD // tn, d, tn), lambda i, j: (0, 0, 0), pipeline_mode=pl.Buffered(1)),
                  pl.BlockSpec((tm, LANES), pmap),
                  pl.BlockSpec((tm, LANES), pmap),
                  pl.BlockSpec((tm, LANES), pmap)],
        out_specs=[pl.BlockSpec((tm, tn), lambda i, j: (i, j)),
                   pl.BlockSpec((None, tm // 4, 4 * tn), dmap),
                   pl.BlockSpec((None, tm // 16, 16 * tn), dmap)],
        out_shape=[jax.ShapeDtypeStruct((t, NPAD), BF16),
                   jax.ShapeDtypeStruct((3, t // 4, 4 * tn), BF16),
                   jax.ShapeDtypeStruct((3, t // 16, 16 * tn), BF16)],
        scratch_shapes=[pltpu.VMEM((tm, d), BF16), pltpu.VMEM((tn // LANES, tm, LANES), F32),
                        pltpu.VMEM((tn // LANES, tm, LANES), F32)],
        compiler_params=_cparams(("arbitrary", "arbitrary")),
        name="in_projection",
    )(x, mul, shift, w_pad.reshape(d, NPAD // tn, tn).transpose(1, 0, 2), ct, s1, s2)


def _band_kernel(q_ref, kp_ref, km_ref, kn_ref, vp_ref, vm_ref, vn_ref, o_ref, lse_ref, *, tb, tq, rows1, l1, l2):
    i = pl.program_id(1)
    tk = tq + 2 * A_HALF
    k = jnp.concatenate([kp_ref[...], km_ref[...], kn_ref[...]], axis=0)
    v = jnp.concatenate([vp_ref[...], vm_ref[...], vn_ref[...]], axis=0)
    qi = lax.broadcasted_iota(jnp.int32, (tq, tk), 0)
    kj = lax.broadcasted_iota(jnp.int32, (tq, tk), 1)
    band = jnp.abs(kj - A_HALF - qi) <= A_HALF
    lo = lax.broadcasted_iota(jnp.int32, (tq, LANES), 1) < A_HEAD_DIM
    scale = A_HEAD_DIM ** -0.5
    for u in range(tb // tq):
        row0 = i * tb + u * tq
        in1 = row0 < rows1
        seq_len = jnp.where(in1, l1, l2)
        pos0 = jnp.where(in1, row0 % l1, (row0 - rows1) % l2)
        kpos = pos0 - A_HALF + kj
        valid = band & (kpos >= 0) & (kpos < seq_len)
        valid2 = jnp.concatenate([valid, valid], axis=0)
        rows = slice(u * tq, (u + 1) * tq)
        krows = slice(u * tq, u * tq + tk)
        for p in range(A_WIDTH // LANES):
            sl = slice(p * LANES, (p + 1) * LANES)
            qp, kp, vp = q_ref[rows, sl] * scale, k[krows, sl], v[krows, sl]
            zero = jnp.zeros_like(qp)
            qm = jnp.concatenate([jnp.where(lo, qp, zero), jnp.where(lo, zero, qp)], axis=0)
            s = jnp.where(valid2, _dot_nt(qm, kp), NEG_INF)
            m = jnp.max(s, axis=1, keepdims=True)
            e = jnp.exp(s - m)
            l = jnp.sum(e, axis=1, keepdims=True)
            o2 = _dot(e.astype(BF16), vp) / l
            lse2 = m + jnp.log(l)
            o_ref[rows, sl] = jnp.where(lo, o2[:tq], o2[tq:]).astype(BF16)
            lse_ref[rows, sl] = jnp.where(lo, lse2[:tq], lse2[tq:])


def band_attention(g, qkv, dil, cols, cpb):
    rows = g.t // dil
    tq = 128
    tb = 256
    sub = tb // A_HALF
    nsub = rows // A_HALF

    def main(c):
        ld, col = cols[c]
        return pl.BlockSpec((None, tb, A_WIDTH), lambda r, i: (ld, i, r * cpb + col))

    def prev(c):
        ld, col = cols[c]
        return pl.BlockSpec((None, A_HALF, A_WIDTH), lambda r, i: (ld, jnp.maximum(i * sub - 1, 0), r * cpb + col))

    def nxt(c):
        ld, col = cols[c]
        return pl.BlockSpec((None, A_HALF, A_WIDTH),
                            lambda r, i: (ld, jnp.minimum((i + 1) * sub, nsub - 1), r * cpb + col))

    assert (g.s1 // dil) % tq == 0 and (g.s2 // dil) % tq == 0 and rows % tb == 0
    kern = functools.partial(_band_kernel, tb=tb, tq=tq, rows1=g.t1 // dil, l1=g.s1 // dil, l2=g.s2 // dil)
    return pl.pallas_call(
        kern,
        grid=(dil, rows // tb),
        in_specs=[main(0), prev(1), main(1), nxt(1), prev(2), main(2), nxt(2)],
        out_specs=[pl.BlockSpec((tb, A_WIDTH), lambda r, i: (i, r)),
                   pl.BlockSpec((tb, A_WIDTH), lambda r, i: (i, r))],
        out_shape=[jax.ShapeDtypeStruct((rows, dil * A_WIDTH), BF16),
                   jax.ShapeDtypeStruct((rows, dil * A_WIDTH), F32)],
        compiler_params=_cparams(("arbitrary", "arbitrary")),
        name="band_attention_d%d" % dil,
    )(qkv, qkv, qkv, qkv, qkv, qkv, qkv)


def _gla_kernel(*refs, reverse, final, tc, nblk, t1, s1, s2):
    if final:
        q_ref, k_ref, v_ref, z_ref, wa_ref, ba_ref, tri_ref, ob_ref, r_ref, gain_ref, o_ref, st_scr = refs
    else:
        q_ref, k_ref, v_ref, z_ref, wa_ref, ba_ref, tri_ref, o_ref, st_scr = refs
    i = pl.program_id(0)
    blk = (nblk - 1 - i) if reverse else i
    row0 = blk * tc
    in1 = row0 < t1
    pos0 = jnp.where(in1, row0 % s1, (row0 - t1) % s2)
    slen = jnp.where(in1, s1, s2)
    start = (pos0 + tc == slen) if reverse else (pos0 == 0)

    @pl.when(start)
    def _():
        st_scr[...] = jnp.zeros_like(st_scr)

    zl = _dot(z_ref[...], wa_ref[...]) + ba_ref[...]
    la = (jnp.minimum(zl, 0.0) - jnp.log(1.0 + jnp.exp(-jnp.abs(zl)))) * (1.0 / B_GATE_TAU)
    hi, mid, _ = _split3(la)
    tri = tri_ref[...]
    bc_all = _dot(tri, hi) + _dot(tri, mid)

    qi = lax.broadcasted_iota(jnp.int32, (B_CHUNK, B_CHUNK), 0)
    si = lax.broadcasted_iota(jnp.int32, (B_CHUNK, B_CHUNK), 1)
    mask = (si > qi) if reverse else (si <= qi)
    nch = tc // B_CHUNK
    q_all = q_ref[...].astype(F32) * (B_KEY_DIM ** -0.5)
    k_all = k_ref[...].astype(F32)
    q_dec = (q_all * jnp.exp(bc_all)).astype(BF16)
    k_inv = (k_all * jnp.exp(-bc_all)).astype(BF16)
    st = [st_scr[h] for h in range(B_HEADS)]
    for c in (range(nch - 1, -1, -1) if reverse else range(nch)):
        sl = slice(c * B_CHUNK, (c + 1) * B_CHUNK)
        edge = c * B_CHUNK if reverse else (c + 1) * B_CHUNK - 1
        tot = bc_all[edge:edge + 1]
        k_end = (k_all[sl] * jnp.exp(tot - bc_all[sl])).astype(BF16)
        dec = jnp.exp(tot)
        for h in range(B_HEADS):
            ks = slice(h * B_KEY_DIM, (h + 1) * B_KEY_DIM)
            vs = slice(h * B_VAL_DIM, (h + 1) * B_VAL_DIM)
            v = v_ref[sl, vs]
            att = jnp.where(mask, _dot_nt(q_dec[sl, ks], k_inv[sl, ks]), 0.0).astype(BF16)
            o = _dot(att, v) + _dot_nt(q_dec[sl, ks], st[h].astype(BF16))
            st[h] = st[h] * dec[:, ks] + _dot_tn(v, k_end[:, ks])
            if final:
                o = o + ob_ref[sl, vs]
                on = o * lax.rsqrt(jnp.mean(o * o, axis=-1, keepdims=True) + EPS) * gain_ref[...]
                r = r_ref[sl, vs].astype(F32)
                o_ref[sl, vs] = (on * (r * _sigmoid(r))).astype(BF16)
            else:
                o_ref[sl, vs] = o
    for h in range(B_HEADS):
        st_scr[h] = st[h]


def gla_pass(g, y, wa, ba, reverse, o_back=None, gain=None):
    t = g.t
    tc = 256
    nblk = t // tc
    final = o_back is not None
    rowmap = (lambda i: nblk - 1 - i) if reverse else (lambda i: i)
    row = lambda w, off: pl.BlockSpec((tc, w), lambda i: (rowmap(i), off // w))
    full = lambda a: pl.BlockSpec(a.shape, lambda i: (0,) * a.ndim)
    ri = jnp.arange(tc)[:, None]
    ci = jnp.arange(tc)[None, :]
    tri = ((ri // B_CHUNK == ci // B_CHUNK) & ((ci >= ri) if reverse else (ci <= ri))).astype(BF16)
    in_specs = [row(B_KW, OFF_BQ), row(B_KW, OFF_BK), row(B_VW, OFF_BV), row(LANES, OFF_BZ),
                full(wa), full(ba), full(tri)]
    args = [y, y, y, y, wa, ba, tri]
    if final:
        in_specs += [row(B_VW, 0), row(B_VW, OFF_BR), full(gain)]
        args += [o_back, y, gain]
    kern = functools.partial(_gla_kernel, reverse=reverse, final=final, tc=tc, nblk=nblk, t1=g.t1, s1=g.s1, s2=g.s2)
    return pl.pallas_call(
        kern,
        grid=(nblk,),
        in_specs=in_specs,
        out_specs=row(B_VW, 0),
        out_shape=jax.ShapeDtypeStruct((t, B_VW), BF16 if final else F32),
        scratch_shapes=[pltpu.VMEM((B_HEADS, B_VAL_DIM, B_KEY_DIM), F32)],
        compiler_params=_cparams(("arbitrary",)),
        name="gla_forward_final" if final else "gla_backward",
    )(*args)


def _mla_prep_kernel(ckv_ref, cq_ref, kr_ref, qn_ref, kvn_ref, wq_ref, wk_ref, wv_ref, ct_ref, s1_ref, s2_ref,
                     q_out, k_out, v_out):
    def norm(ref, gain_ref):
        xf = ref[...].astype(F32)
        return (xf * lax.rsqrt(jnp.mean(xf * xf, axis=-1, keepdims=True) + EPS) * gain_ref[...]).astype(BF16)

    ckv_n = norm(ckv_ref, kvn_ref)
    cq_n = norm(cq_ref, qn_ref)
    q = _dot(cq_n, wq_ref[...])
    kn = _dot(ckv_n, wk_ref[...])
    v = _dot(ckv_n, wv_ref[...])
    ct, s1, s2 = ct_ref[...], s1_ref[...], s2_ref[...]
    half = C_ROPE_DIM // 2
    kr_rot = _rope(kr_ref[...].astype(F32), ct, s1, s2, half)
    ones_lane = lax.broadcasted_iota(jnp.int32, kr_rot.shape, 1) == C_V_DIM
    scale = C_QK_DIM ** -0.5 * LOG2E
    for h in range(C_HEADS):
        sl = slice(h * C_HEAD_PAD, (h + 1) * C_HEAD_PAD)
        q_out[:, sl] = (_rope(q[:, sl], ct, s1, s2, half) * scale).astype(BF16)
        k_out[:, sl] = (kn[:, sl] + kr_rot).astype(BF16)
        v_out[:, sl] = jnp.where(ones_lane, 1.0, v[:, sl]).astype(BF16)


def mla_prep(g, y, qn, kvn, wq, wk, wv, tabs, tm):
    t = g.t
    ct, s1, s2 = tabs
    pmap = lambda i: (g.pos_block(i, tm), 0)
    full = lambda a: pl.BlockSpec(a.shape, lambda i: (0,) * a.ndim)
    out = jax.ShapeDtypeStruct((t, C_PAD_WIDTH), BF16)
    ospec = pl.BlockSpec((tm, C_PAD_WIDTH), lambda i: (i, 0))
    return pl.pallas_call(
        _mla_prep_kernel,
        grid=(t // tm,),
        in_specs=[pl.BlockSpec((tm, C_KV_RANK), lambda i: (i, OFF_CKV // C_KV_RANK)),
                  pl.BlockSpec((tm, C_Q_RANK), lambda i: (i, OFF_CQ // C_Q_RANK)),
                  pl.BlockSpec((tm, LANES), lambda i: (i, OFF_KR // LANES)),
                  full(qn), full(kvn), full(wq), full(wk), full(wv),
                  pl.BlockSpec((tm, LANES), pmap), pl.BlockSpec((tm, LANES), pmap), pl.BlockSpec((tm, LANES), pmap)],
        out_specs=[ospec, ospec, ospec],
        out_shape=[out, out, out],
        compiler_params=_cparams(("arbitrary",)),
        name="mla_prep",
    )(y, y, y, qn, kvn, wq, wk, wv, ct, s1, s2)


def _flash_kernel(q_ref, k_ref, v_ref, o_ref, s0_scr, s1_scr, p_scr, m_scr, a_scr, acc_scr, *, tk, nk, rc, unroll):
    tq = q_ref.shape[0]

    def logits(t, scr):
        off = pl.multiple_of(t * tk, tk)
        scr[...] = _dot_nt(q_ref[...], k_ref[pl.ds(off, tk), :])

    def softmax_pv(t, scr):
        off = pl.multiple_of(t * tk, tk)
        for r in range(tq // rc):
            rows = slice(r * rc, (r + 1) * rc)
            s = scr[rows, :]
            m_old = m_scr[rows, :]
            m_new = jnp.maximum(m_old, jnp.max(s, axis=1, keepdims=True))
            p_scr[rows, :] = jnp.exp2(s - jnp.concatenate([m_new] * (tk // LANES), axis=1)).astype(BF16)
            a_scr[rows, :] = jnp.exp2(m_old - m_new)
            m_scr[rows, :] = m_new
        acc_scr[...] = acc_scr[...] * a_scr[...] + _dot(p_scr[...], v_ref[pl.ds(off, tk), :])

    bufs = (s0_scr, s1_scr)

    def group(t0, last):
        for u in range(unroll):
            if not (last and u == unroll - 1):
                logits(t0 + u + 1, bufs[(u + 1) % 2])
            softmax_pv(t0 + u, bufs[u % 2])

    def body(jj, carry):
        group(unroll * jj, False)
        return carry

    m_scr[...] = jnp.full(m_scr.shape, NEG_INF, F32)
    acc_scr[...] = jnp.zeros(acc_scr.shape, F32)
    logits(0, s0_scr)
    lax.fori_loop(0, nk // unroll - 1, body, 0)
    group(nk - unroll, True)
    acc = acc_scr[...]
    o_ref[...] = (acc / acc[:, C_V_DIM:C_V_DIM + 1]).astype(BF16)


def mla_flash(q, k, v, row_off, nseq, s, tq, tk):
    assert row_off % s == 0 and row_off % tq == 0
    qb0 = row_off // tq
    sb0 = row_off // s
    nq = s // tq
    nk = s // tk
    unroll = 4 if nk % 4 == 0 else 2
    assert nk % unroll == 0
    kern = functools.partial(_flash_kernel, tk=tk, nk=nk, rc=64, unroll=unroll)
    return pl.pallas_call(
        kern,
        grid=(nseq, C_HEADS, nq),
        scratch_shapes=[pltpu.VMEM((tq, tk), F32), pltpu.VMEM((tq, tk), F32), pltpu.VMEM((tq, tk), BF16),
                        pltpu.VMEM((tq, LANES), F32), pltpu.VMEM((tq, LANES), F32),
                        pltpu.VMEM((tq, C_HEAD_PAD), F32)],
        in_specs=[pl.BlockSpec((tq, C_HEAD_PAD), lambda b, h, i: (qb0 + b * nq + i, h)),
                  pl.BlockSpec((s, C_HEAD_PAD), lambda b, h, i: (sb0 + b, h)),
                  pl.BlockSpec((s, C_HEAD_PAD), lambda b, h, i: (sb0 + b, h))],
        out_specs=pl.BlockSpec((tq, C_HEAD_PAD), lambda b, h, i: (b * nq + i, h)),
        out_shape=jax.ShapeDtypeStruct((nseq * s, C_PAD_WIDTH), BF16),
        compiler_params=_cparams(("arbitrary", "arbitrary", "arbitrary")),
        name="mla_flash_s%d" % s,
    )(q, k, v)


def _store_row_tiles(ref, x, rows):
    for c in range(ROW_TILE):
        ref[pl.ds(c, rows, stride=ROW_TILE), :] = x[:, c * LANES:(c + 1) * LANES]


def _load_row_tiles(ref, rows):
    return jnp.concatenate([ref[pl.ds(c, rows, stride=ROW_TILE), :] for c in range(ROW_TILE)], axis=1)


def _router_scores(h, wrh_ref, wrm_ref):
    hh, hm, _ = _split3(h)
    return _sigmoid(_dot(hh, wrh_ref[...]) + _dot(hm, wrh_ref[...]) + _dot(hh, wrm_ref[...]))


def _rank_in_group(sel, pos):
    rank = jnp.zeros(sel.shape, jnp.int32)
    for k in range(1, EXPERTS_PER_GROUP):
        below = pltpu.roll(sel, k, 1)
        above = pltpu.roll(sel, LANES - k, 1)
        rank += jnp.where((pos >= k) & (below >= sel), 1, 0)
        rank += jnp.where((pos + k < EXPERTS_PER_GROUP) & (above > sel), 1, 0)
    return rank


def _route_in_group(s, bias, g):
    lane = lax.broadcasted_iota(jnp.int32, s.shape, 1)
    rank = _rank_in_group(s + bias, lane % EXPERTS_PER_GROUP)
    chosen = (lane // EXPERTS_PER_GROUP == g) & (rank < TOP_K)
    total = jnp.sum(jnp.where(chosen, s, 0.0), axis=1, keepdims=True)
    return jnp.where(chosen, s / total, 0.0)


def _route(s, bias):
    lane = lax.broadcasted_iota(jnp.int32, s.shape, 1)
    pos = lane % EXPERTS_PER_GROUP
    grp = lane // EXPERTS_PER_GROUP
    sel = s + bias
    rank = _rank_in_group(sel, pos)
    top = rank < GROUP_SCORE_K
    contrib = jnp.where(top, sel, 0.0)
    score = contrib
    for k in range(1, EXPERTS_PER_GROUP):
        score += jnp.where(pos >= k, pltpu.roll(contrib, k, 1), 0.0)
        score += jnp.where(pos + k < EXPERTS_PER_GROUP, pltpu.roll(contrib, LANES - k, 1), 0.0)
    best = lane < N_EXPERTS
    for k in range(1, N_GROUPS):
        earlier = pltpu.roll(score, k * EXPERTS_PER_GROUP, 1)
        later = pltpu.roll(score, LANES - k * EXPERTS_PER_GROUP, 1)
        best &= jnp.logical_not((grp >= k) & (earlier >= score))
        best &= jnp.logical_not((grp + k < N_GROUPS) & (later > score))
    chosen = best & (rank < TOP_K)
    total = jnp.sum(jnp.where(chosen, s, 0.0), axis=1, keepdims=True)
    return jnp.where(chosen, s / total, 0.0)


def _proj_kernel(o1_ref, o2_ref, o3_ref, l1_ref, l2_ref, l3_ref, yb_ref, yc1_ref, yc2_ref, ga_ref, gb_ref, gc_ref,
                 x_ref, g1_ref, mul2_ref, sh2_ref, wpa_ref, wpb_ref, wpc_ref, wout_ref, wrh_ref, wrm_ref, br_ref,
                 x_out, h_out, comb_out, o2_scr, l2_scr, o3_scr, l3_scr, *, tm, n1):
    for dil, o_ref, l_ref, o_scr, l_scr in ((A_PATTERNS[1][1], o2_ref, l2_ref, o2_scr, l2_scr),
                                           (A_PATTERNS[2][1], o3_ref, l3_ref, o3_scr, l3_scr)):
        for r in range(dil):
            for c in range(A_WIDTH // LANES):
                sl = slice(r * A_WIDTH + c * LANES, r * A_WIDTH + (c + 1) * LANES)
                o_scr[c, pl.ds(r, tm // dil, stride=dil), :] = o_ref[:, sl].astype(F32)
                l_scr[c, pl.ds(r, tm // dil, stride=dil), :] = l_ref[:, sl]
    slabs = lambda scr: jnp.concatenate([scr[c] for c in range(A_WIDTH // LANES)], axis=1)
    l1, l2, l3 = l1_ref[...], slabs(l2_scr), slabs(l3_scr)
    m = jnp.maximum(jnp.maximum(l1, l2), l3)
    e1, e2, e3 = jnp.exp(l1 - m), jnp.exp(l2 - m), jnp.exp(l3 - m)
    ya = (e1 * o1_ref[...].astype(F32) + e2 * slabs(o2_scr) + e3 * slabs(o3_scr)) / (e1 + e2 + e3)
    sig = lambda ref: _sigmoid(ref[...].astype(F32))
    yc = jnp.where(pl.program_id(0) < n1, yc1_ref[...], yc2_ref[...])
    merged = (sig(ga_ref) * _dot(ya.astype(BF16), wpa_ref[...])
              + sig(gb_ref) * _dot(yb_ref[...], wpb_ref[...])
              + sig(gc_ref) * _dot(yc, wpc_ref[...]))
    out = _dot(merged.astype(BF16), wout_ref[...])
    x = x_ref[...] + g1_ref[0] * out
    x_out[...] = x
    h = x * lax.rsqrt(jnp.mean(x * x, axis=-1, keepdims=True) + EPS) * mul2_ref[0] + sh2_ref[0]
    _store_row_tiles(h_out, h, tm)
    comb_out[...] = _route(_router_scores(h, wrh_ref, wrm_ref), br_ref[...])


def proj_merge(g, o_list, lse_list, yb, yc1, yc2, y, x, gate1, mul2, sh2, wpa, wpb, wpc, wout, wrh, wrm, br, tm):
    t, d = x.shape
    n1 = g.t1 // tm
    n2 = g.t2 // tm
    bmap = lambda i: (g.batch_of_block(i, tm), 0, 0)
    row = lambda w, col=0: pl.BlockSpec((tm, w), lambda i: (i, col))
    dil_row = lambda dil: pl.BlockSpec((tm // dil, dil * A_WIDTH), lambda i: (i, 0))
    full = lambda a: pl.BlockSpec(a.shape, lambda i: (0,) * a.ndim)
    gcol = OFF_GATES // d
    d2, d3 = A_PATTERNS[1][1], A_PATTERNS[2][1]
    attn_specs = [row(A_WIDTH), dil_row(d2), dil_row(d3)]
    return pl.pallas_call(
        functools.partial(_proj_kernel, tm=tm, n1=n1),
        grid=(t // tm,),
        in_specs=attn_specs + attn_specs + [
            row(B_VW),
            pl.BlockSpec((tm, C_PAD_WIDTH), lambda i: (jnp.minimum(i, n1 - 1), 0)),
            pl.BlockSpec((tm, C_PAD_WIDTH), lambda i: (jnp.clip(i - n1, 0, n2 - 1), 0)),
            row(d, gcol), row(d, gcol + 1), row(d, gcol + 2),
            row(d), pl.BlockSpec((1, 1, d), bmap), pl.BlockSpec((1, 1, d), bmap), pl.BlockSpec((1, 1, d), bmap),
            full(wpa), full(wpb), full(wpc), full(wout), full(wrh), full(wrm), full(br)],
        out_specs=[row(d), pl.BlockSpec((tm * ROW_TILE, LANES), lambda i: (i, 0)), row(LANES)],
        out_shape=[jax.ShapeDtypeStruct((t, d), F32), jax.ShapeDtypeStruct((t * ROW_TILE, LANES), F32),
                   jax.ShapeDtypeStruct((t, LANES), F32)],
        scratch_shapes=[pltpu.VMEM((A_WIDTH // LANES, tm, LANES), F32)] * 4,
        compiler_params=_cparams(("arbitrary",)),
        name="proj_merge_route",
    )(*o_list, *lse_list, yb, yc1, yc2, y, y, y, x, gate1, mul2, sh2, wpa, wpb, wpc, wout, wrh, wrm, br)


def _sorted_layout(comb, tmoe):
    t = comb.shape[0]
    gw = comb[:, :N_EXPERTS].reshape(t, N_GROUPS, EXPERTS_PER_GROUP).sum(-1)
    gid = jnp.argmax(gw, axis=-1).astype(jnp.int32)
    onehot = (gid[:, None] == jnp.arange(N_GROUPS, dtype=jnp.int32)[None, :]).astype(jnp.int32)
    incl = jnp.cumsum(onehot, axis=0)
    counts = incl[-1]
    padded = (counts + tmoe - 1) // tmoe * tmoe
    ends = jnp.cumsum(padded)
    dest = jnp.sum(onehot * (incl - onehot + (ends - padded)[None, :]), axis=1).astype(jnp.int32)
    ntiles = t // tmoe + N_GROUPS
    tile_group = jnp.sum((jnp.arange(ntiles, dtype=jnp.int32) * tmoe)[:, None] >= ends[None, :], axis=1)
    tile_group = jnp.where(tile_group < N_GROUPS, tile_group, -1).astype(jnp.int32)
    return dest, tile_group, ntiles


def _row_copy(src, src_row, dst, dst_row, sem):
    return pltpu.make_async_copy(src.at[pl.ds(src_row * ROW_TILE, ROW_TILE), :],
                                 dst.at[pl.ds(dst_row * ROW_TILE, ROW_TILE), :], sem)


def _scatter_rows_kernel(dest_ref, h_hbm, init_hbm, hs_hbm, sem, *, tm):
    del init_hbm
    i = pl.program_id(0)

    def issue(r, c):
        _row_copy(h_hbm, i * tm + r, hs_hbm, dest_ref[i * tm + r], sem).start()
        return c

    def drain(r, c):
        _row_copy(h_hbm, 0, hs_hbm, 0, sem).wait()
        return c

    lax.fori_loop(0, tm, issue, 0, unroll=8)
    lax.fori_loop(0, tm, drain, 0, unroll=8)


def scatter_rows(dest, h_tiles, rows_out, tm):
    t = dest.shape[0]
    init = jnp.zeros((rows_out * ROW_TILE, LANES), F32)
    return pl.pallas_call(
        functools.partial(_scatter_rows_kernel, tm=tm),
        grid_spec=pltpu.PrefetchScalarGridSpec(
            num_scalar_prefetch=1,
            grid=(t // tm,),
            in_specs=[pl.BlockSpec(memory_space=pl.ANY), pl.BlockSpec(memory_space=pl.ANY)],
            out_specs=pl.BlockSpec(memory_space=pl.ANY),
            scratch_shapes=[pltpu.SemaphoreType.DMA(())]),
        out_shape=jax.ShapeDtypeStruct(init.shape, F32),
        input_output_aliases={2: 0},
        compiler_params=_cparams(("arbitrary",)),
        name="moe_scatter_rows",
    )(dest, h_tiles, init)


def _moe_group_kernel(tg_ref, hs_ref, wrh_ref, wrm_ref, br_ref, wg_ref, wu_ref, wd_ref, y_ref,
                      h_scr, comb_scr, acc_scr, *, tm):
    j = pl.program_id(0)
    k = pl.program_id(1)
    g = tg_ref[j]

    @pl.when(k == 0)
    def _():
        h = _load_row_tiles(hs_ref, tm)
        h_scr[...] = h.astype(BF16)
        comb_scr[...] = _route_in_group(_router_scores(h, wrh_ref, wrm_ref), br_ref[...], g)
        acc_scr[...] = jnp.zeros_like(acc_scr)

    @pl.when(g >= 0)
    def _():
        h = h_scr[...]
        gate = _dot(h, wg_ref[0])
        a = gate * _sigmoid(gate) * _dot(h, wu_ref[0])
        yv = _dot(a.astype(BF16), wd_ref[0])
        comb = comb_scr[...]
        lane = lax.broadcasted_iota(jnp.int32, comb.shape, 1)
        ce = jnp.sum(jnp.where(lane == g * EXPERTS_PER_GROUP + k, comb, 0.0), axis=1, keepdims=True)
        acc_scr[...] += ce * yv

    @pl.when(k == EXPERTS_PER_GROUP - 1)
    def _():
        _store_row_tiles(y_ref, acc_scr[...], tm)


def moe_grouped(tile_group, ntiles, hs, wrh, wrm, br, wg, wu, wd, tm):
    d = D_MODEL
    full = lambda a: pl.BlockSpec(a.shape, lambda j, k, tg: (0,) * a.ndim)
    emap = lambda j, k, tg: (jnp.maximum(tg[j], 0) * EXPERTS_PER_GROUP + k, 0, 0)
    return pl.pallas_call(
        functools.partial(_moe_group_kernel, tm=tm),
        grid_spec=pltpu.PrefetchScalarGridSpec(
            num_scalar_prefetch=1,
            grid=(ntiles, EXPERTS_PER_GROUP),
            in_specs=[pl.BlockSpec((tm * ROW_TILE, LANES), lambda j, k, tg: (j, 0)),
                      full(wrh), full(wrm), full(br),
                      pl.BlockSpec((1, d, D_EXPERT), emap),
                      pl.BlockSpec((1, d, D_EXPERT), emap),
                      pl.BlockSpec((1, D_EXPERT, d), emap)],
            out_specs=pl.BlockSpec((tm * ROW_TILE, LANES), lambda j, k, tg: (j, 0)),
            scratch_shapes=[pltpu.VMEM((tm, d), BF16), pltpu.VMEM((tm, LANES), F32), pltpu.VMEM((tm, d), F32)]),
        out_shape=jax.ShapeDtypeStruct(hs.shape, F32),
        compiler_params=_cparams(("arbitrary", "arbitrary")),
        name="moe_grouped",
    )(tile_group, hs, wrh, wrm, br, wg, wu, wd)


def _gather_residual_kernel(dest_ref, y_hbm, x_ref, g2_ref, fn_ref, o_ref, buf, sem, *, tm, nsteps, final):
    i = pl.program_id(0)

    def fetch(step, slot):
        def issue(r, c):
            _row_copy(y_hbm, dest_ref[step * tm + r], buf.at[slot], r, sem.at[slot]).start()
            return c
        lax.fori_loop(0, tm, issue, 0, unroll=8)

    @pl.when(i == 0)
    def _():
        fetch(0, 0)

    @pl.when(i + 1 < nsteps)
    def _():
        fetch(i + 1, (i + 1) % 2)

    slot = i % 2

    def drain(r, c):
        _row_copy(y_hbm, 0, buf.at[slot], 0, sem.at[slot]).wait()
        return c

    lax.fori_loop(0, tm, drain, 0, unroll=8)
    x = x_ref[...] + g2_ref[0] * _load_row_tiles(buf.at[slot], tm)
    if final:
        x = x * lax.rsqrt(jnp.mean(x * x, axis=-1, keepdims=True) + EPS) * fn_ref[...]
    o_ref[...] = x


def gather_residual(g, dest, y_tiles, x, gate2, fnorm, final, tm):
    t, d = x.shape
    nsteps = t // tm
    kern = functools.partial(_gather_residual_kernel, tm=tm, nsteps=nsteps, final=final)
    return pl.pallas_call(
        kern,
        grid_spec=pltpu.PrefetchScalarGridSpec(
            num_scalar_prefetch=1,
            grid=(nsteps,),
            in_specs=[pl.BlockSpec(memory_space=pl.ANY),
                      pl.BlockSpec((tm, d), lambda i, dst: (i, 0)),
                      pl.BlockSpec((1, 1, d), lambda i, dst: (g.batch_of_block(i, tm), 0, 0)),
                      pl.BlockSpec((1, d), lambda i, dst: (0, 0))],
            out_specs=pl.BlockSpec((tm, d), lambda i, dst: (i, 0)),
            scratch_shapes=[pltpu.VMEM((2, tm * ROW_TILE, LANES), F32), pltpu.SemaphoreType.DMA((2,))]),
        out_shape=jax.ShapeDtypeStruct((t, d), F32),
        compiler_params=_cparams(("arbitrary",)),
        name="moe_gather_residual",
    )(dest, y_tiles, x, gate2, fnorm)


def _rope_tables(smax, rot_dim, period, lane_off):
    half = rot_dim // 2
    inv_freq = ROPE_THETA ** (-jnp.arange(half, dtype=F32) / half)
    ang = jnp.arange(smax).astype(F32)[:, None] * inv_freq[None, :]
    cos, sin = jnp.cos(ang), jnp.sin(ang)
    gl = (jnp.arange(LANES) % period) - lane_off
    first = (gl >= 0) & (gl < half)
    second = (gl >= half) & (gl < rot_dim)
    j = jnp.clip(jnp.where(first, gl, gl - half), 0, half - 1)
    cl, sn = cos[:, j], sin[:, j]
    ct = jnp.where((first | second)[None, :], cl, 1.0)
    s1 = jnp.where(first[None, :], -sn, 0.0)
    s2 = jnp.where(second[None, :], sn, 0.0)
    return ct, s1, s2


def _pad_w_in(w):
    d = w.shape[0]
    parts, off = [], 0
    for sz in IN_SIZES:
        parts.append(w[:, off:off + sz])
        off += sz
    a_q, a_k, a_v, b_q, b_k, b_v, b_r, b_zf, b_zb, c_cq, c_ckv, c_kr, gates = parts
    z = lambda n: jnp.zeros((d, n), w.dtype)
    bz = jnp.concatenate([b_zf, b_zb, z(LANES - 2 * B_GATE_RANK)], axis=1)
    kr = jnp.concatenate([z(KR_LANE), c_kr, z(LANES - KR_LANE - C_ROPE_DIM)], axis=1)
    out = jnp.concatenate([gates, b_v, b_r, a_q, a_k, a_v, b_q, b_k, c_ckv, bz, c_cq, kr, z(LANES)], axis=1)
    assert out.shape[1] == NPAD
    return out.astype(BF16)


def _pad_heads_cols(w, real, take_lo, take_hi):
    kdim = w.shape[0]
    wh = w.reshape(kdim, C_HEADS, real)[:, :, take_lo:take_hi]
    wh = jnp.pad(wh, ((0, 0), (0, 0), (0, C_HEAD_PAD - (take_hi - take_lo))))
    return wh.reshape(kdim, C_PAD_WIDTH).astype(BF16)


def _gate_weights(w_a, b_a, row_off):
    wa = jnp.zeros((LANES, B_KW), F32).at[row_off:row_off + B_GATE_RANK].set(w_a)
    return wa.astype(BF16), b_a.reshape(1, B_KW).astype(F32)


def kernel(x_prompt, x_sample, c_prompt, c_sample, norm_mix, norm_moe, w_mod, b_mod, w_in, w_gla_af, b_gla_af,
           w_gla_ab, b_gla_ab, gla_norm, mla_q_norm, w_mla_uq, mla_kv_norm, w_mla_ukv, w_proj_a, w_proj_b, w_proj_c,
           w_out, w_router, b_router, w_exp_gate, w_exp_up, w_exp_down, final_norm):
    b1, s1, d = x_prompt.shape
    b2, s2, _ = x_sample.shape
    g = Groups(b1, s1, b2, s2)
    depth = w_in.shape[0]
    tm_in = min(1024, s1, s2)
    tm_prep = min(512, s1, s2)
    tm_proj = min(512, s1, s2)
    tm_moe = min(1024, s1, s2)

    x = jnp.concatenate([x_prompt.reshape(g.t1, d), x_sample.reshape(g.t2, d)], axis=0)
    c = jnp.concatenate([c_prompt, c_sample], axis=0)
    nbp = -(-g.nb // 8) * 8
    c_pad = jnp.pad(c, ((0, nbp - g.nb), (0, 0)))
    mod = modulation(c_pad, w_mod, b_mod)[:, :g.nb]

    smax = max(s1, s2)
    tabs_a = _rope_tables(smax, A_ROPE_DIM, A_HEAD_DIM, 0)
    tabs_c = _rope_tables(smax, C_ROPE_DIM, C_HEAD_PAD, KR_LANE)

    wr_hi = jnp.pad(w_router, ((0, 0), (0, LANES - N_EXPERTS)))
    wrh = wr_hi.astype(BF16)
    wrm = (wr_hi - wrh.astype(F32)).astype(BF16)
    fnorm = final_norm.reshape(1, d)
    br = jnp.pad(b_router.astype(F32), (0, LANES - N_EXPERTS)).reshape(1, LANES)

    for l in range(depth):
        sh1, sc1, gt1, sh2, sc2, gt2 = [m.reshape(g.nb, 1, d) for m in jnp.split(mod[l], N_MOD, axis=-1)]
        mul1 = norm_mix[l][None, None, :] * (1.0 + sc1)
        mul2 = norm_moe[l][None, None, :] * (1.0 + sc2)

        y, yd4, yd16 = in_projection(g, x, mul1, sh1, _pad_w_in(w_in[l]), tabs_a, tm_in)

        nat_cols = tuple((0, off // A_WIDTH) for off in (OFF_AQ, OFF_AK, OFF_AV))
        dil_cols = ((0, 0), (1, 0), (2, 0))
        o_list, lse_list = [], []
        for (_, dil), src in zip(A_PATTERNS, (y.reshape(1, g.t, NPAD), yd4, yd16)):
            o, lse = band_attention(g, src, dil, nat_cols if dil == 1 else dil_cols, NPAD // A_WIDTH if dil == 1 else 1)
            o_list.append(o)
            lse_list.append(lse)

        waf, baf = _gate_weights(w_gla_af[l], b_gla_af[l], 0)
        wab, bab = _gate_weights(w_gla_ab[l], b_gla_ab[l], B_GATE_RANK)
        o_back = gla_pass(g, y, wab, bab, reverse=True)
        yb = gla_pass(g, y, waf, baf, reverse=False, o_back=o_back, gain=gla_norm[l].reshape(1, B_VAL_DIM))

        wq = jnp.concatenate(
            [w_mla_uq[l].reshape(C_Q_RANK, C_HEADS, C_QK_DIM),
             jnp.zeros((C_Q_RANK, C_HEADS, C_HEAD_PAD - C_QK_DIM), F32)], axis=2
        ).reshape(C_Q_RANK, C_PAD_WIDTH).astype(BF16)
        wk = _pad_heads_cols(w_mla_ukv[l], C_NOPE_DIM + C_V_DIM, 0, C_NOPE_DIM)
        wv = _pad_heads_cols(w_mla_ukv[l], C_NOPE_DIM + C_V_DIM, C_NOPE_DIM, C_NOPE_DIM + C_V_DIM)
        qm, km, vm = mla_prep(g, y, mla_q_norm[l].reshape(1, C_Q_RANK), mla_kv_norm[l].reshape(1, C_KV_RANK),
                              wq, wk, wv, tabs_c, tm_prep)
        yc1 = mla_flash(qm, km, vm, 0, b1, s1, min(512, s1), min(512, s1))
        yc2 = mla_flash(qm, km, vm, g.t1, b2, s2, min(512, s2), min(1024, s2))

        wpc = jnp.pad(w_proj_c[l].reshape(C_HEADS, C_V_DIM, d), ((0, 0), (0, C_HEAD_PAD - C_V_DIM), (0, 0)))
        wpc = wpc.reshape(C_PAD_WIDTH, d).astype(BF16)
        x, h2, comb = proj_merge(g, o_list, lse_list, yb, yc1, yc2, y, x, gt1, mul2, sh2,
                                 w_proj_a[l].astype(BF16), w_proj_b[l].astype(BF16), wpc, w_out[l].astype(BF16),
                                 wrh, wrm, br, tm_proj)

        dest, tile_group, ntiles = _sorted_layout(comb, tm_moe)
        hs = scatter_rows(dest, h2, ntiles * tm_moe, tm_moe)
        ys = moe_grouped(tile_group, ntiles, hs, wrh, wrm, br, w_exp_gate[l].astype(BF16),
                         w_exp_up[l].astype(BF16), w_exp_down[l].astype(BF16), tm_moe)
        x = gather_residual(g, dest, ys, x, gt2, fnorm, l == depth - 1, tm_proj)

    return x[:g.t1].reshape(b1, s1, d), x[g.t1:].reshape(b2, s2, d)
```

```python
import functools

import jax
import jax.numpy as jnp
from jax import lax
from jax.experimental import pallas as pl
from jax.experimental.pallas import tpu as pltpu

F32 = jnp.float32
BF16 = jnp.bfloat16

D_MODEL = 1024
DEPTH = 2
EPS = 1e-6
ROPE_THETA = 500000.0
NEG_INF = -1e30

A_HEADS = 8
A_HEAD_DIM = 64
A_ROPE_DIM = A_HEAD_DIM // 4
A_PATTERNS = ((128, 1), (512, 4), (2048, 16))
A_WIDTH = A_HEADS * A_HEAD_DIM
A_HALF = 64

B_HEADS = 4
B_KEY_DIM = 128
B_VAL_DIM = 256
B_GATE_RANK = 16
B_GATE_TAU = 16.0
B_CHUNK = 64
B_KW = B_HEADS * B_KEY_DIM
B_VW = B_HEADS * B_VAL_DIM

C_HEADS = 8
C_NOPE_DIM = 64
C_ROPE_DIM = 32
C_V_DIM = 64
C_QK_DIM = C_NOPE_DIM + C_ROPE_DIM
C_Q_RANK = 384
C_KV_RANK = 256
C_HEAD_PAD = 128
C_PAD_WIDTH = C_HEADS * C_HEAD_PAD

N_EXPERTS = 16
N_GROUPS = 4
EXPERTS_PER_GROUP = N_EXPERTS // N_GROUPS
TOP_K = 2
GROUP_SCORE_K = 2
D_EXPERT = 512
N_MOD = 6

IN_SIZES = (A_WIDTH, A_WIDTH, A_WIDTH, B_KW, B_KW, B_VW, B_VW, B_GATE_RANK, B_GATE_RANK,
            C_Q_RANK, C_KV_RANK, C_ROPE_DIM, 3 * D_MODEL)

LANES = 128
ROW_TILE = D_MODEL // LANES
LOG2E = 1.4426950408889634

OFF_GATES = 0
OFF_BV = 3072
OFF_BR = 4096
OFF_AQ = 5120
OFF_AK = 5632
OFF_AV = 6144
OFF_BQ = 6656
OFF_BK = 7168
OFF_CKV = 7680
OFF_BZ = 7936
OFF_CQ = 8064
OFF_KR = 8448
NPAD = 8704
KR_LANE = 64

VMEM_LIMIT = 56 * 1024 * 1024


def _cparams(sem):
    return pltpu.CompilerParams(dimension_semantics=sem, vmem_limit_bytes=VMEM_LIMIT)


class Groups:
    def __init__(self, b1, s1, b2, s2):
        self.b1, self.s1, self.b2, self.s2 = b1, s1, b2, s2
        self.t1, self.t2 = b1 * s1, b2 * s2
        self.t = self.t1 + self.t2
        self.nb = b1 + b2

    def batch_of_block(self, i, tm):
        n1 = self.t1 // tm
        return jnp.where(i < n1, i // (self.s1 // tm), self.b1 + (i - n1) // (self.s2 // tm))

    def pos_block(self, i, tm):
        n1 = self.t1 // tm
        return jnp.where(i < n1, i % (self.s1 // tm), (i - n1) % (self.s2 // tm))


def _dot(a, b):
    return jnp.dot(a, b, preferred_element_type=F32)


def _dot_nt(a, b):
    return lax.dot_general(a, b, (((1,), (1,)), ((), ())), preferred_element_type=F32)


def _dot_tn(a, b):
    return lax.dot_general(a, b, (((0,), (0,)), ((), ())), preferred_element_type=F32)


def _split3(a):
    hi = a.astype(BF16)
    r1 = a - hi.astype(F32)
    mid = r1.astype(BF16)
    lo = (r1 - mid.astype(F32)).astype(BF16)
    return hi, mid, lo


def _rope(y, ct, s1, s2, half):
    return y * ct + pltpu.roll(y, LANES - half, 1) * s1 + pltpu.roll(y, half, 1) * s2


def _sigmoid(x):
    return 0.5 * jnp.tanh(0.5 * x) + 0.5


def _mod_kernel(c_ref, w_ref, b_ref, o_ref):
    c = c_ref[...]
    ca = c * _sigmoid(c)
    ch, cm, _ = _split3(ca)
    wh, wm, _ = _split3(w_ref[0])
    o_ref[0] = _dot(ch, wh) + _dot(cm, wh) + _dot(ch, wm) + b_ref[0]


def modulation(c_pad, w_mod, b_mod):
    nbp, d = c_pad.shape
    depth, _, n = w_mod.shape
    tn = 512
    return pl.pallas_call(
        _mod_kernel,
        grid=(depth, n // tn),
        in_specs=[pl.BlockSpec((nbp, d), lambda l, j: (0, 0)),
                  pl.BlockSpec((1, d, tn), lambda l, j: (l, 0, j)),
                  pl.BlockSpec((1, 1, tn), lambda l, j: (l, 0, j))],
        out_specs=pl.BlockSpec((1, nbp, tn), lambda l, j: (l, 0, j)),
        out_shape=jax.ShapeDtypeStruct((depth, nbp, n), F32),
        compiler_params=_cparams(("arbitrary", "arbitrary")),
        name="modulation",
    )(c_pad, w_mod, b_mod.reshape(depth, 1, n))


def _inproj_kernel(x_ref, mul_ref, sh_ref, w_ref, ct_ref, s1_ref, s2_ref, o_ref, od4_ref, od16_ref,
                   h_scr, y_scr, y4_scr, *, tm, tn, jq, jv):
    j = pl.program_id(1)

    @pl.when(j == 0)
    def _():
        x = x_ref[...]
        ms = jnp.mean(x * x, axis=-1, keepdims=True)
        h = x * lax.rsqrt(ms + EPS) * mul_ref[0] + sh_ref[0]
        h_scr[...] = h.astype(BF16)

    is_attn = (j >= jq) & (j <= jv)

    @pl.when(jnp.logical_not(is_attn))
    def _():
        o_ref[...] = _dot(h_scr[...], w_ref[j]).astype(BF16)

    @pl.when(is_attn)
    def _():
        y = _dot(h_scr[...], w_ref[j])
        ncol = tn // LANES

        @pl.when(j < jv)
        def _():
            ct, s1, s2 = ct_ref[...], s1_ref[...], s2_ref[...]
            for c in range(ncol):
                y_scr[c] = _rope(y[:, c * LANES:(c + 1) * LANES], ct, s1, s2, A_ROPE_DIM // 2)

        @pl.when(j == jv)
        def _():
            for c in range(ncol):
                y_scr[c] = y[:, c * LANES:(c + 1) * LANES]

        n4, n16 = tm // 4, tm // 16
        for c in range(ncol):
            cs = slice(c * LANES, (c + 1) * LANES)
            o_ref[:, cs] = y_scr[c].astype(BF16)
            for r in range(4):
                y4 = y_scr[c, pl.ds(r, n4, stride=4), :]
                y4_scr[c, r * n4:(r + 1) * n4, :] = y4
                od4_ref[:, r * tn + c * LANES:r * tn + (c + 1) * LANES] = y4.astype(BF16)
            for r in range(4):
                for r2 in range(4):
                    col = (r + 4 * r2) * tn + c * LANES
                    od16_ref[:, col:col + LANES] = y4_scr[c, pl.ds(r * n4 + r2, n16, stride=4), :].astype(BF16)


def in_projection(g, x, mul, shift, w_pad, tabs, tm):
    t, d = x.shape
    tn = A_WIDTH
    ct, s1, s2 = tabs
    jq, jv = OFF_AQ // tn, OFF_AV // tn
    bmap = lambda i, j: (g.batch_of_block(i, tm), 0, 0)
    pmap = lambda i, j: (g.pos_block(i, tm), 0)
    dmap = lambda i, j: (jnp.clip(j - jq, 0, jv - jq), i, 0)
    kern = functools.partial(_inproj_kernel, tm=tm, tn=tn, jq=jq, jv=jv)
    return pl.pallas_call(
        kern,
        grid=(t // tm, NPAD // tn),
        in_specs=[pl.BlockSpec((tm, d), lambda i, j: (i, 0)),
                  pl.BlockSpec((1, 1, d), bmap),
                  pl.BlockSpec((1, 1, d), bmap),
                  pl.BlockSpec((NPAD // tn, d, tn), lambda i, j: (0, 0, 0), pipeline_mode=pl.Buffered(1)),
                  pl.BlockSpec((tm, LANES), pmap),
                  pl.BlockSpec((tm, LANES), pmap),
                  pl.BlockSpec((tm, LANES), pmap)],
        out_specs=[pl.BlockSpec((tm, tn), lambda i, j: (i, j)),
                   pl.BlockSpec((None, tm // 4, 4 * tn), dmap),
                   pl.BlockSpec((None, tm // 16, 16 * tn), dmap)],
        out_shape=[jax.ShapeDtypeStruct((t, NPAD), BF16),
                   jax.ShapeDtypeStruct((3, t // 4, 4 * tn), BF16),
                   jax.ShapeDtypeStruct((3, t // 16, 16 * tn), BF16)],
        scratch_shapes=[pltpu.VMEM((tm, d), BF16), pltpu.VMEM((tn // LANES, tm, LANES), F32),
                        pltpu.VMEM((tn // LANES, tm, LANES), F32)],
        compiler_params=_cparams(("arbitrary", "arbitrary")),
        name="in_projection",
    )(x, mul, shift, w_pad.reshape(d, NPAD // tn, tn).transpose(1, 0, 2), ct, s1, s2)


def _band_kernel(q_ref, kp_ref, km_ref, kn_ref, vp_ref, vm_ref, vn_ref, o_ref, lse_ref, *, tb, tq, rows1, l1, l2):
    i = pl.program_id(1)
    tk = tq + 2 * A_HALF
    k = jnp.concatenate([kp_ref[...], km_ref[...], kn_ref[...]], axis=0)
    v = jnp.concatenate([vp_ref[...], vm_ref[...], vn_ref[...]], axis=0)
    qi = lax.broadcasted_iota(jnp.int32, (tq, tk), 0)
    kj = lax.broadcasted_iota(jnp.int32, (tq, tk), 1)
    band = jnp.abs(kj - A_HALF - qi) <= A_HALF
    lo = lax.broadcasted_iota(jnp.int32, (tq, LANES), 1) < A_HEAD_DIM
    scale = A_HEAD_DIM ** -0.5
    for u in range(tb // tq):
        row0 = i * tb + u * tq
        in1 = row0 < rows1
        seq_len = jnp.where(in1, l1, l2)
        pos0 = jnp.where(in1, row0 % l1, (row0 - rows1) % l2)
        kpos = pos0 - A_HALF + kj
        valid = band & (kpos >= 0) & (kpos < seq_len)
        valid2 = jnp.concatenate([valid, valid], axis=0)
        rows = slice(u * tq, (u + 1) * tq)
        krows = slice(u * tq, u * tq + tk)
        for p in range(A_WIDTH // LANES):
            sl = slice(p * LANES, (p + 1) * LANES)
            qp, kp, vp = q_ref[rows, sl] * scale, k[krows, sl], v[krows, sl]
            zero = jnp.zeros_like(qp)
            qm = jnp.concatenate([jnp.where(lo, qp, zero), jnp.where(lo, zero, qp)], axis=0)
            s = jnp.where(valid2, _dot_nt(qm, kp), NEG_INF)
            m = jnp.max(s, axis=1, keepdims=True)
            e = jnp.exp(s - m)
            l = jnp.sum(e, axis=1, keepdims=True)
            o2 = _dot(e.astype(BF16), vp) / l
            lse2 = m + jnp.log(l)
            o_ref[rows, sl] = jnp.where(lo, o2[:tq], o2[tq:]).astype(BF16)
            lse_ref[rows, sl] = jnp.where(lo, lse2[:tq], lse2[tq:])


def band_attention(g, qkv, dil, cols, cpb):
    rows = g.t // dil
    tq = 128
    tb = 256
    sub = tb // A_HALF
    nsub = rows // A_HALF

    def main(c):
        ld, col = cols[c]
        return pl.BlockSpec((None, tb, A_WIDTH), lambda r, i: (ld, i, r * cpb + col))

    def prev(c):
        ld, col = cols[c]
        return pl.BlockSpec((None, A_HALF, A_WIDTH), lambda r, i: (ld, jnp.maximum(i * sub - 1, 0), r * cpb + col))

    def nxt(c):
        ld, col = cols[c]
        return pl.BlockSpec((None, A_HALF, A_WIDTH),
                            lambda r, i: (ld, jnp.minimum((i + 1) * sub, nsub - 1), r * cpb + col))

    assert (g.s1 // dil) % tq == 0 and (g.s2 // dil) % tq == 0 and rows % tb == 0
    kern = functools.partial(_band_kernel, tb=tb, tq=tq, rows1=g.t1 // dil, l1=g.s1 // dil, l2=g.s2 // dil)
    return pl.pallas_call(
        kern,
        grid=(dil, rows // tb),
        in_specs=[main(0), prev(1), main(1), nxt(1), prev(2), main(2), nxt(2)],
        out_specs=[pl.BlockSpec((tb, A_WIDTH), lambda r, i: (i, r)),
                   pl.BlockSpec((tb, A_WIDTH), lambda r, i: (i, r))],
        out_shape=[jax.ShapeDtypeStruct((rows, dil * A_WIDTH), BF16),
                   jax.ShapeDtypeStruct((rows, dil * A_WIDTH), F32)],
        compiler_params=_cparams(("arbitrary", "arbitrary")),
        name="band_attention_d%d" % dil,
    )(qkv, qkv, qkv, qkv, qkv, qkv, qkv)


def _gla_kernel(*refs, reverse, final, tc, nblk, t1, s1, s2):
    if final:
        q_ref, k_ref, v_ref, z_ref, wa_ref, ba_ref, tri_ref, ob_ref, r_ref, gain_ref, o_ref, st_scr = refs
    else:
        q_ref, k_ref, v_ref, z_ref, wa_ref, ba_ref, tri_ref, o_ref, st_scr = refs
    i = pl.program_id(0)
    blk = (nblk - 1 - i) if reverse else i
    row0 = blk * tc
    in1 = row0 < t1
    pos0 = jnp.where(in1, row0 % s1, (row0 - t1) % s2)
    slen = jnp.where(in1, s1, s2)
    start = (pos0 + tc == slen) if reverse else (pos0 == 0)

    @pl.when(start)
    def _():
        st_scr[...] = jnp.zeros_like(st_scr)

    zl = _dot(z_ref[...], wa_ref[...]) + ba_ref[...]
    la = (jnp.minimum(zl, 0.0) - jnp.log(1.0 + jnp.exp(-jnp.abs(zl)))) * (1.0 / B_GATE_TAU)
    hi, mid, _ = _split3(la)
    tri = tri_ref[...]
    bc_all = _dot(tri, hi) + _dot(tri, mid)

    qi = lax.broadcasted_iota(jnp.int32, (B_CHUNK, B_CHUNK), 0)
    si = lax.broadcasted_iota(jnp.int32, (B_CHUNK, B_CHUNK), 1)
    mask = (si > qi) if reverse else (si <= qi)
    nch = tc // B_CHUNK
    q_all = q_ref[...].astype(F32) * (B_KEY_DIM ** -0.5)
    k_all = k_ref[...].astype(F32)
    q_dec = (q_all * jnp.exp(bc_all)).astype(BF16)
    k_inv = (k_all * jnp.exp(-bc_all)).astype(BF16)
    st = [st_scr[h] for h in range(B_HEADS)]
    for c in (range(nch - 1, -1, -1) if reverse else range(nch)):
        sl = slice(c * B_CHUNK, (c + 1) * B_CHUNK)
        edge = c * B_CHUNK if reverse else (c + 1) * B_CHUNK - 1
        tot = bc_all[edge:edge + 1]
        k_end = (k_all[sl] * jnp.exp(tot - bc_all[sl])).astype(BF16)
        dec = jnp.exp(tot)
        for h in range(B_HEADS):
            ks = slice(h * B_KEY_DIM, (h + 1) * B_KEY_DIM)
            vs = slice(h * B_VAL_DIM, (h + 1) * B_VAL_DIM)
            v = v_ref[sl, vs]
            att = jnp.where(mask, _dot_nt(q_dec[sl, ks], k_inv[sl, ks]), 0.0).astype(BF16)
            o = _dot(att, v) + _dot_nt(q_dec[sl, ks], st[h].astype(BF16))
            st[h] = st[h] * dec[:, ks] + _dot_tn(v, k_end[:, ks])
            if final:
                o = o + ob_ref[sl, vs]
                on = o * lax.rsqrt(jnp.mean(o * o, axis=-1, keepdims=True) + EPS) * gain_ref[...]
                r = r_ref[sl, vs].astype(F32)
                o_ref[sl, vs] = (on * (r * _sigmoid(r))).astype(BF16)
            else:
                o_ref[sl, vs] = o
    for h in range(B_HEADS):
        st_scr[h] = st[h]


def gla_pass(g, y, wa, ba, reverse, o_back=None, gain=None):
    t = g.t
    tc = 256
    nblk = t // tc
    final = o_back is not None
    rowmap = (lambda i: nblk - 1 - i) if reverse else (lambda i: i)
    row = lambda w, off: pl.BlockSpec((tc, w), lambda i: (rowmap(i), off // w))
    full = lambda a: pl.BlockSpec(a.shape, lambda i: (0,) * a.ndim)
    ri = jnp.arange(tc)[:, None]
    ci = jnp.arange(tc)[None, :]
    tri = ((ri // B_CHUNK == ci // B_CHUNK) & ((ci >= ri) if reverse else (ci <= ri))).astype(BF16)
    in_specs = [row(B_KW, OFF_BQ), row(B_KW, OFF_BK), row(B_VW, OFF_BV), row(LANES, OFF_BZ),
                full(wa), full(ba), full(tri)]
    args = [y, y, y, y, wa, ba, tri]
    if final:
        in_specs += [row(B_VW, 0), row(B_VW, OFF_BR), full(gain)]
        args += [o_back, y, gain]
    kern = functools.partial(_gla_kernel, reverse=reverse, final=final, tc=tc, nblk=nblk, t1=g.t1, s1=g.s1, s2=g.s2)
    return pl.pallas_call(
        kern,
        grid=(nblk,),
        in_specs=in_specs,
        out_specs=row(B_VW, 0),
        out_shape=jax.ShapeDtypeStruct((t, B_VW), BF16 if final else F32),
        scratch_shapes=[pltpu.VMEM((B_HEADS, B_VAL_DIM, B_KEY_DIM), F32)],
        compiler_params=_cparams(("arbitrary",)),
        name="gla_forward_final" if final else "gla_backward",
    )(*args)


def _mla_prep_kernel(ckv_ref, cq_ref, kr_ref, qn_ref, kvn_ref, wq_ref, wk_ref, wv_ref, ct_ref, s1_ref, s2_ref,
                     q_out, k_out, v_out):
    def norm(ref, gain_ref):
        xf = ref[...].astype(F32)
        return (xf * lax.rsqrt(jnp.mean(xf * xf, axis=-1, keepdims=True) + EPS) * gain_ref[...]).astype(BF16)

    ckv_n = norm(ckv_ref, kvn_ref)
    cq_n = norm(cq_ref, qn_ref)
    q = _dot(cq_n, wq_ref[...])
    kn = _dot(ckv_n, wk_ref[...])
    v = _dot(ckv_n, wv_ref[...])
    ct, s1, s2 = ct_ref[...], s1_ref[...], s2_ref[...]
    half = C_ROPE_DIM // 2
    kr_rot = _rope(kr_ref[...].astype(F32), ct, s1, s2, half)
    ones_lane = lax.broadcasted_iota(jnp.int32, kr_rot.shape, 1) == C_V_DIM
    scale = C_QK_DIM ** -0.5 * LOG2E
    for h in range(C_HEADS):
        sl = slice(h * C_HEAD_PAD, (h + 1) * C_HEAD_PAD)
        q_out[:, sl] = (_rope(q[:, sl], ct, s1, s2, half) * scale).astype(BF16)
        k_out[:, sl] = (kn[:, sl] + kr_rot).astype(BF16)
        v_out[:, sl] = jnp.where(ones_lane, 1.0, v[:, sl]).astype(BF16)


def mla_prep(g, y, qn, kvn, wq, wk, wv, tabs, tm):
    t = g.t
    ct, s1, s2 = tabs
    pmap = lambda i: (g.pos_block(i, tm), 0)
    full = lambda a: pl.BlockSpec(a.shape, lambda i: (0,) * a.ndim)
    out = jax.ShapeDtypeStruct((t, C_PAD_WIDTH), BF16)
    ospec = pl.BlockSpec((tm, C_PAD_WIDTH), lambda i: (i, 0))
    return pl.pallas_call(
        _mla_prep_kernel,
        grid=(t // tm,),
        in_specs=[pl.BlockSpec((tm, C_KV_RANK), lambda i: (i, OFF_CKV // C_KV_RANK)),
                  pl.BlockSpec((tm, C_Q_RANK), lambda i: (i, OFF_CQ // C_Q_RANK)),
                  pl.BlockSpec((tm, LANES), lambda i: (i, OFF_KR // LANES)),
                  full(qn), full(kvn), full(wq), full(wk), full(wv),
                  pl.BlockSpec((tm, LANES), pmap), pl.BlockSpec((tm, LANES), pmap), pl.BlockSpec((tm, LANES), pmap)],
        out_specs=[ospec, ospec, ospec],
        out_shape=[out, out, out],
        compiler_params=_cparams(("arbitrary",)),
        name="mla_prep",
    )(y, y, y, qn, kvn, wq, wk, wv, ct, s1, s2)


def _flash_kernel(q_ref, k_ref, v_ref, o_ref, s0_scr, s1_scr, p_scr, m_scr, a_scr, acc_scr, *, tk, nk, rc, unroll):
    tq = q_ref.shape[0]

    def logits(t, scr):
        off = pl.multiple_of(t * tk, tk)
        scr[...] = _dot_nt(q_ref[...], k_ref[pl.ds(off, tk), :])

    def softmax_pv(t, scr):
        off = pl.multiple_of(t * tk, tk)
        for r in range(tq // rc):
            rows = slice(r * rc, (r + 1) * rc)
            s = scr[rows, :]
            m_old = m_scr[rows, :]
            m_new = jnp.maximum(m_old, jnp.max(s, axis=1, keepdims=True))
            p_scr[rows, :] = jnp.exp2(s - jnp.concatenate([m_new] * (tk // LANES), axis=1)).astype(BF16)
            a_scr[rows, :] = jnp.exp2(m_old - m_new)
            m_scr[rows, :] = m_new
        acc_scr[...] = acc_scr[...] * a_scr[...] + _dot(p_scr[...], v_ref[pl.ds(off, tk), :])

    bufs = (s0_scr, s1_scr)

    def group(t0, last):
        for u in range(unroll):
            if not (last and u == unroll - 1):
                logits(t0 + u + 1, bufs[(u + 1) % 2])
            softmax_pv(t0 + u, bufs[u % 2])

    def body(jj, carry):
        group(unroll * jj, False)
        return carry

    m_scr[...] = jnp.full(m_scr.shape, NEG_INF, F32)
    acc_scr[...] = jnp.zeros(acc_scr.shape, F32)
    logits(0, s0_scr)
    lax.fori_loop(0, nk // unroll - 1, body, 0)
    group(nk - unroll, True)
    acc = acc_scr[...]
    o_ref[...] = (acc / acc[:, C_V_DIM:C_V_DIM + 1]).astype(BF16)


def mla_flash(q, k, v, row_off, nseq, s, tq, tk):
    assert row_off % s == 0 and row_off % tq == 0
    qb0 = row_off // tq
    sb0 = row_off // s
    nq = s // tq
    nk = s // tk
    unroll = 4 if nk % 4 == 0 else 2
    assert nk % unroll == 0
    kern = functools.partial(_flash_kernel, tk=tk, nk=nk, rc=64, unroll=unroll)
    return pl.pallas_call(
        kern,
        grid=(nseq, C_HEADS, nq),
        scratch_shapes=[pltpu.VMEM((tq, tk), F32), pltpu.VMEM((tq, tk), F32), pltpu.VMEM((tq, tk), BF16),
                        pltpu.VMEM((tq, LANES), F32), pltpu.VMEM((tq, LANES), F32),
                        pltpu.VMEM((tq, C_HEAD_PAD), F32)],
        in_specs=[pl.BlockSpec((tq, C_HEAD_PAD), lambda b, h, i: (qb0 + b * nq + i, h)),
                  pl.BlockSpec((s, C_HEAD_PAD), lambda b, h, i: (sb0 + b, h)),
                  pl.BlockSpec((s, C_HEAD_PAD), lambda b, h, i: (sb0 + b, h))],
        out_specs=pl.BlockSpec((tq, C_HEAD_PAD), lambda b, h, i: (b * nq + i, h)),
        out_shape=jax.ShapeDtypeStruct((nseq * s, C_PAD_WIDTH), BF16),
        compiler_params=_cparams(("arbitrary", "arbitrary", "arbitrary")),
        name="mla_flash_s%d" % s,
    )(q, k, v)


def _store_row_tiles(ref, x, rows):
    for c in range(ROW_TILE):
        ref[pl.ds(c, rows, stride=ROW_TILE), :] = x[:, c * LANES:(c + 1) * LANES]


def _load_row_tiles(ref, rows):
    return jnp.concatenate([ref[pl.ds(c, rows, stride=ROW_TILE), :] for c in range(ROW_TILE)], axis=1)


def _router_scores(h, wrh_ref, wrm_ref):
    hh, hm, _ = _split3(h)
    return _sigmoid(_dot(hh, wrh_ref[...]) + _dot(hm, wrh_ref[...]) + _dot(hh, wrm_ref[...]))


def _rank_in_group(sel, pos):
    rank = jnp.zeros(sel.shape, jnp.int32)
    for k in range(1, EXPERTS_PER_GROUP):
        below = pltpu.roll(sel, k, 1)
        above = pltpu.roll(sel, LANES - k, 1)
        rank += jnp.where((pos >= k) & (below >= sel), 1, 0)
        rank += jnp.where((pos + k < EXPERTS_PER_GROUP) & (above > sel), 1, 0)
    return rank


def _route_in_group(s, bias, g):
    lane = lax.broadcasted_iota(jnp.int32, s.shape, 1)
    rank = _rank_in_group(s + bias, lane % EXPERTS_PER_GROUP)
    chosen = (lane // EXPERTS_PER_GROUP == g) & (rank < TOP_K)
    total = jnp.sum(jnp.where(chosen, s, 0.0), axis=1, keepdims=True)
    return jnp.where(chosen, s / total, 0.0)


def _route(s, bias):
    lane = lax.broadcasted_iota(jnp.int32, s.shape, 1)
    pos = lane % EXPERTS_PER_GROUP
    grp = lane // EXPERTS_PER_GROUP
    sel = s + bias
    rank = _rank_in_group(sel, pos)
    top = rank < GROUP_SCORE_K
    contrib = jnp.where(top, sel, 0.0)
    score = contrib
    for k in range(1, EXPERTS_PER_GROUP):
        score += jnp.where(pos >= k, pltpu.roll(contrib, k, 1), 0.0)
        score += jnp.where(pos + k < EXPERTS_PER_GROUP, pltpu.roll(contrib, LANES - k, 1), 0.0)
    best = lane < N_EXPERTS
    for k in range(1, N_GROUPS):
        earlier = pltpu.roll(score, k * EXPERTS_PER_GROUP, 1)
        later = pltpu.roll(score, LANES - k * EXPERTS_PER_GROUP, 1)
        best &= jnp.logical_not((grp >= k) & (earlier >= score))
        best &= jnp.logical_not((grp + k < N_GROUPS) & (later > score))
    chosen = best & (rank < TOP_K)
    total = jnp.sum(jnp.where(chosen, s, 0.0), axis=1, keepdims=True)
    return jnp.where(chosen, s / total, 0.0)


def _proj_kernel(o1_ref, o2_ref, o3_ref, l1_ref, l2_ref, l3_ref, yb_ref, yc1_ref, yc2_ref, ga_ref, gb_ref, gc_ref,
                 x_ref, g1_ref, mul2_ref, sh2_ref, wpa_ref, wpb_ref, wpc_ref, wout_ref, wrh_ref, wrm_ref, br_ref,
                 x_out, h_out, comb_out, o2_scr, l2_scr, o3_scr, l3_scr, *, tm, n1):
    for dil, o_ref, l_ref, o_scr, l_scr in ((A_PATTERNS[1][1], o2_ref, l2_ref, o2_scr, l2_scr),
                                           (A_PATTERNS[2][1], o3_ref, l3_ref, o3_scr, l3_scr)):
        for r in range(dil):
            for c in range(A_WIDTH // LANES):
                sl = slice(r * A_WIDTH + c * LANES, r * A_WIDTH + (c + 1) * LANES)
                o_scr[c, pl.ds(r, tm // dil, stride=dil), :] = o_ref[:, sl].astype(F32)
                l_scr[c, pl.ds(r, tm // dil, stride=dil), :] = l_ref[:, sl]
    slabs = lambda scr: jnp.concatenate([scr[c] for c in range(A_WIDTH // LANES)], axis=1)
    l1, l2, l3 = l1_ref[...], slabs(l2_scr), slabs(l3_scr)
    m = jnp.maximum(jnp.maximum(l1, l2), l3)
    e1, e2, e3 = jnp.exp(l1 - m), jnp.exp(l2 - m), jnp.exp(l3 - m)
    ya = (e1 * o1_ref[...].astype(F32) + e2 * slabs(o2_scr) + e3 * slabs(o3_scr)) / (e1 + e2 + e3)
    sig = lambda ref: _sigmoid(ref[...].astype(F32))
    yc = jnp.where(pl.program_id(0) < n1, yc1_ref[...], yc2_ref[...])
    merged = (sig(ga_ref) * _dot(ya.astype(BF16), wpa_ref[...])
              + sig(gb_ref) * _dot(yb_ref[...], wpb_ref[...])
              + sig(gc_ref) * _dot(yc, wpc_ref[...]))
    out = _dot(merged.astype(BF16), wout_ref[...])
    x = x_ref[...] + g1_ref[0] * out
    x_out[...] = x
    h = x * lax.rsqrt(jnp.mean(x * x, axis=-1, keepdims=True) + EPS) * mul2_ref[0] + sh2_ref[0]
    _store_row_tiles(h_out, h, tm)
    comb_out[...] = _route(_router_scores(h, wrh_ref, wrm_ref), br_ref[...])


def proj_merge(g, o_list, lse_list, yb, yc1, yc2, y, x, gate1, mul2, sh2, wpa, wpb, wpc, wout, wrh, wrm, br, tm):
    t, d = x.shape
    n1 = g.t1 // tm
    n2 = g.t2 // tm
    bmap = lambda i: (g.batch_of_block(i, tm), 0, 0)
    row = lambda w, col=0: pl.BlockSpec((tm, w), lambda i: (i, col))
    dil_row = lambda dil: pl.BlockSpec((tm // dil, dil * A_WIDTH), lambda i: (i, 0))
    full = lambda a: pl.BlockSpec(a.shape, lambda i: (0,) * a.ndim)
    gcol = OFF_GATES // d
    d2, d3 = A_PATTERNS[1][1], A_PATTERNS[2][1]
    attn_specs = [row(A_WIDTH), dil_row(d2), dil_row(d3)]
    return pl.pallas_call(
        functools.partial(_proj_kernel, tm=tm, n1=n1),
        grid=(t // tm,),
        in_specs=attn_specs + attn_specs + [
            row(B_VW),
            pl.BlockSpec((tm, C_PAD_WIDTH), lambda i: (jnp.minimum(i, n1 - 1), 0)),
            pl.BlockSpec((tm, C_PAD_WIDTH), lambda i: (jnp.clip(i - n1, 0, n2 - 1), 0)),
            row(d, gcol), row(d, gcol + 1), row(d, gcol + 2),
            row(d), pl.BlockSpec((1, 1, d), bmap), pl.BlockSpec((1, 1, d), bmap), pl.BlockSpec((1, 1, d), bmap),
            full(wpa), full(wpb), full(wpc), full(wout), full(wrh), full(wrm), full(br)],
        out_specs=[row(d), pl.BlockSpec((tm * ROW_TILE, LANES), lambda i: (i, 0)), row(LANES)],
        out_shape=[jax.ShapeDtypeStruct((t, d), F32), jax.ShapeDtypeStruct((t * ROW_TILE, LANES), F32),
                   jax.ShapeDtypeStruct((t, LANES), F32)],
        scratch_shapes=[pltpu.VMEM((A_WIDTH // LANES, tm, LANES), F32)] * 4,
        compiler_params=_cparams(("arbitrary",)),
        name="proj_merge_route",
    )(*o_list, *lse_list, yb, yc1, yc2, y, y, y, x, gate1, mul2, sh2, wpa, wpb, wpc, wout, wrh, wrm, br)


def _sorted_layout(comb, tmoe):
    t = comb.shape[0]
    gw = comb[:, :N_EXPERTS].reshape(t, N_GROUPS, EXPERTS_PER_GROUP).sum(-1)
    gid = jnp.argmax(gw, axis=-1).astype(jnp.int32)
    onehot = (gid[:, None] == jnp.arange(N_GROUPS, dtype=jnp.int32)[None, :]).astype(jnp.int32)
    incl = jnp.cumsum(onehot, axis=0)
    counts = incl[-1]
    padded = (counts + tmoe - 1) // tmoe * tmoe
    ends = jnp.cumsum(padded)
    dest = jnp.sum(onehot * (incl - onehot + (ends - padded)[None, :]), axis=1).astype(jnp.int32)
    ntiles = t // tmoe + N_GROUPS
    tile_group = jnp.sum((jnp.arange(ntiles, dtype=jnp.int32) * tmoe)[:, None] >= ends[None, :], axis=1)
    tile_group = jnp.where(tile_group < N_GROUPS, tile_group, -1).astype(jnp.int32)
    return dest, tile_group, ntiles


def _row_copy(src, src_row, dst, dst_row, sem):
    return pltpu.make_async_copy(src.at[pl.ds(src_row * ROW_TILE, ROW_TILE), :],
                                 dst.at[pl.ds(dst_row * ROW_TILE, ROW_TILE), :], sem)


def _scatter_rows_kernel(dest_ref, h_ref, init_hbm, hs_hbm, sem, *, tm):
    del init_hbm
    i = pl.program_id(0)

    def issue(r, c):
        _row_copy(h_ref, r, hs_hbm, dest_ref[i * tm + r], sem).start()
        return c

    def drain(r, c):
        _row_copy(h_ref, 0, hs_hbm, 0, sem).wait()
        return c

    lax.fori_loop(0, tm, issue, 0, unroll=8)
    lax.fori_loop(0, tm, drain, 0, unroll=8)


def scatter_rows(dest, h_tiles, rows_out, tm):
    t = dest.shape[0]
    init = jnp.zeros((rows_out * ROW_TILE, LANES), F32)
    return pl.pallas_call(
        functools.partial(_scatter_rows_kernel, tm=tm),
        grid_spec=pltpu.PrefetchScalarGridSpec(
            num_scalar_prefetch=1,
            grid=(t // tm,),
            in_specs=[pl.BlockSpec((tm * ROW_TILE, LANES), lambda i, dst: (i, 0)),
                      pl.BlockSpec(memory_space=pl.ANY)],
            out_specs=pl.BlockSpec(memory_space=pl.ANY),
            scratch_shapes=[pltpu.SemaphoreType.DMA(())]),
        out_shape=jax.ShapeDtypeStruct(init.shape, F32),
        input_output_aliases={2: 0},
        compiler_params=_cparams(("arbitrary",)),
        name="moe_scatter_rows",
    )(dest, h_tiles, init)


def _moe_group_kernel(tg_ref, hs_ref, wrh_ref, wrm_ref, br_ref, wg_ref, wu_ref, wd_ref, y_ref,
                      h_scr, comb_scr, acc_scr, *, tm):
    j = pl.program_id(0)
    k = pl.program_id(1)
    g = tg_ref[j]

    @pl.when(k == 0)
    def _():
        h = _load_row_tiles(hs_ref, tm)
        h_scr[...] = h.astype(BF16)
        comb_scr[...] = _route_in_group(_router_scores(h, wrh_ref, wrm_ref), br_ref[...], g)
        acc_scr[...] = jnp.zeros_like(acc_scr)

    @pl.when(g >= 0)
    def _():
        h = h_scr[...]
        gate = _dot(h, wg_ref[0])
        a = gate * _sigmoid(gate) * _dot(h, wu_ref[0])
        yv = _dot(a.astype(BF16), wd_ref[0])
        comb = comb_scr[...]
        lane = lax.broadcasted_iota(jnp.int32, comb.shape, 1)
        ce = jnp.sum(jnp.where(lane == g * EXPERTS_PER_GROUP + k, comb, 0.0), axis=1, keepdims=True)
        acc_scr[...] += ce * yv

    @pl.when(k == EXPERTS_PER_GROUP - 1)
    def _():
        _store_row_tiles(y_ref, acc_scr[...], tm)


def moe_grouped(tile_group, ntiles, hs, wrh, wrm, br, wg, wu, wd, tm):
    d = D_MODEL
    full = lambda a: pl.BlockSpec(a.shape, lambda j, k, tg: (0,) * a.ndim)
    emap = lambda j, k, tg: (jnp.maximum(tg[j], 0) * EXPERTS_PER_GROUP + k, 0, 0)
    return pl.pallas_call(
        functools.partial(_moe_group_kernel, tm=tm),
        grid_spec=pltpu.PrefetchScalarGridSpec(
            num_scalar_prefetch=1,
            grid=(ntiles, EXPERTS_PER_GROUP),
            in_specs=[pl.BlockSpec((tm * ROW_TILE, LANES), lambda j, k, tg: (j, 0)),
                      full(wrh), full(wrm), full(br),
                      pl.BlockSpec((1, d, D_EXPERT), emap),
                      pl.BlockSpec((1, d, D_EXPERT), emap),
                      pl.BlockSpec((1, D_EXPERT, d), emap)],
            out_specs=pl.BlockSpec((tm * ROW_TILE, LANES), lambda j, k, tg: (j, 0)),
            scratch_shapes=[pltpu.VMEM((tm, d), BF16), pltpu.VMEM((tm, LANES), F32), pltpu.VMEM((tm, d), F32)]),
        out_shape=jax.ShapeDtypeStruct(hs.shape, F32),
        compiler_params=_cparams(("arbitrary", "arbitrary")),
        name="moe_grouped",
    )(tile_group, hs, wrh, wrm, br, wg, wu, wd)


def _gather_residual_kernel(dest_ref, y_hbm, x_ref, g2_ref, fn_ref, o_ref, buf, sem, *, tm, nsteps, final):
    i = pl.program_id(0)

    def fetch(step, slot):
        def issue(r, c):
            _row_copy(y_hbm, dest_ref[step * tm + r], buf.at[slot], r, sem.at[slot]).start()
            return c
        lax.fori_loop(0, tm, issue, 0, unroll=8)

    @pl.when(i == 0)
    def _():
        fetch(0, 0)

    @pl.when(i + 1 < nsteps)
    def _():
        fetch(i + 1, (i + 1) % 2)

    slot = i % 2

    def drain(r, c):
        _row_copy(y_hbm, 0, buf.at[slot], 0, sem.at[slot]).wait()
        return c

    lax.fori_loop(0, tm, drain, 0, unroll=8)
    x = x_ref[...] + g2_ref[0] * _load_row_tiles(buf.at[slot], tm)
    if final:
        x = x * lax.rsqrt(jnp.mean(x * x, axis=-1, keepdims=True) + EPS) * fn_ref[...]
    o_ref[...] = x


def gather_residual(g, dest, y_tiles, x, gate2, fnorm, final, tm):
    t, d = x.shape
    nsteps = t // tm
    kern = functools.partial(_gather_residual_kernel, tm=tm, nsteps=nsteps, final=final)
    return pl.pallas_call(
        kern,
        grid_spec=pltpu.PrefetchScalarGridSpec(
            num_scalar_prefetch=1,
            grid=(nsteps,),
            in_specs=[pl.BlockSpec(memory_space=pl.ANY),
                      pl.BlockSpec((tm, d), lambda i, dst: (i, 0)),
                      pl.BlockSpec((1, 1, d), lambda i, dst: (g.batch_of_block(i, tm), 0, 0)),
                      pl.BlockSpec((1, d), lambda i, dst: (0, 0))],
            out_specs=pl.BlockSpec((tm, d), lambda i, dst: (i, 0)),
            scratch_shapes=[pltpu.VMEM((2, tm * ROW_TILE, LANES), F32), pltpu.SemaphoreType.DMA((2,))]),
        out_shape=jax.ShapeDtypeStruct((t, d), F32),
        compiler_params=_cparams(("arbitrary",)),
        name="moe_gather_residual",
    )(dest, y_tiles, x, gate2, fnorm)


def _rope_tables(smax, rot_dim, period, lane_off):
    half = rot_dim // 2
    inv_freq = ROPE_THETA ** (-jnp.arange(half, dtype=F32) / half)
    ang = jnp.arange(smax).astype(F32)[:, None] * inv_freq[None, :]
    cos, sin = jnp.cos(ang), jnp.sin(ang)
    gl = (jnp.arange(LANES) % period) - lane_off
    first = (gl >= 0) & (gl < half)
    second = (gl >= half) & (gl < rot_dim)
    j = jnp.clip(jnp.where(first, gl, gl - half), 0, half - 1)
    cl, sn = cos[:, j], sin[:, j]
    ct = jnp.where((first | second)[None, :], cl, 1.0)
    s1 = jnp.where(first[None, :], -sn, 0.0)
    s2 = jnp.where(second[None, :], sn, 0.0)
    return ct, s1, s2


def _pad_w_in(w):
    d = w.shape[0]
    parts, off = [], 0
    for sz in IN_SIZES:
        parts.append(w[:, off:off + sz])
        off += sz
    a_q, a_k, a_v, b_q, b_k, b_v, b_r, b_zf, b_zb, c_cq, c_ckv, c_kr, gates = parts
    z = lambda n: jnp.zeros((d, n), w.dtype)
    bz = jnp.concatenate([b_zf, b_zb, z(LANES - 2 * B_GATE_RANK)], axis=1)
    kr = jnp.concatenate([z(KR_LANE), c_kr, z(LANES - KR_LANE - C_ROPE_DIM)], axis=1)
    out = jnp.concatenate([gates, b_v, b_r, a_q, a_k, a_v, b_q, b_k, c_ckv, bz, c_cq, kr, z(LANES)], axis=1)
    assert out.shape[1] == NPAD
    return out.astype(BF16)


def _pad_heads_cols(w, real, take_lo, take_hi):
    kdim = w.shape[0]
    wh = w.reshape(kdim, C_HEADS, real)[:, :, take_lo:take_hi]
    wh = jnp.pad(wh, ((0, 0), (0, 0), (0, C_HEAD_PAD - (take_hi - take_lo))))
    return wh.reshape(kdim, C_PAD_WIDTH).astype(BF16)


def _gate_weights(w_a, b_a, row_off):
    wa = jnp.zeros((LANES, B_KW), F32).at[row_off:row_off + B_GATE_RANK].set(w_a)
    return wa.astype(BF16), b_a.reshape(1, B_KW).astype(F32)


def kernel(x_prompt, x_sample, c_prompt, c_sample, norm_mix, norm_moe, w_mod, b_mod, w_in, w_gla_af, b_gla_af,
           w_gla_ab, b_gla_ab, gla_norm, mla_q_norm, w_mla_uq, mla_kv_norm, w_mla_ukv, w_proj_a, w_proj_b, w_proj_c,
           w_out, w_router, b_router, w_exp_gate, w_exp_up, w_exp_down, final_norm):
    b1, s1, d = x_prompt.shape
    b2, s2, _ = x_sample.shape
    g = Groups(b1, s1, b2, s2)
    depth = w_in.shape[0]
    tm_in = min(1024, s1, s2)
    tm_prep = min(512, s1, s2)
    tm_proj = min(512, s1, s2)
    tm_moe = min(1024, s1, s2)

    x = jnp.concatenate([x_prompt.reshape(g.t1, d), x_sample.reshape(g.t2, d)], axis=0)
    c = jnp.concatenate([c_prompt, c_sample], axis=0)
    nbp = -(-g.nb // 8) * 8
    c_pad = jnp.pad(c, ((0, nbp - g.nb), (0, 0)))
    mod = modulation(c_pad, w_mod, b_mod)[:, :g.nb]

    smax = max(s1, s2)
    tabs_a = _rope_tables(smax, A_ROPE_DIM, A_HEAD_DIM, 0)
    tabs_c = _rope_tables(smax, C_ROPE_DIM, C_HEAD_PAD, KR_LANE)

    wr_hi = jnp.pad(w_router, ((0, 0), (0, LANES - N_EXPERTS)))
    wrh = wr_hi.astype(BF16)
    wrm = (wr_hi - wrh.astype(F32)).astype(BF16)
    fnorm = final_norm.reshape(1, d)
    br = jnp.pad(b_router.astype(F32), (0, LANES - N_EXPERTS)).reshape(1, LANES)

    for l in range(depth):
        sh1, sc1, gt1, sh2, sc2, gt2 = [m.reshape(g.nb, 1, d) for m in jnp.split(mod[l], N_MOD, axis=-1)]
        mul1 = norm_mix[l][None, None, :] * (1.0 + sc1)
        mul2 = norm_moe[l][None, None, :] * (1.0 + sc2)

        y, yd4, yd16 = in_projection(g, x, mul1, sh1, _pad_w_in(w_in[l]), tabs_a, tm_in)

        nat_cols = tuple((0, off // A_WIDTH) for off in (OFF_AQ, OFF_AK, OFF_AV))
        dil_cols = ((0, 0), (1, 0), (2, 0))
        o_list, lse_list = [], []
        for (_, dil), src in zip(A_PATTERNS, (y.reshape(1, g.t, NPAD), yd4, yd16)):
            o, lse = band_attention(g, src, dil, nat_cols if dil == 1 else dil_cols, NPAD // A_WIDTH if dil == 1 else 1)
            o_list.append(o)
            lse_list.append(lse)

        waf, baf = _gate_weights(w_gla_af[l], b_gla_af[l], 0)
        wab, bab = _gate_weights(w_gla_ab[l], b_gla_ab[l], B_GATE_RANK)
        o_back = gla_pass(g, y, wab, bab, reverse=True)
        yb = gla_pass(g, y, waf, baf, reverse=False, o_back=o_back, gain=gla_norm[l].reshape(1, B_VAL_DIM))

        wq = jnp.concatenate(
            [w_mla_uq[l].reshape(C_Q_RANK, C_HEADS, C_QK_DIM),
             jnp.zeros((C_Q_RANK, C_HEADS, C_HEAD_PAD - C_QK_DIM), F32)], axis=2
        ).reshape(C_Q_RANK, C_PAD_WIDTH).astype(BF16)
        wk = _pad_heads_cols(w_mla_ukv[l], C_NOPE_DIM + C_V_DIM, 0, C_NOPE_DIM)
        wv = _pad_heads_cols(w_mla_ukv[l], C_NOPE_DIM + C_V_DIM, C_NOPE_DIM, C_NOPE_DIM + C_V_DIM)
        qm, km, vm = mla_prep(g, y, mla_q_norm[l].reshape(1, C_Q_RANK), mla_kv_norm[l].reshape(1, C_KV_RANK),
                              wq, wk, wv, tabs_c, tm_prep)
        yc1 = mla_flash(qm, km, vm, 0, b1, s1, min(512, s1), min(512, s1))
        yc2 = mla_flash(qm, km, vm, g.t1, b2, s2, min(512, s2), min(1024, s2))

        wpc = jnp.pad(w_proj_c[l].reshape(C_HEADS, C_V_DIM, d), ((0, 0), (0, C_HEAD_PAD - C_V_DIM), (0, 0)))
        wpc = wpc.reshape(C_PAD_WIDTH, d).astype(BF16)
        x, h2, comb = proj_merge(g, o_list, lse_list, yb, yc1, yc2, y, x, gt1, mul2, sh2,
                                 w_proj_a[l].astype(BF16), w_proj_b[l].astype(BF16), wpc, w_out[l].astype(BF16),
                                 wrh, wrm, br, tm_proj)

        dest, tile_group, ntiles = _sorted_layout(comb, tm_moe)
        hs = scatter_rows(dest, h2, ntiles * tm_moe, tm_moe)
        ys = moe_grouped(tile_group, ntiles, hs, wrh, wrm, br, w_exp_gate[l].astype(BF16),
                         w_exp_up[l].astype(BF16), w_exp_down[l].astype(BF16), tm_moe)
        x = gather_residual(g, dest, ys, x, gt2, fnorm, l == depth - 1, tm_proj)

    return x[:g.t1].reshape(b1, s1, d), x[g.t1:].reshape(b2, s2, d)
```

```python
import functools

import jax
import jax.numpy as jnp
from jax import lax
from jax.experimental import pallas as pl
from jax.experimental.pallas import tpu as pltpu

F32 = jnp.float32
BF16 = jnp.bfloat16

D_MODEL = 1024
DEPTH = 2
EPS = 1e-6
ROPE_THETA = 500000.0
NEG_INF = -1e30

A_HEADS = 8
A_HEAD_DIM = 64
A_ROPE_DIM = A_HEAD_DIM // 4
A_PATTERNS = ((128, 1), (512, 4), (2048, 16))
A_WIDTH = A_HEADS * A_HEAD_DIM
A_HALF = 64

B_HEADS = 4
B_KEY_DIM = 128
B_VAL_DIM = 256
B_GATE_RANK = 16
B_GATE_TAU = 16.0
B_CHUNK = 64
B_KW = B_HEADS * B_KEY_DIM
B_VW = B_HEADS * B_VAL_DIM

C_HEADS = 8
C_NOPE_DIM = 64
C_ROPE_DIM = 32
C_V_DIM = 64
C_QK_DIM = C_NOPE_DIM + C_ROPE_DIM
C_Q_RANK = 384
C_KV_RANK = 256
C_HEAD_PAD = 128
C_PAD_WIDTH = C_HEADS * C_HEAD_PAD

N_EXPERTS = 16
N_GROUPS = 4
EXPERTS_PER_GROUP = N_EXPERTS // N_GROUPS
TOP_K = 2
GROUP_SCORE_K = 2
D_EXPERT = 512
N_MOD = 6

IN_SIZES = (A_WIDTH, A_WIDTH, A_WIDTH, B_KW, B_KW, B_VW, B_VW, B_GATE_RANK, B_GATE_RANK,
            C_Q_RANK, C_KV_RANK, C_ROPE_DIM, 3 * D_MODEL)

LANES = 128
ROW_TILE = D_MODEL // LANES
LOG2E = 1.4426950408889634

OFF_GATES = 0
OFF_BV = 3072
OFF_BR = 4096
OFF_AQ = 5120
OFF_AK = 5632
OFF_AV = 6144
OFF_BQ = 6656
OFF_BK = 7168
OFF_CKV = 7680
OFF_BZ = 7936
OFF_CQ = 8064
OFF_KR = 8448
NPAD = 8704
KR_LANE = 64

VMEM_LIMIT = 56 * 1024 * 1024


def _cparams(sem):
    return pltpu.CompilerParams(dimension_semantics=sem, vmem_limit_bytes=VMEM_LIMIT)


class Groups:
    def __init__(self, b1, s1, b2, s2):
        self.b1, self.s1, self.b2, self.s2 = b1, s1, b2, s2
        self.t1, self.t2 = b1 * s1, b2 * s2
        self.t = self.t1 + self.t2
        self.nb = b1 + b2

    def batch_of_block(self, i, tm):
        n1 = self.t1 // tm
        return jnp.where(i < n1, i // (self.s1 // tm), self.b1 + (i - n1) // (self.s2 // tm))

    def pos_block(self, i, tm):
        n1 = self.t1 // tm
        return jnp.where(i < n1, i % (self.s1 // tm), (i - n1) % (self.s2 // tm))


def _dot(a, b):
    return jnp.dot(a, b, preferred_element_type=F32)


def _dot_nt(a, b):
    return lax.dot_general(a, b, (((1,), (1,)), ((), ())), preferred_element_type=F32)


def _dot_tn(a, b):
    return lax.dot_general(a, b, (((0,), (0,)), ((), ())), preferred_element_type=F32)


def _split3(a):
    hi = a.astype(BF16)
    r1 = a - hi.astype(F32)
    mid = r1.astype(BF16)
    lo = (r1 - mid.astype(F32)).astype(BF16)
    return hi, mid, lo


def _rope(y, ct, s1, s2, half):
    return y * ct + pltpu.roll(y, LANES - half, 1) * s1 + pltpu.roll(y, half, 1) * s2


def _sigmoid(x):
    return 0.5 * jnp.tanh(0.5 * x) + 0.5


def _mod_kernel(c_ref, w_ref, b_ref, o_ref):
    c = c_ref[...]
    ca = c * _sigmoid(c)
    ch, cm, _ = _split3(ca)
    wh, wm, _ = _split3(w_ref[0])
    o_ref[0] = _dot(ch, wh) + _dot(cm, wh) + _dot(ch, wm) + b_ref[0]


def modulation(c_pad, w_mod, b_mod):
    nbp, d = c_pad.shape
    depth, _, n = w_mod.shape
    tn = 512
    return pl.pallas_call(
        _mod_kernel,
        grid=(depth, n // tn),
        in_specs=[pl.BlockSpec((nbp, d), lambda l, j: (0, 0)),
                  pl.BlockSpec((1, d, tn), lambda l, j: (l, 0, j)),
                  pl.BlockSpec((1, 1, tn), lambda l, j: (l, 0, j))],
        out_specs=pl.BlockSpec((1, nbp, tn), lambda l, j: (l, 0, j)),
        out_shape=jax.ShapeDtypeStruct((depth, nbp, n), F32),
        compiler_params=_cparams(("arbitrary", "arbitrary")),
        name="modulation",
    )(c_pad, w_mod, b_mod.reshape(depth, 1, n))


def _inproj_kernel(x_ref, mul_ref, sh_ref, w_ref, ct_ref, s1_ref, s2_ref, o_ref, od4_ref, od16_ref,
                   h_scr, y_scr, y4_scr, *, tm, tn, jq, jv):
    j = pl.program_id(1)

    @pl.when(j == 0)
    def _():
        x = x_ref[...]
        ms = jnp.mean(x * x, axis=-1, keepdims=True)
        h = x * lax.rsqrt(ms + EPS) * mul_ref[0] + sh_ref[0]
        h_scr[...] = h.astype(BF16)

    is_attn = (j >= jq) & (j <= jv)

    @pl.when(jnp.logical_not(is_attn))
    def _():
        o_ref[...] = _dot(h_scr[...], w_ref[j]).astype(BF16)

    @pl.when(is_attn)
    def _():
        y = _dot(h_scr[...], w_ref[j])
        ncol = tn // LANES

        @pl.when(j < jv)
        def _():
            ct, s1, s2 = ct_ref[...], s1_ref[...], s2_ref[...]
            for c in range(ncol):
                y_scr[c] = _rope(y[:, c * LANES:(c + 1) * LANES], ct, s1, s2, A_ROPE_DIM // 2)

        @pl.when(j == jv)
        def _():
            for c in range(ncol):
                y_scr[c] = y[:, c * LANES:(c + 1) * LANES]

        n4, n16 = tm // 4, tm // 16
        for c in range(ncol):
            cs = slice(c * LANES, (c + 1) * LANES)
            o_ref[:, cs] = y_scr[c].astype(BF16)
            for r in range(4):
                y4 = y_scr[c, pl.ds(r, n4, stride=4), :]
                y4_scr[c, r * n4:(r + 1) * n4, :] = y4
                od4_ref[:, r * tn + c * LANES:r * tn + (c + 1) * LANES] = y4.astype(BF16)
            for r in range(4):
                for r2 in range(4):
                    col = (r + 4 * r2) * tn + c * LANES
                    od16_ref[:, col:col + LANES] = y4_scr[c, pl.ds(r * n4 + r2, n16, stride=4), :].astype(BF16)


def in_projection(g, x, mul, shift, w_pad, tabs, tm):
    t, d = x.shape
    tn = A_WIDTH
    ct, s1, s2 = tabs
    jq, jv = OFF_AQ // tn, OFF_AV // tn
    bmap = lambda i, j: (g.batch_of_block(i, tm), 0, 0)
    pmap = lambda i, j: (g.pos_block(i, tm), 0)
    dmap = lambda i, j: (jnp.clip(j - jq, 0, jv - jq), i, 0)
    kern = functools.partial(_inproj_kernel, tm=tm, tn=tn, jq=jq, jv=jv)
    return pl.pallas_call(
        kern,
        grid=(t // tm, NPAD // tn),
        in_specs=[pl.BlockSpec((tm, d), lambda i, j: (i, 0)),
                  pl.BlockSpec((1, 1, d), bmap),
                  pl.BlockSpec((1, 1, d), bmap),
                  pl.BlockSpec((NPAD // tn, d, tn), lambda i, j: (0, 0, 0), pipeline_mode=pl.Buffered(1)),
                  pl.BlockSpec((tm, LANES), pmap),
                  pl.BlockSpec((tm, LANES), pmap),
                  pl.BlockSpec((tm, LANES), pmap)],
        out_specs=[pl.BlockSpec((tm, tn), lambda i, j: (i, j)),
                   pl.BlockSpec((None, tm // 4, 4 * tn), dmap),
                   pl.BlockSpec((None, tm // 16, 16 * tn), dmap)],
        out_shape=[jax.ShapeDtypeStruct((t, NPAD), BF16),
                   jax.ShapeDtypeStruct((3, t // 4, 4 * tn), BF16),
                   jax.ShapeDtypeStruct((3, t // 16, 16 * tn), BF16)],
        scratch_shapes=[pltpu.VMEM((tm, d), BF16), pltpu.VMEM((tn // LANES, tm, LANES), F32),
                        pltpu.VMEM((tn // LANES, tm, LANES), F32)],
        compiler_params=_cparams(("arbitrary", "arbitrary")),
        name="in_projection",
    )(x, mul, shift, w_pad.reshape(d, NPAD // tn, tn).transpose(1, 0, 2), ct, s1, s2)


def _band_kernel(q_ref, kp_ref, km_ref, kn_ref, vp_ref, vm_ref, vn_ref, o_ref, lse_ref, *, tb, tq, rows1, l1, l2):
    i = pl.program_id(1)
    tk = tq + 2 * A_HALF
    k = jnp.concatenate([kp_ref[...], km_ref[...], kn_ref[...]], axis=0)
    v = jnp.concatenate([vp_ref[...], vm_ref[...], vn_ref[...]], axis=0)
    qi = lax.broadcasted_iota(jnp.int32, (tq, tk), 0)
    kj = lax.broadcasted_iota(jnp.int32, (tq, tk), 1)
    band = jnp.abs(kj - A_HALF - qi) <= A_HALF
    lo = lax.broadcasted_iota(jnp.int32, (tq, LANES), 1) < A_HEAD_DIM
    scale = A_HEAD_DIM ** -0.5
    for u in range(tb // tq):
        row0 = i * tb + u * tq
        in1 = row0 < rows1
        seq_len = jnp.where(in1, l1, l2)
        pos0 = jnp.where(in1, row0 % l1, (row0 - rows1) % l2)
        kpos = pos0 - A_HALF + kj
        valid = band & (kpos >= 0) & (kpos < seq_len)
        valid2 = jnp.concatenate([valid, valid], axis=0)
        rows = slice(u * tq, (u + 1) * tq)
        krows = slice(u * tq, u * tq + tk)
        for p in range(A_WIDTH // LANES):
            sl = slice(p * LANES, (p + 1) * LANES)
            qp, kp, vp = q_ref[rows, sl] * scale, k[krows, sl], v[krows, sl]
            zero = jnp.zeros_like(qp)
            qm = jnp.concatenate([jnp.where(lo, qp, zero), jnp.where(lo, zero, qp)], axis=0)
            s = jnp.where(valid2, _dot_nt(qm, kp), NEG_INF)
            m = jnp.max(s, axis=1, keepdims=True)
            e = jnp.exp(s - m)
            l = jnp.sum(e, axis=1, keepdims=True)
            o2 = _dot(e.astype(BF16), vp) / l
            lse2 = m + jnp.log(l)
            o_ref[rows, sl] = jnp.where(lo, o2[:tq], o2[tq:]).astype(BF16)
            lse_ref[rows, sl] = jnp.where(lo, lse2[:tq], lse2[tq:])


def band_attention(g, qkv, dil, cols, cpb):
    rows = g.t // dil
    tq = 128
    tb = 256
    sub = tb // A_HALF
    nsub = rows // A_HALF

    def main(c):
        ld, col = cols[c]
        return pl.BlockSpec((None, tb, A_WIDTH), lambda r, i: (ld, i, r * cpb + col))

    def prev(c):
        ld, col = cols[c]
        return pl.BlockSpec((None, A_HALF, A_WIDTH), lambda r, i: (ld, jnp.maximum(i * sub - 1, 0), r * cpb + col))

    def nxt(c):
        ld, col = cols[c]
        return pl.BlockSpec((None, A_HALF, A_WIDTH),
                            lambda r, i: (ld, jnp.minimum((i + 1) * sub, nsub - 1), r * cpb + col))

    assert (g.s1 // dil) % tq == 0 and (g.s2 // dil) % tq == 0 and rows % tb == 0
    kern = functools.partial(_band_kernel, tb=tb, tq=tq, rows1=g.t1 // dil, l1=g.s1 // dil, l2=g.s2 // dil)
    return pl.pallas_call(
        kern,
        grid=(dil, rows // tb),
        in_specs=[main(0), prev(1), main(1), nxt(1), prev(2), main(2), nxt(2)],
        out_specs=[pl.BlockSpec((tb, A_WIDTH), lambda r, i: (i, r)),
                   pl.BlockSpec((tb, A_WIDTH), lambda r, i: (i, r))],
        out_shape=[jax.ShapeDtypeStruct((rows, dil * A_WIDTH), BF16),
                   jax.ShapeDtypeStruct((rows, dil * A_WIDTH), F32)],
        compiler_params=_cparams(("arbitrary", "arbitrary")),
        name="band_attention_d%d" % dil,
    )(qkv, qkv, qkv, qkv, qkv, qkv, qkv)


def _gla_kernel(*refs, reverse, final, tc, nblk, t1, s1, s2):
    if final:
        q_ref, k_ref, v_ref, z_ref, wa_ref, ba_ref, tri_ref, ob_ref, r_ref, gain_ref, o_ref, st_scr = refs
    else:
        q_ref, k_ref, v_ref, z_ref, wa_ref, ba_ref, tri_ref, o_ref, st_scr = refs
    i = pl.program_id(0)
    blk = (nblk - 1 - i) if reverse else i
    row0 = blk * tc
    in1 = row0 < t1
    pos0 = jnp.where(in1, row0 % s1, (row0 - t1) % s2)
    slen = jnp.where(in1, s1, s2)
    start = (pos0 + tc == slen) if reverse else (pos0 == 0)

    @pl.when(start)
    def _():
        st_scr[...] = jnp.zeros_like(st_scr)

    zl = _dot(z_ref[...], wa_ref[...]) + ba_ref[...]
    la = (jnp.minimum(zl, 0.0) - jnp.log(1.0 + jnp.exp(-jnp.abs(zl)))) * (1.0 / B_GATE_TAU)
    hi, mid, _ = _split3(la)
    tri = tri_ref[...]
    bc_all = _dot(tri, hi) + _dot(tri, mid)

    qi = lax.broadcasted_iota(jnp.int32, (B_CHUNK, B_CHUNK), 0)
    si = lax.broadcasted_iota(jnp.int32, (B_CHUNK, B_CHUNK), 1)
    mask = (si > qi) if reverse else (si <= qi)
    nch = tc // B_CHUNK
    q_all = q_ref[...].astype(F32) * (B_KEY_DIM ** -0.5)
    k_all = k_ref[...].astype(F32)
    q_dec = (q_all * jnp.exp(bc_all)).astype(BF16)
    k_inv = (k_all * jnp.exp(-bc_all)).astype(BF16)
    st = [st_scr[h] for h in range(B_HEADS)]
    for c in (range(nch - 1, -1, -1) if reverse else range(nch)):
        sl = slice(c * B_CHUNK, (c + 1) * B_CHUNK)
        edge = c * B_CHUNK if reverse else (c + 1) * B_CHUNK - 1
        tot = bc_all[edge:edge + 1]
        k_end = (k_all[sl] * jnp.exp(tot - bc_all[sl])).astype(BF16)
        dec = jnp.exp(tot)
        for h in range(B_HEADS):
            ks = slice(h * B_KEY_DIM, (h + 1) * B_KEY_DIM)
            vs = slice(h * B_VAL_DIM, (h + 1) * B_VAL_DIM)
            v = v_ref[sl, vs]
            att = jnp.where(mask, _dot_nt(q_dec[sl, ks], k_inv[sl, ks]), 0.0).astype(BF16)
            o = _dot(att, v) + _dot_nt(q_dec[sl, ks], st[h].astype(BF16))
            st[h] = st[h] * dec[:, ks] + _dot_tn(v, k_end[:, ks])
            if final:
                o = o + ob_ref[sl, vs]
                on = o * lax.rsqrt(jnp.mean(o * o, axis=-1, keepdims=True) + EPS) * gain_ref[...]
                r = r_ref[sl, vs].astype(F32)
                o_ref[sl, vs] = (on * (r * _sigmoid(r))).astype(BF16)
            else:
                o_ref[sl, vs] = o
    for h in range(B_HEADS):
        st_scr[h] = st[h]


def gla_pass(g, y, wa, ba, reverse, o_back=None, gain=None):
    t = g.t
    tc = 256
    nblk = t // tc
    final = o_back is not None
    rowmap = (lambda i: nblk - 1 - i) if reverse else (lambda i: i)
    row = lambda w, off: pl.BlockSpec((tc, w), lambda i: (rowmap(i), off // w))
    full = lambda a: pl.BlockSpec(a.shape, lambda i: (0,) * a.ndim)
    ri = jnp.arange(tc)[:, None]
    ci = jnp.arange(tc)[None, :]
    tri = ((ri // B_CHUNK == ci // B_CHUNK) & ((ci >= ri) if reverse else (ci <= ri))).astype(BF16)
    in_specs = [row(B_KW, OFF_BQ), row(B_KW, OFF_BK), row(B_VW, OFF_BV), row(LANES, OFF_BZ),
                full(wa), full(ba), full(tri)]
    args = [y, y, y, y, wa, ba, tri]
    if final:
        in_specs += [row(B_VW, 0), row(B_VW, OFF_BR), full(gain)]
        args += [o_back, y, gain]
    kern = functools.partial(_gla_kernel, reverse=reverse, final=final, tc=tc, nblk=nblk, t1=g.t1, s1=g.s1, s2=g.s2)
    return pl.pallas_call(
        kern,
        grid=(nblk,),
        in_specs=in_specs,
        out_specs=row(B_VW, 0),
        out_shape=jax.ShapeDtypeStruct((t, B_VW), BF16 if final else F32),
        scratch_shapes=[pltpu.VMEM((B_HEADS, B_VAL_DIM, B_KEY_DIM), F32)],
        compiler_params=_cparams(("arbitrary",)),
        name="gla_forward_final" if final else "gla_backward",
    )(*args)


def _mla_prep_kernel(ckv_ref, cq_ref, kr_ref, qn_ref, kvn_ref, wq_ref, wk_ref, wv_ref, ct_ref, s1_ref, s2_ref,
                     q_out, k_out, v_out):
    def norm(ref, gain_ref):
        xf = ref[...].astype(F32)
        return (xf * lax.rsqrt(jnp.mean(xf * xf, axis=-1, keepdims=True) + EPS) * gain_ref[...]).astype(BF16)

    ckv_n = norm(ckv_ref, kvn_ref)
    cq_n = norm(cq_ref, qn_ref)
    q = _dot(cq_n, wq_ref[...])
    kn = _dot(ckv_n, wk_ref[...])
    v = _dot(ckv_n, wv_ref[...])
    ct, s1, s2 = ct_ref[...], s1_ref[...], s2_ref[...]
    half = C_ROPE_DIM // 2
    kr_rot = _rope(kr_ref[...].astype(F32), ct, s1, s2, half)
    ones_lane = lax.broadcasted_iota(jnp.int32, kr_rot.shape, 1) == C_V_DIM
    scale = C_QK_DIM ** -0.5 * LOG2E
    for h in range(C_HEADS):
        sl = slice(h * C_HEAD_PAD, (h + 1) * C_HEAD_PAD)
        q_out[:, sl] = (_rope(q[:, sl], ct, s1, s2, half) * scale).astype(BF16)
        k_out[:, sl] = (kn[:, sl] + kr_rot).astype(BF16)
        v_out[:, sl] = jnp.where(ones_lane, 1.0, v[:, sl]).astype(BF16)


def mla_prep(g, y, qn, kvn, wq, wk, wv, tabs, tm):
    t = g.t
    ct, s1, s2 = tabs
    pmap = lambda i: (g.pos_block(i, tm), 0)
    full = lambda a: pl.BlockSpec(a.shape, lambda i: (0,) * a.ndim)
    out = jax.ShapeDtypeStruct((t, C_PAD_WIDTH), BF16)
    ospec = pl.BlockSpec((tm, C_PAD_WIDTH), lambda i: (i, 0))
    return pl.pallas_call(
        _mla_prep_kernel,
        grid=(t // tm,),
        in_specs=[pl.BlockSpec((tm, C_KV_RANK), lambda i: (i, OFF_CKV // C_KV_RANK)),
                  pl.BlockSpec((tm, C_Q_RANK), lambda i: (i, OFF_CQ // C_Q_RANK)),
                  pl.BlockSpec((tm, LANES), lambda i: (i, OFF_KR // LANES)),
                  full(qn), full(kvn), full(wq), full(wk), full(wv),
                  pl.BlockSpec((tm, LANES), pmap), pl.BlockSpec((tm, LANES), pmap), pl.BlockSpec((tm, LANES), pmap)],
        out_specs=[ospec, ospec, ospec],
        out_shape=[out, out, out],
        compiler_params=_cparams(("arbitrary",)),
        name="mla_prep",
    )(y, y, y, qn, kvn, wq, wk, wv, ct, s1, s2)


def _flash_kernel(q_ref, k_ref, v_ref, o_ref, s0_scr, s1_scr, p_scr, m_scr, a_scr, acc_scr, *, tk, nk, rc, unroll):
    tq = q_ref.shape[0]

    def logits(t, scr):
        off = pl.multiple_of(t * tk, tk)
        scr[...] = _dot_nt(q_ref[...], k_ref[pl.ds(off, tk), :])

    def softmax_pv(t, scr):
        off = pl.multiple_of(t * tk, tk)
        for r in range(tq // rc):
            rows = slice(r * rc, (r + 1) * rc)
            s = scr[rows, :]
            m_old = m_scr[rows, :]
            m_new = jnp.maximum(m_old, jnp.max(s, axis=1, keepdims=True))
            p_scr[rows, :] = jnp.exp2(s - jnp.concatenate([m_new] * (tk // LANES), axis=1)).astype(BF16)
            a_scr[rows, :] = jnp.exp2(m_old - m_new)
            m_scr[rows, :] = m_new
        acc_scr[...] = acc_scr[...] * a_scr[...] + _dot(p_scr[...], v_ref[pl.ds(off, tk), :])

    bufs = (s0_scr, s1_scr)

    def group(t0, last):
        for u in range(unroll):
            if not (last and u == unroll - 1):
                logits(t0 + u + 1, bufs[(u + 1) % 2])
            softmax_pv(t0 + u, bufs[u % 2])

    def body(jj, carry):
        group(unroll * jj, False)
        return carry

    m_scr[...] = jnp.full(m_scr.shape, NEG_INF, F32)
    acc_scr[...] = jnp.zeros(acc_scr.shape, F32)
    logits(0, s0_scr)
    lax.fori_loop(0, nk // unroll - 1, body, 0)
    group(nk - unroll, True)
    acc = acc_scr[...]
    o_ref[...] = (acc / acc[:, C_V_DIM:C_V_DIM + 1]).astype(BF16)


def mla_flash(q, k, v, row_off, nseq, s, tq, tk):
    assert row_off % s == 0 and row_off % tq == 0
    qb0 = row_off // tq
    sb0 = row_off // s
    nq = s // tq
    nk = s // tk
    unroll = 4 if nk % 4 == 0 else 2
    assert nk % unroll == 0
    kern = functools.partial(_flash_kernel, tk=tk, nk=nk, rc=64, unroll=unroll)
    return pl.pallas_call(
        kern,
        grid=(nseq, C_HEADS, nq),
        scratch_shapes=[pltpu.VMEM((tq, tk), F32), pltpu.VMEM((tq, tk), F32), pltpu.VMEM((tq, tk), BF16),
                        pltpu.VMEM((tq, LANES), F32), pltpu.VMEM((tq, LANES), F32),
                        pltpu.VMEM((tq, C_HEAD_PAD), F32)],
        in_specs=[pl.BlockSpec((tq, C_HEAD_PAD), lambda b, h, i: (qb0 + b * nq + i, h)),
                  pl.BlockSpec((s, C_HEAD_PAD), lambda b, h, i: (sb0 + b, h)),
                  pl.BlockSpec((s, C_HEAD_PAD), lambda b, h, i: (sb0 + b, h))],
        out_specs=pl.BlockSpec((tq, C_HEAD_PAD), lambda b, h, i: (b * nq + i, h)),
        out_shape=jax.ShapeDtypeStruct((nseq * s, C_PAD_WIDTH), BF16),
        compiler_params=_cparams(("arbitrary", "arbitrary", "arbitrary")),
        name="mla_flash_s%d" % s,
    )(q, k, v)


def _store_row_tiles(ref, x, rows):
    for c in range(ROW_TILE):
        ref[pl.ds(c, rows, stride=ROW_TILE), :] = x[:, c * LANES:(c + 1) * LANES]


def _load_row_tiles(ref, rows):
    return jnp.concatenate([ref[pl.ds(c, rows, stride=ROW_TILE), :] for c in range(ROW_TILE)], axis=1)


def _router_scores(h, wrh_ref, wrm_ref):
    hh, hm, _ = _split3(h)
    return _sigmoid(_dot(hh, wrh_ref[...]) + _dot(hm, wrh_ref[...]) + _dot(hh, wrm_ref[...]))


def _rank_in_group(sel, pos):
    rank = jnp.zeros(sel.shape, jnp.int32)
    for k in range(1, EXPERTS_PER_GROUP):
        below = pltpu.roll(sel, k, 1)
        above = pltpu.roll(sel, LANES - k, 1)
        rank += jnp.where((pos >= k) & (below >= sel), 1, 0)
        rank += jnp.where((pos + k < EXPERTS_PER_GROUP) & (above > sel), 1, 0)
    return rank


def _route_in_group(s, bias, g):
    lane = lax.broadcasted_iota(jnp.int32, s.shape, 1)
    rank = _rank_in_group(s + bias, lane % EXPERTS_PER_GROUP)
    chosen = (lane // EXPERTS_PER_GROUP == g) & (rank < TOP_K)
    total = jnp.sum(jnp.where(chosen, s, 0.0), axis=1, keepdims=True)
    return jnp.where(chosen, s / total, 0.0)


def _route(s, bias):
    lane = lax.broadcasted_iota(jnp.int32, s.shape, 1)
    pos = lane % EXPERTS_PER_GROUP
    grp = lane // EXPERTS_PER_GROUP
    sel = s + bias
    rank = _rank_in_group(sel, pos)
    top = rank < GROUP_SCORE_K
    contrib = jnp.where(top, sel, 0.0)
    score = contrib
    for k in range(1, EXPERTS_PER_GROUP):
        score += jnp.where(pos >= k, pltpu.roll(contrib, k, 1), 0.0)
        score += jnp.where(pos + k < EXPERTS_PER_GROUP, pltpu.roll(contrib, LANES - k, 1), 0.0)
    best = lane < N_EXPERTS
    for k in range(1, N_GROUPS):
        earlier = pltpu.roll(score, k * EXPERTS_PER_GROUP, 1)
        later = pltpu.roll(score, LANES - k * EXPERTS_PER_GROUP, 1)
        best &= jnp.logical_not((grp >= k) & (earlier >= score))
        best &= jnp.logical_not((grp + k < N_GROUPS) & (later > score))
    chosen = best & (rank < TOP_K)
    total = jnp.sum(jnp.where(chosen, s, 0.0), axis=1, keepdims=True)
    return jnp.where(chosen, s / total, 0.0)


def _proj_kernel(o1_ref, o2_ref, o3_ref, l1_ref, l2_ref, l3_ref, yb_ref, yc1_ref, yc2_ref, ga_ref, gb_ref, gc_ref,
                 x_ref, g1_ref, mul2_ref, sh2_ref, wpa_ref, wpb_ref, wpc_ref, wout_ref, wrh_ref, wrm_ref, br_ref,
                 x_out, h_out, comb_out, o2_scr, l2_scr, o3_scr, l3_scr, *, tm, n1):
    for dil, o_ref, l_ref, o_scr, l_scr in ((A_PATTERNS[1][1], o2_ref, l2_ref, o2_scr, l2_scr),
                                           (A_PATTERNS[2][1], o3_ref, l3_ref, o3_scr, l3_scr)):
        for r in range(dil):
            for c in range(A_WIDTH // LANES):
                sl = slice(r * A_WIDTH + c * LANES, r * A_WIDTH + (c + 1) * LANES)
                o_scr[c, pl.ds(r, tm // dil, stride=dil), :] = o_ref[:, sl].astype(F32)
                l_scr[c, pl.ds(r, tm // dil, stride=dil), :] = l_ref[:, sl]
    slabs = lambda scr: jnp.concatenate([scr[c] for c in range(A_WIDTH // LANES)], axis=1)
    l1, l2, l3 = l1_ref[...], slabs(l2_scr), slabs(l3_scr)
    m = jnp.maximum(jnp.maximum(l1, l2), l3)
    e1, e2, e3 = jnp.exp(l1 - m), jnp.exp(l2 - m), jnp.exp(l3 - m)
    ya = (e1 * o1_ref[...].astype(F32) + e2 * slabs(o2_scr) + e3 * slabs(o3_scr)) / (e1 + e2 + e3)
    sig = lambda ref: _sigmoid(ref[...].astype(F32))
    yc = jnp.where(pl.program_id(0) < n1, yc1_ref[...], yc2_ref[...])
    merged = (sig(ga_ref) * _dot(ya.astype(BF16), wpa_ref[...])
              + sig(gb_ref) * _dot(yb_ref[...], wpb_ref[...])
              + sig(gc_ref) * _dot(yc, wpc_ref[...]))
    out = _dot(merged.astype(BF16), wout_ref[...])
    x = x_ref[...] + g1_ref[0] * out
    x_out[...] = x
    h = x * lax.rsqrt(jnp.mean(x * x, axis=-1, keepdims=True) + EPS) * mul2_ref[0] + sh2_ref[0]
    _store_row_tiles(h_out, h, tm)
    comb_out[...] = _route(_router_scores(h, wrh_ref, wrm_ref), br_ref[...])


def proj_merge(g, o_list, lse_list, yb, yc1, yc2, y, x, gate1, mul2, sh2, wpa, wpb, wpc, wout, wrh, wrm, br, tm):
    t, d = x.shape
    n1 = g.t1 // tm
    n2 = g.t2 // tm
    bmap = lambda i: (g.batch_of_block(i, tm), 0, 0)
    row = lambda w, col=0: pl.BlockSpec((tm, w), lambda i: (i, col))
    dil_row = lambda dil: pl.BlockSpec((tm // dil, dil * A_WIDTH), lambda i: (i, 0))
    full = lambda a: pl.BlockSpec(a.shape, lambda i: (0,) * a.ndim)
    gcol = OFF_GATES // d
    d2, d3 = A_PATTERNS[1][1], A_PATTERNS[2][1]
    attn_specs = [row(A_WIDTH), dil_row(d2), dil_row(d3)]
    return pl.pallas_call(
        functools.partial(_proj_kernel, tm=tm, n1=n1),
        grid=(t // tm,),
        in_specs=attn_specs + attn_specs + [
            row(B_VW),
            pl.BlockSpec((tm, C_PAD_WIDTH), lambda i: (jnp.minimum(i, n1 - 1), 0)),
            pl.BlockSpec((tm, C_PAD_WIDTH), lambda i: (jnp.clip(i - n1, 0, n2 - 1), 0)),
            row(d, gcol), row(d, gcol + 1), row(d, gcol + 2),
            row(d), pl.BlockSpec((1, 1, d), bmap), pl.BlockSpec((1, 1, d), bmap), pl.BlockSpec((1, 1, d), bmap),
            full(wpa), full(wpb), full(wpc), full(wout), full(wrh), full(wrm), full(br)],
        out_specs=[row(d), pl.BlockSpec((tm * ROW_TILE, LANES), lambda i: (i, 0)), row(LANES)],
        out_shape=[jax.ShapeDtypeStruct((t, d), F32), jax.ShapeDtypeStruct((t * ROW_TILE, LANES), F32),
                   jax.ShapeDtypeStruct((t, LANES), F32)],
        scratch_shapes=[pltpu.VMEM((A_WIDTH // LANES, tm, LANES), F32)] * 4,
        compiler_params=_cparams(("arbitrary",)),
        name="proj_merge_route",
    )(*o_list, *lse_list, yb, yc1, yc2, y, y, y, x, gate1, mul2, sh2, wpa, wpb, wpc, wout, wrh, wrm, br)


def _sorted_layout(comb, tmoe):
    t = comb.shape[0]
    gw = comb[:, :N_EXPERTS].reshape(t, N_GROUPS, EXPERTS_PER_GROUP).sum(-1)
    gid = jnp.argmax(gw, axis=-1).astype(jnp.int32)
    onehot = (gid[:, None] == jnp.arange(N_GROUPS, dtype=jnp.int32)[None, :]).astype(jnp.int32)
    incl = jnp.cumsum(onehot, axis=0)
    counts = incl[-1]
    padded = (counts + tmoe - 1) // tmoe * tmoe
    ends = jnp.cumsum(padded)
    dest = jnp.sum(onehot * (incl - onehot + (ends - padded)[None, :]), axis=1).astype(jnp.int32)
    ntiles = t // tmoe + N_GROUPS
    tile_group = jnp.sum((jnp.arange(ntiles, dtype=jnp.int32) * tmoe)[:, None] >= ends[None, :], axis=1)
    tile_group = jnp.where(tile_group < N_GROUPS, tile_group, -1).astype(jnp.int32)
    return dest, tile_group, ntiles


def _row_copy(src, src_row, dst, dst_row, sem):
    return pltpu.make_async_copy(src.at[pl.ds(src_row * ROW_TILE, ROW_TILE), :],
                                 dst.at[pl.ds(dst_row * ROW_TILE, ROW_TILE), :], sem)


def _scatter_rows_kernel(dest_ref, h_ref, init_hbm, hs_hbm, sem, *, tm):
    del init_hbm
    i = pl.program_id(0)

    def issue(r, c):
        _row_copy(h_ref, r, hs_hbm, dest_ref[i * tm + r], sem).start()
        return c

    def drain(r, c):
        _row_copy(h_ref, 0, hs_hbm, 0, sem).wait()
        return c

    lax.fori_loop(0, tm, issue, 0, unroll=8)
    lax.fori_loop(0, tm, drain, 0, unroll=8)


def scatter_rows(dest, h_tiles, rows_out, tm):
    t = dest.shape[0]
    init = jnp.zeros((rows_out * ROW_TILE, LANES), F32)
    return pl.pallas_call(
        functools.partial(_scatter_rows_kernel, tm=tm),
        grid_spec=pltpu.PrefetchScalarGridSpec(
            num_scalar_prefetch=1,
            grid=(t // tm,),
            in_specs=[pl.BlockSpec((tm * ROW_TILE, LANES), lambda i, dst: (i, 0)),
                      pl.BlockSpec(memory_space=pl.ANY)],
            out_specs=pl.BlockSpec(memory_space=pl.ANY),
            scratch_shapes=[pltpu.SemaphoreType.DMA(())]),
        out_shape=jax.ShapeDtypeStruct(init.shape, F32),
        input_output_aliases={2: 0},
        compiler_params=_cparams(("arbitrary",)),
        name="moe_scatter_rows",
    )(dest, h_tiles, init)


def _moe_group_kernel(tg_ref, hs_ref, wrh_ref, wrm_ref, br_ref, wg_ref, wu_ref, wd_ref, y_ref,
                      h_scr, comb_scr, acc_scr, *, tm):
    j = pl.program_id(0)
    k = pl.program_id(1)
    g = tg_ref[j]

    @pl.when(k == 0)
    def _():
        h = _load_row_tiles(hs_ref, tm)
        h_scr[...] = h.astype(BF16)
        comb_scr[...] = _route_in_group(_router_scores(h, wrh_ref, wrm_ref), br_ref[...], g)
        acc_scr[...] = jnp.zeros_like(acc_scr)

    @pl.when(g >= 0)
    def _():
        h = h_scr[...]
        gate = _dot(h, wg_ref[0])
        a = gate * _sigmoid(gate) * _dot(h, wu_ref[0])
        yv = _dot(a.astype(BF16), wd_ref[0])
        comb = comb_scr[...]
        lane = lax.broadcasted_iota(jnp.int32, comb.shape, 1)
        ce = jnp.sum(jnp.where(lane == g * EXPERTS_PER_GROUP + k, comb, 0.0), axis=1, keepdims=True)
        acc_scr[...] += ce * yv

    @pl.when(k == EXPERTS_PER_GROUP - 1)
    def _():
        _store_row_tiles(y_ref, acc_scr[...], tm)


def moe_grouped(tile_group, ntiles, hs, wrh, wrm, br, wg, wu, wd, tm):
    d = D_MODEL
    full = lambda a: pl.BlockSpec(a.shape, lambda j, k, tg: (0,) * a.ndim)
    emap = lambda j, k, tg: (jnp.maximum(tg[j], 0) * EXPERTS_PER_GROUP + k, 0, 0)
    return pl.pallas_call(
        functools.partial(_moe_group_kernel, tm=tm),
        grid_spec=pltpu.PrefetchScalarGridSpec(
            num_scalar_prefetch=1,
            grid=(ntiles, EXPERTS_PER_GROUP),
            in_specs=[pl.BlockSpec((tm * ROW_TILE, LANES), lambda j, k, tg: (j, 0)),
                      full(wrh), full(wrm), full(br),
                      pl.BlockSpec((1, d, D_EXPERT), emap),
                      pl.BlockSpec((1, d, D_EXPERT), emap),
                      pl.BlockSpec((1, D_EXPERT, d), emap)],
            out_specs=pl.BlockSpec((tm * ROW_TILE, LANES), lambda j, k, tg: (j, 0)),
            scratch_shapes=[pltpu.VMEM((tm, d), BF16), pltpu.VMEM((tm, LANES), F32), pltpu.VMEM((tm, d), F32)]),
        out_shape=jax.ShapeDtypeStruct(hs.shape, F32),
        compiler_params=_cparams(("arbitrary", "arbitrary")),
        name="moe_grouped",
    )(tile_group, hs, wrh, wrm, br, wg, wu, wd)


def _gather_residual_kernel(dest_ref, y_hbm, x_ref, g2_ref, fn_ref, o_ref, buf, sem, *, tm, nsteps, final):
    i = pl.program_id(0)

    def fetch(step, slot):
        def issue(r, c):
            _row_copy(y_hbm, dest_ref[step * tm + r], buf.at[slot], r, sem.at[slot]).start()
            return c
        lax.fori_loop(0, tm, issue, 0, unroll=8)

    @pl.when(i == 0)
    def _():
        fetch(0, 0)

    @pl.when(i + 1 < nsteps)
    def _():
        fetch(i + 1, (i + 1) % 2)

    slot = i % 2

    def drain(r, c):
        _row_copy(y_hbm, 0, buf.at[slot], 0, sem.at[slot]).wait()
        return c

    lax.fori_loop(0, tm, drain, 0, unroll=8)
    x = x_ref[...] + g2_ref[0] * _load_row_tiles(buf.at[slot], tm)
    if final:
        x = x * lax.rsqrt(jnp.mean(x * x, axis=-1, keepdims=True) + EPS) * fn_ref[...]
    o_ref[...] = x


def gather_residual(g, dest, y_tiles, x, gate2, fnorm, final, tm):
    t, d = x.shape
    nsteps = t // tm
    kern = functools.partial(_gather_residual_kernel, tm=tm, nsteps=nsteps, final=final)
    return pl.pallas_call(
        kern,
        grid_spec=pltpu.PrefetchScalarGridSpec(
            num_scalar_prefetch=1,
            grid=(nsteps,),
            in_specs=[pl.BlockSpec(memory_space=pl.ANY),
                      pl.BlockSpec((tm, d), lambda i, dst: (i, 0)),
                      pl.BlockSpec((1, 1, d), lambda i, dst: (g.batch_of_block(i, tm), 0, 0)),
                      pl.BlockSpec((1, d), lambda i, dst: (0, 0))],
            out_specs=pl.BlockSpec((tm, d), lambda i, dst: (i, 0)),
            scratch_shapes=[pltpu.VMEM((2, tm * ROW_TILE, LANES), F32), pltpu.SemaphoreType.DMA((2,))]),
        out_shape=jax.ShapeDtypeStruct((t, d), F32),
        compiler_params=_cparams(("arbitrary",)),
        name="moe_gather_residual",
    )(dest, y_tiles, x, gate2, fnorm)


def _rope_tables(smax, rot_dim, period, lane_off):
    half = rot_dim // 2
    inv_freq = ROPE_THETA ** (-jnp.arange(half, dtype=F32) / half)
    ang = jnp.arange(smax).astype(F32)[:, None] * inv_freq[None, :]
    cos, sin = jnp.cos(ang), jnp.sin(ang)
    gl = (jnp.arange(LANES) % period) - lane_off
    first = (gl >= 0) & (gl < half)
    second = (gl >= half) & (gl < rot_dim)
    j = jnp.clip(jnp.where(first, gl, gl - half), 0, half - 1)
    cl, sn = cos[:, j], sin[:, j]
    ct = jnp.where((first | second)[None, :], cl, 1.0)
    s1 = jnp.where(first[None, :], -sn, 0.0)
    s2 = jnp.where(second[None, :], sn, 0.0)
    return ct, s1, s2


def _pad_w_in(w):
    d = w.shape[0]
    parts, off = [], 0
    for sz in IN_SIZES:
        parts.append(w[:, off:off + sz])
        off += sz
    a_q, a_k, a_v, b_q, b_k, b_v, b_r, b_zf, b_zb, c_cq, c_ckv, c_kr, gates = parts
    z = lambda n: jnp.zeros((d, n), w.dtype)
    bz = jnp.concatenate([b_zf, b_zb, z(LANES - 2 * B_GATE_RANK)], axis=1)
    kr = jnp.concatenate([z(KR_LANE), c_kr, z(LANES - KR_LANE - C_ROPE_DIM)], axis=1)
    out = jnp.concatenate([gates, b_v, b_r, a_q, a_k, a_v, b_q, b_k, c_ckv, bz, c_cq, kr, z(LANES)], axis=1)
    assert out.shape[1] == NPAD
    return out.astype(BF16)


def _pad_heads_cols(w, real, take_lo, take_hi):
    kdim = w.shape[0]
    wh = w.reshape(kdim, C_HEADS, real)[:, :, take_lo:take_hi]
    wh = jnp.pad(wh, ((0, 0), (0, 0), (0, C_HEAD_PAD - (take_hi - take_lo))))
    return wh.reshape(kdim, C_PAD_WIDTH).astype(BF16)


def _gate_weights(w_a, b_a, row_off):
    wa = jnp.zeros((LANES, B_KW), F32).at[row_off:row_off + B_GATE_RANK].set(w_a)
    return wa.astype(BF16), b_a.reshape(1, B_KW).astype(F32)


def kernel(x_prompt, x_sample, c_prompt, c_sample, norm_mix, norm_moe, w_mod, b_mod, w_in, w_gla_af, b_gla_af,
           w_gla_ab, b_gla_ab, gla_norm, mla_q_norm, w_mla_uq, mla_kv_norm, w_mla_ukv, w_proj_a, w_proj_b, w_proj_c,
           w_out, w_router, b_router, w_exp_gate, w_exp_up, w_exp_down, final_norm):
    b1, s1, d = x_prompt.shape
    b2, s2, _ = x_sample.shape
    g = Groups(b1, s1, b2, s2)
    depth = w_in.shape[0]
    tm_in = min(1024, s1, s2)
    tm_prep = min(512, s1, s2)
    tm_proj = min(512, s1, s2)
    tm_moe = min(1024, s1, s2)

    x = jnp.concatenate([x_prompt.reshape(g.t1, d), x_sample.reshape(g.t2, d)], axis=0)
    c = jnp.concatenate([c_prompt, c_sample], axis=0)
    nbp = -(-g.nb // 8) * 8
    c_pad = jnp.pad(c, ((0, nbp - g.nb), (0, 0)))
    mod = modulation(c_pad, w_mod, b_mod)[:, :g.nb]

    smax = max(s1, s2)
    tabs_a = _rope_tables(smax, A_ROPE_DIM, A_HEAD_DIM, 0)
    tabs_c = _rope_tables(smax, C_ROPE_DIM, C_HEAD_PAD, KR_LANE)

    wr_hi = jnp.pad(w_router, ((0, 0), (0, LANES - N_EXPERTS)))
    wrh = wr_hi.astype(BF16)
    wrm = (wr_hi - wrh.astype(F32)).astype(BF16)
    fnorm = final_norm.reshape(1, d)
    br = jnp.pad(b_router.astype(F32), (0, LANES - N_EXPERTS)).reshape(1, LANES)

    for l in range(depth):
        sh1, sc1, gt1, sh2, sc2, gt2 = [m.reshape(g.nb, 1, d) for m in jnp.split(mod[l], N_MOD, axis=-1)]
        mul1 = norm_mix[l][None, None, :] * (1.0 + sc1)
        mul2 = norm_moe[l][None, None, :] * (1.0 + sc2)

        y, yd4, yd16 = in_projection(g, x, mul1, sh1, _pad_w_in(w_in[l]), tabs_a, tm_in)

        nat_cols = tuple((0, off // A_WIDTH) for off in (OFF_AQ, OFF_AK, OFF_AV))
        dil_cols = ((0, 0), (1, 0), (2, 0))
        o_list, lse_list = [], []
        for (_, dil), src in zip(A_PATTERNS, (y.reshape(1, g.t, NPAD), yd4, yd16)):
            o, lse = band_attention(g, src, dil, nat_cols if dil == 1 else dil_cols, NPAD // A_WIDTH if dil == 1 else 1)
            o_list.append(o)
            lse_list.append(lse)

        waf, baf = _gate_weights(w_gla_af[l], b_gla_af[l], 0)
        wab, bab = _gate_weights(w_gla_ab[l], b_gla_ab[l], B_GATE_RANK)
        o_back = gla_pass(g, y, wab, bab, reverse=True)
        yb = gla_pass(g, y, waf, baf, reverse=False, o_back=o_back, gain=gla_norm[l].reshape(1, B_VAL_DIM))

        wq = jnp.concatenate(
            [w_mla_uq[l].reshape(C_Q_RANK, C_HEADS, C_QK_DIM),
             jnp.zeros((C_Q_RANK, C_HEADS, C_HEAD_PAD - C_QK_DIM), F32)], axis=2
        ).reshape(C_Q_RANK, C_PAD_WIDTH).astype(BF16)
        wk = _pad_heads_cols(w_mla_ukv[l], C_NOPE_DIM + C_V_DIM, 0, C_NOPE_DIM)
        wv = _pad_heads_cols(w_mla_ukv[l], C_NOPE_DIM + C_V_DIM, C_NOPE_DIM, C_NOPE_DIM + C_V_DIM)
        qm, km, vm = mla_prep(g, y, mla_q_norm[l].reshape(1, C_Q_RANK), mla_kv_norm[l].reshape(1, C_KV_RANK),
                              wq, wk, wv, tabs_c, tm_prep)
        yc1 = mla_flash(qm, km, vm, 0, b1, s1, min(512, s1), min(512, s1))
        yc2 = mla_flash(qm, km, vm, g.t1, b2, s2, min(512, s2), min(512, s2))

        wpc = jnp.pad(w_proj_c[l].reshape(C_HEADS, C_V_DIM, d), ((0, 0), (0, C_HEAD_PAD - C_V_DIM), (0, 0)))
        wpc = wpc.reshape(C_PAD_WIDTH, d).astype(BF16)
        x, h2, comb = proj_merge(g, o_list, lse_list, yb, yc1, yc2, y, x, gt1, mul2, sh2,
                                 w_proj_a[l].astype(BF16), w_proj_b[l].astype(BF16), wpc, w_out[l].astype(BF16),
                                 wrh, wrm, br, tm_proj)

        dest, tile_group, ntiles = _sorted_layout(comb, tm_moe)
        hs = scatter_rows(dest, h2, ntiles * tm_moe, tm_moe)
        ys = moe_grouped(tile_group, ntiles, hs, wrh, wrm, br, w_exp_gate[l].astype(BF16),
                         w_exp_up[l].astype(BF16), w_exp_down[l].astype(BF16), tm_moe)
        x = gather_residual(g, dest, ys, x, gt2, fnorm, l == depth - 1, tm_proj)

    return x[:g.t1].reshape(b1, s1, d), x[g.t1:].reshape(b2, s2, d)
```

```python
import functools

import jax
import jax.numpy as jnp
from jax import lax
from jax.experimental import pallas as pl
from jax.experimental.pallas import tpu as pltpu

F32 = jnp.float32
BF16 = jnp.bfloat16

D_MODEL = 1024
DEPTH = 2
EPS = 1e-6
ROPE_THETA = 500000.0
NEG_INF = -1e30

A_HEADS = 8
A_HEAD_DIM = 64
A_ROPE_DIM = A_HEAD_DIM // 4
A_PATTERNS = ((128, 1), (512, 4), (2048, 16))
A_WIDTH = A_HEADS * A_HEAD_DIM
A_HALF = 64

B_HEADS = 4
B_KEY_DIM = 128
B_VAL_DIM = 256
B_GATE_RANK = 16
B_GATE_TAU = 16.0
B_CHUNK = 64
B_KW = B_HEADS * B_KEY_DIM
B_VW = B_HEADS * B_VAL_DIM

C_HEADS = 8
C_NOPE_DIM = 64
C_ROPE_DIM = 32
C_V_DIM = 64
C_QK_DIM = C_NOPE_DIM + C_ROPE_DIM
C_Q_RANK = 384
C_KV_RANK = 256
C_HEAD_PAD = 128
C_PAD_WIDTH = C_HEADS * C_HEAD_PAD

N_EXPERTS = 16
N_GROUPS = 4
EXPERTS_PER_GROUP = N_EXPERTS // N_GROUPS
TOP_K = 2
GROUP_SCORE_K = 2
D_EXPERT = 512
N_MOD = 6

IN_SIZES = (A_WIDTH, A_WIDTH, A_WIDTH, B_KW, B_KW, B_VW, B_VW, B_GATE_RANK, B_GATE_RANK,
            C_Q_RANK, C_KV_RANK, C_ROPE_DIM, 3 * D_MODEL)

LANES = 128
ROW_TILE = D_MODEL // LANES
LOG2E = 1.4426950408889634

OFF_GATES = 0
OFF_BV = 3072
OFF_BR = 4096
OFF_AQ = 5120
OFF_AK = 5632
OFF_AV = 6144
OFF_BQ = 6656
OFF_BK = 7168
OFF_CKV = 7680
OFF_BZ = 7936
OFF_CQ = 8064
OFF_KR = 8448
NPAD = 8704
KR_LANE = 64

VMEM_LIMIT = 56 * 1024 * 1024


def _cparams(sem):
    return pltpu.CompilerParams(dimension_semantics=sem, vmem_limit_bytes=VMEM_LIMIT)


class Groups:
    def __init__(self, b1, s1, b2, s2):
        self.b1, self.s1, self.b2, self.s2 = b1, s1, b2, s2
        self.t1, self.t2 = b1 * s1, b2 * s2
        self.t = self.t1 + self.t2
        self.nb = b1 + b2

    def batch_of_block(self, i, tm):
        n1 = self.t1 // tm
        return jnp.where(i < n1, i // (self.s1 // tm), self.b1 + (i - n1) // (self.s2 // tm))

    def pos_block(self, i, tm):
        n1 = self.t1 // tm
        return jnp.where(i < n1, i % (self.s1 // tm), (i - n1) % (self.s2 // tm))


def _dot(a, b):
    return jnp.dot(a, b, preferred_element_type=F32)


def _dot_nt(a, b):
    return lax.dot_general(a, b, (((1,), (1,)), ((), ())), preferred_element_type=F32)


def _dot_tn(a, b):
    return lax.dot_general(a, b, (((0,), (0,)), ((), ())), preferred_element_type=F32)


def _split3(a):
    hi = a.astype(BF16)
    r1 = a - hi.astype(F32)
    mid = r1.astype(BF16)
    lo = (r1 - mid.astype(F32)).astype(BF16)
    return hi, mid, lo


def _rope(y, ct, s1, s2, half):
    return y * ct + pltpu.roll(y, LANES - half, 1) * s1 + pltpu.roll(y, half, 1) * s2


def _sigmoid(x):
    return 0.5 * jnp.tanh(0.5 * x) + 0.5


def _mod_kernel(c_ref, w_ref, b_ref, o_ref):
    c = c_ref[...]
    ca = c * _sigmoid(c)
    ch, cm, _ = _split3(ca)
    wh, wm, _ = _split3(w_ref[0])
    o_ref[0] = _dot(ch, wh) + _dot(cm, wh) + _dot(ch, wm) + b_ref[0]


def modulation(c_pad, w_mod, b_mod):
    nbp, d = c_pad.shape
    depth, _, n = w_mod.shape
    tn = 512
    return pl.pallas_call(
        _mod_kernel,
        grid=(depth, n // tn),
        in_specs=[pl.BlockSpec((nbp, d), lambda l, j: (0, 0)),
                  pl.BlockSpec((1, d, tn), lambda l, j: (l, 0, j)),
                  pl.BlockSpec((1, 1, tn), lambda l, j: (l, 0, j))],
        out_specs=pl.BlockSpec((1, nbp, tn), lambda l, j: (l, 0, j)),
        out_shape=jax.ShapeDtypeStruct((depth, nbp, n), F32),
        compiler_params=_cparams(("arbitrary", "arbitrary")),
        name="modulation",
    )(c_pad, w_mod, b_mod.reshape(depth, 1, n))


def _inproj_kernel(x_ref, mul_ref, sh_ref, w_ref, ct_ref, s1_ref, s2_ref, o_ref, od4_ref, od16_ref,
                   h_scr, y_scr, y4_scr, *, tm, tn, jq, jv):
    j = pl.program_id(1)

    @pl.when(j == 0)
    def _():
        x = x_ref[...]
        ms = jnp.mean(x * x, axis=-1, keepdims=True)
        h = x * lax.rsqrt(ms + EPS) * mul_ref[0] + sh_ref[0]
        h_scr[...] = h.astype(BF16)

    is_attn = (j >= jq) & (j <= jv)

    @pl.when(jnp.logical_not(is_attn))
    def _():
        o_ref[...] = _dot(h_scr[...], w_ref[j]).astype(BF16)

    @pl.when(is_attn)
    def _():
        y = _dot(h_scr[...], w_ref[j])
        ncol = tn // LANES

        @pl.when(j < jv)
        def _():
            ct, s1, s2 = ct_ref[...], s1_ref[...], s2_ref[...]
            for c in range(ncol):
                y_scr[c] = _rope(y[:, c * LANES:(c + 1) * LANES], ct, s1, s2, A_ROPE_DIM // 2)

        @pl.when(j == jv)
        def _():
            for c in range(ncol):
                y_scr[c] = y[:, c * LANES:(c + 1) * LANES]

        n4, n16 = tm // 4, tm // 16
        for c in range(ncol):
            cs = slice(c * LANES, (c + 1) * LANES)
            o_ref[:, cs] = y_scr[c].astype(BF16)
            for r in range(4):
                y4 = y_scr[c, pl.ds(r, n4, stride=4), :]
                y4_scr[c, r * n4:(r + 1) * n4, :] = y4
                od4_ref[:, r * tn + c * LANES:r * tn + (c + 1) * LANES] = y4.astype(BF16)
            for r in range(4):
                for r2 in range(4):
                    col = (r + 4 * r2) * tn + c * LANES
                    od16_ref[:, col:col + LANES] = y4_scr[c, pl.ds(r * n4 + r2, n16, stride=4), :].astype(BF16)


def in_projection(g, x, mul, shift, w_pad, tabs, tm):
    t, d = x.shape
    tn = A_WIDTH
    ct, s1, s2 = tabs
    jq, jv = OFF_AQ // tn, OFF_AV // tn
    bmap = lambda i, j: (g.batch_of_block(i, tm), 0, 0)
    pmap = lambda i, j: (g.pos_block(i, tm), 0)
    dmap = lambda i, j: (jnp.clip(j - jq, 0, jv - jq), i, 0)
    kern = functools.partial(_inproj_kernel, tm=tm, tn=tn, jq=jq, jv=jv)
    return pl.pallas_call(
        kern,
        grid=(t // tm, NPAD // tn),
        in_specs=[pl.BlockSpec((tm, d), lambda i, j: (i, 0)),
                  pl.BlockSpec((1, 1, d), bmap),
                  pl.BlockSpec((1, 1, d), bmap),
                  pl.BlockSpec((NPAD // tn, d, tn), lambda i, j: (0, 0, 0), pipeline_mode=pl.Buffered(1)),
                  pl.BlockSpec((tm, LANES), pmap),
                  pl.BlockSpec((tm, LANES), pmap),
                  pl.BlockSpec((tm, LANES), pmap)],
        out_specs=[pl.BlockSpec((tm, tn), lambda i, j: (i, j)),
                   pl.BlockSpec((None, tm // 4, 4 * tn), dmap),
                   pl.BlockSpec((None, tm // 16, 16 * tn), dmap)],
        out_shape=[jax.ShapeDtypeStruct((t, NPAD), BF16),
                   jax.ShapeDtypeStruct((3, t // 4, 4 * tn), BF16),
                   jax.ShapeDtypeStruct((3, t // 16, 16 * tn), BF16)],
        scratch_shapes=[pltpu.VMEM((tm, d), BF16), pltpu.VMEM((tn // LANES, tm, LANES), F32),
                        pltpu.VMEM((tn // LANES, tm, LANES), F32)],
        compiler_params=_cparams(("arbitrary", "arbitrary")),
        name="in_projection",
    )(x, mul, shift, w_pad.reshape(d, NPAD // tn, tn).transpose(1, 0, 2), ct, s1, s2)


def _band_kernel(q_ref, kp_ref, km_ref, kn_ref, vp_ref, vm_ref, vn_ref, o_ref, lse_ref, *, tb, tq, rows1, l1, l2):
    i = pl.program_id(1)
    tk = tq + 2 * A_HALF
    k = jnp.concatenate([kp_ref[...], km_ref[...], kn_ref[...]], axis=0)
    v = jnp.concatenate([vp_ref[...], vm_ref[...], vn_ref[...]], axis=0)
    qi = lax.broadcasted_iota(jnp.int32, (tq, tk), 0)
    kj = lax.broadcasted_iota(jnp.int32, (tq, tk), 1)
    band = jnp.abs(kj - A_HALF - qi) <= A_HALF
    lo = lax.broadcasted_iota(jnp.int32, (tq, LANES), 1) < A_HEAD_DIM
    scale = A_HEAD_DIM ** -0.5
    for u in range(tb // tq):
        row0 = i * tb + u * tq
        in1 = row0 < rows1
        seq_len = jnp.where(in1, l1, l2)
        pos0 = jnp.where(in1, row0 % l1, (row0 - rows1) % l2)
        kpos = pos0 - A_HALF + kj
        valid = band & (kpos >= 0) & (kpos < seq_len)
        valid2 = jnp.concatenate([valid, valid], axis=0)
        rows = slice(u * tq, (u + 1) * tq)
        krows = slice(u * tq, u * tq + tk)
        for p in range(A_WIDTH // LANES):
            sl = slice(p * LANES, (p + 1) * LANES)
            qp, kp, vp = q_ref[rows, sl] * scale, k[krows, sl], v[krows, sl]
            zero = jnp.zeros_like(qp)
            qm = jnp.concatenate([jnp.where(lo, qp, zero), jnp.where(lo, zero, qp)], axis=0)
            s = jnp.where(valid2, _dot_nt(qm, kp), NEG_INF)
            m = jnp.max(s, axis=1, keepdims=True)
            e = jnp.exp(s - m)
            l = jnp.sum(e, axis=1, keepdims=True)
            o2 = _dot(e.astype(BF16), vp) / l
            lse2 = m + jnp.log(l)
            o_ref[rows, sl] = jnp.where(lo, o2[:tq], o2[tq:]).astype(BF16)
            lse_ref[rows, sl] = jnp.where(lo, lse2[:tq], lse2[tq:])


def band_attention(g, qkv, dil, cols, cpb):
    rows = g.t // dil
    tq = 128
    tb = 256
    sub = tb // A_HALF
    nsub = rows // A_HALF

    def main(c):
        ld, col = cols[c]
        return pl.BlockSpec((None, tb, A_WIDTH), lambda r, i: (ld, i, r * cpb + col))

    def prev(c):
        ld, col = cols[c]
        return pl.BlockSpec((None, A_HALF, A_WIDTH), lambda r, i: (ld, jnp.maximum(i * sub - 1, 0), r * cpb + col))

    def nxt(c):
        ld, col = cols[c]
        return pl.BlockSpec((None, A_HALF, A_WIDTH),
                            lambda r, i: (ld, jnp.minimum((i + 1) * sub, nsub - 1), r * cpb + col))

    assert (g.s1 // dil) % tq == 0 and (g.s2 // dil) % tq == 0 and rows % tb == 0
    kern = functools.partial(_band_kernel, tb=tb, tq=tq, rows1=g.t1 // dil, l1=g.s1 // dil, l2=g.s2 // dil)
    return pl.pallas_call(
        kern,
        grid=(dil, rows // tb),
        in_specs=[main(0), prev(1), main(1), nxt(1), prev(2), main(2), nxt(2)],
        out_specs=[pl.BlockSpec((tb, A_WIDTH), lambda r, i: (i, r)),
                   pl.BlockSpec((tb, A_WIDTH), lambda r, i: (i, r))],
        out_shape=[jax.ShapeDtypeStruct((rows, dil * A_WIDTH), BF16),
                   jax.ShapeDtypeStruct((rows, dil * A_WIDTH), F32)],
        compiler_params=_cparams(("arbitrary", "arbitrary")),
        name="band_attention_d%d" % dil,
    )(qkv, qkv, qkv, qkv, qkv, qkv, qkv)


def _gla_kernel(*refs, reverse, final, tc, nblk, t1, s1, s2):
    if final:
        q_ref, k_ref, v_ref, z_ref, wa_ref, ba_ref, tri_ref, ob_ref, r_ref, gain_ref, o_ref, st_scr = refs
    else:
        q_ref, k_ref, v_ref, z_ref, wa_ref, ba_ref, tri_ref, o_ref, st_scr = refs
    i = pl.program_id(0)
    blk = (nblk - 1 - i) if reverse else i
    row0 = blk * tc
    in1 = row0 < t1
    pos0 = jnp.where(in1, row0 % s1, (row0 - t1) % s2)
    slen = jnp.where(in1, s1, s2)
    start = (pos0 + tc == slen) if reverse else (pos0 == 0)

    @pl.when(start)
    def _():
        st_scr[...] = jnp.zeros_like(st_scr)

    zl = _dot(z_ref[...], wa_ref[...]) + ba_ref[...]
    la = (jnp.minimum(zl, 0.0) - jnp.log(1.0 + jnp.exp(-jnp.abs(zl)))) * (1.0 / B_GATE_TAU)
    hi, mid, _ = _split3(la)
    tri = tri_ref[...]
    tb = tri.shape[0]
    bc_all = jnp.concatenate(
        [_dot(tri, hi[b * tb:(b + 1) * tb]) + _dot(tri, mid[b * tb:(b + 1) * tb]) for b in range(tc // tb)], axis=0)

    qi = lax.broadcasted_iota(jnp.int32, (B_CHUNK, B_CHUNK), 0)
    si = lax.broadcasted_iota(jnp.int32, (B_CHUNK, B_CHUNK), 1)
    mask = (si > qi) if reverse else (si <= qi)
    nch = tc // B_CHUNK
    q_all = q_ref[...].astype(F32) * (B_KEY_DIM ** -0.5)
    k_all = k_ref[...].astype(F32)
    q_dec = (q_all * jnp.exp(bc_all)).astype(BF16)
    k_inv = (k_all * jnp.exp(-bc_all)).astype(BF16)
    st = [st_scr[h] for h in range(B_HEADS)]
    for c in (range(nch - 1, -1, -1) if reverse else range(nch)):
        sl = slice(c * B_CHUNK, (c + 1) * B_CHUNK)
        edge = c * B_CHUNK if reverse else (c + 1) * B_CHUNK - 1
        tot = bc_all[edge:edge + 1]
        k_end = (k_all[sl] * jnp.exp(tot - bc_all[sl])).astype(BF16)
        dec = jnp.exp(tot)
        for h in range(B_HEADS):
            ks = slice(h * B_KEY_DIM, (h + 1) * B_KEY_DIM)
            vs = slice(h * B_VAL_DIM, (h + 1) * B_VAL_DIM)
            v = v_ref[sl, vs]
            att = jnp.where(mask, _dot_nt(q_dec[sl, ks], k_inv[sl, ks]), 0.0).astype(BF16)
            o = _dot(att, v) + _dot_nt(q_dec[sl, ks], st[h].astype(BF16))
            st[h] = st[h] * dec[:, ks] + _dot_tn(v, k_end[:, ks])
            if final:
                o = o + ob_ref[sl, vs]
                on = o * lax.rsqrt(jnp.mean(o * o, axis=-1, keepdims=True) + EPS) * gain_ref[...]
                r = r_ref[sl, vs].astype(F32)
                o_ref[sl, vs] = (on * (r * _sigmoid(r))).astype(BF16)
            else:
                o_ref[sl, vs] = o
    for h in range(B_HEADS):
        st_scr[h] = st[h]


def gla_pass(g, y, wa, ba, reverse, o_back=None, gain=None):
    t = g.t
    tc = min(512, g.s1, g.s2)
    nblk = t // tc
    final = o_back is not None
    rowmap = (lambda i: nblk - 1 - i) if reverse else (lambda i: i)
    row = lambda w, off: pl.BlockSpec((tc, w), lambda i: (rowmap(i), off // w))
    full = lambda a: pl.BlockSpec(a.shape, lambda i: (0,) * a.ndim)
    tb = min(tc, 256)
    ri = jnp.arange(tb)[:, None]
    ci = jnp.arange(tb)[None, :]
    tri = ((ri // B_CHUNK == ci // B_CHUNK) & ((ci >= ri) if reverse else (ci <= ri))).astype(BF16)
    in_specs = [row(B_KW, OFF_BQ), row(B_KW, OFF_BK), row(B_VW, OFF_BV), row(LANES, OFF_BZ),
                full(wa), full(ba), full(tri)]
    args = [y, y, y, y, wa, ba, tri]
    if final:
        in_specs += [row(B_VW, 0), row(B_VW, OFF_BR), full(gain)]
        args += [o_back, y, gain]
    kern = functools.partial(_gla_kernel, reverse=reverse, final=final, tc=tc, nblk=nblk, t1=g.t1, s1=g.s1, s2=g.s2)
    return pl.pallas_call(
        kern,
        grid=(nblk,),
        in_specs=in_specs,
        out_specs=row(B_VW, 0),
        out_shape=jax.ShapeDtypeStruct((t, B_VW), BF16 if final else F32),
        scratch_shapes=[pltpu.VMEM((B_HEADS, B_VAL_DIM, B_KEY_DIM), F32)],
        compiler_params=_cparams(("arbitrary",)),
        name="gla_forward_final" if final else "gla_backward",
    )(*args)


def _mla_prep_kernel(ckv_ref, cq_ref, kr_ref, qn_ref, kvn_ref, wq_ref, wk_ref, wv_ref, ct_ref, s1_ref, s2_ref,
                     q_out, k_out, v_out):
    def norm(ref, gain_ref):
        xf = ref[...].astype(F32)
        return (xf * lax.rsqrt(jnp.mean(xf * xf, axis=-1, keepdims=True) + EPS) * gain_ref[...]).astype(BF16)

    ckv_n = norm(ckv_ref, kvn_ref)
    cq_n = norm(cq_ref, qn_ref)
    q = _dot(cq_n, wq_ref[...])
    kn = _dot(ckv_n, wk_ref[...])
    v = _dot(ckv_n, wv_ref[...])
    ct, s1, s2 = ct_ref[...], s1_ref[...], s2_ref[...]
    half = C_ROPE_DIM // 2
    kr_rot = _rope(kr_ref[...].astype(F32), ct, s1, s2, half)
    ones_lane = lax.broadcasted_iota(jnp.int32, kr_rot.shape, 1) == C_V_DIM
    scale = C_QK_DIM ** -0.5 * LOG2E
    for h in range(C_HEADS):
        sl = slice(h * C_HEAD_PAD, (h + 1) * C_HEAD_PAD)
        q_out[:, sl] = (_rope(q[:, sl], ct, s1, s2, half) * scale).astype(BF16)
        k_out[:, sl] = (kn[:, sl] + kr_rot).astype(BF16)
        v_out[:, sl] = jnp.where(ones_lane, 1.0, v[:, sl]).astype(BF16)


def mla_prep(g, y, qn, kvn, wq, wk, wv, tabs, tm):
    t = g.t
    ct, s1, s2 = tabs
    pmap = lambda i: (g.pos_block(i, tm), 0)
    full = lambda a: pl.BlockSpec(a.shape, lambda i: (0,) * a.ndim)
    out = jax.ShapeDtypeStruct((t, C_PAD_WIDTH), BF16)
    ospec = pl.BlockSpec((tm, C_PAD_WIDTH), lambda i: (i, 0))
    return pl.pallas_call(
        _mla_prep_kernel,
        grid=(t // tm,),
        in_specs=[pl.BlockSpec((tm, C_KV_RANK), lambda i: (i, OFF_CKV // C_KV_RANK)),
                  pl.BlockSpec((tm, C_Q_RANK), lambda i: (i, OFF_CQ // C_Q_RANK)),
                  pl.BlockSpec((tm, LANES), lambda i: (i, OFF_KR // LANES)),
                  full(qn), full(kvn), full(wq), full(wk), full(wv),
                  pl.BlockSpec((tm, LANES), pmap), pl.BlockSpec((tm, LANES), pmap), pl.BlockSpec((tm, LANES), pmap)],
        out_specs=[ospec, ospec, ospec],
        out_shape=[out, out, out],
        compiler_params=_cparams(("arbitrary",)),
        name="mla_prep",
    )(y, y, y, qn, kvn, wq, wk, wv, ct, s1, s2)


def _flash_kernel(q_ref, k_ref, v_ref, o_ref, s0_scr, s1_scr, p_scr, m_scr, a_scr, acc_scr, *, tk, nk, rc, unroll):
    tq = q_ref.shape[0]

    def logits(t, scr):
        off = pl.multiple_of(t * tk, tk)
        scr[...] = _dot_nt(q_ref[...], k_ref[pl.ds(off, tk), :])

    def softmax_pv(t, scr):
        off = pl.multiple_of(t * tk, tk)
        for r in range(tq // rc):
            rows = slice(r * rc, (r + 1) * rc)
            s = scr[rows, :]
            m_old = m_scr[rows, :]
            m_new = jnp.maximum(m_old, jnp.max(s, axis=1, keepdims=True))
            p_scr[rows, :] = jnp.exp2(s - jnp.concatenate([m_new] * (tk // LANES), axis=1)).astype(BF16)
            a_scr[rows, :] = jnp.exp2(m_old - m_new)
            m_scr[rows, :] = m_new
        acc_scr[...] = acc_scr[...] * a_scr[...] + _dot(p_scr[...], v_ref[pl.ds(off, tk), :])

    bufs = (s0_scr, s1_scr)

    def group(t0, last):
        for u in range(unroll):
            if not (last and u == unroll - 1):
                logits(t0 + u + 1, bufs[(u + 1) % 2])
            softmax_pv(t0 + u, bufs[u % 2])

    def body(jj, carry):
        group(unroll * jj, False)
        return carry

    m_scr[...] = jnp.full(m_scr.shape, NEG_INF, F32)
    acc_scr[...] = jnp.zeros(acc_scr.shape, F32)
    logits(0, s0_scr)
    lax.fori_loop(0, nk // unroll - 1, body, 0)
    group(nk - unroll, True)
    acc = acc_scr[...]
    o_ref[...] = (acc / acc[:, C_V_DIM:C_V_DIM + 1]).astype(BF16)


def mla_flash(q, k, v, row_off, nseq, s, tq, tk):
    assert row_off % s == 0 and row_off % tq == 0
    qb0 = row_off // tq
    sb0 = row_off // s
    nq = s // tq
    nk = s // tk
    unroll = 8 if nk % 8 == 0 else (4 if nk % 4 == 0 else 2)
    assert nk % unroll == 0
    kern = functools.partial(_flash_kernel, tk=tk, nk=nk, rc=64, unroll=unroll)
    return pl.pallas_call(
        kern,
        grid=(nseq, C_HEADS, nq),
        scratch_shapes=[pltpu.VMEM((tq, tk), F32), pltpu.VMEM((tq, tk), F32), pltpu.VMEM((tq, tk), BF16),
                        pltpu.VMEM((tq, LANES), F32), pltpu.VMEM((tq, LANES), F32),
                        pltpu.VMEM((tq, C_HEAD_PAD), F32)],
        in_specs=[pl.BlockSpec((tq, C_HEAD_PAD), lambda b, h, i: (qb0 + b * nq + i, h)),
                  pl.BlockSpec((s, C_HEAD_PAD), lambda b, h, i: (sb0 + b, h)),
                  pl.BlockSpec((s, C_HEAD_PAD), lambda b, h, i: (sb0 + b, h))],
        out_specs=pl.BlockSpec((tq, C_HEAD_PAD), lambda b, h, i: (b * nq + i, h)),
        out_shape=jax.ShapeDtypeStruct((nseq * s, C_PAD_WIDTH), BF16),
        compiler_params=_cparams(("arbitrary", "arbitrary", "arbitrary")),
        name="mla_flash_s%d" % s,
    )(q, k, v)


def _store_row_tiles(ref, x, rows):
    for c in range(ROW_TILE):
        ref[pl.ds(c, rows, stride=ROW_TILE), :] = x[:, c * LANES:(c + 1) * LANES]


def _load_row_tiles(ref, rows):
    return jnp.concatenate([ref[pl.ds(c, rows, stride=ROW_TILE), :] for c in range(ROW_TILE)], axis=1)


def _router_scores(h, wrh_ref, wrm_ref):
    hh, hm, _ = _split3(h)
    return _sigmoid(_dot(hh, wrh_ref[...]) + _dot(hm, wrh_ref[...]) + _dot(hh, wrm_ref[...]))


def _rank_in_group(sel, pos):
    rank = jnp.zeros(sel.shape, jnp.int32)
    for k in range(1, EXPERTS_PER_GROUP):
        below = pltpu.roll(sel, k, 1)
        above = pltpu.roll(sel, LANES - k, 1)
        rank += jnp.where((pos >= k) & (below >= sel), 1, 0)
        rank += jnp.where((pos + k < EXPERTS_PER_GROUP) & (above > sel), 1, 0)
    return rank


def _route_in_group(s, bias, g):
    lane = lax.broadcasted_iota(jnp.int32, s.shape, 1)
    rank = _rank_in_group(s + bias, lane % EXPERTS_PER_GROUP)
    chosen = (lane // EXPERTS_PER_GROUP == g) & (rank < TOP_K)
    total = jnp.sum(jnp.where(chosen, s, 0.0), axis=1, keepdims=True)
    return jnp.where(chosen, s / total, 0.0)


def _route(s, bias):
    lane = lax.broadcasted_iota(jnp.int32, s.shape, 1)
    pos = lane % EXPERTS_PER_GROUP
    grp = lane // EXPERTS_PER_GROUP
    sel = s + bias
    rank = _rank_in_group(sel, pos)
    top = rank < GROUP_SCORE_K
    contrib = jnp.where(top, sel, 0.0)
    score = contrib
    for k in range(1, EXPERTS_PER_GROUP):
        score += jnp.where(pos >= k, pltpu.roll(contrib, k, 1), 0.0)
        score += jnp.where(pos + k < EXPERTS_PER_GROUP, pltpu.roll(contrib, LANES - k, 1), 0.0)
    best = lane < N_EXPERTS
    for k in range(1, N_GROUPS):
        earlier = pltpu.roll(score, k * EXPERTS_PER_GROUP, 1)
        later = pltpu.roll(score, LANES - k * EXPERTS_PER_GROUP, 1)
        best &= jnp.logical_not((grp >= k) & (earlier >= score))
        best &= jnp.logical_not((grp + k < N_GROUPS) & (later > score))
    chosen = best & (rank < TOP_K)
    total = jnp.sum(jnp.where(chosen, s, 0.0), axis=1, keepdims=True)
    return jnp.where(chosen, s / total, 0.0)


def _proj_kernel(o1_ref, o2_ref, o3_ref, l1_ref, l2_ref, l3_ref, yb_ref, yc1_ref, yc2_ref, ga_ref, gb_ref, gc_ref,
                 x_ref, g1_ref, mul2_ref, sh2_ref, wpa_ref, wpb_ref, wpc_ref, wout_ref, wrh_ref, wrm_ref, br_ref,
                 x_out, h_out, comb_out, o2_scr, l2_scr, o3_scr, l3_scr, *, tm, n1):
    in_group1 = pl.program_id(0) < n1
    ncol = A_WIDTH // LANES
    hm = tm // 2
    for half in range(2):
        rows = slice(half * hm, (half + 1) * hm)
        for dil, o_ref, l_ref, o_scr, l_scr in ((A_PATTERNS[1][1], o2_ref, l2_ref, o2_scr, l2_scr),
                                               (A_PATTERNS[2][1], o3_ref, l3_ref, o3_scr, l3_scr)):
            drows = slice(half * hm // dil, (half + 1) * hm // dil)
            for r in range(dil):
                for c in range(ncol):
                    sl = slice(r * A_WIDTH + c * LANES, r * A_WIDTH + (c + 1) * LANES)
                    o_scr[c, pl.ds(half * hm + r, hm // dil, stride=dil), :] = o_ref[drows, sl].astype(F32)
                    l_scr[c, pl.ds(half * hm + r, hm // dil, stride=dil), :] = l_ref[drows, sl]
        slabs = lambda scr: jnp.concatenate([scr[c, rows, :] for c in range(ncol)], axis=1)
        l1, l2, l3 = l1_ref[rows, :], slabs(l2_scr), slabs(l3_scr)
        m = jnp.maximum(jnp.maximum(l1, l2), l3)
        e1, e2, e3 = jnp.exp(l1 - m), jnp.exp(l2 - m), jnp.exp(l3 - m)
        ya = (e1 * o1_ref[rows, :].astype(F32) + e2 * slabs(o2_scr) + e3 * slabs(o3_scr)) / (e1 + e2 + e3)
        sig = lambda ref: _sigmoid(ref[rows, :].astype(F32))
        yc = jnp.where(in_group1, yc1_ref[rows, :], yc2_ref[rows, :])
        merged = (sig(ga_ref) * _dot(ya.astype(BF16), wpa_ref[...])
                  + sig(gb_ref) * _dot(yb_ref[rows, :], wpb_ref[...])
                  + sig(gc_ref) * _dot(yc, wpc_ref[...]))
        out = _dot(merged.astype(BF16), wout_ref[...])
        x = x_ref[rows, :] + g1_ref[0] * out
        x_out[rows, :] = x
        h = x * lax.rsqrt(jnp.mean(x * x, axis=-1, keepdims=True) + EPS) * mul2_ref[0] + sh2_ref[0]
        _store_row_tiles(h_out.at[pl.ds(half * hm * ROW_TILE, hm * ROW_TILE), :], h, hm)
        comb_out[rows, :] = _route(_router_scores(h, wrh_ref, wrm_ref), br_ref[...])


def proj_merge(g, o_list, lse_list, yb, yc1, yc2, y, x, gate1, mul2, sh2, wpa, wpb, wpc, wout, wrh, wrm, br, tm):
    t, d = x.shape
    n1 = g.t1 // tm
    n2 = g.t2 // tm
    bmap = lambda i: (g.batch_of_block(i, tm), 0, 0)
    row = lambda w, col=0: pl.BlockSpec((tm, w), lambda i: (i, col))
    dil_row = lambda dil: pl.BlockSpec((tm // dil, dil * A_WIDTH), lambda i: (i, 0))
    full = lambda a: pl.BlockSpec(a.shape, lambda i: (0,) * a.ndim)
    gcol = OFF_GATES // d
    d2, d3 = A_PATTERNS[1][1], A_PATTERNS[2][1]
    attn_specs = [row(A_WIDTH), dil_row(d2), dil_row(d3)]
    return pl.pallas_call(
        functools.partial(_proj_kernel, tm=tm, n1=n1),
        grid=(t // tm,),
        in_specs=attn_specs + attn_specs + [
            row(B_VW),
            pl.BlockSpec((tm, C_PAD_WIDTH), lambda i: (jnp.minimum(i, n1 - 1), 0)),
            pl.BlockSpec((tm, C_PAD_WIDTH), lambda i: (jnp.clip(i - n1, 0, n2 - 1), 0)),
            row(d, gcol), row(d, gcol + 1), row(d, gcol + 2),
            row(d), pl.BlockSpec((1, 1, d), bmap), pl.BlockSpec((1, 1, d), bmap), pl.BlockSpec((1, 1, d), bmap),
            full(wpa), full(wpb), full(wpc), full(wout), full(wrh), full(wrm), full(br)],
        out_specs=[row(d), pl.BlockSpec((tm * ROW_TILE, LANES), lambda i: (i, 0)), row(LANES)],
        out_shape=[jax.ShapeDtypeStruct((t, d), F32), jax.ShapeDtypeStruct((t * ROW_TILE, LANES), F32),
                   jax.ShapeDtypeStruct((t, LANES), F32)],
        scratch_shapes=[pltpu.VMEM((A_WIDTH // LANES, tm, LANES), F32)] * 4,
        compiler_params=_cparams(("arbitrary",)),
        name="proj_merge_route",
    )(*o_list, *lse_list, yb, yc1, yc2, y, y, y, x, gate1, mul2, sh2, wpa, wpb, wpc, wout, wrh, wrm, br)


def _sorted_layout(comb, tmoe):
    t = comb.shape[0]
    gw = comb[:, :N_EXPERTS].reshape(t, N_GROUPS, EXPERTS_PER_GROUP).sum(-1)
    gid = jnp.argmax(gw, axis=-1).astype(jnp.int32)
    onehot = (gid[:, None] == jnp.arange(N_GROUPS, dtype=jnp.int32)[None, :]).astype(jnp.int32)
    incl = jnp.cumsum(onehot, axis=0)
    counts = incl[-1]
    padded = (counts + tmoe - 1) // tmoe * tmoe
    ends = jnp.cumsum(padded)
    dest = jnp.sum(onehot * (incl - onehot + (ends - padded)[None, :]), axis=1).astype(jnp.int32)
    ntiles = t // tmoe + N_GROUPS
    tile_group = jnp.sum((jnp.arange(ntiles, dtype=jnp.int32) * tmoe)[:, None] >= ends[None, :], axis=1)
    tile_group = jnp.where(tile_group < N_GROUPS, tile_group, -1).astype(jnp.int32)
    return dest, tile_group, ntiles


def _row_copy(src, src_row, dst, dst_row, sem):
    return pltpu.make_async_copy(src.at[pl.ds(src_row * ROW_TILE, ROW_TILE), :],
                                 dst.at[pl.ds(dst_row * ROW_TILE, ROW_TILE), :], sem)


def _scatter_rows_kernel(dest_ref, h_ref, init_hbm, hs_hbm, sem, *, tm):
    del init_hbm
    i = pl.program_id(0)

    def issue(r, c):
        _row_copy(h_ref, r, hs_hbm, dest_ref[i * tm + r], sem).start()
        return c

    def drain(r, c):
        _row_copy(h_ref, 0, hs_hbm, 0, sem).wait()
        return c

    lax.fori_loop(0, tm, issue, 0, unroll=8)
    lax.fori_loop(0, tm, drain, 0, unroll=8)


def scatter_rows(dest, h_tiles, rows_out, tm):
    t = dest.shape[0]
    init = jnp.zeros((rows_out * ROW_TILE, LANES), F32)
    return pl.pallas_call(
        functools.partial(_scatter_rows_kernel, tm=tm),
        grid_spec=pltpu.PrefetchScalarGridSpec(
            num_scalar_prefetch=1,
            grid=(t // tm,),
            in_specs=[pl.BlockSpec((tm * ROW_TILE, LANES), lambda i, dst: (i, 0)),
                      pl.BlockSpec(memory_space=pl.ANY)],
            out_specs=pl.BlockSpec(memory_space=pl.ANY),
            scratch_shapes=[pltpu.SemaphoreType.DMA(())]),
        out_shape=jax.ShapeDtypeStruct(init.shape, F32),
        input_output_aliases={2: 0},
        compiler_params=_cparams(("arbitrary",)),
        name="moe_scatter_rows",
    )(dest, h_tiles, init)


def _moe_group_kernel(tg_ref, hs_ref, wrh_ref, wrm_ref, br_ref, wg_ref, wu_ref, wd_ref, y_ref,
                      h_scr, comb_scr, acc_scr, *, tm):
    j = pl.program_id(0)
    k = pl.program_id(1)
    g = tg_ref[j]

    @pl.when(k == 0)
    def _():
        h = _load_row_tiles(hs_ref, tm)
        h_scr[...] = h.astype(BF16)
        comb_scr[...] = _route_in_group(_router_scores(h, wrh_ref, wrm_ref), br_ref[...], g)
        acc_scr[...] = jnp.zeros_like(acc_scr)

    @pl.when(g >= 0)
    def _():
        h = h_scr[...]
        gate = _dot(h, wg_ref[0])
        a = gate * _sigmoid(gate) * _dot(h, wu_ref[0])
        yv = _dot(a.astype(BF16), wd_ref[0])
        comb = comb_scr[...]
        lane = lax.broadcasted_iota(jnp.int32, comb.shape, 1)
        ce = jnp.sum(jnp.where(lane == g * EXPERTS_PER_GROUP + k, comb, 0.0), axis=1, keepdims=True)
        acc_scr[...] += ce * yv

    @pl.when(k == EXPERTS_PER_GROUP - 1)
    def _():
        _store_row_tiles(y_ref, acc_scr[...], tm)


def moe_grouped(tile_group, ntiles, hs, wrh, wrm, br, wg, wu, wd, tm):
    d = D_MODEL
    full = lambda a: pl.BlockSpec(a.shape, lambda j, k, tg: (0,) * a.ndim)
    emap = lambda j, k, tg: (jnp.maximum(tg[j], 0) * EXPERTS_PER_GROUP + k, 0, 0)
    return pl.pallas_call(
        functools.partial(_moe_group_kernel, tm=tm),
        grid_spec=pltpu.PrefetchScalarGridSpec(
            num_scalar_prefetch=1,
            grid=(ntiles, EXPERTS_PER_GROUP),
            in_specs=[pl.BlockSpec((tm * ROW_TILE, LANES), lambda j, k, tg: (j, 0)),
                      full(wrh), full(wrm), full(br),
                      pl.BlockSpec((1, d, D_EXPERT), emap),
                      pl.BlockSpec((1, d, D_EXPERT), emap),
                      pl.BlockSpec((1, D_EXPERT, d), emap)],
            out_specs=pl.BlockSpec((tm * ROW_TILE, LANES), lambda j, k, tg: (j, 0)),
            scratch_shapes=[pltpu.VMEM((tm, d), BF16), pltpu.VMEM((tm, LANES), F32), pltpu.VMEM((tm, d), F32)]),
        out_shape=jax.ShapeDtypeStruct(hs.shape, F32),
        compiler_params=_cparams(("arbitrary", "arbitrary")),
        name="moe_grouped",
    )(tile_group, hs, wrh, wrm, br, wg, wu, wd)


def _gather_residual_kernel(dest_ref, y_hbm, x_ref, g2_ref, fn_ref, o_ref, buf, sem, *, tm, nsteps, final):
    i = pl.program_id(0)

    def fetch(step, slot):
        def issue(r, c):
            _row_copy(y_hbm, dest_ref[step * tm + r], buf.at[slot], r, sem.at[slot]).start()
            return c
        lax.fori_loop(0, tm, issue, 0, unroll=8)

    @pl.when(i == 0)
    def _():
        fetch(0, 0)

    @pl.when(i + 1 < nsteps)
    def _():
        fetch(i + 1, (i + 1) % 2)

    slot = i % 2

    def drain(r, c):
        _row_copy(y_hbm, 0, buf.at[slot], 0, sem.at[slot]).wait()
        return c

    lax.fori_loop(0, tm, drain, 0, unroll=8)
    x = x_ref[...] + g2_ref[0] * _load_row_tiles(buf.at[slot], tm)
    if final:
        x = x * lax.rsqrt(jnp.mean(x * x, axis=-1, keepdims=True) + EPS) * fn_ref[...]
    o_ref[...] = x


def gather_residual(g, dest, y_tiles, x, gate2, fnorm, final, tm):
    t, d = x.shape
    nsteps = t // tm
    kern = functools.partial(_gather_residual_kernel, tm=tm, nsteps=nsteps, final=final)
    return pl.pallas_call(
        kern,
        grid_spec=pltpu.PrefetchScalarGridSpec(
            num_scalar_prefetch=1,
            grid=(nsteps,),
            in_specs=[pl.BlockSpec(memory_space=pl.ANY),
                      pl.BlockSpec((tm, d), lambda i, dst: (i, 0)),
                      pl.BlockSpec((1, 1, d), lambda i, dst: (g.batch_of_block(i, tm), 0, 0)),
                      pl.BlockSpec((1, d), lambda i, dst: (0, 0))],
            out_specs=pl.BlockSpec((tm, d), lambda i, dst: (i, 0)),
            scratch_shapes=[pltpu.VMEM((2, tm * ROW_TILE, LANES), F32), pltpu.SemaphoreType.DMA((2,))]),
        out_shape=jax.ShapeDtypeStruct((t, d), F32),
        compiler_params=_cparams(("arbitrary",)),
        name="moe_gather_residual",
    )(dest, y_tiles, x, gate2, fnorm)


def _rope_tables(smax, rot_dim, period, lane_off):
    half = rot_dim // 2
    inv_freq = ROPE_THETA ** (-jnp.arange(half, dtype=F32) / half)
    ang = jnp.arange(smax).astype(F32)[:, None] * inv_freq[None, :]
    cos, sin = jnp.cos(ang), jnp.sin(ang)
    gl = (jnp.arange(LANES) % period) - lane_off
    first = (gl >= 0) & (gl < half)
    second = (gl >= half) & (gl < rot_dim)
    j = jnp.clip(jnp.where(first, gl, gl - half), 0, half - 1)
    cl, sn = cos[:, j], sin[:, j]
    ct = jnp.where((first | second)[None, :], cl, 1.0)
    s1 = jnp.where(first[None, :], -sn, 0.0)
    s2 = jnp.where(second[None, :], sn, 0.0)
    return ct, s1, s2


def _pad_w_in(w):
    d = w.shape[0]
    parts, off = [], 0
    for sz in IN_SIZES:
        parts.append(w[:, off:off + sz])
        off += sz
    a_q, a_k, a_v, b_q, b_k, b_v, b_r, b_zf, b_zb, c_cq, c_ckv, c_kr, gates = parts
    z = lambda n: jnp.zeros((d, n), w.dtype)
    bz = jnp.concatenate([b_zf, b_zb, z(LANES - 2 * B_GATE_RANK)], axis=1)
    kr = jnp.concatenate([z(KR_LANE), c_kr, z(LANES - KR_LANE - C_ROPE_DIM)], axis=1)
    out = jnp.concatenate([gates, b_v, b_r, a_q, a_k, a_v, b_q, b_k, c_ckv, bz, c_cq, kr, z(LANES)], axis=1)
    assert out.shape[1] == NPAD
    return out.astype(BF16)


def _pad_heads_cols(w, real, take_lo, take_hi):
    kdim = w.shape[0]
    wh = w.reshape(kdim, C_HEADS, real)[:, :, take_lo:take_hi]
    wh = jnp.pad(wh, ((0, 0), (0, 0), (0, C_HEAD_PAD - (take_hi - take_lo))))
    return wh.reshape(kdim, C_PAD_WIDTH).astype(BF16)


def _gate_weights(w_a, b_a, row_off):
    wa = jnp.zeros((LANES, B_KW), F32).at[row_off:row_off + B_GATE_RANK].set(w_a)
    return wa.astype(BF16), b_a.reshape(1, B_KW).astype(F32)


def kernel(x_prompt, x_sample, c_prompt, c_sample, norm_mix, norm_moe, w_mod, b_mod, w_in, w_gla_af, b_gla_af,
           w_gla_ab, b_gla_ab, gla_norm, mla_q_norm, w_mla_uq, mla_kv_norm, w_mla_ukv, w_proj_a, w_proj_b, w_proj_c,
           w_out, w_router, b_router, w_exp_gate, w_exp_up, w_exp_down, final_norm):
    b1, s1, d = x_prompt.shape
    b2, s2, _ = x_sample.shape
    g = Groups(b1, s1, b2, s2)
    depth = w_in.shape[0]
    tm_in = min(1024, s1, s2)
    tm_prep = min(512, s1, s2)
    tm_proj = min(512, s1, s2)
    tm_moe = min(1024, s1, s2)

    x = jnp.concatenate([x_prompt.reshape(g.t1, d), x_sample.reshape(g.t2, d)], axis=0)
    c = jnp.concatenate([c_prompt, c_sample], axis=0)
    nbp = -(-g.nb // 8) * 8
    c_pad = jnp.pad(c, ((0, nbp - g.nb), (0, 0)))
    mod = modulation(c_pad, w_mod, b_mod)[:, :g.nb]

    smax = max(s1, s2)
    tabs_a = _rope_tables(smax, A_ROPE_DIM, A_HEAD_DIM, 0)
    tabs_c = _rope_tables(smax, C_ROPE_DIM, C_HEAD_PAD, KR_LANE)

    wr_hi = jnp.pad(w_router, ((0, 0), (0, LANES - N_EXPERTS)))
    wrh = wr_hi.astype(BF16)
    wrm = (wr_hi - wrh.astype(F32)).astype(BF16)
    fnorm = final_norm.reshape(1, d)
    br = jnp.pad(b_router.astype(F32), (0, LANES - N_EXPERTS)).reshape(1, LANES)

    for l in range(depth):
        sh1, sc1, gt1, sh2, sc2, gt2 = [m.reshape(g.nb, 1, d) for m in jnp.split(mod[l], N_MOD, axis=-1)]
        mul1 = norm_mix[l][None, None, :] * (1.0 + sc1)
        mul2 = norm_moe[l][None, None, :] * (1.0 + sc2)

        y, yd4, yd16 = in_projection(g, x, mul1, sh1, _pad_w_in(w_in[l]), tabs_a, tm_in)

        nat_cols = tuple((0, off // A_WIDTH) for off in (OFF_AQ, OFF_AK, OFF_AV))
        dil_cols = ((0, 0), (1, 0), (2, 0))
        o_list, lse_list = [], []
        for (_, dil), src in zip(A_PATTERNS, (y.reshape(1, g.t, NPAD), yd4, yd16)):
            o, lse = band_attention(g, src, dil, nat_cols if dil == 1 else dil_cols, NPAD // A_WIDTH if dil == 1 else 1)
            o_list.append(o)
            lse_list.append(lse)

        waf, baf = _gate_weights(w_gla_af[l], b_gla_af[l], 0)
        wab, bab = _gate_weights(w_gla_ab[l], b_gla_ab[l], B_GATE_RANK)
        o_back = gla_pass(g, y, wab, bab, reverse=True)
        yb = gla_pass(g, y, waf, baf, reverse=False, o_back=o_back, gain=gla_norm[l].reshape(1, B_VAL_DIM))

        wq = jnp.concatenate(
            [w_mla_uq[l].reshape(C_Q_RANK, C_HEADS, C_QK_DIM),
             jnp.zeros((C_Q_RANK, C_HEADS, C_HEAD_PAD - C_QK_DIM), F32)], axis=2
        ).reshape(C_Q_RANK, C_PAD_WIDTH).astype(BF16)
        wk = _pad_heads_cols(w_mla_ukv[l], C_NOPE_DIM + C_V_DIM, 0, C_NOPE_DIM)
        wv = _pad_heads_cols(w_mla_ukv[l], C_NOPE_DIM + C_V_DIM, C_NOPE_DIM, C_NOPE_DIM + C_V_DIM)
        qm, km, vm = mla_prep(g, y, mla_q_norm[l].reshape(1, C_Q_RANK), mla_kv_norm[l].reshape(1, C_KV_RANK),
                              wq, wk, wv, tabs_c, tm_prep)
        yc1 = mla_flash(qm, km, vm, 0, b1, s1, min(512, s1), min(512, s1))
        yc2 = mla_flash(qm, km, vm, g.t1, b2, s2, min(512, s2), min(512, s2))

        wpc = jnp.pad(w_proj_c[l].reshape(C_HEADS, C_V_DIM, d), ((0, 0), (0, C_HEAD_PAD - C_V_DIM), (0, 0)))
        wpc = wpc.reshape(C_PAD_WIDTH, d).astype(BF16)
        x, h2, comb = proj_merge(g, o_list, lse_list, yb, yc1, yc2, y, x, gt1, mul2, sh2,
                                 w_proj_a[l].astype(BF16), w_proj_b[l].astype(BF16), wpc, w_out[l].astype(BF16),
                                 wrh, wrm, br, tm_proj)

        dest, tile_group, ntiles = _sorted_layout(comb, tm_moe)
        hs = scatter_rows(dest, h2, ntiles * tm_moe, tm_moe)
        ys = moe_grouped(tile_group, ntiles, hs, wrh, wrm, br, w_exp_gate[l].astype(BF16),
                         w_exp_up[l].astype(BF16), w_exp_down[l].astype(BF16), tm_moe)
        x = gather_residual(g, dest, ys, x, gt2, fnorm, l == depth - 1, tm_proj)

    return x[:g.t1].reshape(b1, s1, d), x[g.t1:].reshape(b2, s2, d)
```

```python
import functools

import jax
import jax.numpy as jnp
from jax import lax
from jax.experimental import pallas as pl
from jax.experimental.pallas import tpu as pltpu

F32 = jnp.float32
BF16 = jnp.bfloat16

D_MODEL = 1024
DEPTH = 2
EPS = 1e-6
ROPE_THETA = 500000.0
NEG_INF = -1e30

A_HEADS = 8
A_HEAD_DIM = 64
A_ROPE_DIM = A_HEAD_DIM // 4
A_PATTERNS = ((128, 1), (512, 4), (2048, 16))
A_WIDTH = A_HEADS * A_HEAD_DIM
A_HALF = 64

B_HEADS = 4
B_KEY_DIM = 128
B_VAL_DIM = 256
B_GATE_RANK = 16
B_GATE_TAU = 16.0
B_CHUNK = 64
B_KW = B_HEADS * B_KEY_DIM
B_VW = B_HEADS * B_VAL_DIM

C_HEADS = 8
C_NOPE_DIM = 64
C_ROPE_DIM = 32
C_V_DIM = 64
C_QK_DIM = C_NOPE_DIM + C_ROPE_DIM
C_Q_RANK = 384
C_KV_RANK = 256
C_HEAD_PAD = 128
C_PAD_WIDTH = C_HEADS * C_HEAD_PAD

N_EXPERTS = 16
N_GROUPS = 4
EXPERTS_PER_GROUP = N_EXPERTS // N_GROUPS
TOP_K = 2
GROUP_SCORE_K = 2
D_EXPERT = 512
N_MOD = 6

IN_SIZES = (A_WIDTH, A_WIDTH, A_WIDTH, B_KW, B_KW, B_VW, B_VW, B_GATE_RANK, B_GATE_RANK,
            C_Q_RANK, C_KV_RANK, C_ROPE_DIM, 3 * D_MODEL)

LANES = 128
ROW_TILE = D_MODEL // LANES
MOE_EXPERTS_PER_STEP = 2
LOG2E = 1.4426950408889634

OFF_GATES = 0
OFF_BV = 3072
OFF_BR = 4096
OFF_AQ = 5120
OFF_AK = 5632
OFF_AV = 6144
OFF_BQ = 6656
OFF_BK = 7168
OFF_CKV = 7680
OFF_BZ = 7936
OFF_CQ = 8064
OFF_KR = 8448
NPAD = 8704
KR_LANE = 64

VMEM_LIMIT = 56 * 1024 * 1024


def _cparams(sem):
    return pltpu.CompilerParams(dimension_semantics=sem, vmem_limit_bytes=VMEM_LIMIT)


class Groups:
    def __init__(self, b1, s1, b2, s2):
        self.b1, self.s1, self.b2, self.s2 = b1, s1, b2, s2
        self.t1, self.t2 = b1 * s1, b2 * s2
        self.t = self.t1 + self.t2
        self.nb = b1 + b2

    def batch_of_block(self, i, tm):
        n1 = self.t1 // tm
        return jnp.where(i < n1, i // (self.s1 // tm), self.b1 + (i - n1) // (self.s2 // tm))

    def pos_block(self, i, tm):
        n1 = self.t1 // tm
        return jnp.where(i < n1, i % (self.s1 // tm), (i - n1) % (self.s2 // tm))


def _dot(a, b):
    return jnp.dot(a, b, preferred_element_type=F32)


def _dot_nt(a, b):
    return lax.dot_general(a, b, (((1,), (1,)), ((), ())), preferred_element_type=F32)


def _dot_tn(a, b):
    return lax.dot_general(a, b, (((0,), (0,)), ((), ())), preferred_element_type=F32)


def _split3(a):
    hi = a.astype(BF16)
    r1 = a - hi.astype(F32)
    mid = r1.astype(BF16)
    lo = (r1 - mid.astype(F32)).astype(BF16)
    return hi, mid, lo


def _rope(y, ct, s1, s2, half):
    return y * ct + pltpu.roll(y, LANES - half, 1) * s1 + pltpu.roll(y, half, 1) * s2


def _sigmoid(x):
    return 0.5 * jnp.tanh(0.5 * x) + 0.5


def _mod_kernel(c_ref, w_ref, b_ref, o_ref):
    c = c_ref[...]
    ca = c * _sigmoid(c)
    ch, cm, _ = _split3(ca)
    wh, wm, _ = _split3(w_ref[0])
    o_ref[0] = _dot(ch, wh) + _dot(cm, wh) + _dot(ch, wm) + b_ref[0]


def modulation(c_pad, w_mod, b_mod):
    nbp, d = c_pad.shape
    depth, _, n = w_mod.shape
    tn = 512
    return pl.pallas_call(
        _mod_kernel,
        grid=(depth, n // tn),
        in_specs=[pl.BlockSpec((nbp, d), lambda l, j: (0, 0)),
                  pl.BlockSpec((1, d, tn), lambda l, j: (l, 0, j)),
                  pl.BlockSpec((1, 1, tn), lambda l, j: (l, 0, j))],
        out_specs=pl.BlockSpec((1, nbp, tn), lambda l, j: (l, 0, j)),
        out_shape=jax.ShapeDtypeStruct((depth, nbp, n), F32),
        compiler_params=_cparams(("arbitrary", "arbitrary")),
        name="modulation",
    )(c_pad, w_mod, b_mod.reshape(depth, 1, n))


def _inproj_kernel(x_ref, mul_ref, sh_ref, w_ref, ct_ref, s1_ref, s2_ref, o_ref, od4_ref, od16_ref,
                   h_scr, y_scr, y4_scr, *, tm, tn, jq, jv):
    j = pl.program_id(1)

    @pl.when(j == 0)
    def _():
        x = x_ref[...]
        ms = jnp.mean(x * x, axis=-1, keepdims=True)
        h = x * lax.rsqrt(ms + EPS) * mul_ref[0] + sh_ref[0]
        h_scr[...] = h.astype(BF16)

    is_attn = (j >= jq) & (j <= jv)

    @pl.when(jnp.logical_not(is_attn))
    def _():
        o_ref[...] = _dot(h_scr[...], w_ref[j]).astype(BF16)

    @pl.when(is_attn)
    def _():
        y = _dot(h_scr[...], w_ref[j])
        ncol = tn // LANES

        @pl.when(j < jv)
        def _():
            ct, s1, s2 = ct_ref[...], s1_ref[...], s2_ref[...]
            for c in range(ncol):
                y_scr[c] = _rope(y[:, c * LANES:(c + 1) * LANES], ct, s1, s2, A_ROPE_DIM // 2)

        @pl.when(j == jv)
        def _():
            for c in range(ncol):
                y_scr[c] = y[:, c * LANES:(c + 1) * LANES]

        n4, n16 = tm // 4, tm // 16
        for c in range(ncol):
            cs = slice(c * LANES, (c + 1) * LANES)
            o_ref[:, cs] = y_scr[c].astype(BF16)
            for r in range(4):
                y4 = y_scr[c, pl.ds(r, n4, stride=4), :]
                y4_scr[c, r * n4:(r + 1) * n4, :] = y4
                od4_ref[:, r * tn + c * LANES:r * tn + (c + 1) * LANES] = y4.astype(BF16)
            for r in range(4):
                for r2 in range(4):
                    col = (r + 4 * r2) * tn + c * LANES
                    od16_ref[:, col:col + LANES] = y4_scr[c, pl.ds(r * n4 + r2, n16, stride=4), :].astype(BF16)


def in_projection(g, x, mul, shift, w_pad, tabs, tm):
    t, d = x.shape
    tn = A_WIDTH
    ct, s1, s2 = tabs
    jq, jv = OFF_AQ // tn, OFF_AV // tn
    bmap = lambda i, j: (g.batch_of_block(i, tm), 0, 0)
    pmap = lambda i, j: (g.pos_block(i, tm), 0)
    dmap = lambda i, j: (jnp.clip(j - jq, 0, jv - jq), i, 0)
    kern = functools.partial(_inproj_kernel, tm=tm, tn=tn, jq=jq, jv=jv)
    return pl.pallas_call(
        kern,
        grid=(t // tm, NPAD // tn),
        in_specs=[pl.BlockSpec((tm, d), lambda i, j: (i, 0)),
                  pl.BlockSpec((1, 1, d), bmap),
                  pl.BlockSpec((1, 1, d), bmap),
                  pl.BlockSpec((NPAD // tn, d, tn), lambda i, j: (0, 0, 0), pipeline_mode=pl.Buffered(1)),
                  pl.BlockSpec((tm, LANES), pmap),
                  pl.BlockSpec((tm, LANES), pmap),
                  pl.BlockSpec((tm, LANES), pmap)],
        out_specs=[pl.BlockSpec((tm, tn), lambda i, j: (i, j)),
                   pl.BlockSpec((None, tm // 4, 4 * tn), dmap),
                   pl.BlockSpec((None, tm // 16, 16 * tn), dmap)],
        out_shape=[jax.ShapeDtypeStruct((t, NPAD), BF16),
                   jax.ShapeDtypeStruct((3, t // 4, 4 * tn), BF16),
                   jax.ShapeDtypeStruct((3, t // 16, 16 * tn), BF16)],
        scratch_shapes=[pltpu.VMEM((tm, d), BF16), pltpu.VMEM((tn // LANES, tm, LANES), F32),
                        pltpu.VMEM((tn // LANES, tm, LANES), F32)],
        compiler_params=_cparams(("arbitrary", "arbitrary")),
        name="in_projection",
    )(x, mul, shift, w_pad.reshape(d, NPAD // tn, tn).transpose(1, 0, 2), ct, s1, s2)


def _band_kernel(q_ref, kp_ref, km_ref, kn_ref, vp_ref, vm_ref, vn_ref, o_ref, lse_ref, *, tb, tq, rows1, l1, l2):
    i = pl.program_id(1)
    tk = tq + 2 * A_HALF
    k = jnp.concatenate([kp_ref[...], km_ref[...], kn_ref[...]], axis=0)
    v = jnp.concatenate([vp_ref[...], vm_ref[...], vn_ref[...]], axis=0)
    qi = lax.broadcasted_iota(jnp.int32, (tq, tk), 0)
    kj = lax.broadcasted_iota(jnp.int32, (tq, tk), 1)
    band = jnp.abs(kj - A_HALF - qi) <= A_HALF
    lo = lax.broadcasted_iota(jnp.int32, (tq, LANES), 1) < A_HEAD_DIM
    scale = A_HEAD_DIM ** -0.5
    for u in range(tb // tq):
        row0 = i * tb + u * tq
        in1 = row0 < rows1
        seq_len = jnp.where(in1, l1, l2)
        pos0 = jnp.where(in1, row0 % l1, (row0 - rows1) % l2)
        kpos = pos0 - A_HALF + kj
        valid = band & (kpos >= 0) & (kpos < seq_len)
        valid2 = jnp.concatenate([valid, valid], axis=0)
        rows = slice(u * tq, (u + 1) * tq)
        krows = slice(u * tq, u * tq + tk)
        for p in range(A_WIDTH // LANES):
            sl = slice(p * LANES, (p + 1) * LANES)
            qp, kp, vp = q_ref[rows, sl] * scale, k[krows, sl], v[krows, sl]
            zero = jnp.zeros_like(qp)
            qm = jnp.concatenate([jnp.where(lo, qp, zero), jnp.where(lo, zero, qp)], axis=0)
            s = jnp.where(valid2, _dot_nt(qm, kp), NEG_INF)
            m = jnp.max(s, axis=1, keepdims=True)
            e = jnp.exp(s - m)
            l = jnp.sum(e, axis=1, keepdims=True)
            o2 = _dot(e.astype(BF16), vp) / l
            lse2 = m + jnp.log(l)
            o_ref[rows, sl] = jnp.where(lo, o2[:tq], o2[tq:]).astype(BF16)
            lse_ref[rows, sl] = jnp.where(lo, lse2[:tq], lse2[tq:])


def band_attention(g, qkv, dil, cols, cpb):
    rows = g.t // dil
    tq = 128
    tb = 256
    sub = tb // A_HALF
    nsub = rows // A_HALF

    def main(c):
        ld, col = cols[c]
        return pl.BlockSpec((None, tb, A_WIDTH), lambda r, i: (ld, i, r * cpb + col))

    def prev(c):
        ld, col = cols[c]
        return pl.BlockSpec((None, A_HALF, A_WIDTH), lambda r, i: (ld, jnp.maximum(i * sub - 1, 0), r * cpb + col))

    def nxt(c):
        ld, col = cols[c]
        return pl.BlockSpec((None, A_HALF, A_WIDTH),
                            lambda r, i: (ld, jnp.minimum((i + 1) * sub, nsub - 1), r * cpb + col))

    assert (g.s1 // dil) % tq == 0 and (g.s2 // dil) % tq == 0 and rows % tb == 0
    kern = functools.partial(_band_kernel, tb=tb, tq=tq, rows1=g.t1 // dil, l1=g.s1 // dil, l2=g.s2 // dil)
    return pl.pallas_call(
        kern,
        grid=(dil, rows // tb),
        in_specs=[main(0), prev(1), main(1), nxt(1), prev(2), main(2), nxt(2)],
        out_specs=[pl.BlockSpec((tb, A_WIDTH), lambda r, i: (i, r)),
                   pl.BlockSpec((tb, A_WIDTH), lambda r, i: (i, r))],
        out_shape=[jax.ShapeDtypeStruct((rows, dil * A_WIDTH), BF16),
                   jax.ShapeDtypeStruct((rows, dil * A_WIDTH), F32)],
        compiler_params=_cparams(("arbitrary", "arbitrary")),
        name="band_attention_d%d" % dil,
    )(qkv, qkv, qkv, qkv, qkv, qkv, qkv)


def _gla_kernel(*refs, reverse, final, tc, nblk, t1, s1, s2):
    if final:
        q_ref, k_ref, v_ref, z_ref, wa_ref, ba_ref, tri_ref, ob_ref, r_ref, gain_ref, o_ref, st_scr = refs
    else:
        q_ref, k_ref, v_ref, z_ref, wa_ref, ba_ref, tri_ref, o_ref, st_scr = refs
    i = pl.program_id(0)
    blk = (nblk - 1 - i) if reverse else i
    row0 = blk * tc
    in1 = row0 < t1
    pos0 = jnp.where(in1, row0 % s1, (row0 - t1) % s2)
    slen = jnp.where(in1, s1, s2)
    start = (pos0 + tc == slen) if reverse else (pos0 == 0)

    @pl.when(start)
    def _():
        st_scr[...] = jnp.zeros_like(st_scr)

    zl = _dot(z_ref[...], wa_ref[...]) + ba_ref[...]
    la = (jnp.minimum(zl, 0.0) - jnp.log(1.0 + jnp.exp(-jnp.abs(zl)))) * (1.0 / B_GATE_TAU)
    hi, mid, _ = _split3(la)
    tri = tri_ref[...]
    tb = tri.shape[0]
    bc_all = jnp.concatenate(
        [_dot(tri, hi[b * tb:(b + 1) * tb]) + _dot(tri, mid[b * tb:(b + 1) * tb]) for b in range(tc // tb)], axis=0)

    qi = lax.broadcasted_iota(jnp.int32, (B_CHUNK, B_CHUNK), 0)
    si = lax.broadcasted_iota(jnp.int32, (B_CHUNK, B_CHUNK), 1)
    mask = (si > qi) if reverse else (si <= qi)
    nch = tc // B_CHUNK
    q_all = q_ref[...].astype(F32) * (B_KEY_DIM ** -0.5)
    k_all = k_ref[...].astype(F32)
    q_dec = (q_all * jnp.exp(bc_all)).astype(BF16)
    k_inv = (k_all * jnp.exp(-bc_all)).astype(BF16)
    st = [st_scr[h] for h in range(B_HEADS)]
    for c in (range(nch - 1, -1, -1) if reverse else range(nch)):
        sl = slice(c * B_CHUNK, (c + 1) * B_CHUNK)
        edge = c * B_CHUNK if reverse else (c + 1) * B_CHUNK - 1
        tot = bc_all[edge:edge + 1]
        k_end = (k_all[sl] * jnp.exp(tot - bc_all[sl])).astype(BF16)
        dec = jnp.exp(tot)
        for h in range(B_HEADS):
            ks = slice(h * B_KEY_DIM, (h + 1) * B_KEY_DIM)
            vs = slice(h * B_VAL_DIM, (h + 1) * B_VAL_DIM)
            v = v_ref[sl, vs]
            att = jnp.where(mask, _dot_nt(q_dec[sl, ks], k_inv[sl, ks]), 0.0).astype(BF16)
            o = _dot(att, v) + _dot_nt(q_dec[sl, ks], st[h].astype(BF16))
            st[h] = st[h] * dec[:, ks] + _dot_tn(v, k_end[:, ks])
            if final:
                o = o + ob_ref[sl, vs]
                on = o * lax.rsqrt(jnp.mean(o * o, axis=-1, keepdims=True) + EPS) * gain_ref[...]
                r = r_ref[sl, vs].astype(F32)
                o_ref[sl, vs] = (on * (r * _sigmoid(r))).astype(BF16)
            else:
                o_ref[sl, vs] = o
    for h in range(B_HEADS):
        st_scr[h] = st[h]


def gla_pass(g, y, wa, ba, reverse, o_back=None, gain=None):
    t = g.t
    tc = min(512, g.s1, g.s2)
    nblk = t // tc
    final = o_back is not None
    rowmap = (lambda i: nblk - 1 - i) if reverse else (lambda i: i)
    row = lambda w, off: pl.BlockSpec((tc, w), lambda i: (rowmap(i), off // w))
    full = lambda a: pl.BlockSpec(a.shape, lambda i: (0,) * a.ndim)
    tb = min(tc, 256)
    ri = jnp.arange(tb)[:, None]
    ci = jnp.arange(tb)[None, :]
    tri = ((ri // B_CHUNK == ci // B_CHUNK) & ((ci >= ri) if reverse else (ci <= ri))).astype(BF16)
    in_specs = [row(B_KW, OFF_BQ), row(B_KW, OFF_BK), row(B_VW, OFF_BV), row(LANES, OFF_BZ),
                full(wa), full(ba), full(tri)]
    args = [y, y, y, y, wa, ba, tri]
    if final:
        in_specs += [row(B_VW, 0), row(B_VW, OFF_BR), full(gain)]
        args += [o_back, y, gain]
    kern = functools.partial(_gla_kernel, reverse=reverse, final=final, tc=tc, nblk=nblk, t1=g.t1, s1=g.s1, s2=g.s2)
    return pl.pallas_call(
        kern,
        grid=(nblk,),
        in_specs=in_specs,
        out_specs=row(B_VW, 0),
        out_shape=jax.ShapeDtypeStruct((t, B_VW), BF16 if final else F32),
        scratch_shapes=[pltpu.VMEM((B_HEADS, B_VAL_DIM, B_KEY_DIM), F32)],
        compiler_params=_cparams(("arbitrary",)),
        name="gla_forward_final" if final else "gla_backward",
    )(*args)


def _mla_prep_kernel(ckv_ref, cq_ref, kr_ref, qn_ref, kvn_ref, wq_ref, wk_ref, wv_ref, ct_ref, s1_ref, s2_ref,
                     q_out, k_out, v_out):
    def norm(ref, gain_ref):
        xf = ref[...].astype(F32)
        return (xf * lax.rsqrt(jnp.mean(xf * xf, axis=-1, keepdims=True) + EPS) * gain_ref[...]).astype(BF16)

    ckv_n = norm(ckv_ref, kvn_ref)
    cq_n = norm(cq_ref, qn_ref)
    q = _dot(cq_n, wq_ref[...])
    kn = _dot(ckv_n, wk_ref[...])
    v = _dot(ckv_n, wv_ref[...])
    ct, s1, s2 = ct_ref[...], s1_ref[...], s2_ref[...]
    half = C_ROPE_DIM // 2
    kr_rot = _rope(kr_ref[...].astype(F32), ct, s1, s2, half)
    ones_lane = lax.broadcasted_iota(jnp.int32, kr_rot.shape, 1) == C_V_DIM
    scale = C_QK_DIM ** -0.5 * LOG2E
    for h in range(C_HEADS):
        sl = slice(h * C_HEAD_PAD, (h + 1) * C_HEAD_PAD)
        q_out[:, sl] = (_rope(q[:, sl], ct, s1, s2, half) * scale).astype(BF16)
        k_out[:, sl] = (kn[:, sl] + kr_rot).astype(BF16)
        v_out[:, sl] = jnp.where(ones_lane, 1.0, v[:, sl]).astype(BF16)


def mla_prep(g, y, qn, kvn, wq, wk, wv, tabs, tm):
    t = g.t
    ct, s1, s2 = tabs
    pmap = lambda i: (g.pos_block(i, tm), 0)
    full = lambda a: pl.BlockSpec(a.shape, lambda i: (0,) * a.ndim)
    out = jax.ShapeDtypeStruct((t, C_PAD_WIDTH), BF16)
    ospec = pl.BlockSpec((tm, C_PAD_WIDTH), lambda i: (i, 0))
    return pl.pallas_call(
        _mla_prep_kernel,
        grid=(t // tm,),
        in_specs=[pl.BlockSpec((tm, C_KV_RANK), lambda i: (i, OFF_CKV // C_KV_RANK)),
                  pl.BlockSpec((tm, C_Q_RANK), lambda i: (i, OFF_CQ // C_Q_RANK)),
                  pl.BlockSpec((tm, LANES), lambda i: (i, OFF_KR // LANES)),
                  full(qn), full(kvn), full(wq), full(wk), full(wv),
                  pl.BlockSpec((tm, LANES), pmap), pl.BlockSpec((tm, LANES), pmap), pl.BlockSpec((tm, LANES), pmap)],
        out_specs=[ospec, ospec, ospec],
        out_shape=[out, out, out],
        compiler_params=_cparams(("arbitrary",)),
        name="mla_prep",
    )(y, y, y, qn, kvn, wq, wk, wv, ct, s1, s2)


def _flash_kernel(q_ref, k_ref, v_ref, o_ref, s0_scr, s1_scr, p_scr, m_scr, a_scr, acc_scr, *, tk, nk, rc, unroll):
    tq = q_ref.shape[0]

    def logits(t, scr):
        off = pl.multiple_of(t * tk, tk)
        scr[...] = _dot_nt(q_ref[...], k_ref[pl.ds(off, tk), :])

    def softmax_pv(t, scr):
        off = pl.multiple_of(t * tk, tk)
        for r in range(tq // rc):
            rows = slice(r * rc, (r + 1) * rc)
            s = scr[rows, :]
            m_old = m_scr[rows, :]
            m_new = jnp.maximum(m_old, jnp.max(s, axis=1, keepdims=True))
            p_scr[rows, :] = jnp.exp2(s - jnp.concatenate([m_new] * (tk // LANES), axis=1)).astype(BF16)
            a_scr[rows, :] = jnp.exp2(m_old - m_new)
            m_scr[rows, :] = m_new
        acc_scr[...] = acc_scr[...] * a_scr[...] + _dot(p_scr[...], v_ref[pl.ds(off, tk), :])

    bufs = (s0_scr, s1_scr)

    def group(t0, last):
        for u in range(unroll):
            if not (last and u == unroll - 1):
                logits(t0 + u + 1, bufs[(u + 1) % 2])
            softmax_pv(t0 + u, bufs[u % 2])

    def body(jj, carry):
        group(unroll * jj, False)
        return carry

    m_scr[...] = jnp.full(m_scr.shape, NEG_INF, F32)
    acc_scr[...] = jnp.zeros(acc_scr.shape, F32)
    logits(0, s0_scr)
    lax.fori_loop(0, nk // unroll - 1, body, 0)
    group(nk - unroll, True)
    acc = acc_scr[...]
    o_ref[...] = (acc / acc[:, C_V_DIM:C_V_DIM + 1]).astype(BF16)


def mla_flash(q, k, v, row_off, nseq, s, tq, tk):
    assert row_off % s == 0 and row_off % tq == 0
    qb0 = row_off // tq
    sb0 = row_off // s
    nq = s // tq
    nk = s // tk
    unroll = next(u for u in (16, 8, 4, 2) if nk % u == 0)
    assert nk % unroll == 0
    kern = functools.partial(_flash_kernel, tk=tk, nk=nk, rc=64, unroll=unroll)
    return pl.pallas_call(
        kern,
        grid=(nseq, C_HEADS, nq),
        scratch_shapes=[pltpu.VMEM((tq, tk), F32), pltpu.VMEM((tq, tk), F32), pltpu.VMEM((tq, tk), BF16),
                        pltpu.VMEM((tq, LANES), F32), pltpu.VMEM((tq, LANES), F32),
                        pltpu.VMEM((tq, C_HEAD_PAD), F32)],
        in_specs=[pl.BlockSpec((tq, C_HEAD_PAD), lambda b, h, i: (qb0 + b * nq + i, h)),
                  pl.BlockSpec((s, C_HEAD_PAD), lambda b, h, i: (sb0 + b, h)),
                  pl.BlockSpec((s, C_HEAD_PAD), lambda b, h, i: (sb0 + b, h))],
        out_specs=pl.BlockSpec((tq, C_HEAD_PAD), lambda b, h, i: (b * nq + i, h)),
        out_shape=jax.ShapeDtypeStruct((nseq * s, C_PAD_WIDTH), BF16),
        compiler_params=_cparams(("arbitrary", "arbitrary", "arbitrary")),
        name="mla_flash_s%d" % s,
    )(q, k, v)


def _store_row_tiles(ref, x, rows):
    for c in range(ROW_TILE):
        ref[pl.ds(c, rows, stride=ROW_TILE), :] = x[:, c * LANES:(c + 1) * LANES]


def _load_row_tiles(ref, rows):
    return jnp.concatenate([ref[pl.ds(c, rows, stride=ROW_TILE), :] for c in range(ROW_TILE)], axis=1)


def _router_scores(h, wrh_ref, wrm_ref):
    hh, hm, _ = _split3(h)
    return _sigmoid(_dot(hh, wrh_ref[...]) + _dot(hm, wrh_ref[...]) + _dot(hh, wrm_ref[...]))


def _rank_in_group(sel, pos):
    rank = jnp.zeros(sel.shape, jnp.int32)
    for k in range(1, EXPERTS_PER_GROUP):
        below = pltpu.roll(sel, k, 1)
        above = pltpu.roll(sel, LANES - k, 1)
        rank += jnp.where((pos >= k) & (below >= sel), 1, 0)
        rank += jnp.where((pos + k < EXPERTS_PER_GROUP) & (above > sel), 1, 0)
    return rank


def _route_in_group(s, bias, g):
    lane = lax.broadcasted_iota(jnp.int32, s.shape, 1)
    rank = _rank_in_group(s + bias, lane % EXPERTS_PER_GROUP)
    chosen = (lane // EXPERTS_PER_GROUP == g) & (rank < TOP_K)
    total = jnp.sum(jnp.where(chosen, s, 0.0), axis=1, keepdims=True)
    return jnp.where(chosen, s / total, 0.0)


def _route(s, bias):
    lane = lax.broadcasted_iota(jnp.int32, s.shape, 1)
    pos = lane % EXPERTS_PER_GROUP
    grp = lane // EXPERTS_PER_GROUP
    sel = s + bias
    rank = _rank_in_group(sel, pos)
    top = rank < GROUP_SCORE_K
    contrib = jnp.where(top, sel, 0.0)
    score = contrib
    for k in range(1, EXPERTS_PER_GROUP):
        score += jnp.where(pos >= k, pltpu.roll(contrib, k, 1), 0.0)
        score += jnp.where(pos + k < EXPERTS_PER_GROUP, pltpu.roll(contrib, LANES - k, 1), 0.0)
    best = lane < N_EXPERTS
    for k in range(1, N_GROUPS):
        earlier = pltpu.roll(score, k * EXPERTS_PER_GROUP, 1)
        later = pltpu.roll(score, LANES - k * EXPERTS_PER_GROUP, 1)
        best &= jnp.logical_not((grp >= k) & (earlier >= score))
        best &= jnp.logical_not((grp + k < N_GROUPS) & (later > score))
    chosen = best & (rank < TOP_K)
    total = jnp.sum(jnp.where(chosen, s, 0.0), axis=1, keepdims=True)
    return jnp.where(chosen, s / total, 0.0)


def _proj_kernel(o1_ref, o2_ref, o3_ref, l1_ref, l2_ref, l3_ref, yb_ref, yc1_ref, yc2_ref, ga_ref, gb_ref, gc_ref,
                 x_ref, g1_ref, mul2_ref, sh2_ref, wpa_ref, wpb_ref, wpc_ref, wout_ref, wrh_ref, wrm_ref, br_ref,
                 x_out, h_out, comb_out, o2_scr, l2_scr, o3_scr, l3_scr, *, tm, n1):
    in_group1 = pl.program_id(0) < n1
    ncol = A_WIDTH // LANES
    hm = tm // 2
    for half in range(2):
        rows = slice(half * hm, (half + 1) * hm)
        for dil, o_ref, l_ref, o_scr, l_scr in ((A_PATTERNS[1][1], o2_ref, l2_ref, o2_scr, l2_scr),
                                               (A_PATTERNS[2][1], o3_ref, l3_ref, o3_scr, l3_scr)):
            drows = slice(half * hm // dil, (half + 1) * hm // dil)
            for r in range(dil):
                for c in range(ncol):
                    sl = slice(r * A_WIDTH + c * LANES, r * A_WIDTH + (c + 1) * LANES)
                    o_scr[c, pl.ds(half * hm + r, hm // dil, stride=dil), :] = o_ref[drows, sl].astype(F32)
                    l_scr[c, pl.ds(half * hm + r, hm // dil, stride=dil), :] = l_ref[drows, sl]
        slabs = lambda scr: jnp.concatenate([scr[c, rows, :] for c in range(ncol)], axis=1)
        l1, l2, l3 = l1_ref[rows, :], slabs(l2_scr), slabs(l3_scr)
        m = jnp.maximum(jnp.maximum(l1, l2), l3)
        e1, e2, e3 = jnp.exp(l1 - m), jnp.exp(l2 - m), jnp.exp(l3 - m)
        ya = (e1 * o1_ref[rows, :].astype(F32) + e2 * slabs(o2_scr) + e3 * slabs(o3_scr)) / (e1 + e2 + e3)
        sig = lambda ref: _sigmoid(ref[rows, :].astype(F32))
        yc = jnp.where(in_group1, yc1_ref[rows, :], yc2_ref[rows, :])
        merged = (sig(ga_ref) * _dot(ya.astype(BF16), wpa_ref[...])
                  + sig(gb_ref) * _dot(yb_ref[rows, :], wpb_ref[...])
                  + sig(gc_ref) * _dot(yc, wpc_ref[...]))
        out = _dot(merged.astype(BF16), wout_ref[...])
        x = x_ref[rows, :] + g1_ref[0] * out
        x_out[rows, :] = x
        h = x * lax.rsqrt(jnp.mean(x * x, axis=-1, keepdims=True) + EPS) * mul2_ref[0] + sh2_ref[0]
        _store_row_tiles(h_out.at[pl.ds(half * hm * ROW_TILE, hm * ROW_TILE), :], h, hm)
        comb_out[rows, :] = _route(_router_scores(h, wrh_ref, wrm_ref), br_ref[...])


def proj_merge(g, o_list, lse_list, yb, yc1, yc2, y, x, gate1, mul2, sh2, wpa, wpb, wpc, wout, wrh, wrm, br, tm):
    t, d = x.shape
    n1 = g.t1 // tm
    n2 = g.t2 // tm
    bmap = lambda i: (g.batch_of_block(i, tm), 0, 0)
    row = lambda w, col=0: pl.BlockSpec((tm, w), lambda i: (i, col))
    dil_row = lambda dil: pl.BlockSpec((tm // dil, dil * A_WIDTH), lambda i: (i, 0))
    full = lambda a: pl.BlockSpec(a.shape, lambda i: (0,) * a.ndim)
    gcol = OFF_GATES // d
    d2, d3 = A_PATTERNS[1][1], A_PATTERNS[2][1]
    attn_specs = [row(A_WIDTH), dil_row(d2), dil_row(d3)]
    return pl.pallas_call(
        functools.partial(_proj_kernel, tm=tm, n1=n1),
        grid=(t // tm,),
        in_specs=attn_specs + attn_specs + [
            row(B_VW),
            pl.BlockSpec((tm, C_PAD_WIDTH), lambda i: (jnp.minimum(i, n1 - 1), 0)),
            pl.BlockSpec((tm, C_PAD_WIDTH), lambda i: (jnp.clip(i - n1, 0, n2 - 1), 0)),
            row(d, gcol), row(d, gcol + 1), row(d, gcol + 2),
            row(d), pl.BlockSpec((1, 1, d), bmap), pl.BlockSpec((1, 1, d), bmap), pl.BlockSpec((1, 1, d), bmap),
            full(wpa), full(wpb), full(wpc), full(wout), full(wrh), full(wrm), full(br)],
        out_specs=[row(d), pl.BlockSpec((tm * ROW_TILE, LANES), lambda i: (i, 0)), row(LANES)],
        out_shape=[jax.ShapeDtypeStruct((t, d), F32), jax.ShapeDtypeStruct((t * ROW_TILE, LANES), F32),
                   jax.ShapeDtypeStruct((t, LANES), F32)],
        scratch_shapes=[pltpu.VMEM((A_WIDTH // LANES, tm, LANES), F32)] * 4,
        compiler_params=_cparams(("arbitrary",)),
        name="proj_merge_route",
    )(*o_list, *lse_list, yb, yc1, yc2, y, y, y, x, gate1, mul2, sh2, wpa, wpb, wpc, wout, wrh, wrm, br)


def _sorted_layout(comb, tmoe):
    t = comb.shape[0]
    gw = comb[:, :N_EXPERTS].reshape(t, N_GROUPS, EXPERTS_PER_GROUP).sum(-1)
    gid = jnp.argmax(gw, axis=-1).astype(jnp.int32)
    onehot = (gid[:, None] == jnp.arange(N_GROUPS, dtype=jnp.int32)[None, :]).astype(jnp.int32)
    incl = jnp.cumsum(onehot, axis=0)
    counts = incl[-1]
    padded = (counts + tmoe - 1) // tmoe * tmoe
    ends = jnp.cumsum(padded)
    dest = jnp.sum(onehot * (incl - onehot + (ends - padded)[None, :]), axis=1).astype(jnp.int32)
    ntiles = t // tmoe + N_GROUPS
    tile_group = jnp.sum((jnp.arange(ntiles, dtype=jnp.int32) * tmoe)[:, None] >= ends[None, :], axis=1)
    tile_group = jnp.where(tile_group < N_GROUPS, tile_group, -1).astype(jnp.int32)
    return dest, tile_group, ntiles


def _row_copy(src, src_row, dst, dst_row, sem):
    return pltpu.make_async_copy(src.at[pl.ds(src_row * ROW_TILE, ROW_TILE), :],
                                 dst.at[pl.ds(dst_row * ROW_TILE, ROW_TILE), :], sem)


def _scatter_rows_kernel(dest_ref, h_ref, init_hbm, hs_hbm, sem, *, tm):
    del init_hbm
    i = pl.program_id(0)

    def issue(r, c):
        _row_copy(h_ref, r, hs_hbm, dest_ref[i * tm + r], sem).start()
        return c

    def drain(r, c):
        _row_copy(h_ref, 0, hs_hbm, 0, sem).wait()
        return c

    lax.fori_loop(0, tm, issue, 0, unroll=8)
    lax.fori_loop(0, tm, drain, 0, unroll=8)


def scatter_rows(dest, h_tiles, rows_out, tm):
    t = dest.shape[0]
    init = jnp.zeros((rows_out * ROW_TILE, LANES), F32)
    return pl.pallas_call(
        functools.partial(_scatter_rows_kernel, tm=tm),
        grid_spec=pltpu.PrefetchScalarGridSpec(
            num_scalar_prefetch=1,
            grid=(t // tm,),
            in_specs=[pl.BlockSpec((tm * ROW_TILE, LANES), lambda i, dst: (i, 0)),
                      pl.BlockSpec(memory_space=pl.ANY)],
            out_specs=pl.BlockSpec(memory_space=pl.ANY),
            scratch_shapes=[pltpu.SemaphoreType.DMA(())]),
        out_shape=jax.ShapeDtypeStruct(init.shape, F32),
        input_output_aliases={2: 0},
        compiler_params=_cparams(("arbitrary",)),
        name="moe_scatter_rows",
    )(dest, h_tiles, init)


def _moe_group_kernel(tg_ref, hs_ref, wrh_ref, wrm_ref, br_ref, wg_ref, wu_ref, wd_ref, y_ref,
                      h_scr, comb_scr, acc_scr, *, tm):
    j = pl.program_id(0)
    k = pl.program_id(1)
    g = tg_ref[j]

    @pl.when(k == 0)
    def _():
        h = _load_row_tiles(hs_ref, tm)
        h_scr[...] = h.astype(BF16)
        comb_scr[...] = _route_in_group(_router_scores(h, wrh_ref, wrm_ref), br_ref[...], g)
        acc_scr[...] = jnp.zeros_like(acc_scr)

    @pl.when(g >= 0)
    def _():
        h = h_scr[...]
        comb = comb_scr[...]
        lane = lax.broadcasted_iota(jnp.int32, comb.shape, 1)
        total = acc_scr[...]
        for kk in range(MOE_EXPERTS_PER_STEP):
            gate = _dot(h, wg_ref[kk])
            a = gate * _sigmoid(gate) * _dot(h, wu_ref[kk])
            yv = _dot(a.astype(BF16), wd_ref[kk])
            e = g * EXPERTS_PER_GROUP + k * MOE_EXPERTS_PER_STEP + kk
            total = total + jnp.sum(jnp.where(lane == e, comb, 0.0), axis=1, keepdims=True) * yv
        acc_scr[...] = total

    @pl.when(k == EXPERTS_PER_GROUP // MOE_EXPERTS_PER_STEP - 1)
    def _():
        _store_row_tiles(y_ref, acc_scr[...], tm)


def moe_grouped(tile_group, ntiles, hs, wrh, wrm, br, wg, wu, wd, tm):
    d = D_MODEL
    full = lambda a: pl.BlockSpec(a.shape, lambda j, k, tg: (0,) * a.ndim)
    ksteps = EXPERTS_PER_GROUP // MOE_EXPERTS_PER_STEP
    emap = lambda j, k, tg: (jnp.maximum(tg[j], 0) * ksteps + k, 0, 0)
    return pl.pallas_call(
        functools.partial(_moe_group_kernel, tm=tm),
        grid_spec=pltpu.PrefetchScalarGridSpec(
            num_scalar_prefetch=1,
            grid=(ntiles, ksteps),
            in_specs=[pl.BlockSpec((tm * ROW_TILE, LANES), lambda j, k, tg: (j, 0)),
                      full(wrh), full(wrm), full(br),
                      pl.BlockSpec((MOE_EXPERTS_PER_STEP, d, D_EXPERT), emap),
                      pl.BlockSpec((MOE_EXPERTS_PER_STEP, d, D_EXPERT), emap),
                      pl.BlockSpec((MOE_EXPERTS_PER_STEP, D_EXPERT, d), emap)],
            out_specs=pl.BlockSpec((tm * ROW_TILE, LANES), lambda j, k, tg: (j, 0)),
            scratch_shapes=[pltpu.VMEM((tm, d), BF16), pltpu.VMEM((tm, LANES), F32), pltpu.VMEM((tm, d), F32)]),
        out_shape=jax.ShapeDtypeStruct(hs.shape, F32),
        compiler_params=_cparams(("arbitrary", "arbitrary")),
        name="moe_grouped",
    )(tile_group, hs, wrh, wrm, br, wg, wu, wd)


def _gather_residual_kernel(dest_ref, y_hbm, x_ref, g2_ref, fn_ref, o_ref, buf, sem, *, tm, nsteps, final):
    i = pl.program_id(0)

    def fetch(step, slot):
        def issue(r, c):
            _row_copy(y_hbm, dest_ref[step * tm + r], buf.at[slot], r, sem.at[slot]).start()
            return c
        lax.fori_loop(0, tm, issue, 0, unroll=8)

    @pl.when(i == 0)
    def _():
        fetch(0, 0)

    @pl.when(i + 1 < nsteps)
    def _():
        fetch(i + 1, (i + 1) % 2)

    slot = i % 2

    def drain(r, c):
        _row_copy(y_hbm, 0, buf.at[slot], 0, sem.at[slot]).wait()
        return c

    lax.fori_loop(0, tm, drain, 0, unroll=8)
    x = x_ref[...] + g2_ref[0] * _load_row_tiles(buf.at[slot], tm)
    if final:
        x = x * lax.rsqrt(jnp.mean(x * x, axis=-1, keepdims=True) + EPS) * fn_ref[...]
    o_ref[...] = x


def gather_residual(g, dest, y_tiles, x, gate2, fnorm, final, tm):
    t, d = x.shape
    nsteps = t // tm
    kern = functools.partial(_gather_residual_kernel, tm=tm, nsteps=nsteps, final=final)
    return pl.pallas_call(
        kern,
        grid_spec=pltpu.PrefetchScalarGridSpec(
            num_scalar_prefetch=1,
            grid=(nsteps,),
            in_specs=[pl.BlockSpec(memory_space=pl.ANY),
                      pl.BlockSpec((tm, d), lambda i, dst: (i, 0)),
                      pl.BlockSpec((1, 1, d), lambda i, dst: (g.batch_of_block(i, tm), 0, 0)),
                      pl.BlockSpec((1, d), lambda i, dst: (0, 0))],
            out_specs=pl.BlockSpec((tm, d), lambda i, dst: (i, 0)),
            scratch_shapes=[pltpu.VMEM((2, tm * ROW_TILE, LANES), F32), pltpu.SemaphoreType.DMA((2,))]),
        out_shape=jax.ShapeDtypeStruct((t, d), F32),
        compiler_params=_cparams(("arbitrary",)),
        name="moe_gather_residual",
    )(dest, y_tiles, x, gate2, fnorm)


def _rope_tables(smax, rot_dim, period, lane_off):
    half = rot_dim // 2
    inv_freq = ROPE_THETA ** (-jnp.arange(half, dtype=F32) / half)
    ang = jnp.arange(smax).astype(F32)[:, None] * inv_freq[None, :]
    cos, sin = jnp.cos(ang), jnp.sin(ang)
    gl = (jnp.arange(LANES) % period) - lane_off
    first = (gl >= 0) & (gl < half)
    second = (gl >= half) & (gl < rot_dim)
    j = jnp.clip(jnp.where(first, gl, gl - half), 0, half - 1)
    cl, sn = cos[:, j], sin[:, j]
    ct = jnp.where((first | second)[None, :], cl, 1.0)
    s1 = jnp.where(first[None, :], -sn, 0.0)
    s2 = jnp.where(second[None, :], sn, 0.0)
    return ct, s1, s2


def _pad_w_in(w):
    d = w.shape[0]
    parts, off = [], 0
    for sz in IN_SIZES:
        parts.append(w[:, off:off + sz])
        off += sz
    a_q, a_k, a_v, b_q, b_k, b_v, b_r, b_zf, b_zb, c_cq, c_ckv, c_kr, gates = parts
    z = lambda n: jnp.zeros((d, n), w.dtype)
    bz = jnp.concatenate([b_zf, b_zb, z(LANES - 2 * B_GATE_RANK)], axis=1)
    kr = jnp.concatenate([z(KR_LANE), c_kr, z(LANES - KR_LANE - C_ROPE_DIM)], axis=1)
    out = jnp.concatenate([gates, b_v, b_r, a_q, a_k, a_v, b_q, b_k, c_ckv, bz, c_cq, kr, z(LANES)], axis=1)
    assert out.shape[1] == NPAD
    return out.astype(BF16)


def _pad_heads_cols(w, real, take_lo, take_hi):
    kdim = w.shape[0]
    wh = w.reshape(kdim, C_HEADS, real)[:, :, take_lo:take_hi]
    wh = jnp.pad(wh, ((0, 0), (0, 0), (0, C_HEAD_PAD - (take_hi - take_lo))))
    return wh.reshape(kdim, C_PAD_WIDTH).astype(BF16)


def _gate_weights(w_a, b_a, row_off):
    wa = jnp.zeros((LANES, B_KW), F32).at[row_off:row_off + B_GATE_RANK].set(w_a)
    return wa.astype(BF16), b_a.reshape(1, B_KW).astype(F32)


def kernel(x_prompt, x_sample, c_prompt, c_sample, norm_mix, norm_moe, w_mod, b_mod, w_in, w_gla_af, b_gla_af,
           w_gla_ab, b_gla_ab, gla_norm, mla_q_norm, w_mla_uq, mla_kv_norm, w_mla_ukv, w_proj_a, w_proj_b, w_proj_c,
           w_out, w_router, b_router, w_exp_gate, w_exp_up, w_exp_down, final_norm):
    b1, s1, d = x_prompt.shape
    b2, s2, _ = x_sample.shape
    g = Groups(b1, s1, b2, s2)
    depth = w_in.shape[0]
    tm_in = min(1024, s1, s2)
    tm_prep = min(512, s1, s2)
    tm_proj = min(512, s1, s2)
    tm_moe = min(1024, s1, s2)

    x = jnp.concatenate([x_prompt.reshape(g.t1, d), x_sample.reshape(g.t2, d)], axis=0)
    c = jnp.concatenate([c_prompt, c_sample], axis=0)
    nbp = -(-g.nb // 8) * 8
    c_pad = jnp.pad(c, ((0, nbp - g.nb), (0, 0)))
    mod = modulation(c_pad, w_mod, b_mod)[:, :g.nb]

    smax = max(s1, s2)
    tabs_a = _rope_tables(smax, A_ROPE_DIM, A_HEAD_DIM, 0)
    tabs_c = _rope_tables(smax, C_ROPE_DIM, C_HEAD_PAD, KR_LANE)

    wr_hi = jnp.pad(w_router, ((0, 0), (0, LANES - N_EXPERTS)))
    wrh = wr_hi.astype(BF16)
    wrm = (wr_hi - wrh.astype(F32)).astype(BF16)
    fnorm = final_norm.reshape(1, d)
    br = jnp.pad(b_router.astype(F32), (0, LANES - N_EXPERTS)).reshape(1, LANES)

    for l in range(depth):
        sh1, sc1, gt1, sh2, sc2, gt2 = [m.reshape(g.nb, 1, d) for m in jnp.split(mod[l], N_MOD, axis=-1)]
        mul1 = norm_mix[l][None, None, :] * (1.0 + sc1)
        mul2 = norm_moe[l][None, None, :] * (1.0 + sc2)

        y, yd4, yd16 = in_projection(g, x, mul1, sh1, _pad_w_in(w_in[l]), tabs_a, tm_in)

        nat_cols = tuple((0, off // A_WIDTH) for off in (OFF_AQ, OFF_AK, OFF_AV))
        dil_cols = ((0, 0), (1, 0), (2, 0))
        o_list, lse_list = [], []
        for (_, dil), src in zip(A_PATTERNS, (y.reshape(1, g.t, NPAD), yd4, yd16)):
            o, lse = band_attention(g, src, dil, nat_cols if dil == 1 else dil_cols, NPAD // A_WIDTH if dil == 1 else 1)
            o_list.append(o)
            lse_list.append(lse)

        waf, baf = _gate_weights(w_gla_af[l], b_gla_af[l], 0)
        wab, bab = _gate_weights(w_gla_ab[l], b_gla_ab[l], B_GATE_RANK)
        o_back = gla_pass(g, y, wab, bab, reverse=True)
        yb = gla_pass(g, y, waf, baf, reverse=False, o_back=o_back, gain=gla_norm[l].reshape(1, B_VAL_DIM))

        wq = jnp.concatenate(
            [w_mla_uq[l].reshape(C_Q_RANK, C_HEADS, C_QK_DIM),
             jnp.zeros((C_Q_RANK, C_HEADS, C_HEAD_PAD - C_QK_DIM), F32)], axis=2
        ).reshape(C_Q_RANK, C_PAD_WIDTH).astype(BF16)
        wk = _pad_heads_cols(w_mla_ukv[l], C_NOPE_DIM + C_V_DIM, 0, C_NOPE_DIM)
        wv = _pad_heads_cols(w_mla_ukv[l], C_NOPE_DIM + C_V_DIM, C_NOPE_DIM, C_NOPE_DIM + C_V_DIM)
        qm, km, vm = mla_prep(g, y, mla_q_norm[l].reshape(1, C_Q_RANK), mla_kv_norm[l].reshape(1, C_KV_RANK),
                              wq, wk, wv, tabs_c, tm_prep)
        yc1 = mla_flash(qm, km, vm, 0, b1, s1, min(1024, s1), min(512, s1))
        yc2 = mla_flash(qm, km, vm, g.t1, b2, s2, min(512, s2), min(512, s2))

        wpc = jnp.pad(w_proj_c[l].reshape(C_HEADS, C_V_DIM, d), ((0, 0), (0, C_HEAD_PAD - C_V_DIM), (0, 0)))
        wpc = wpc.reshape(C_PAD_WIDTH, d).astype(BF16)
        x, h2, comb = proj_merge(g, o_list, lse_list, yb, yc1, yc2, y, x, gt1, mul2, sh2,
                                 w_proj_a[l].astype(BF16), w_proj_b[l].astype(BF16), wpc, w_out[l].astype(BF16),
                                 wrh, wrm, br, tm_proj)

        dest, tile_group, ntiles = _sorted_layout(comb, tm_moe)
        hs = scatter_rows(dest, h2, ntiles * tm_moe, tm_moe)
        ys = moe_grouped(tile_group, ntiles, hs, wrh, wrm, br, w_exp_gate[l].astype(BF16),
                         w_exp_up[l].astype(BF16), w_exp_down[l].astype(BF16), tm_moe)
        x = gather_residual(g, dest, ys, x, gt2, fnorm, l == depth - 1, tm_proj)

    return x[:g.t1].reshape(b1, s1, d), x[g.t1:].reshape(b2, s2, d)
```

```python
import functools

import jax
import jax.numpy as jnp
from jax import lax
from jax.experimental import pallas as pl
from jax.experimental.pallas import tpu as pltpu

F32 = jnp.float32
BF16 = jnp.bfloat16

D_MODEL = 1024
DEPTH = 2
EPS = 1e-6
ROPE_THETA = 500000.0
NEG_INF = -1e30

A_HEADS = 8
A_HEAD_DIM = 64
A_ROPE_DIM = A_HEAD_DIM // 4
A_PATTERNS = ((128, 1), (512, 4), (2048, 16))
A_WIDTH = A_HEADS * A_HEAD_DIM
A_HALF = 64

B_HEADS = 4
B_KEY_DIM = 128
B_VAL_DIM = 256
B_GATE_RANK = 16
B_GATE_TAU = 16.0
B_CHUNK = 64
B_KW = B_HEADS * B_KEY_DIM
B_VW = B_HEADS * B_VAL_DIM

C_HEADS = 8
C_NOPE_DIM = 64
C_ROPE_DIM = 32
C_V_DIM = 64
C_QK_DIM = C_NOPE_DIM + C_ROPE_DIM
C_Q_RANK = 384
C_KV_RANK = 256
C_HEAD_PAD = 128
C_PAD_WIDTH = C_HEADS * C_HEAD_PAD

N_EXPERTS = 16
N_GROUPS = 4
EXPERTS_PER_GROUP = N_EXPERTS // N_GROUPS
TOP_K = 2
GROUP_SCORE_K = 2
D_EXPERT = 512
N_MOD = 6

IN_SIZES = (A_WIDTH, A_WIDTH, A_WIDTH, B_KW, B_KW, B_VW, B_VW, B_GATE_RANK, B_GATE_RANK,
            C_Q_RANK, C_KV_RANK, C_ROPE_DIM, 3 * D_MODEL)

LANES = 128
ROW_TILE = D_MODEL // LANES
MOE_EXPERTS_PER_STEP = 2
LOG2E = 1.4426950408889634

OFF_GATES = 0
OFF_BV = 3072
OFF_BR = 4096
OFF_AQ = 5120
OFF_AK = 5632
OFF_AV = 6144
OFF_BQ = 6656
OFF_BK = 7168
OFF_CKV = 7680
OFF_BZ = 7936
OFF_CQ = 8064
OFF_KR = 8448
NPAD = 9216
KR_LANE = 64

VMEM_LIMIT = 56 * 1024 * 1024


def _cparams(sem):
    return pltpu.CompilerParams(dimension_semantics=sem, vmem_limit_bytes=VMEM_LIMIT)


class Groups:
    def __init__(self, b1, s1, b2, s2):
        self.b1, self.s1, self.b2, self.s2 = b1, s1, b2, s2
        self.t1, self.t2 = b1 * s1, b2 * s2
        self.t = self.t1 + self.t2
        self.nb = b1 + b2

    def batch_of_block(self, i, tm):
        n1 = self.t1 // tm
        return jnp.where(i < n1, i // (self.s1 // tm), self.b1 + (i - n1) // (self.s2 // tm))

    def pos_block(self, i, tm):
        n1 = self.t1 // tm
        return jnp.where(i < n1, i % (self.s1 // tm), (i - n1) % (self.s2 // tm))


def _dot(a, b):
    return jnp.dot(a, b, preferred_element_type=F32)


def _dot_nt(a, b):
    return lax.dot_general(a, b, (((1,), (1,)), ((), ())), preferred_element_type=F32)


def _dot_tn(a, b):
    return lax.dot_general(a, b, (((0,), (0,)), ((), ())), preferred_element_type=F32)


def _split3(a):
    hi = a.astype(BF16)
    r1 = a - hi.astype(F32)
    mid = r1.astype(BF16)
    lo = (r1 - mid.astype(F32)).astype(BF16)
    return hi, mid, lo


def _rope(y, ct, s1, s2, half):
    return y * ct + pltpu.roll(y, LANES - half, 1) * s1 + pltpu.roll(y, half, 1) * s2


def _sigmoid(x):
    return 0.5 * jnp.tanh(0.5 * x) + 0.5


def _mod_kernel(c_ref, w_ref, b_ref, o_ref):
    c = c_ref[...]
    ca = c * _sigmoid(c)
    ch, cm, _ = _split3(ca)
    wh, wm, _ = _split3(w_ref[0])
    o_ref[0] = _dot(ch, wh) + _dot(cm, wh) + _dot(ch, wm) + b_ref[0]


def modulation(c_pad, w_mod, b_mod):
    nbp, d = c_pad.shape
    depth, _, n = w_mod.shape
    tn = 512
    return pl.pallas_call(
        _mod_kernel,
        grid=(depth, n // tn),
        in_specs=[pl.BlockSpec((nbp, d), lambda l, j: (0, 0)),
                  pl.BlockSpec((1, d, tn), lambda l, j: (l, 0, j)),
                  pl.BlockSpec((1, 1, tn), lambda l, j: (l, 0, j))],
        out_specs=pl.BlockSpec((1, nbp, tn), lambda l, j: (l, 0, j)),
        out_shape=jax.ShapeDtypeStruct((depth, nbp, n), F32),
        compiler_params=_cparams(("arbitrary", "arbitrary")),
        name="modulation",
    )(c_pad, w_mod, b_mod.reshape(depth, 1, n))


def _inproj_kernel(x_ref, mul_ref, sh_ref, w_ref, ct_ref, s1_ref, s2_ref, o_ref, od4_ref, od16_ref,
                   h_scr, y_scr, y4_scr, *, tm, j_qk, j_v):
    j = pl.program_id(1)
    aw = A_WIDTH
    ncol = aw // LANES

    @pl.when(j == 0)
    def _():
        x = x_ref[...]
        ms = jnp.mean(x * x, axis=-1, keepdims=True)
        h = x * lax.rsqrt(ms + EPS) * mul_ref[0] + sh_ref[0]
        h_scr[...] = h.astype(BF16)

    def attention_block(kind, half, rope):
        y = _dot(h_scr[...], w_ref[j, :, half * aw:(half + 1) * aw])
        if rope:
            ct, s1, s2 = ct_ref[...], s1_ref[...], s2_ref[...]
        for c in range(ncol):
            yc = y[:, c * LANES:(c + 1) * LANES]
            y_scr[c] = _rope(yc, ct, s1, s2, A_ROPE_DIM // 2) if rope else yc
        n4, n16 = tm // 4, tm // 16
        for c in range(ncol):
            o_ref[:, half * aw + c * LANES:half * aw + (c + 1) * LANES] = y_scr[c].astype(BF16)
            for r in range(4):
                y4 = y_scr[c, pl.ds(r, n4, stride=4), :]
                y4_scr[c, r * n4:(r + 1) * n4, :] = y4
                od4_ref[kind, :, r * aw + c * LANES:r * aw + (c + 1) * LANES] = y4.astype(BF16)
            for r in range(4):
                for r2 in range(4):
                    col = (r + 4 * r2) * aw + c * LANES
                    od16_ref[kind, :, col:col + LANES] = y4_scr[c, pl.ds(r * n4 + r2, n16, stride=4), :].astype(BF16)

    @pl.when((j != j_qk) & (j != j_v))
    def _():
        o_ref[...] = _dot(h_scr[...], w_ref[j]).astype(BF16)

    @pl.when(j == j_qk)
    def _():
        attention_block(0, 0, True)
        attention_block(1, 1, True)

    @pl.when(j == j_v)
    def _():
        attention_block(2, 0, False)
        o_ref[:, aw:] = _dot(h_scr[...], w_ref[j, :, aw:]).astype(BF16)


def in_projection(g, x, mul, shift, w_pad, tabs, tm):
    t, d = x.shape
    tn = 2 * A_WIDTH
    aw = A_WIDTH
    ct, s1, s2 = tabs
    assert OFF_AQ % tn == 0 and OFF_AK == OFF_AQ + aw and OFF_AV == OFF_AQ + tn and NPAD % tn == 0
    bmap = lambda i, j: (g.batch_of_block(i, tm), 0, 0)
    pmap = lambda i, j: (g.pos_block(i, tm), 0)
    kern = functools.partial(_inproj_kernel, tm=tm, j_qk=OFF_AQ // tn, j_v=OFF_AV // tn)
    return pl.pallas_call(
        kern,
        grid=(t // tm, NPAD // tn),
        in_specs=[pl.BlockSpec((tm, d), lambda i, j: (i, 0)),
                  pl.BlockSpec((1, 1, d), bmap),
                  pl.BlockSpec((1, 1, d), bmap),
                  pl.BlockSpec((NPAD // tn, d, tn), lambda i, j: (0, 0, 0), pipeline_mode=pl.Buffered(1)),
                  pl.BlockSpec((tm, LANES), pmap),
                  pl.BlockSpec((tm, LANES), pmap),
                  pl.BlockSpec((tm, LANES), pmap)],
        out_specs=[pl.BlockSpec((tm, tn), lambda i, j: (i, j)),
                   pl.BlockSpec((3, tm // 4, 4 * aw), lambda i, j: (0, i, 0)),
                   pl.BlockSpec((3, tm // 16, 16 * aw), lambda i, j: (0, i, 0))],
        out_shape=[jax.ShapeDtypeStruct((t, NPAD), BF16),
                   jax.ShapeDtypeStruct((3, t // 4, 4 * aw), BF16),
                   jax.ShapeDtypeStruct((3, t // 16, 16 * aw), BF16)],
        scratch_shapes=[pltpu.VMEM((tm, d), BF16), pltpu.VMEM((aw // LANES, tm, LANES), F32),
                        pltpu.VMEM((aw // LANES, tm, LANES), F32)],
        compiler_params=_cparams(("arbitrary", "arbitrary")),
        name="in_projection",
    )(x, mul, shift, w_pad.reshape(d, NPAD // tn, tn).transpose(1, 0, 2), ct, s1, s2)


def _band_kernel(q_ref, kp_ref, km_ref, kn_ref, vp_ref, vm_ref, vn_ref, o_ref, lse_ref, *, tb, tq, rows1, l1, l2):
    i = pl.program_id(1)
    tk = tq + 2 * A_HALF
    k = jnp.concatenate([kp_ref[...], km_ref[...], kn_ref[...]], axis=0)
    v = jnp.concatenate([vp_ref[...], vm_ref[...], vn_ref[...]], axis=0)
    qi = lax.broadcasted_iota(jnp.int32, (tq, tk), 0)
    kj = lax.broadcasted_iota(jnp.int32, (tq, tk), 1)
    band = jnp.abs(kj - A_HALF - qi) <= A_HALF
    lo = lax.broadcasted_iota(jnp.int32, (tq, LANES), 1) < A_HEAD_DIM
    scale = A_HEAD_DIM ** -0.5
    for u in range(tb // tq):
        row0 = i * tb + u * tq
        in1 = row0 < rows1
        seq_len = jnp.where(in1, l1, l2)
        pos0 = jnp.where(in1, row0 % l1, (row0 - rows1) % l2)
        kpos = pos0 - A_HALF + kj
        valid = band & (kpos >= 0) & (kpos < seq_len)
        valid2 = jnp.concatenate([valid, valid], axis=0)
        rows = slice(u * tq, (u + 1) * tq)
        krows = slice(u * tq, u * tq + tk)
        for p in range(A_WIDTH // LANES):
            sl = slice(p * LANES, (p + 1) * LANES)
            qp, kp, vp = q_ref[rows, sl] * scale, k[krows, sl], v[krows, sl]
            zero = jnp.zeros_like(qp)
            qm = jnp.concatenate([jnp.where(lo, qp, zero), jnp.where(lo, zero, qp)], axis=0)
            s = jnp.where(valid2, _dot_nt(qm, kp), NEG_INF)
            m = jnp.max(s, axis=1, keepdims=True)
            e = jnp.exp(s - m)
            l = jnp.sum(e, axis=1, keepdims=True)
            o2 = _dot(e.astype(BF16), vp) / l
            lse2 = m + jnp.log(l)
            o_ref[rows, sl] = jnp.where(lo, o2[:tq], o2[tq:]).astype(BF16)
            lse_ref[rows, sl] = jnp.where(lo, lse2[:tq], lse2[tq:])


def band_attention(g, qkv, dil, cols, cpb):
    rows = g.t // dil
    tq = 128
    tb = 256
    sub = tb // A_HALF
    nsub = rows // A_HALF

    def main(c):
        ld, col = cols[c]
        return pl.BlockSpec((None, tb, A_WIDTH), lambda r, i: (ld, i, r * cpb + col))

    def prev(c):
        ld, col = cols[c]
        return pl.BlockSpec((None, A_HALF, A_WIDTH), lambda r, i: (ld, jnp.maximum(i * sub - 1, 0), r * cpb + col))

    def nxt(c):
        ld, col = cols[c]
        return pl.BlockSpec((None, A_HALF, A_WIDTH),
                            lambda r, i: (ld, jnp.minimum((i + 1) * sub, nsub - 1), r * cpb + col))

    assert (g.s1 // dil) % tq == 0 and (g.s2 // dil) % tq == 0 and rows % tb == 0
    kern = functools.partial(_band_kernel, tb=tb, tq=tq, rows1=g.t1 // dil, l1=g.s1 // dil, l2=g.s2 // dil)
    return pl.pallas_call(
        kern,
        grid=(dil, rows // tb),
        in_specs=[main(0), prev(1), main(1), nxt(1), prev(2), main(2), nxt(2)],
        out_specs=[pl.BlockSpec((tb, A_WIDTH), lambda r, i: (i, r)),
                   pl.BlockSpec((tb, A_WIDTH), lambda r, i: (i, r))],
        out_shape=[jax.ShapeDtypeStruct((rows, dil * A_WIDTH), BF16),
                   jax.ShapeDtypeStruct((rows, dil * A_WIDTH), F32)],
        compiler_params=_cparams(("arbitrary", "arbitrary")),
        name="band_attention_d%d" % dil,
    )(qkv, qkv, qkv, qkv, qkv, qkv, qkv)


def _gla_kernel(*refs, reverse, final, tc, nblk, t1, s1, s2):
    if final:
        q_ref, k_ref, v_ref, z_ref, wa_ref, ba_ref, tri_ref, ob_ref, r_ref, gain_ref, o_ref, st_scr = refs
    else:
        q_ref, k_ref, v_ref, z_ref, wa_ref, ba_ref, tri_ref, o_ref, st_scr = refs
    i = pl.program_id(0)
    blk = (nblk - 1 - i) if reverse else i
    row0 = blk * tc
    in1 = row0 < t1
    pos0 = jnp.where(in1, row0 % s1, (row0 - t1) % s2)
    slen = jnp.where(in1, s1, s2)
    start = (pos0 + tc == slen) if reverse else (pos0 == 0)

    @pl.when(start)
    def _():
        st_scr[...] = jnp.zeros_like(st_scr)

    zl = _dot(z_ref[...], wa_ref[...]) + ba_ref[...]
    la = (jnp.minimum(zl, 0.0) - jnp.log(1.0 + jnp.exp(-jnp.abs(zl)))) * (1.0 / B_GATE_TAU)
    hi, mid, _ = _split3(la)
    tri = tri_ref[...]
    tb = tri.shape[0]
    bc_all = jnp.concatenate(
        [_dot(tri, hi[b * tb:(b + 1) * tb]) + _dot(tri, mid[b * tb:(b + 1) * tb]) for b in range(tc // tb)], axis=0)

    qi = lax.broadcasted_iota(jnp.int32, (B_CHUNK, B_CHUNK), 0)
    si = lax.broadcasted_iota(jnp.int32, (B_CHUNK, B_CHUNK), 1)
    mask = (si > qi) if reverse else (si <= qi)
    nch = tc // B_CHUNK
    q_all = q_ref[...].astype(F32) * (B_KEY_DIM ** -0.5)
    k_all = k_ref[...].astype(F32)
    q_dec = (q_all * jnp.exp(bc_all)).astype(BF16)
    k_inv = (k_all * jnp.exp(-bc_all)).astype(BF16)
    st = [st_scr[h] for h in range(B_HEADS)]
    for c in (range(nch - 1, -1, -1) if reverse else range(nch)):
        sl = slice(c * B_CHUNK, (c + 1) * B_CHUNK)
        edge = c * B_CHUNK if reverse else (c + 1) * B_CHUNK - 1
        tot = bc_all[edge:edge + 1]
        k_end = (k_all[sl] * jnp.exp(tot - bc_all[sl])).astype(BF16)
        dec = jnp.exp(tot)
        for h in range(B_HEADS):
            ks = slice(h * B_KEY_DIM, (h + 1) * B_KEY_DIM)
            vs = slice(h * B_VAL_DIM, (h + 1) * B_VAL_DIM)
            v = v_ref[sl, vs]
            att = jnp.where(mask, _dot_nt(q_dec[sl, ks], k_inv[sl, ks]), 0.0).astype(BF16)
            o = _dot(att, v) + _dot_nt(q_dec[sl, ks], st[h].astype(BF16))
            st[h] = st[h] * dec[:, ks] + _dot_tn(v, k_end[:, ks])
            if final:
                o = o + ob_ref[sl, vs]
                on = o * lax.rsqrt(jnp.mean(o * o, axis=-1, keepdims=True) + EPS) * gain_ref[...]
                r = r_ref[sl, vs].astype(F32)
                o_ref[sl, vs] = (on * (r * _sigmoid(r))).astype(BF16)
            else:
                o_ref[sl, vs] = o
    for h in range(B_HEADS):
        st_scr[h] = st[h]


def gla_pass(g, y, wa, ba, reverse, o_back=None, gain=None):
    t = g.t
    tc = min(512, g.s1, g.s2)
    nblk = t // tc
    final = o_back is not None
    rowmap = (lambda i: nblk - 1 - i) if reverse else (lambda i: i)
    row = lambda w, off: pl.BlockSpec((tc, w), lambda i: (rowmap(i), off // w))
    full = lambda a: pl.BlockSpec(a.shape, lambda i: (0,) * a.ndim)
    tb = min(tc, 256)
    ri = jnp.arange(tb)[:, None]
    ci = jnp.arange(tb)[None, :]
    tri = ((ri // B_CHUNK == ci // B_CHUNK) & ((ci >= ri) if reverse else (ci <= ri))).astype(BF16)
    in_specs = [row(B_KW, OFF_BQ), row(B_KW, OFF_BK), row(B_VW, OFF_BV), row(LANES, OFF_BZ),
                full(wa), full(ba), full(tri)]
    args = [y, y, y, y, wa, ba, tri]
    if final:
        in_specs += [row(B_VW, 0), row(B_VW, OFF_BR), full(gain)]
        args += [o_back, y, gain]
    kern = functools.partial(_gla_kernel, reverse=reverse, final=final, tc=tc, nblk=nblk, t1=g.t1, s1=g.s1, s2=g.s2)
    return pl.pallas_call(
        kern,
        grid=(nblk,),
        in_specs=in_specs,
        out_specs=row(B_VW, 0),
        out_shape=jax.ShapeDtypeStruct((t, B_VW), BF16 if final else F32),
        scratch_shapes=[pltpu.VMEM((B_HEADS, B_VAL_DIM, B_KEY_DIM), F32)],
        compiler_params=_cparams(("arbitrary",)),
        name="gla_forward_final" if final else "gla_backward",
    )(*args)


def _mla_prep_kernel(ckv_ref, cq_ref, kr_ref, qn_ref, kvn_ref, wq_ref, wk_ref, wv_ref, ct_ref, s1_ref, s2_ref,
                     q_out, k_out, v_out):
    def norm(ref, gain_ref):
        xf = ref[...].astype(F32)
        return (xf * lax.rsqrt(jnp.mean(xf * xf, axis=-1, keepdims=True) + EPS) * gain_ref[...]).astype(BF16)

    ckv_n = norm(ckv_ref, kvn_ref)
    cq_n = norm(cq_ref, qn_ref)
    q = _dot(cq_n, wq_ref[...])
    kn = _dot(ckv_n, wk_ref[...])
    v = _dot(ckv_n, wv_ref[...])
    ct, s1, s2 = ct_ref[...], s1_ref[...], s2_ref[...]
    half = C_ROPE_DIM // 2
    kr_rot = _rope(kr_ref[...].astype(F32), ct, s1, s2, half)
    ones_lane = lax.broadcasted_iota(jnp.int32, kr_rot.shape, 1) == C_V_DIM
    scale = C_QK_DIM ** -0.5 * LOG2E
    for h in range(C_HEADS):
        sl = slice(h * C_HEAD_PAD, (h + 1) * C_HEAD_PAD)
        q_out[:, sl] = (_rope(q[:, sl], ct, s1, s2, half) * scale).astype(BF16)
        k_out[:, sl] = (kn[:, sl] + kr_rot).astype(BF16)
        v_out[:, sl] = jnp.where(ones_lane, 1.0, v[:, sl]).astype(BF16)


def mla_prep(g, y, qn, kvn, wq, wk, wv, tabs, tm):
    t = g.t
    ct, s1, s2 = tabs
    pmap = lambda i: (g.pos_block(i, tm), 0)
    full = lambda a: pl.BlockSpec(a.shape, lambda i: (0,) * a.ndim)
    out = jax.ShapeDtypeStruct((t, C_PAD_WIDTH), BF16)
    ospec = pl.BlockSpec((tm, C_PAD_WIDTH), lambda i: (i, 0))
    return pl.pallas_call(
        _mla_prep_kernel,
        grid=(t // tm,),
        in_specs=[pl.BlockSpec((tm, C_KV_RANK), lambda i: (i, OFF_CKV // C_KV_RANK)),
                  pl.BlockSpec((tm, C_Q_RANK), lambda i: (i, OFF_CQ // C_Q_RANK)),
                  pl.BlockSpec((tm, LANES), lambda i: (i, OFF_KR // LANES)),
                  full(qn), full(kvn), full(wq), full(wk), full(wv),
                  pl.BlockSpec((tm, LANES), pmap), pl.BlockSpec((tm, LANES), pmap), pl.BlockSpec((tm, LANES), pmap)],
        out_specs=[ospec, ospec, ospec],
        out_shape=[out, out, out],
        compiler_params=_cparams(("arbitrary",)),
        name="mla_prep",
    )(y, y, y, qn, kvn, wq, wk, wv, ct, s1, s2)


def _flash_kernel(q_ref, k_ref, v_ref, o_ref, s0_scr, s1_scr, p_scr, m_scr, a_scr, acc_scr, *, tk, nk, rc, unroll):
    tq = q_ref.shape[0]

    def logits(t, scr):
        off = pl.multiple_of(t * tk, tk)
        scr[...] = _dot_nt(q_ref[...], k_ref[pl.ds(off, tk), :])

    def softmax_pv(t, scr):
        off = pl.multiple_of(t * tk, tk)
        for r in range(tq // rc):
            rows = slice(r * rc, (r + 1) * rc)
            s = scr[rows, :]
            m_old = m_scr[rows, :]
            m_new = jnp.maximum(m_old, jnp.max(s, axis=1, keepdims=True))
            p_scr[rows, :] = jnp.exp2(s - jnp.concatenate([m_new] * (tk // LANES), axis=1)).astype(BF16)
            a_scr[rows, :] = jnp.exp2(m_old - m_new)
            m_scr[rows, :] = m_new
        acc_scr[...] = acc_scr[...] * a_scr[...] + _dot(p_scr[...], v_ref[pl.ds(off, tk), :])

    bufs = (s0_scr, s1_scr)

    def group(t0, last):
        for u in range(unroll):
            if not (last and u == unroll - 1):
                logits(t0 + u + 1, bufs[(u + 1) % 2])
            softmax_pv(t0 + u, bufs[u % 2])

    def body(jj, carry):
        group(unroll * jj, False)
        return carry

    m_scr[...] = jnp.full(m_scr.shape, NEG_INF, F32)
    acc_scr[...] = jnp.zeros(acc_scr.shape, F32)
    logits(0, s0_scr)
    lax.fori_loop(0, nk // unroll - 1, body, 0)
    group(nk - unroll, True)
    acc = acc_scr[...]
    o_ref[...] = (acc / acc[:, C_V_DIM:C_V_DIM + 1]).astype(BF16)


def mla_flash(q, k, v, row_off, nseq, s, tq, tk):
    assert row_off % s == 0 and row_off % tq == 0
    qb0 = row_off // tq
    sb0 = row_off // s
    nq = s // tq
    nk = s // tk
    unroll = next(u for u in (16, 8, 4, 2) if nk % u == 0)
    assert nk % unroll == 0
    kern = functools.partial(_flash_kernel, tk=tk, nk=nk, rc=64, unroll=unroll)
    return pl.pallas_call(
        kern,
        grid=(nseq, C_HEADS, nq),
        scratch_shapes=[pltpu.VMEM((tq, tk), F32), pltpu.VMEM((tq, tk), F32), pltpu.VMEM((tq, tk), BF16),
                        pltpu.VMEM((tq, LANES), F32), pltpu.VMEM((tq, LANES), F32),
                        pltpu.VMEM((tq, C_HEAD_PAD), F32)],
        in_specs=[pl.BlockSpec((tq, C_HEAD_PAD), lambda b, h, i: (qb0 + b * nq + i, h)),
                  pl.BlockSpec((s, C_HEAD_PAD), lambda b, h, i: (sb0 + b, h)),
                  pl.BlockSpec((s, C_HEAD_PAD), lambda b, h, i: (sb0 + b, h))],
        out_specs=pl.BlockSpec((tq, C_HEAD_PAD), lambda b, h, i: (b * nq + i, h)),
        out_shape=jax.ShapeDtypeStruct((nseq * s, C_PAD_WIDTH), BF16),
        compiler_params=_cparams(("arbitrary", "arbitrary", "arbitrary")),
        name="mla_flash_s%d" % s,
    )(q, k, v)


def _store_row_tiles(ref, x, rows):
    for c in range(ROW_TILE):
        ref[pl.ds(c, rows, stride=ROW_TILE), :] = x[:, c * LANES:(c + 1) * LANES]


def _load_row_tiles(ref, rows):
    return jnp.concatenate([ref[pl.ds(c, rows, stride=ROW_TILE), :] for c in range(ROW_TILE)], axis=1)


def _router_scores(h, wrh_ref, wrm_ref):
    hh, hm, _ = _split3(h)
    return _sigmoid(_dot(hh, wrh_ref[...]) + _dot(hm, wrh_ref[...]) + _dot(hh, wrm_ref[...]))


def _rank_in_group(sel, pos):
    rank = jnp.zeros(sel.shape, jnp.int32)
    for k in range(1, EXPERTS_PER_GROUP):
        below = pltpu.roll(sel, k, 1)
        above = pltpu.roll(sel, LANES - k, 1)
        rank += jnp.where((pos >= k) & (below >= sel), 1, 0)
        rank += jnp.where((pos + k < EXPERTS_PER_GROUP) & (above > sel), 1, 0)
    return rank


def _route_in_group(s, bias, g):
    lane = lax.broadcasted_iota(jnp.int32, s.shape, 1)
    rank = _rank_in_group(s + bias, lane % EXPERTS_PER_GROUP)
    chosen = (lane // EXPERTS_PER_GROUP == g) & (rank < TOP_K)
    total = jnp.sum(jnp.where(chosen, s, 0.0), axis=1, keepdims=True)
    return jnp.where(chosen, s / total, 0.0)


def _route(s, bias):
    lane = lax.broadcasted_iota(jnp.int32, s.shape, 1)
    pos = lane % EXPERTS_PER_GROUP
    grp = lane // EXPERTS_PER_GROUP
    sel = s + bias
    rank = _rank_in_group(sel, pos)
    top = rank < GROUP_SCORE_K
    contrib = jnp.where(top, sel, 0.0)
    score = contrib
    for k in range(1, EXPERTS_PER_GROUP):
        score += jnp.where(pos >= k, pltpu.roll(contrib, k, 1), 0.0)
        score += jnp.where(pos + k < EXPERTS_PER_GROUP, pltpu.roll(contrib, LANES - k, 1), 0.0)
    best = lane < N_EXPERTS
    for k in range(1, N_GROUPS):
        earlier = pltpu.roll(score, k * EXPERTS_PER_GROUP, 1)
        later = pltpu.roll(score, LANES - k * EXPERTS_PER_GROUP, 1)
        best &= jnp.logical_not((grp >= k) & (earlier >= score))
        best &= jnp.logical_not((grp + k < N_GROUPS) & (later > score))
    chosen = best & (rank < TOP_K)
    total = jnp.sum(jnp.where(chosen, s, 0.0), axis=1, keepdims=True)
    return jnp.where(chosen, s / total, 0.0)


def _proj_kernel(o1_ref, o2_ref, o3_ref, l1_ref, l2_ref, l3_ref, yb_ref, yc1_ref, yc2_ref, ga_ref, gb_ref, gc_ref,
                 x_ref, g1_ref, mul2_ref, sh2_ref, wpa_ref, wpb_ref, wpc_ref, wout_ref, wrh_ref, wrm_ref, br_ref,
                 x_out, h_out, comb_out, o2_scr, l2_scr, o3_scr, l3_scr, *, tm, n1):
    in_group1 = pl.program_id(0) < n1
    ncol = A_WIDTH // LANES
    hm = tm // 2
    for half in range(2):
        rows = slice(half * hm, (half + 1) * hm)
        for dil, o_ref, l_ref, o_scr, l_scr in ((A_PATTERNS[1][1], o2_ref, l2_ref, o2_scr, l2_scr),
                                               (A_PATTERNS[2][1], o3_ref, l3_ref, o3_scr, l3_scr)):
            drows = slice(half * hm // dil, (half + 1) * hm // dil)
            for r in range(dil):
                for c in range(ncol):
                    sl = slice(r * A_WIDTH + c * LANES, r * A_WIDTH + (c + 1) * LANES)
                    o_scr[c, pl.ds(half * hm + r, hm // dil, stride=dil), :] = o_ref[drows, sl].astype(F32)
                    l_scr[c, pl.ds(half * hm + r, hm // dil, stride=dil), :] = l_ref[drows, sl]
        slabs = lambda scr: jnp.concatenate([scr[c, rows, :] for c in range(ncol)], axis=1)
        l1, l2, l3 = l1_ref[rows, :], slabs(l2_scr), slabs(l3_scr)
        m = jnp.maximum(jnp.maximum(l1, l2), l3)
        e1, e2, e3 = jnp.exp(l1 - m), jnp.exp(l2 - m), jnp.exp(l3 - m)
        ya = (e1 * o1_ref[rows, :].astype(F32) + e2 * slabs(o2_scr) + e3 * slabs(o3_scr)) / (e1 + e2 + e3)
        sig = lambda ref: _sigmoid(ref[rows, :].astype(F32))
        yc = jnp.where(in_group1, yc1_ref[rows, :], yc2_ref[rows, :])
        merged = (sig(ga_ref) * _dot(ya.astype(BF16), wpa_ref[...])
                  + sig(gb_ref) * _dot(yb_ref[rows, :], wpb_ref[...])
                  + sig(gc_ref) * _dot(yc, wpc_ref[...]))
        out = _dot(merged.astype(BF16), wout_ref[...])
        x = x_ref[rows, :] + g1_ref[0] * out
        x_out[rows, :] = x
        h = x * lax.rsqrt(jnp.mean(x * x, axis=-1, keepdims=True) + EPS) * mul2_ref[0] + sh2_ref[0]
        _store_row_tiles(h_out.at[pl.ds(half * hm * ROW_TILE, hm * ROW_TILE), :], h, hm)
        comb_out[rows, :] = _route(_router_scores(h, wrh_ref, wrm_ref), br_ref[...])


def proj_merge(g, o_list, lse_list, yb, yc1, yc2, y, x, gate1, mul2, sh2, wpa, wpb, wpc, wout, wrh, wrm, br, tm):
    t, d = x.shape
    n1 = g.t1 // tm
    n2 = g.t2 // tm
    bmap = lambda i: (g.batch_of_block(i, tm), 0, 0)
    row = lambda w, col=0: pl.BlockSpec((tm, w), lambda i: (i, col))
    dil_row = lambda dil: pl.BlockSpec((tm // dil, dil * A_WIDTH), lambda i: (i, 0))
    full = lambda a: pl.BlockSpec(a.shape, lambda i: (0,) * a.ndim)
    gcol = OFF_GATES // d
    d2, d3 = A_PATTERNS[1][1], A_PATTERNS[2][1]
    attn_specs = [row(A_WIDTH), dil_row(d2), dil_row(d3)]
    return pl.pallas_call(
        functools.partial(_proj_kernel, tm=tm, n1=n1),
        grid=(t // tm,),
        in_specs=attn_specs + attn_specs + [
            row(B_VW),
            pl.BlockSpec((tm, C_PAD_WIDTH), lambda i: (jnp.minimum(i, n1 - 1), 0)),
            pl.BlockSpec((tm, C_PAD_WIDTH), lambda i: (jnp.clip(i - n1, 0, n2 - 1), 0)),
            row(d, gcol), row(d, gcol + 1), row(d, gcol + 2),
            row(d), pl.BlockSpec((1, 1, d), bmap), pl.BlockSpec((1, 1, d), bmap), pl.BlockSpec((1, 1, d), bmap),
            full(wpa), full(wpb), full(wpc), full(wout), full(wrh), full(wrm), full(br)],
        out_specs=[row(d), pl.BlockSpec((tm * ROW_TILE, LANES), lambda i: (i, 0)), row(LANES)],
        out_shape=[jax.ShapeDtypeStruct((t, d), F32), jax.ShapeDtypeStruct((t * ROW_TILE, LANES), F32),
                   jax.ShapeDtypeStruct((t, LANES), F32)],
        scratch_shapes=[pltpu.VMEM((A_WIDTH // LANES, tm, LANES), F32)] * 4,
        compiler_params=_cparams(("arbitrary",)),
        name="proj_merge_route",
    )(*o_list, *lse_list, yb, yc1, yc2, y, y, y, x, gate1, mul2, sh2, wpa, wpb, wpc, wout, wrh, wrm, br)


def _sorted_layout(comb, tmoe):
    t = comb.shape[0]
    gw = comb[:, :N_EXPERTS].reshape(t, N_GROUPS, EXPERTS_PER_GROUP).sum(-1)
    gid = jnp.argmax(gw, axis=-1).astype(jnp.int32)
    onehot = (gid[:, None] == jnp.arange(N_GROUPS, dtype=jnp.int32)[None, :]).astype(jnp.int32)
    incl = jnp.cumsum(onehot, axis=0)
    counts = incl[-1]
    padded = (counts + tmoe - 1) // tmoe * tmoe
    ends = jnp.cumsum(padded)
    dest = jnp.sum(onehot * (incl - onehot + (ends - padded)[None, :]), axis=1).astype(jnp.int32)
    ntiles = t // tmoe + N_GROUPS
    tile_group = jnp.sum((jnp.arange(ntiles, dtype=jnp.int32) * tmoe)[:, None] >= ends[None, :], axis=1)
    tile_group = jnp.where(tile_group < N_GROUPS, tile_group, -1).astype(jnp.int32)
    return dest, tile_group, ntiles


def _row_copy(src, src_row, dst, dst_row, sem):
    return pltpu.make_async_copy(src.at[pl.ds(src_row * ROW_TILE, ROW_TILE), :],
                                 dst.at[pl.ds(dst_row * ROW_TILE, ROW_TILE), :], sem)


def _scatter_rows_kernel(dest_ref, h_ref, init_hbm, hs_hbm, sem, *, tm):
    del init_hbm
    i = pl.program_id(0)

    def issue(r, c):
        _row_copy(h_ref, r, hs_hbm, dest_ref[i * tm + r], sem).start()
        return c

    def drain(r, c):
        _row_copy(h_ref, 0, hs_hbm, 0, sem).wait()
        return c

    lax.fori_loop(0, tm, issue, 0, unroll=8)
    lax.fori_loop(0, tm, drain, 0, unroll=8)


def scatter_rows(dest, h_tiles, rows_out, tm):
    t = dest.shape[0]
    init = jnp.zeros((rows_out * ROW_TILE, LANES), F32)
    return pl.pallas_call(
        functools.partial(_scatter_rows_kernel, tm=tm),
        grid_spec=pltpu.PrefetchScalarGridSpec(
            num_scalar_prefetch=1,
            grid=(t // tm,),
            in_specs=[pl.BlockSpec((tm * ROW_TILE, LANES), lambda i, dst: (i, 0)),
                      pl.BlockSpec(memory_space=pl.ANY)],
            out_specs=pl.BlockSpec(memory_space=pl.ANY),
            scratch_shapes=[pltpu.SemaphoreType.DMA(())]),
        out_shape=jax.ShapeDtypeStruct(init.shape, F32),
        input_output_aliases={2: 0},
        compiler_params=_cparams(("arbitrary",)),
        name="moe_scatter_rows",
    )(dest, h_tiles, init)


def _moe_group_kernel(tg_ref, hs_ref, wrh_ref, wrm_ref, br_ref, wg_ref, wu_ref, wd_ref, y_ref,
                      h_scr, comb_scr, acc_scr, *, tm):
    j = pl.program_id(0)
    k = pl.program_id(1)
    g = tg_ref[j]

    @pl.when(k == 0)
    def _():
        h = _load_row_tiles(hs_ref, tm)
        h_scr[...] = h.astype(BF16)
        comb_scr[...] = _route_in_group(_router_scores(h, wrh_ref, wrm_ref), br_ref[...], g)
        acc_scr[...] = jnp.zeros_like(acc_scr)

    @pl.when(g >= 0)
    def _():
        h = h_scr[...]
        comb = comb_scr[...]
        lane = lax.broadcasted_iota(jnp.int32, comb.shape, 1)
        total = acc_scr[...]
        for kk in range(MOE_EXPERTS_PER_STEP):
            gate = _dot(h, wg_ref[kk])
            a = gate * _sigmoid(gate) * _dot(h, wu_ref[kk])
            yv = _dot(a.astype(BF16), wd_ref[kk])
            e = g * EXPERTS_PER_GROUP + k * MOE_EXPERTS_PER_STEP + kk
            total = total + jnp.sum(jnp.where(lane == e, comb, 0.0), axis=1, keepdims=True) * yv
        acc_scr[...] = total

    @pl.when(k == EXPERTS_PER_GROUP // MOE_EXPERTS_PER_STEP - 1)
    def _():
        _store_row_tiles(y_ref, acc_scr[...], tm)


def moe_grouped(tile_group, ntiles, hs, wrh, wrm, br, wg, wu, wd, tm):
    d = D_MODEL
    full = lambda a: pl.BlockSpec(a.shape, lambda j, k, tg: (0,) * a.ndim)
    ksteps = EXPERTS_PER_GROUP // MOE_EXPERTS_PER_STEP
    emap = lambda j, k, tg: (jnp.maximum(tg[j], 0) * ksteps + k, 0, 0)
    return pl.pallas_call(
        functools.partial(_moe_group_kernel, tm=tm),
        grid_spec=pltpu.PrefetchScalarGridSpec(
            num_scalar_prefetch=1,
            grid=(ntiles, ksteps),
            in_specs=[pl.BlockSpec((tm * ROW_TILE, LANES), lambda j, k, tg: (j, 0)),
                      full(wrh), full(wrm), full(br),
                      pl.BlockSpec((MOE_EXPERTS_PER_STEP, d, D_EXPERT), emap),
                      pl.BlockSpec((MOE_EXPERTS_PER_STEP, d, D_EXPERT), emap),
                      pl.BlockSpec((MOE_EXPERTS_PER_STEP, D_EXPERT, d), emap)],
            out_specs=pl.BlockSpec((tm * ROW_TILE, LANES), lambda j, k, tg: (j, 0)),
            scratch_shapes=[pltpu.VMEM((tm, d), BF16), pltpu.VMEM((tm, LANES), F32), pltpu.VMEM((tm, d), F32)]),
        out_shape=jax.ShapeDtypeStruct(hs.shape, F32),
        compiler_params=_cparams(("arbitrary", "arbitrary")),
        name="moe_grouped",
    )(tile_group, hs, wrh, wrm, br, wg, wu, wd)


def _gather_residual_kernel(dest_ref, y_hbm, x_ref, g2_ref, fn_ref, o_ref, buf, sem, *, tm, nsteps, final):
    i = pl.program_id(0)

    def fetch(step, slot):
        def issue(r, c):
            _row_copy(y_hbm, dest_ref[step * tm + r], buf.at[slot], r, sem.at[slot]).start()
            return c
        lax.fori_loop(0, tm, issue, 0, unroll=8)

    @pl.when(i == 0)
    def _():
        fetch(0, 0)

    @pl.when(i + 1 < nsteps)
    def _():
        fetch(i + 1, (i + 1) % 2)

    slot = i % 2

    def drain(r, c):
        _row_copy(y_hbm, 0, buf.at[slot], 0, sem.at[slot]).wait()
        return c

    lax.fori_loop(0, tm, drain, 0, unroll=8)
    x = x_ref[...] + g2_ref[0] * _load_row_tiles(buf.at[slot], tm)
    if final:
        x = x * lax.rsqrt(jnp.mean(x * x, axis=-1, keepdims=True) + EPS) * fn_ref[...]
    o_ref[...] = x


def gather_residual(g, dest, y_tiles, x, gate2, fnorm, final, tm):
    t, d = x.shape
    nsteps = t // tm
    kern = functools.partial(_gather_residual_kernel, tm=tm, nsteps=nsteps, final=final)
    return pl.pallas_call(
        kern,
        grid_spec=pltpu.PrefetchScalarGridSpec(
            num_scalar_prefetch=1,
            grid=(nsteps,),
            in_specs=[pl.BlockSpec(memory_space=pl.ANY),
                      pl.BlockSpec((tm, d), lambda i, dst: (i, 0)),
                      pl.BlockSpec((1, 1, d), lambda i, dst: (g.batch_of_block(i, tm), 0, 0)),
                      pl.BlockSpec((1, d), lambda i, dst: (0, 0))],
            out_specs=pl.BlockSpec((tm, d), lambda i, dst: (i, 0)),
            scratch_shapes=[pltpu.VMEM((2, tm * ROW_TILE, LANES), F32), pltpu.SemaphoreType.DMA((2,))]),
        out_shape=jax.ShapeDtypeStruct((t, d), F32),
        compiler_params=_cparams(("arbitrary",)),
        name="moe_gather_residual",
    )(dest, y_tiles, x, gate2, fnorm)


def _rope_tables(smax, rot_dim, period, lane_off):
    half = rot_dim // 2
    inv_freq = ROPE_THETA ** (-jnp.arange(half, dtype=F32) / half)
    ang = jnp.arange(smax).astype(F32)[:, None] * inv_freq[None, :]
    cos, sin = jnp.cos(ang), jnp.sin(ang)
    gl = (jnp.arange(LANES) % period) - lane_off
    first = (gl >= 0) & (gl < half)
    second = (gl >= half) & (gl < rot_dim)
    j = jnp.clip(jnp.where(first, gl, gl - half), 0, half - 1)
    cl, sn = cos[:, j], sin[:, j]
    ct = jnp.where((first | second)[None, :], cl, 1.0)
    s1 = jnp.where(first[None, :], -sn, 0.0)
    s2 = jnp.where(second[None, :], sn, 0.0)
    return ct, s1, s2


def _pad_w_in(w):
    d = w.shape[0]
    parts, off = [], 0
    for sz in IN_SIZES:
        parts.append(w[:, off:off + sz])
        off += sz
    a_q, a_k, a_v, b_q, b_k, b_v, b_r, b_zf, b_zb, c_cq, c_ckv, c_kr, gates = parts
    z = lambda n: jnp.zeros((d, n), w.dtype)
    bz = jnp.concatenate([b_zf, b_zb, z(LANES - 2 * B_GATE_RANK)], axis=1)
    kr = jnp.concatenate([z(KR_LANE), c_kr, z(LANES - KR_LANE - C_ROPE_DIM)], axis=1)
    used = OFF_KR + LANES
    out = jnp.concatenate([gates, b_v, b_r, a_q, a_k, a_v, b_q, b_k, c_ckv, bz, c_cq, kr, z(NPAD - used)], axis=1)
    assert out.shape[1] == NPAD
    return out.astype(BF16)


def _pad_heads_cols(w, real, take_lo, take_hi):
    kdim = w.shape[0]
    wh = w.reshape(kdim, C_HEADS, real)[:, :, take_lo:take_hi]
    wh = jnp.pad(wh, ((0, 0), (0, 0), (0, C_HEAD_PAD - (take_hi - take_lo))))
    return wh.reshape(kdim, C_PAD_WIDTH).astype(BF16)


def _gate_weights(w_a, b_a, row_off):
    wa = jnp.zeros((LANES, B_KW), F32).at[row_off:row_off + B_GATE_RANK].set(w_a)
    return wa.astype(BF16), b_a.reshape(1, B_KW).astype(F32)


def kernel(x_prompt, x_sample, c_prompt, c_sample, norm_mix, norm_moe, w_mod, b_mod, w_in, w_gla_af, b_gla_af,
           w_gla_ab, b_gla_ab, gla_norm, mla_q_norm, w_mla_uq, mla_kv_norm, w_mla_ukv, w_proj_a, w_proj_b, w_proj_c,
           w_out, w_router, b_router, w_exp_gate, w_exp_up, w_exp_down, final_norm):
    b1, s1, d = x_prompt.shape
    b2, s2, _ = x_sample.shape
    g = Groups(b1, s1, b2, s2)
    depth = w_in.shape[0]
    tm_in = min(1024, s1, s2)
    tm_prep = min(512, s1, s2)
    tm_proj = min(512, s1, s2)
    tm_moe = min(1024, s1, s2)

    x = jnp.concatenate([x_prompt.reshape(g.t1, d), x_sample.reshape(g.t2, d)], axis=0)
    c = jnp.concatenate([c_prompt, c_sample], axis=0)
    nbp = -(-g.nb // 8) * 8
    c_pad = jnp.pad(c, ((0, nbp - g.nb), (0, 0)))
    mod = modulation(c_pad, w_mod, b_mod)[:, :g.nb]

    smax = max(s1, s2)
    tabs_a = _rope_tables(smax, A_ROPE_DIM, A_HEAD_DIM, 0)
    tabs_c = _rope_tables(smax, C_ROPE_DIM, C_HEAD_PAD, KR_LANE)

    wr_hi = jnp.pad(w_router, ((0, 0), (0, LANES - N_EXPERTS)))
    wrh = wr_hi.astype(BF16)
    wrm = (wr_hi - wrh.astype(F32)).astype(BF16)
    fnorm = final_norm.reshape(1, d)
    br = jnp.pad(b_router.astype(F32), (0, LANES - N_EXPERTS)).reshape(1, LANES)

    for l in range(depth):
        sh1, sc1, gt1, sh2, sc2, gt2 = [m.reshape(g.nb, 1, d) for m in jnp.split(mod[l], N_MOD, axis=-1)]
        mul1 = norm_mix[l][None, None, :] * (1.0 + sc1)
        mul2 = norm_moe[l][None, None, :] * (1.0 + sc2)

        y, yd4, yd16 = in_projection(g, x, mul1, sh1, _pad_w_in(w_in[l]), tabs_a, tm_in)

        nat_cols = tuple((0, off // A_WIDTH) for off in (OFF_AQ, OFF_AK, OFF_AV))
        dil_cols = ((0, 0), (1, 0), (2, 0))
        o_list, lse_list = [], []
        for (_, dil), src in zip(A_PATTERNS, (y.reshape(1, g.t, NPAD), yd4, yd16)):
            o, lse = band_attention(g, src, dil, nat_cols if dil == 1 else dil_cols, NPAD // A_WIDTH if dil == 1 else 1)
            o_list.append(o)
            lse_list.append(lse)

        waf, baf = _gate_weights(w_gla_af[l], b_gla_af[l], 0)
        wab, bab = _gate_weights(w_gla_ab[l], b_gla_ab[l], B_GATE_RANK)
        o_back = gla_pass(g, y, wab, bab, reverse=True)
        yb = gla_pass(g, y, waf, baf, reverse=False, o_back=o_back, gain=gla_norm[l].reshape(1, B_VAL_DIM))

        wq = jnp.concatenate(
            [w_mla_uq[l].reshape(C_Q_RANK, C_HEADS, C_QK_DIM),
             jnp.zeros((C_Q_RANK, C_HEADS, C_HEAD_PAD - C_QK_DIM), F32)], axis=2
        ).reshape(C_Q_RANK, C_PAD_WIDTH).astype(BF16)
        wk = _pad_heads_cols(w_mla_ukv[l], C_NOPE_DIM + C_V_DIM, 0, C_NOPE_DIM)
        wv = _pad_heads_cols(w_mla_ukv[l], C_NOPE_DIM + C_V_DIM, C_NOPE_DIM, C_NOPE_DIM + C_V_DIM)
        qm, km, vm = mla_prep(g, y, mla_q_norm[l].reshape(1, C_Q_RANK), mla_kv_norm[l].reshape(1, C_KV_RANK),
                              wq, wk, wv, tabs_c, tm_prep)
        yc1 = mla_flash(qm, km, vm, 0, b1, s1, min(1024, s1), min(512, s1))
        yc2 = mla_flash(qm, km, vm, g.t1, b2, s2, min(512, s2), min(512, s2))

        wpc = jnp.pad(w_proj_c[l].reshape(C_HEADS, C_V_DIM, d), ((0, 0), (0, C_HEAD_PAD - C_V_DIM), (0, 0)))
        wpc = wpc.reshape(C_PAD_WIDTH, d).astype(BF16)
        x, h2, comb = proj_merge(g, o_list, lse_list, yb, yc1, yc2, y, x, gt1, mul2, sh2,
                                 w_proj_a[l].astype(BF16), w_proj_b[l].astype(BF16), wpc, w_out[l].astype(BF16),
                                 wrh, wrm, br, tm_proj)

        dest, tile_group, ntiles = _sorted_layout(comb, tm_moe)
        hs = scatter_rows(dest, h2, ntiles * tm_moe, tm_moe)
        ys = moe_grouped(tile_group, ntiles, hs, wrh, wrm, br, w_exp_gate[l].astype(BF16),
                         w_exp_up[l].astype(BF16), w_exp_down[l].astype(BF16), tm_moe)
        x = gather_residual(g, dest, ys, x, gt2, fnorm, l == depth - 1, tm_proj)

    return x[:g.t1].reshape(b1, s1, d), x[g.t1:].reshape(b2, s2, d)
```

```python
import functools

import jax
import jax.numpy as jnp
from jax import lax
from jax.experimental import pallas as pl
from jax.experimental.pallas import tpu as pltpu

F32 = jnp.float32
BF16 = jnp.bfloat16

D_MODEL = 1024
DEPTH = 2
EPS = 1e-6
ROPE_THETA = 500000.0
NEG_INF = -1e30

A_HEADS = 8
A_HEAD_DIM = 64
A_ROPE_DIM = A_HEAD_DIM // 4
A_PATTERNS = ((128, 1), (512, 4), (2048, 16))
A_WIDTH = A_HEADS * A_HEAD_DIM
A_HALF = 64

B_HEADS = 4
B_KEY_DIM = 128
B_VAL_DIM = 256
B_GATE_RANK = 16
B_GATE_TAU = 16.0
B_CHUNK = 64
B_KW = B_HEADS * B_KEY_DIM
B_VW = B_HEADS * B_VAL_DIM

C_HEADS = 8
C_NOPE_DIM = 64
C_ROPE_DIM = 32
C_V_DIM = 64
C_QK_DIM = C_NOPE_DIM + C_ROPE_DIM
C_Q_RANK = 384
C_KV_RANK = 256
C_HEAD_PAD = 128
C_PAD_WIDTH = C_HEADS * C_HEAD_PAD

N_EXPERTS = 16
N_GROUPS = 4
EXPERTS_PER_GROUP = N_EXPERTS // N_GROUPS
TOP_K = 2
GROUP_SCORE_K = 2
D_EXPERT = 512
N_MOD = 6

IN_SIZES = (A_WIDTH, A_WIDTH, A_WIDTH, B_KW, B_KW, B_VW, B_VW, B_GATE_RANK, B_GATE_RANK,
            C_Q_RANK, C_KV_RANK, C_ROPE_DIM, 3 * D_MODEL)

LANES = 128
ROW_TILE = D_MODEL // LANES
MOE_EXPERTS_PER_STEP = 2
LOG2E = 1.4426950408889634

OFF_GATES = 0
OFF_BV = 3072
OFF_BR = 4096
OFF_AQ = 5120
OFF_AK = 5632
OFF_AV = 6144
OFF_BQ = 6656
OFF_BK = 7168
OFF_CKV = 7680
OFF_BZ = 7936
OFF_CQ = 8064
OFF_KR = 8448
NPAD = 9216
KR_LANE = 64

VMEM_LIMIT = 56 * 1024 * 1024


def _cparams(sem):
    return pltpu.CompilerParams(dimension_semantics=sem, vmem_limit_bytes=VMEM_LIMIT)


class Groups:
    def __init__(self, b1, s1, b2, s2):
        self.b1, self.s1, self.b2, self.s2 = b1, s1, b2, s2
        self.t1, self.t2 = b1 * s1, b2 * s2
        self.t = self.t1 + self.t2
        self.nb = b1 + b2

    def batch_of_block(self, i, tm):
        n1 = self.t1 // tm
        return jnp.where(i < n1, i // (self.s1 // tm), self.b1 + (i - n1) // (self.s2 // tm))

    def pos_block(self, i, tm):
        n1 = self.t1 // tm
        return jnp.where(i < n1, i % (self.s1 // tm), (i - n1) % (self.s2 // tm))


def _dot(a, b):
    return jnp.dot(a, b, preferred_element_type=F32)


def _dot_nt(a, b):
    return lax.dot_general(a, b, (((1,), (1,)), ((), ())), preferred_element_type=F32)


def _dot_tn(a, b):
    return lax.dot_general(a, b, (((0,), (0,)), ((), ())), preferred_element_type=F32)


def _split3(a):
    hi = a.astype(BF16)
    r1 = a - hi.astype(F32)
    mid = r1.astype(BF16)
    lo = (r1 - mid.astype(F32)).astype(BF16)
    return hi, mid, lo


def _rope(y, ct, s1, s2, half):
    return y * ct + pltpu.roll(y, LANES - half, 1) * s1 + pltpu.roll(y, half, 1) * s2


def _sigmoid(x):
    return 0.5 * jnp.tanh(0.5 * x) + 0.5


def _mod_kernel(c_ref, w_ref, b_ref, o_ref):
    c = c_ref[...]
    ca = c * _sigmoid(c)
    ch, cm, _ = _split3(ca)
    wh, wm, _ = _split3(w_ref[0])
    o_ref[0] = _dot(ch, wh) + _dot(cm, wh) + _dot(ch, wm) + b_ref[0]


def modulation(c_pad, w_mod, b_mod):
    nbp, d = c_pad.shape
    depth, _, n = w_mod.shape
    tn = 512
    return pl.pallas_call(
        _mod_kernel,
        grid=(depth, n // tn),
        in_specs=[pl.BlockSpec((nbp, d), lambda l, j: (0, 0)),
                  pl.BlockSpec((1, d, tn), lambda l, j: (l, 0, j)),
                  pl.BlockSpec((1, 1, tn), lambda l, j: (l, 0, j))],
        out_specs=pl.BlockSpec((1, nbp, tn), lambda l, j: (l, 0, j)),
        out_shape=jax.ShapeDtypeStruct((depth, nbp, n), F32),
        compiler_params=_cparams(("arbitrary", "arbitrary")),
        name="modulation",
    )(c_pad, w_mod, b_mod.reshape(depth, 1, n))


def _inproj_kernel(x_ref, mul_ref, sh_ref, w_ref, ct_ref, s1_ref, s2_ref, o_ref, od4_ref, od16_ref,
                   h_scr, y_scr, y4_scr, *, tm, j_qk, j_v):
    j = pl.program_id(1)
    aw = A_WIDTH
    ncol = aw // LANES

    @pl.when(j == 0)
    def _():
        x = x_ref[...]
        ms = jnp.mean(x * x, axis=-1, keepdims=True)
        h = x * lax.rsqrt(ms + EPS) * mul_ref[0] + sh_ref[0]
        h_scr[...] = h.astype(BF16)

    def attention_block(kind, half, rope):
        y = _dot(h_scr[...], w_ref[j, :, half * aw:(half + 1) * aw])
        if rope:
            ct, s1, s2 = ct_ref[...], s1_ref[...], s2_ref[...]
        for c in range(ncol):
            yc = y[:, c * LANES:(c + 1) * LANES]
            y_scr[c] = _rope(yc, ct, s1, s2, A_ROPE_DIM // 2) if rope else yc
        n4, n16 = tm // 4, tm // 16
        for c in range(ncol):
            o_ref[:, half * aw + c * LANES:half * aw + (c + 1) * LANES] = y_scr[c].astype(BF16)
            for r in range(4):
                y4 = y_scr[c, pl.ds(r, n4, stride=4), :]
                y4_scr[c, r * n4:(r + 1) * n4, :] = y4
                od4_ref[kind, :, r * aw + c * LANES:r * aw + (c + 1) * LANES] = y4.astype(BF16)
            for r in range(4):
                for r2 in range(4):
                    col = (r + 4 * r2) * aw + c * LANES
                    od16_ref[kind, :, col:col + LANES] = y4_scr[c, pl.ds(r * n4 + r2, n16, stride=4), :].astype(BF16)

    @pl.when((j != j_qk) & (j != j_v))
    def _():
        o_ref[...] = _dot(h_scr[...], w_ref[j]).astype(BF16)

    @pl.when(j == j_qk)
    def _():
        attention_block(0, 0, True)
        attention_block(1, 1, True)

    @pl.when(j == j_v)
    def _():
        attention_block(2, 0, False)
        o_ref[:, aw:] = _dot(h_scr[...], w_ref[j, :, aw:]).astype(BF16)


def in_projection(g, x, mul, shift, w_pad, tabs, tm):
    t, d = x.shape
    tn = 2 * A_WIDTH
    aw = A_WIDTH
    ct, s1, s2 = tabs
    assert OFF_AQ % tn == 0 and OFF_AK == OFF_AQ + aw and OFF_AV == OFF_AQ + tn and NPAD % tn == 0
    bmap = lambda i, j: (g.batch_of_block(i, tm), 0, 0)
    pmap = lambda i, j: (g.pos_block(i, tm), 0)
    kern = functools.partial(_inproj_kernel, tm=tm, j_qk=OFF_AQ // tn, j_v=OFF_AV // tn)
    return pl.pallas_call(
        kern,
        grid=(t // tm, NPAD // tn),
        in_specs=[pl.BlockSpec((tm, d), lambda i, j: (i, 0)),
                  pl.BlockSpec((1, 1, d), bmap),
                  pl.BlockSpec((1, 1, d), bmap),
                  pl.BlockSpec((NPAD // tn, d, tn), lambda i, j: (0, 0, 0), pipeline_mode=pl.Buffered(1)),
                  pl.BlockSpec((tm, LANES), pmap),
                  pl.BlockSpec((tm, LANES), pmap),
                  pl.BlockSpec((tm, LANES), pmap)],
        out_specs=[pl.BlockSpec((tm, tn), lambda i, j: (i, j)),
                   pl.BlockSpec((3, tm // 4, 4 * aw), lambda i, j: (0, i, 0)),
                   pl.BlockSpec((3, tm // 16, 16 * aw), lambda i, j: (0, i, 0))],
        out_shape=[jax.ShapeDtypeStruct((t, NPAD), BF16),
                   jax.ShapeDtypeStruct((3, t // 4, 4 * aw), BF16),
                   jax.ShapeDtypeStruct((3, t // 16, 16 * aw), BF16)],
        scratch_shapes=[pltpu.VMEM((tm, d), BF16), pltpu.VMEM((aw // LANES, tm, LANES), F32),
                        pltpu.VMEM((aw // LANES, tm, LANES), F32)],
        compiler_params=_cparams(("arbitrary", "arbitrary")),
        name="in_projection",
    )(x, mul, shift, w_pad.reshape(d, NPAD // tn, tn).transpose(1, 0, 2), ct, s1, s2)


def _band_kernel(q_ref, kp_ref, km_ref, kn_ref, vp_ref, vm_ref, vn_ref, o_ref, lse_ref, *, tb, tq, rows1, l1, l2):
    i = pl.program_id(1)
    tk = tq + 2 * A_HALF
    k = jnp.concatenate([kp_ref[...], km_ref[...], kn_ref[...]], axis=0)
    v = jnp.concatenate([vp_ref[...], vm_ref[...], vn_ref[...]], axis=0)
    qi = lax.broadcasted_iota(jnp.int32, (tq, tk), 0)
    kj = lax.broadcasted_iota(jnp.int32, (tq, tk), 1)
    band = jnp.abs(kj - A_HALF - qi) <= A_HALF
    lo = lax.broadcasted_iota(jnp.int32, (tq, LANES), 1) < A_HEAD_DIM
    scale = A_HEAD_DIM ** -0.5
    for u in range(tb // tq):
        row0 = i * tb + u * tq
        in1 = row0 < rows1
        seq_len = jnp.where(in1, l1, l2)
        pos0 = jnp.where(in1, row0 % l1, (row0 - rows1) % l2)
        kpos = pos0 - A_HALF + kj
        valid = band & (kpos >= 0) & (kpos < seq_len)
        valid2 = jnp.concatenate([valid, valid], axis=0)
        rows = slice(u * tq, (u + 1) * tq)
        krows = slice(u * tq, u * tq + tk)
        for p in range(A_WIDTH // LANES):
            sl = slice(p * LANES, (p + 1) * LANES)
            qp, kp, vp = q_ref[rows, sl] * scale, k[krows, sl], v[krows, sl]
            zero = jnp.zeros_like(qp)
            qm = jnp.concatenate([jnp.where(lo, qp, zero), jnp.where(lo, zero, qp)], axis=0)
            s = jnp.where(valid2, _dot_nt(qm, kp), NEG_INF)
            m = jnp.max(s, axis=1, keepdims=True)
            e = jnp.exp(s - m)
            l = jnp.sum(e, axis=1, keepdims=True)
            o2 = _dot(e.astype(BF16), vp) / l
            lse2 = m + jnp.log(l)
            o_ref[rows, sl] = jnp.where(lo, o2[:tq], o2[tq:]).astype(BF16)
            lse_ref[rows, sl] = jnp.where(lo, lse2[:tq], lse2[tq:])


def band_attention(g, qkv, dil, cols, cpb):
    rows = g.t // dil
    tq = 128
    tb = 512
    sub = tb // A_HALF
    nsub = rows // A_HALF

    def main(c):
        ld, col = cols[c]
        return pl.BlockSpec((None, tb, A_WIDTH), lambda r, i: (ld, i, r * cpb + col))

    def prev(c):
        ld, col = cols[c]
        return pl.BlockSpec((None, A_HALF, A_WIDTH), lambda r, i: (ld, jnp.maximum(i * sub - 1, 0), r * cpb + col))

    def nxt(c):
        ld, col = cols[c]
        return pl.BlockSpec((None, A_HALF, A_WIDTH),
                            lambda r, i: (ld, jnp.minimum((i + 1) * sub, nsub - 1), r * cpb + col))

    assert (g.s1 // dil) % tq == 0 and (g.s2 // dil) % tq == 0 and rows % tb == 0
    kern = functools.partial(_band_kernel, tb=tb, tq=tq, rows1=g.t1 // dil, l1=g.s1 // dil, l2=g.s2 // dil)
    return pl.pallas_call(
        kern,
        grid=(dil, rows // tb),
        in_specs=[main(0), prev(1), main(1), nxt(1), prev(2), main(2), nxt(2)],
        out_specs=[pl.BlockSpec((tb, A_WIDTH), lambda r, i: (i, r)),
                   pl.BlockSpec((tb, A_WIDTH), lambda r, i: (i, r))],
        out_shape=[jax.ShapeDtypeStruct((rows, dil * A_WIDTH), BF16),
                   jax.ShapeDtypeStruct((rows, dil * A_WIDTH), F32)],
        compiler_params=_cparams(("arbitrary", "arbitrary")),
        name="band_attention_d%d" % dil,
    )(qkv, qkv, qkv, qkv, qkv, qkv, qkv)


def _gla_kernel(*refs, reverse, final, tc, nblk, t1, s1, s2):
    if final:
        q_ref, k_ref, v_ref, z_ref, wa_ref, ba_ref, tri_ref, ob_ref, r_ref, gain_ref, o_ref, st_scr = refs
    else:
        q_ref, k_ref, v_ref, z_ref, wa_ref, ba_ref, tri_ref, o_ref, st_scr = refs
    i = pl.program_id(0)
    blk = (nblk - 1 - i) if reverse else i
    row0 = blk * tc
    in1 = row0 < t1
    pos0 = jnp.where(in1, row0 % s1, (row0 - t1) % s2)
    slen = jnp.where(in1, s1, s2)
    start = (pos0 + tc == slen) if reverse else (pos0 == 0)

    @pl.when(start)
    def _():
        st_scr[...] = jnp.zeros_like(st_scr)

    zl = _dot(z_ref[...], wa_ref[...]) + ba_ref[...]
    la = (jnp.minimum(zl, 0.0) - jnp.log(1.0 + jnp.exp(-jnp.abs(zl)))) * (1.0 / B_GATE_TAU)
    hi, mid, _ = _split3(la)
    tri = tri_ref[...]
    tb = tri.shape[0]
    bc_all = jnp.concatenate(
        [_dot(tri, hi[b * tb:(b + 1) * tb]) + _dot(tri, mid[b * tb:(b + 1) * tb]) for b in range(tc // tb)], axis=0)

    qi = lax.broadcasted_iota(jnp.int32, (B_CHUNK, B_CHUNK), 0)
    si = lax.broadcasted_iota(jnp.int32, (B_CHUNK, B_CHUNK), 1)
    mask = (si > qi) if reverse else (si <= qi)
    nch = tc // B_CHUNK
    q_all = q_ref[...].astype(F32) * (B_KEY_DIM ** -0.5)
    k_all = k_ref[...].astype(F32)
    q_dec = (q_all * jnp.exp(bc_all)).astype(BF16)
    k_inv = (k_all * jnp.exp(-bc_all)).astype(BF16)
    st = [st_scr[h] for h in range(B_HEADS)]
    for c in (range(nch - 1, -1, -1) if reverse else range(nch)):
        sl = slice(c * B_CHUNK, (c + 1) * B_CHUNK)
        edge = c * B_CHUNK if reverse else (c + 1) * B_CHUNK - 1
        tot = bc_all[edge:edge + 1]
        k_end = (k_all[sl] * jnp.exp(tot - bc_all[sl])).astype(BF16)
        dec = jnp.exp(tot)
        for h in range(B_HEADS):
            ks = slice(h * B_KEY_DIM, (h + 1) * B_KEY_DIM)
            vs = slice(h * B_VAL_DIM, (h + 1) * B_VAL_DIM)
            v = v_ref[sl, vs]
            att = jnp.where(mask, _dot_nt(q_dec[sl, ks], k_inv[sl, ks]), 0.0).astype(BF16)
            o = _dot(att, v) + _dot_nt(q_dec[sl, ks], st[h].astype(BF16))
            st[h] = st[h] * dec[:, ks] + _dot_tn(v, k_end[:, ks])
            if final:
                o = o + ob_ref[sl, vs]
                on = o * lax.rsqrt(jnp.mean(o * o, axis=-1, keepdims=True) + EPS) * gain_ref[...]
                r = r_ref[sl, vs].astype(F32)
                o_ref[sl, vs] = (on * (r * _sigmoid(r))).astype(BF16)
            else:
                o_ref[sl, vs] = o
    for h in range(B_HEADS):
        st_scr[h] = st[h]


def gla_pass(g, y, wa, ba, reverse, o_back=None, gain=None):
    t = g.t
    tc = min(512, g.s1, g.s2)
    nblk = t // tc
    final = o_back is not None
    rowmap = (lambda i: nblk - 1 - i) if reverse else (lambda i: i)
    row = lambda w, off: pl.BlockSpec((tc, w), lambda i: (rowmap(i), off // w))
    full = lambda a: pl.BlockSpec(a.shape, lambda i: (0,) * a.ndim)
    tb = min(tc, 256)
    ri = jnp.arange(tb)[:, None]
    ci = jnp.arange(tb)[None, :]
    tri = ((ri // B_CHUNK == ci // B_CHUNK) & ((ci >= ri) if reverse else (ci <= ri))).astype(BF16)
    in_specs = [row(B_KW, OFF_BQ), row(B_KW, OFF_BK), row(B_VW, OFF_BV), row(LANES, OFF_BZ),
                full(wa), full(ba), full(tri)]
    args = [y, y, y, y, wa, ba, tri]
    if final:
        in_specs += [row(B_VW, 0), row(B_VW, OFF_BR), full(gain)]
        args += [o_back, y, gain]
    kern = functools.partial(_gla_kernel, reverse=reverse, final=final, tc=tc, nblk=nblk, t1=g.t1, s1=g.s1, s2=g.s2)
    return pl.pallas_call(
        kern,
        grid=(nblk,),
        in_specs=in_specs,
        out_specs=row(B_VW, 0),
        out_shape=jax.ShapeDtypeStruct((t, B_VW), BF16 if final else F32),
        scratch_shapes=[pltpu.VMEM((B_HEADS, B_VAL_DIM, B_KEY_DIM), F32)],
        compiler_params=_cparams(("arbitrary",)),
        name="gla_forward_final" if final else "gla_backward",
    )(*args)


def _mla_prep_kernel(ckv_ref, cq_ref, kr_ref, qn_ref, kvn_ref, wq_ref, wk_ref, wv_ref, ct_ref, s1_ref, s2_ref,
                     q_out, k_out, v_out):
    def norm(ref, gain_ref):
        xf = ref[...].astype(F32)
        return (xf * lax.rsqrt(jnp.mean(xf * xf, axis=-1, keepdims=True) + EPS) * gain_ref[...]).astype(BF16)

    ckv_n = norm(ckv_ref, kvn_ref)
    cq_n = norm(cq_ref, qn_ref)
    q = _dot(cq_n, wq_ref[...])
    kn = _dot(ckv_n, wk_ref[...])
    v = _dot(ckv_n, wv_ref[...])
    ct, s1, s2 = ct_ref[...], s1_ref[...], s2_ref[...]
    half = C_ROPE_DIM // 2
    kr_rot = _rope(kr_ref[...].astype(F32), ct, s1, s2, half)
    ones_lane = lax.broadcasted_iota(jnp.int32, kr_rot.shape, 1) == C_V_DIM
    scale = C_QK_DIM ** -0.5 * LOG2E
    for h in range(C_HEADS):
        sl = slice(h * C_HEAD_PAD, (h + 1) * C_HEAD_PAD)
        q_out[:, sl] = (_rope(q[:, sl], ct, s1, s2, half) * scale).astype(BF16)
        k_out[:, sl] = (kn[:, sl] + kr_rot).astype(BF16)
        v_out[:, sl] = jnp.where(ones_lane, 1.0, v[:, sl]).astype(BF16)


def mla_prep(g, y, qn, kvn, wq, wk, wv, tabs, tm):
    t = g.t
    ct, s1, s2 = tabs
    pmap = lambda i: (g.pos_block(i, tm), 0)
    full = lambda a: pl.BlockSpec(a.shape, lambda i: (0,) * a.ndim)
    out = jax.ShapeDtypeStruct((t, C_PAD_WIDTH), BF16)
    ospec = pl.BlockSpec((tm, C_PAD_WIDTH), lambda i: (i, 0))
    return pl.pallas_call(
        _mla_prep_kernel,
        grid=(t // tm,),
        in_specs=[pl.BlockSpec((tm, C_KV_RANK), lambda i: (i, OFF_CKV // C_KV_RANK)),
                  pl.BlockSpec((tm, C_Q_RANK), lambda i: (i, OFF_CQ // C_Q_RANK)),
                  pl.BlockSpec((tm, LANES), lambda i: (i, OFF_KR // LANES)),
                  full(qn), full(kvn), full(wq), full(wk), full(wv),
                  pl.BlockSpec((tm, LANES), pmap), pl.BlockSpec((tm, LANES), pmap), pl.BlockSpec((tm, LANES), pmap)],
        out_specs=[ospec, ospec, ospec],
        out_shape=[out, out, out],
        compiler_params=_cparams(("arbitrary",)),
        name="mla_prep",
    )(y, y, y, qn, kvn, wq, wk, wv, ct, s1, s2)


def _flash_kernel(q_ref, k_ref, v_ref, o_ref, s0_scr, s1_scr, p_scr, m_scr, a_scr, acc_scr, *, tk, nk, rc, unroll):
    tq = q_ref.shape[0]

    def logits(t, scr):
        off = pl.multiple_of(t * tk, tk)
        scr[...] = _dot_nt(q_ref[...], k_ref[pl.ds(off, tk), :])

    def softmax_pv(t, scr):
        off = pl.multiple_of(t * tk, tk)
        for r in range(tq // rc):
            rows = slice(r * rc, (r + 1) * rc)
            s = scr[rows, :]
            m_old = m_scr[rows, :]
            m_new = jnp.maximum(m_old, jnp.max(s, axis=1, keepdims=True))
            p_scr[rows, :] = jnp.exp2(s - jnp.concatenate([m_new] * (tk // LANES), axis=1)).astype(BF16)
            a_scr[rows, :] = jnp.exp2(m_old - m_new)
            m_scr[rows, :] = m_new
        acc_scr[...] = acc_scr[...] * a_scr[...] + _dot(p_scr[...], v_ref[pl.ds(off, tk), :])

    bufs = (s0_scr, s1_scr)

    def group(t0, last):
        for u in range(unroll):
            if not (last and u == unroll - 1):
                logits(t0 + u + 1, bufs[(u + 1) % 2])
            softmax_pv(t0 + u, bufs[u % 2])

    def body(jj, carry):
        group(unroll * jj, False)
        return carry

    m_scr[...] = jnp.full(m_scr.shape, NEG_INF, F32)
    acc_scr[...] = jnp.zeros(acc_scr.shape, F32)
    logits(0, s0_scr)
    lax.fori_loop(0, nk // unroll - 1, body, 0)
    group(nk - unroll, True)
    acc = acc_scr[...]
    o_ref[...] = (acc / acc[:, C_V_DIM:C_V_DIM + 1]).astype(BF16)


def mla_flash(q, k, v, row_off, nseq, s, tq, tk):
    assert row_off % s == 0 and row_off % tq == 0
    qb0 = row_off // tq
    sb0 = row_off // s
    nq = s // tq
    nk = s // tk
    unroll = next(u for u in (16, 8, 4, 2) if nk % u == 0)
    assert nk % unroll == 0
    kern = functools.partial(_flash_kernel, tk=tk, nk=nk, rc=64, unroll=unroll)
    return pl.pallas_call(
        kern,
        grid=(nseq, C_HEADS, nq),
        scratch_shapes=[pltpu.VMEM((tq, tk), F32), pltpu.VMEM((tq, tk), F32), pltpu.VMEM((tq, tk), BF16),
                        pltpu.VMEM((tq, LANES), F32), pltpu.VMEM((tq, LANES), F32),
                        pltpu.VMEM((tq, C_HEAD_PAD), F32)],
        in_specs=[pl.BlockSpec((tq, C_HEAD_PAD), lambda b, h, i: (qb0 + b * nq + i, h)),
                  pl.BlockSpec((s, C_HEAD_PAD), lambda b, h, i: (sb0 + b, h)),
                  pl.BlockSpec((s, C_HEAD_PAD), lambda b, h, i: (sb0 + b, h))],
        out_specs=pl.BlockSpec((tq, C_HEAD_PAD), lambda b, h, i: (b * nq + i, h)),
        out_shape=jax.ShapeDtypeStruct((nseq * s, C_PAD_WIDTH), BF16),
        compiler_params=_cparams(("arbitrary", "arbitrary", "arbitrary")),
        name="mla_flash_s%d" % s,
    )(q, k, v)


def _store_row_tiles(ref, x, rows):
    for c in range(ROW_TILE):
        ref[pl.ds(c, rows, stride=ROW_TILE), :] = x[:, c * LANES:(c + 1) * LANES]


def _load_row_tiles(ref, rows):
    return jnp.concatenate([ref[pl.ds(c, rows, stride=ROW_TILE), :] for c in range(ROW_TILE)], axis=1)


def _router_scores(h, wrh_ref, wrm_ref):
    hh, hm, _ = _split3(h)
    return _sigmoid(_dot(hh, wrh_ref[...]) + _dot(hm, wrh_ref[...]) + _dot(hh, wrm_ref[...]))


def _rank_in_group(sel, pos):
    rank = jnp.zeros(sel.shape, jnp.int32)
    for k in range(1, EXPERTS_PER_GROUP):
        below = pltpu.roll(sel, k, 1)
        above = pltpu.roll(sel, LANES - k, 1)
        rank += jnp.where((pos >= k) & (below >= sel), 1, 0)
        rank += jnp.where((pos + k < EXPERTS_PER_GROUP) & (above > sel), 1, 0)
    return rank


def _route_in_group(s, bias, g):
    lane = lax.broadcasted_iota(jnp.int32, s.shape, 1)
    rank = _rank_in_group(s + bias, lane % EXPERTS_PER_GROUP)
    chosen = (lane // EXPERTS_PER_GROUP == g) & (rank < TOP_K)
    total = jnp.sum(jnp.where(chosen, s, 0.0), axis=1, keepdims=True)
    return jnp.where(chosen, s / total, 0.0)


def _route(s, bias):
    lane = lax.broadcasted_iota(jnp.int32, s.shape, 1)
    pos = lane % EXPERTS_PER_GROUP
    grp = lane // EXPERTS_PER_GROUP
    sel = s + bias
    rank = _rank_in_group(sel, pos)
    top = rank < GROUP_SCORE_K
    contrib = jnp.where(top, sel, 0.0)
    score = contrib
    for k in range(1, EXPERTS_PER_GROUP):
        score += jnp.where(pos >= k, pltpu.roll(contrib, k, 1), 0.0)
        score += jnp.where(pos + k < EXPERTS_PER_GROUP, pltpu.roll(contrib, LANES - k, 1), 0.0)
    best = lane < N_EXPERTS
    for k in range(1, N_GROUPS):
        earlier = pltpu.roll(score, k * EXPERTS_PER_GROUP, 1)
        later = pltpu.roll(score, LANES - k * EXPERTS_PER_GROUP, 1)
        best &= jnp.logical_not((grp >= k) & (earlier >= score))
        best &= jnp.logical_not((grp + k < N_GROUPS) & (later > score))
    chosen = best & (rank < TOP_K)
    total = jnp.sum(jnp.where(chosen, s, 0.0), axis=1, keepdims=True)
    return jnp.where(chosen, s / total, 0.0)


def _proj_kernel(o1_ref, o2_ref, o3_ref, l1_ref, l2_ref, l3_ref, yb_ref, yc1_ref, yc2_ref, ga_ref, gb_ref, gc_ref,
                 x_ref, g1_ref, mul2_ref, sh2_ref, wpa_ref, wpb_ref, wpc_ref, wout_ref, wrh_ref, wrm_ref, br_ref,
                 x_out, h_out, comb_out, o2_scr, l2_scr, o3_scr, l3_scr, *, tm, n1):
    in_group1 = pl.program_id(0) < n1
    ncol = A_WIDTH // LANES
    hm = tm // 2
    for half in range(2):
        rows = slice(half * hm, (half + 1) * hm)
        for dil, o_ref, l_ref, o_scr, l_scr in ((A_PATTERNS[1][1], o2_ref, l2_ref, o2_scr, l2_scr),
                                               (A_PATTERNS[2][1], o3_ref, l3_ref, o3_scr, l3_scr)):
            drows = slice(half * hm // dil, (half + 1) * hm // dil)
            for r in range(dil):
                for c in range(ncol):
                    sl = slice(r * A_WIDTH + c * LANES, r * A_WIDTH + (c + 1) * LANES)
                    o_scr[c, pl.ds(half * hm + r, hm // dil, stride=dil), :] = o_ref[drows, sl].astype(F32)
                    l_scr[c, pl.ds(half * hm + r, hm // dil, stride=dil), :] = l_ref[drows, sl]
        slabs = lambda scr: jnp.concatenate([scr[c, rows, :] for c in range(ncol)], axis=1)
        l1, l2, l3 = l1_ref[rows, :], slabs(l2_scr), slabs(l3_scr)
        m = jnp.maximum(jnp.maximum(l1, l2), l3)
        e1, e2, e3 = jnp.exp(l1 - m), jnp.exp(l2 - m), jnp.exp(l3 - m)
        ya = (e1 * o1_ref[rows, :].astype(F32) + e2 * slabs(o2_scr) + e3 * slabs(o3_scr)) / (e1 + e2 + e3)
        sig = lambda ref: _sigmoid(ref[rows, :].astype(F32))
        yc = jnp.where(in_group1, yc1_ref[rows, :], yc2_ref[rows, :])
        merged = (sig(ga_ref) * _dot(ya.astype(BF16), wpa_ref[...])
                  + sig(gb_ref) * _dot(yb_ref[rows, :], wpb_ref[...])
                  + sig(gc_ref) * _dot(yc, wpc_ref[...]))
        out = _dot(merged.astype(BF16), wout_ref[...])
        x = x_ref[rows, :] + g1_ref[0] * out
        x_out[rows, :] = x
        h = x * lax.rsqrt(jnp.mean(x * x, axis=-1, keepdims=True) + EPS) * mul2_ref[0] + sh2_ref[0]
        _store_row_tiles(h_out.at[pl.ds(half * hm * ROW_TILE, hm * ROW_TILE), :], h, hm)
        comb_out[rows, :] = _route(_router_scores(h, wrh_ref, wrm_ref), br_ref[...])


def proj_merge(g, o_list, lse_list, yb, yc1, yc2, y, x, gate1, mul2, sh2, wpa, wpb, wpc, wout, wrh, wrm, br, tm):
    t, d = x.shape
    n1 = g.t1 // tm
    n2 = g.t2 // tm
    bmap = lambda i: (g.batch_of_block(i, tm), 0, 0)
    row = lambda w, col=0: pl.BlockSpec((tm, w), lambda i: (i, col))
    dil_row = lambda dil: pl.BlockSpec((tm // dil, dil * A_WIDTH), lambda i: (i, 0))
    full = lambda a: pl.BlockSpec(a.shape, lambda i: (0,) * a.ndim)
    gcol = OFF_GATES // d
    d2, d3 = A_PATTERNS[1][1], A_PATTERNS[2][1]
    attn_specs = [row(A_WIDTH), dil_row(d2), dil_row(d3)]
    return pl.pallas_call(
        functools.partial(_proj_kernel, tm=tm, n1=n1),
        grid=(t // tm,),
        in_specs=attn_specs + attn_specs + [
            row(B_VW),
            pl.BlockSpec((tm, C_PAD_WIDTH), lambda i: (jnp.minimum(i, n1 - 1), 0)),
            pl.BlockSpec((tm, C_PAD_WIDTH), lambda i: (jnp.clip(i - n1, 0, n2 - 1), 0)),
            row(d, gcol), row(d, gcol + 1), row(d, gcol + 2),
            row(d), pl.BlockSpec((1, 1, d), bmap), pl.BlockSpec((1, 1, d), bmap), pl.BlockSpec((1, 1, d), bmap),
            full(wpa), full(wpb), full(wpc), full(wout), full(wrh), full(wrm), full(br)],
        out_specs=[row(d), pl.BlockSpec((tm * ROW_TILE, LANES), lambda i: (i, 0)), row(LANES)],
        out_shape=[jax.ShapeDtypeStruct((t, d), F32), jax.ShapeDtypeStruct((t * ROW_TILE, LANES), F32),
                   jax.ShapeDtypeStruct((t, LANES), F32)],
        scratch_shapes=[pltpu.VMEM((A_WIDTH // LANES, tm, LANES), F32)] * 4,
        compiler_params=_cparams(("arbitrary",)),
        name="proj_merge_route",
    )(*o_list, *lse_list, yb, yc1, yc2, y, y, y, x, gate1, mul2, sh2, wpa, wpb, wpc, wout, wrh, wrm, br)


def _sorted_layout(comb, tmoe):
    t = comb.shape[0]
    gw = comb[:, :N_EXPERTS].reshape(t, N_GROUPS, EXPERTS_PER_GROUP).sum(-1)
    gid = jnp.argmax(gw, axis=-1).astype(jnp.int32)
    onehot = (gid[:, None] == jnp.arange(N_GROUPS, dtype=jnp.int32)[None, :]).astype(jnp.int32)
    incl = jnp.cumsum(onehot, axis=0)
    counts = incl[-1]
    padded = (counts + tmoe - 1) // tmoe * tmoe
    ends = jnp.cumsum(padded)
    dest = jnp.sum(onehot * (incl - onehot + (ends - padded)[None, :]), axis=1).astype(jnp.int32)
    ntiles = t // tmoe + N_GROUPS
    tile_group = jnp.sum((jnp.arange(ntiles, dtype=jnp.int32) * tmoe)[:, None] >= ends[None, :], axis=1)
    tile_group = jnp.where(tile_group < N_GROUPS, tile_group, -1).astype(jnp.int32)
    return dest, tile_group, ntiles


def _row_copy(src, src_row, dst, dst_row, sem):
    return pltpu.make_async_copy(src.at[pl.ds(src_row * ROW_TILE, ROW_TILE), :],
                                 dst.at[pl.ds(dst_row * ROW_TILE, ROW_TILE), :], sem)


def _scatter_rows_kernel(dest_ref, h_ref, init_hbm, hs_hbm, sem, *, tm):
    del init_hbm
    i = pl.program_id(0)

    def issue(r, c):
        _row_copy(h_ref, r, hs_hbm, dest_ref[i * tm + r], sem).start()
        return c

    def drain(r, c):
        _row_copy(h_ref, 0, hs_hbm, 0, sem).wait()
        return c

    lax.fori_loop(0, tm, issue, 0, unroll=8)
    lax.fori_loop(0, tm, drain, 0, unroll=8)


def scatter_rows(dest, h_tiles, rows_out, tm):
    t = dest.shape[0]
    init = jnp.zeros((rows_out * ROW_TILE, LANES), F32)
    return pl.pallas_call(
        functools.partial(_scatter_rows_kernel, tm=tm),
        grid_spec=pltpu.PrefetchScalarGridSpec(
            num_scalar_prefetch=1,
            grid=(t // tm,),
            in_specs=[pl.BlockSpec((tm * ROW_TILE, LANES), lambda i, dst: (i, 0)),
                      pl.BlockSpec(memory_space=pl.ANY)],
            out_specs=pl.BlockSpec(memory_space=pl.ANY),
            scratch_shapes=[pltpu.SemaphoreType.DMA(())]),
        out_shape=jax.ShapeDtypeStruct(init.shape, F32),
        input_output_aliases={2: 0},
        compiler_params=_cparams(("arbitrary",)),
        name="moe_scatter_rows",
    )(dest, h_tiles, init)


def _moe_group_kernel(tg_ref, hs_ref, wrh_ref, wrm_ref, br_ref, wg_ref, wu_ref, wd_ref, y_ref,
                      h_scr, comb_scr, acc_scr, *, tm):
    j = pl.program_id(0)
    k = pl.program_id(1)
    g = tg_ref[j]

    @pl.when(k == 0)
    def _():
        h = _load_row_tiles(hs_ref, tm)
        h_scr[...] = h.astype(BF16)
        comb_scr[...] = _route_in_group(_router_scores(h, wrh_ref, wrm_ref), br_ref[...], g)
        acc_scr[...] = jnp.zeros_like(acc_scr)

    @pl.when(g >= 0)
    def _():
        h = h_scr[...]
        comb = comb_scr[...]
        lane = lax.broadcasted_iota(jnp.int32, comb.shape, 1)
        total = acc_scr[...]
        for kk in range(MOE_EXPERTS_PER_STEP):
            gate = _dot(h, wg_ref[kk])
            a = gate * _sigmoid(gate) * _dot(h, wu_ref[kk])
            yv = _dot(a.astype(BF16), wd_ref[kk])
            e = g * EXPERTS_PER_GROUP + k * MOE_EXPERTS_PER_STEP + kk
            total = total + jnp.sum(jnp.where(lane == e, comb, 0.0), axis=1, keepdims=True) * yv
        acc_scr[...] = total

    @pl.when(k == EXPERTS_PER_GROUP // MOE_EXPERTS_PER_STEP - 1)
    def _():
        _store_row_tiles(y_ref, acc_scr[...], tm)


def moe_grouped(tile_group, ntiles, hs, wrh, wrm, br, wg, wu, wd, tm):
    d = D_MODEL
    full = lambda a: pl.BlockSpec(a.shape, lambda j, k, tg: (0,) * a.ndim)
    ksteps = EXPERTS_PER_GROUP // MOE_EXPERTS_PER_STEP
    emap = lambda j, k, tg: (jnp.maximum(tg[j], 0) * ksteps + k, 0, 0)
    return pl.pallas_call(
        functools.partial(_moe_group_kernel, tm=tm),
        grid_spec=pltpu.PrefetchScalarGridSpec(
            num_scalar_prefetch=1,
            grid=(ntiles, ksteps),
            in_specs=[pl.BlockSpec((tm * ROW_TILE, LANES), lambda j, k, tg: (j, 0)),
                      full(wrh), full(wrm), full(br),
                      pl.BlockSpec((MOE_EXPERTS_PER_STEP, d, D_EXPERT), emap),
                      pl.BlockSpec((MOE_EXPERTS_PER_STEP, d, D_EXPERT), emap),
                      pl.BlockSpec((MOE_EXPERTS_PER_STEP, D_EXPERT, d), emap)],
            out_specs=pl.BlockSpec((tm * ROW_TILE, LANES), lambda j, k, tg: (j, 0)),
            scratch_shapes=[pltpu.VMEM((tm, d), BF16), pltpu.VMEM((tm, LANES), F32), pltpu.VMEM((tm, d), F32)]),
        out_shape=jax.ShapeDtypeStruct(hs.shape, F32),
        compiler_params=_cparams(("arbitrary", "arbitrary")),
        name="moe_grouped",
    )(tile_group, hs, wrh, wrm, br, wg, wu, wd)


def _gather_residual_kernel(dest_ref, y_hbm, x_ref, g2_ref, fn_ref, o_ref, buf, sem, *, tm, nsteps, final):
    i = pl.program_id(0)

    def fetch(step, slot):
        def issue(r, c):
            _row_copy(y_hbm, dest_ref[step * tm + r], buf.at[slot], r, sem.at[slot]).start()
            return c
        lax.fori_loop(0, tm, issue, 0, unroll=8)

    @pl.when(i == 0)
    def _():
        fetch(0, 0)

    @pl.when(i + 1 < nsteps)
    def _():
        fetch(i + 1, (i + 1) % 2)

    slot = i % 2

    def drain(r, c):
        _row_copy(y_hbm, 0, buf.at[slot], 0, sem.at[slot]).wait()
        return c

    lax.fori_loop(0, tm, drain, 0, unroll=8)
    x = x_ref[...] + g2_ref[0] * _load_row_tiles(buf.at[slot], tm)
    if final:
        x = x * lax.rsqrt(jnp.mean(x * x, axis=-1, keepdims=True) + EPS) * fn_ref[...]
    o_ref[...] = x


def gather_residual(g, dest, y_tiles, x, gate2, fnorm, final, tm):
    t, d = x.shape
    nsteps = t // tm
    kern = functools.partial(_gather_residual_kernel, tm=tm, nsteps=nsteps, final=final)
    return pl.pallas_call(
        kern,
        grid_spec=pltpu.PrefetchScalarGridSpec(
            num_scalar_prefetch=1,
            grid=(nsteps,),
            in_specs=[pl.BlockSpec(memory_space=pl.ANY),
                      pl.BlockSpec((tm, d), lambda i, dst: (i, 0)),
                      pl.BlockSpec((1, 1, d), lambda i, dst: (g.batch_of_block(i, tm), 0, 0)),
                      pl.BlockSpec((1, d), lambda i, dst: (0, 0))],
            out_specs=pl.BlockSpec((tm, d), lambda i, dst: (i, 0)),
            scratch_shapes=[pltpu.VMEM((2, tm * ROW_TILE, LANES), F32), pltpu.SemaphoreType.DMA((2,))]),
        out_shape=jax.ShapeDtypeStruct((t, d), F32),
        compiler_params=_cparams(("arbitrary",)),
        name="moe_gather_residual",
    )(dest, y_tiles, x, gate2, fnorm)


def _rope_tables(smax, rot_dim, period, lane_off):
    half = rot_dim // 2
    inv_freq = ROPE_THETA ** (-jnp.arange(half, dtype=F32) / half)
    ang = jnp.arange(smax).astype(F32)[:, None] * inv_freq[None, :]
    cos, sin = jnp.cos(ang), jnp.sin(ang)
    gl = (jnp.arange(LANES) % period) - lane_off
    first = (gl >= 0) & (gl < half)
    second = (gl >= half) & (gl < rot_dim)
    j = jnp.clip(jnp.where(first, gl, gl - half), 0, half - 1)
    cl, sn = cos[:, j], sin[:, j]
    ct = jnp.where((first | second)[None, :], cl, 1.0)
    s1 = jnp.where(first[None, :], -sn, 0.0)
    s2 = jnp.where(second[None, :], sn, 0.0)
    return ct, s1, s2


def _pad_w_in(w):
    d = w.shape[0]
    parts, off = [], 0
    for sz in IN_SIZES:
        parts.append(w[:, off:off + sz])
        off += sz
    a_q, a_k, a_v, b_q, b_k, b_v, b_r, b_zf, b_zb, c_cq, c_ckv, c_kr, gates = parts
    z = lambda n: jnp.zeros((d, n), w.dtype)
    bz = jnp.concatenate([b_zf, b_zb, z(LANES - 2 * B_GATE_RANK)], axis=1)
    kr = jnp.concatenate([z(KR_LANE), c_kr, z(LANES - KR_LANE - C_ROPE_DIM)], axis=1)
    used = OFF_KR + LANES
    out = jnp.concatenate([gates, b_v, b_r, a_q, a_k, a_v, b_q, b_k, c_ckv, bz, c_cq, kr, z(NPAD - used)], axis=1)
    assert out.shape[1] == NPAD
    return out.astype(BF16)


def _pad_heads_cols(w, real, take_lo, take_hi):
    kdim = w.shape[0]
    wh = w.reshape(kdim, C_HEADS, real)[:, :, take_lo:take_hi]
    wh = jnp.pad(wh, ((0, 0), (0, 0), (0, C_HEAD_PAD - (take_hi - take_lo))))
    return wh.reshape(kdim, C_PAD_WIDTH).astype(BF16)


def _gate_weights(w_a, b_a, row_off):
    wa = jnp.zeros((LANES, B_KW), F32).at[row_off:row_off + B_GATE_RANK].set(w_a)
    return wa.astype(BF16), b_a.reshape(1, B_KW).astype(F32)


def kernel(x_prompt, x_sample, c_prompt, c_sample, norm_mix, norm_moe, w_mod, b_mod, w_in, w_gla_af, b_gla_af,
           w_gla_ab, b_gla_ab, gla_norm, mla_q_norm, w_mla_uq, mla_kv_norm, w_mla_ukv, w_proj_a, w_proj_b, w_proj_c,
           w_out, w_router, b_router, w_exp_gate, w_exp_up, w_exp_down, final_norm):
    b1, s1, d = x_prompt.shape
    b2, s2, _ = x_sample.shape
    g = Groups(b1, s1, b2, s2)
    depth = w_in.shape[0]
    tm_in = min(1024, s1, s2)
    tm_prep = min(1024, s1, s2)
    tm_proj = min(512, s1, s2)
    tm_moe = min(1024, s1, s2)

    x = jnp.concatenate([x_prompt.reshape(g.t1, d), x_sample.reshape(g.t2, d)], axis=0)
    c = jnp.concatenate([c_prompt, c_sample], axis=0)
    nbp = -(-g.nb // 8) * 8
    c_pad = jnp.pad(c, ((0, nbp - g.nb), (0, 0)))
    mod = modulation(c_pad, w_mod, b_mod)[:, :g.nb]

    smax = max(s1, s2)
    tabs_a = _rope_tables(smax, A_ROPE_DIM, A_HEAD_DIM, 0)
    tabs_c = _rope_tables(smax, C_ROPE_DIM, C_HEAD_PAD, KR_LANE)

    wr_hi = jnp.pad(w_router, ((0, 0), (0, LANES - N_EXPERTS)))
    wrh = wr_hi.astype(BF16)
    wrm = (wr_hi - wrh.astype(F32)).astype(BF16)
    fnorm = final_norm.reshape(1, d)
    br = jnp.pad(b_router.astype(F32), (0, LANES - N_EXPERTS)).reshape(1, LANES)

    for l in range(depth):
        sh1, sc1, gt1, sh2, sc2, gt2 = [m.reshape(g.nb, 1, d) for m in jnp.split(mod[l], N_MOD, axis=-1)]
        mul1 = norm_mix[l][None, None, :] * (1.0 + sc1)
        mul2 = norm_moe[l][None, None, :] * (1.0 + sc2)

        y, yd4, yd16 = in_projection(g, x, mul1, sh1, _pad_w_in(w_in[l]), tabs_a, tm_in)

        nat_cols = tuple((0, off // A_WIDTH) for off in (OFF_AQ, OFF_AK, OFF_AV))
        dil_cols = ((0, 0), (1, 0), (2, 0))
        o_list, lse_list = [], []
        for (_, dil), src in zip(A_PATTERNS, (y.reshape(1, g.t, NPAD), yd4, yd16)):
            o, lse = band_attention(g, src, dil, nat_cols if dil == 1 else dil_cols, NPAD // A_WIDTH if dil == 1 else 1)
            o_list.append(o)
            lse_list.append(lse)

        waf, baf = _gate_weights(w_gla_af[l], b_gla_af[l], 0)
        wab, bab = _gate_weights(w_gla_ab[l], b_gla_ab[l], B_GATE_RANK)
        o_back = gla_pass(g, y, wab, bab, reverse=True)
        yb = gla_pass(g, y, waf, baf, reverse=False, o_back=o_back, gain=gla_norm[l].reshape(1, B_VAL_DIM))

        wq = jnp.concatenate(
            [w_mla_uq[l].reshape(C_Q_RANK, C_HEADS, C_QK_DIM),
             jnp.zeros((C_Q_RANK, C_HEADS, C_HEAD_PAD - C_QK_DIM), F32)], axis=2
        ).reshape(C_Q_RANK, C_PAD_WIDTH).astype(BF16)
        wk = _pad_heads_cols(w_mla_ukv[l], C_NOPE_DIM + C_V_DIM, 0, C_NOPE_DIM)
        wv = _pad_heads_cols(w_mla_ukv[l], C_NOPE_DIM + C_V_DIM, C_NOPE_DIM, C_NOPE_DIM + C_V_DIM)
        qm, km, vm = mla_prep(g, y, mla_q_norm[l].reshape(1, C_Q_RANK), mla_kv_norm[l].reshape(1, C_KV_RANK),
                              wq, wk, wv, tabs_c, tm_prep)
        yc1 = mla_flash(qm, km, vm, 0, b1, s1, min(1024, s1), min(512, s1))
        yc2 = mla_flash(qm, km, vm, g.t1, b2, s2, min(512, s2), min(512, s2))

        wpc = jnp.pad(w_proj_c[l].reshape(C_HEADS, C_V_DIM, d), ((0, 0), (0, C_HEAD_PAD - C_V_DIM), (0, 0)))
        wpc = wpc.reshape(C_PAD_WIDTH, d).astype(BF16)
        x, h2, comb = proj_merge(g, o_list, lse_list, yb, yc1, yc2, y, x, gt1, mul2, sh2,
                                 w_proj_a[l].astype(BF16), w_proj_b[l].astype(BF16), wpc, w_out[l].astype(BF16),
                                 wrh, wrm, br, tm_proj)

        dest, tile_group, ntiles = _sorted_layout(comb, tm_moe)
        hs = scatter_rows(dest, h2, ntiles * tm_moe, tm_moe)
        ys = moe_grouped(tile_group, ntiles, hs, wrh, wrm, br, w_exp_gate[l].astype(BF16),
                         w_exp_up[l].astype(BF16), w_exp_down[l].astype(BF16), tm_moe)
        x = gather_residual(g, dest, ys, x, gt2, fnorm, l == depth - 1, tm_moe)

    return x[:g.t1].reshape(b1, s1, d), x[g.t1:].reshape(b2, s2, d)
```

```python
import functools

import jax
import jax.numpy as jnp
from jax import lax
from jax.experimental import pallas as pl
from jax.experimental.pallas import tpu as pltpu

F32 = jnp.float32
BF16 = jnp.bfloat16

D_MODEL = 1024
DEPTH = 2
EPS = 1e-6
ROPE_THETA = 500000.0
NEG_INF = -1e30

A_HEADS = 8
A_HEAD_DIM = 64
A_ROPE_DIM = A_HEAD_DIM // 4
A_PATTERNS = ((128, 1), (512, 4), (2048, 16))
A_WIDTH = A_HEADS * A_HEAD_DIM
A_HALF = 64

B_HEADS = 4
B_KEY_DIM = 128
B_VAL_DIM = 256
B_GATE_RANK = 16
B_GATE_TAU = 16.0
B_CHUNK = 64
B_KW = B_HEADS * B_KEY_DIM
B_VW = B_HEADS * B_VAL_DIM

C_HEADS = 8
C_NOPE_DIM = 64
C_ROPE_DIM = 32
C_V_DIM = 64
C_QK_DIM = C_NOPE_DIM + C_ROPE_DIM
C_Q_RANK = 384
C_KV_RANK = 256
C_HEAD_PAD = 128
C_PAD_WIDTH = C_HEADS * C_HEAD_PAD

N_EXPERTS = 16
N_GROUPS = 4
EXPERTS_PER_GROUP = N_EXPERTS // N_GROUPS
TOP_K = 2
GROUP_SCORE_K = 2
D_EXPERT = 512
N_MOD = 6

IN_SIZES = (A_WIDTH, A_WIDTH, A_WIDTH, B_KW, B_KW, B_VW, B_VW, B_GATE_RANK, B_GATE_RANK,
            C_Q_RANK, C_KV_RANK, C_ROPE_DIM, 3 * D_MODEL)

LANES = 128
ROW_TILE = D_MODEL // LANES
MOE_EXPERTS_PER_STEP = 2
LOG2E = 1.4426950408889634

OFF_GATES = 0
OFF_BV = 3072
OFF_BR = 4096
OFF_AQ = 5120
OFF_AK = 5632
OFF_AV = 6144
OFF_BQ = 6656
OFF_BK = 7168
OFF_CKV = 7680
OFF_BZ = 7936
OFF_CQ = 8064
OFF_KR = 8448
NPAD = 9216
KR_LANE = 64

VMEM_LIMIT = 56 * 1024 * 1024


def _cparams(sem):
    return pltpu.CompilerParams(dimension_semantics=sem, vmem_limit_bytes=VMEM_LIMIT)


class Groups:
    def __init__(self, b1, s1, b2, s2):
        self.b1, self.s1, self.b2, self.s2 = b1, s1, b2, s2
        self.t1, self.t2 = b1 * s1, b2 * s2
        self.t = self.t1 + self.t2
        self.nb = b1 + b2

    def batch_of_block(self, i, tm):
        n1 = self.t1 // tm
        return jnp.where(i < n1, i // (self.s1 // tm), self.b1 + (i - n1) // (self.s2 // tm))

    def pos_block(self, i, tm):
        n1 = self.t1 // tm
        return jnp.where(i < n1, i % (self.s1 // tm), (i - n1) % (self.s2 // tm))


def _dot(a, b):
    return jnp.dot(a, b, preferred_element_type=F32)


def _dot_nt(a, b):
    return lax.dot_general(a, b, (((1,), (1,)), ((), ())), preferred_element_type=F32)


def _dot_tn(a, b):
    return lax.dot_general(a, b, (((0,), (0,)), ((), ())), preferred_element_type=F32)


def _split3(a):
    hi = a.astype(BF16)
    r1 = a - hi.astype(F32)
    mid = r1.astype(BF16)
    lo = (r1 - mid.astype(F32)).astype(BF16)
    return hi, mid, lo


def _rope(y, ct, s1, s2, half):
    return y * ct + pltpu.roll(y, LANES - half, 1) * s1 + pltpu.roll(y, half, 1) * s2


def _sigmoid(x):
    return 0.5 * jnp.tanh(0.5 * x) + 0.5


def _mod_kernel(c_ref, w_ref, b_ref, o_ref):
    c = c_ref[...]
    ca = c * _sigmoid(c)
    ch, cm, _ = _split3(ca)
    wh, wm, _ = _split3(w_ref[0])
    o_ref[0] = _dot(ch, wh) + _dot(cm, wh) + _dot(ch, wm) + b_ref[0]


def modulation(c_pad, w_mod, b_mod):
    nbp, d = c_pad.shape
    depth, _, n = w_mod.shape
    tn = 512
    return pl.pallas_call(
        _mod_kernel,
        grid=(depth, n // tn),
        in_specs=[pl.BlockSpec((nbp, d), lambda l, j: (0, 0)),
                  pl.BlockSpec((1, d, tn), lambda l, j: (l, 0, j)),
                  pl.BlockSpec((1, 1, tn), lambda l, j: (l, 0, j))],
        out_specs=pl.BlockSpec((1, nbp, tn), lambda l, j: (l, 0, j)),
        out_shape=jax.ShapeDtypeStruct((depth, nbp, n), F32),
        compiler_params=_cparams(("arbitrary", "arbitrary")),
        name="modulation",
    )(c_pad, w_mod, b_mod.reshape(depth, 1, n))


def _inproj_kernel(x_ref, mul_ref, sh_ref, w_ref, ct_ref, s1_ref, s2_ref, o_ref, od4_ref, od16_ref,
                   h_scr, y_scr, y4_scr, *, tm, j_qk, j_v):
    j = pl.program_id(1)
    aw = A_WIDTH
    ncol = aw // LANES

    @pl.when(j == 0)
    def _():
        x = x_ref[...]
        ms = jnp.mean(x * x, axis=-1, keepdims=True)
        h = x * lax.rsqrt(ms + EPS) * mul_ref[0] + sh_ref[0]
        h_scr[...] = h.astype(BF16)

    def attention_block(kind, half, rope):
        y = _dot(h_scr[...], w_ref[j, :, half * aw:(half + 1) * aw])
        if rope:
            ct, s1, s2 = ct_ref[...], s1_ref[...], s2_ref[...]
        for c in range(ncol):
            yc = y[:, c * LANES:(c + 1) * LANES]
            y_scr[c] = _rope(yc, ct, s1, s2, A_ROPE_DIM // 2) if rope else yc
        n4, n16 = tm // 4, tm // 16
        for c in range(ncol):
            o_ref[:, half * aw + c * LANES:half * aw + (c + 1) * LANES] = y_scr[c].astype(BF16)
            for r in range(4):
                y4 = y_scr[c, pl.ds(r, n4, stride=4), :]
                y4_scr[c, r * n4:(r + 1) * n4, :] = y4
                od4_ref[kind, :, r * aw + c * LANES:r * aw + (c + 1) * LANES] = y4.astype(BF16)
            for r in range(4):
                for r2 in range(4):
                    col = (r + 4 * r2) * aw + c * LANES
                    od16_ref[kind, :, col:col + LANES] = y4_scr[c, pl.ds(r * n4 + r2, n16, stride=4), :].astype(BF16)

    @pl.when((j != j_qk) & (j != j_v))
    def _():
        o_ref[...] = _dot(h_scr[...], w_ref[j]).astype(BF16)

    @pl.when(j == j_qk)
    def _():
        attention_block(0, 0, True)
        attention_block(1, 1, True)

    @pl.when(j == j_v)
    def _():
        attention_block(2, 0, False)
        o_ref[:, aw:] = _dot(h_scr[...], w_ref[j, :, aw:]).astype(BF16)


def in_projection(g, x, mul, shift, w_pad, tabs, tm):
    t, d = x.shape
    tn = 2 * A_WIDTH
    aw = A_WIDTH
    ct, s1, s2 = tabs
    assert OFF_AQ % tn == 0 and OFF_AK == OFF_AQ + aw and OFF_AV == OFF_AQ + tn and NPAD % tn == 0
    bmap = lambda i, j: (g.batch_of_block(i, tm), 0, 0)
    pmap = lambda i, j: (g.pos_block(i, tm), 0)
    kern = functools.partial(_inproj_kernel, tm=tm, j_qk=OFF_AQ // tn, j_v=OFF_AV // tn)
    return pl.pallas_call(
        kern,
        grid=(t // tm, NPAD // tn),
        in_specs=[pl.BlockSpec((tm, d), lambda i, j: (i, 0)),
                  pl.BlockSpec((1, 1, d), bmap),
                  pl.BlockSpec((1, 1, d), bmap),
                  pl.BlockSpec((NPAD // tn, d, tn), lambda i, j: (0, 0, 0), pipeline_mode=pl.Buffered(1)),
                  pl.BlockSpec((tm, LANES), pmap),
                  pl.BlockSpec((tm, LANES), pmap),
                  pl.BlockSpec((tm, LANES), pmap)],
        out_specs=[pl.BlockSpec((tm, tn), lambda i, j: (i, j)),
                   pl.BlockSpec((3, tm // 4, 4 * aw), lambda i, j: (0, i, 0)),
                   pl.BlockSpec((3, tm // 16, 16 * aw), lambda i, j: (0, i, 0))],
        out_shape=[jax.ShapeDtypeStruct((t, NPAD), BF16),
                   jax.ShapeDtypeStruct((3, t // 4, 4 * aw), BF16),
                   jax.ShapeDtypeStruct((3, t // 16, 16 * aw), BF16)],
        scratch_shapes=[pltpu.VMEM((tm, d), BF16), pltpu.VMEM((aw // LANES, tm, LANES), F32),
                        pltpu.VMEM((aw // LANES, tm, LANES), F32)],
        compiler_params=_cparams(("arbitrary", "arbitrary")),
        name="in_projection",
    )(x, mul, shift, w_pad.reshape(d, NPAD // tn, tn).transpose(1, 0, 2), ct, s1, s2)


def _band_kernel(q_ref, kp_ref, km_ref, kn_ref, vp_ref, vm_ref, vn_ref, o_ref, lse_ref, *, tb, tq, rows1, l1, l2):
    i = pl.program_id(1)
    tk = tq + 2 * A_HALF
    k = jnp.concatenate([kp_ref[...], km_ref[...], kn_ref[...]], axis=0)
    v = jnp.concatenate([vp_ref[...], vm_ref[...], vn_ref[...]], axis=0)
    qi = lax.broadcasted_iota(jnp.int32, (tq, tk), 0)
    kj = lax.broadcasted_iota(jnp.int32, (tq, tk), 1)
    band = jnp.abs(kj - A_HALF - qi) <= A_HALF
    lo = lax.broadcasted_iota(jnp.int32, (tq, LANES), 1) < A_HEAD_DIM
    scale = A_HEAD_DIM ** -0.5
    for u in range(tb // tq):
        row0 = i * tb + u * tq
        in1 = row0 < rows1
        seq_len = jnp.where(in1, l1, l2)
        pos0 = jnp.where(in1, row0 % l1, (row0 - rows1) % l2)
        kpos = pos0 - A_HALF + kj
        valid = band & (kpos >= 0) & (kpos < seq_len)
        valid2 = jnp.concatenate([valid, valid], axis=0)
        rows = slice(u * tq, (u + 1) * tq)
        krows = slice(u * tq, u * tq + tk)
        for p in range(A_WIDTH // LANES):
            sl = slice(p * LANES, (p + 1) * LANES)
            qp, kp, vp = q_ref[rows, sl] * scale, k[krows, sl], v[krows, sl]
            zero = jnp.zeros_like(qp)
            qm = jnp.concatenate([jnp.where(lo, qp, zero), jnp.where(lo, zero, qp)], axis=0)
            s = jnp.where(valid2, _dot_nt(qm, kp), NEG_INF)
            m = jnp.max(s, axis=1, keepdims=True)
            e = jnp.exp(s - m)
            l = jnp.sum(e, axis=1, keepdims=True)
            o2 = _dot(e.astype(BF16), vp) / l
            lse2 = m + jnp.log(l)
            o_ref[rows, sl] = jnp.where(lo, o2[:tq], o2[tq:]).astype(BF16)
            lse_ref[rows, sl] = jnp.where(lo, lse2[:tq], lse2[tq:])


def band_attention(g, qkv, dil, cols, cpb):
    rows = g.t // dil
    tq = 128
    tb = 512
    sub = tb // A_HALF
    nsub = rows // A_HALF

    def main(c):
        ld, col = cols[c]
        return pl.BlockSpec((None, tb, A_WIDTH), lambda r, i: (ld, i, r * cpb + col))

    def prev(c):
        ld, col = cols[c]
        return pl.BlockSpec((None, A_HALF, A_WIDTH), lambda r, i: (ld, jnp.maximum(i * sub - 1, 0), r * cpb + col))

    def nxt(c):
        ld, col = cols[c]
        return pl.BlockSpec((None, A_HALF, A_WIDTH),
                            lambda r, i: (ld, jnp.minimum((i + 1) * sub, nsub - 1), r * cpb + col))

    assert (g.s1 // dil) % tq == 0 and (g.s2 // dil) % tq == 0 and rows % tb == 0
    kern = functools.partial(_band_kernel, tb=tb, tq=tq, rows1=g.t1 // dil, l1=g.s1 // dil, l2=g.s2 // dil)
    return pl.pallas_call(
        kern,
        grid=(dil, rows // tb),
        in_specs=[main(0), prev(1), main(1), nxt(1), prev(2), main(2), nxt(2)],
        out_specs=[pl.BlockSpec((tb, A_WIDTH), lambda r, i: (i, r)),
                   pl.BlockSpec((tb, A_WIDTH), lambda r, i: (i, r))],
        out_shape=[jax.ShapeDtypeStruct((rows, dil * A_WIDTH), BF16),
                   jax.ShapeDtypeStruct((rows, dil * A_WIDTH), F32)],
        compiler_params=_cparams(("arbitrary", "arbitrary")),
        name="band_attention_d%d" % dil,
    )(qkv, qkv, qkv, qkv, qkv, qkv, qkv)


def _gla_kernel(*refs, reverse, final, tc, nblk, t1, s1, s2):
    if final:
        q_ref, k_ref, v_ref, z_ref, wa_ref, ba_ref, tri_ref, ob_ref, r_ref, gain_ref, o_ref, st_scr = refs
    else:
        q_ref, k_ref, v_ref, z_ref, wa_ref, ba_ref, tri_ref, o_ref, st_scr = refs
    i = pl.program_id(0)
    blk = (nblk - 1 - i) if reverse else i
    row0 = blk * tc
    in1 = row0 < t1
    pos0 = jnp.where(in1, row0 % s1, (row0 - t1) % s2)
    slen = jnp.where(in1, s1, s2)
    start = (pos0 + tc == slen) if reverse else (pos0 == 0)

    @pl.when(start)
    def _():
        st_scr[...] = jnp.zeros_like(st_scr)

    zl = _dot(z_ref[...], wa_ref[...]) + ba_ref[...]
    la = (jnp.minimum(zl, 0.0) - jnp.log(1.0 + jnp.exp(-jnp.abs(zl)))) * (1.0 / B_GATE_TAU)
    hi, mid, _ = _split3(la)
    tri = tri_ref[...]
    tb = tri.shape[0]
    bc_all = jnp.concatenate(
        [_dot(tri, hi[b * tb:(b + 1) * tb]) + _dot(tri, mid[b * tb:(b + 1) * tb]) for b in range(tc // tb)], axis=0)

    qi = lax.broadcasted_iota(jnp.int32, (B_CHUNK, B_CHUNK), 0)
    si = lax.broadcasted_iota(jnp.int32, (B_CHUNK, B_CHUNK), 1)
    mask = (si > qi) if reverse else (si <= qi)
    nch = tc // B_CHUNK
    q_all = q_ref[...].astype(F32) * (B_KEY_DIM ** -0.5)
    k_all = k_ref[...].astype(F32)
    q_dec = (q_all * jnp.exp(bc_all)).astype(BF16)
    k_inv = (k_all * jnp.exp(-bc_all)).astype(BF16)
    st = [st_scr[h] for h in range(B_HEADS)]
    for c in (range(nch - 1, -1, -1) if reverse else range(nch)):
        sl = slice(c * B_CHUNK, (c + 1) * B_CHUNK)
        edge = c * B_CHUNK if reverse else (c + 1) * B_CHUNK - 1
        tot = bc_all[edge:edge + 1]
        k_end = (k_all[sl] * jnp.exp(tot - bc_all[sl])).astype(BF16)
        dec = jnp.exp(tot)
        for h in range(B_HEADS):
            ks = slice(h * B_KEY_DIM, (h + 1) * B_KEY_DIM)
            vs = slice(h * B_VAL_DIM, (h + 1) * B_VAL_DIM)
            v = v_ref[sl, vs]
            att = jnp.where(mask, _dot_nt(q_dec[sl, ks], k_inv[sl, ks]), 0.0).astype(BF16)
            o = _dot(att, v) + _dot_nt(q_dec[sl, ks], st[h].astype(BF16))
            st[h] = st[h] * dec[:, ks] + _dot_tn(v, k_end[:, ks])
            if final:
                o = o + ob_ref[sl, vs]
                on = o * lax.rsqrt(jnp.mean(o * o, axis=-1, keepdims=True) + EPS) * gain_ref[...]
                r = r_ref[sl, vs].astype(F32)
                o_ref[sl, vs] = (on * (r * _sigmoid(r))).astype(BF16)
            else:
                o_ref[sl, vs] = o
    for h in range(B_HEADS):
        st_scr[h] = st[h]


def gla_pass(g, y, wa, ba, reverse, o_back=None, gain=None):
    t = g.t
    tc = min(512, g.s1, g.s2)
    nblk = t // tc
    final = o_back is not None
    rowmap = (lambda i: nblk - 1 - i) if reverse else (lambda i: i)
    row = lambda w, off: pl.BlockSpec((tc, w), lambda i: (rowmap(i), off // w))
    full = lambda a: pl.BlockSpec(a.shape, lambda i: (0,) * a.ndim)
    tb = min(tc, 256)
    ri = jnp.arange(tb)[:, None]
    ci = jnp.arange(tb)[None, :]
    tri = ((ri // B_CHUNK == ci // B_CHUNK) & ((ci >= ri) if reverse else (ci <= ri))).astype(BF16)
    in_specs = [row(B_KW, OFF_BQ), row(B_KW, OFF_BK), row(B_VW, OFF_BV), row(LANES, OFF_BZ),
                full(wa), full(ba), full(tri)]
    args = [y, y, y, y, wa, ba, tri]
    if final:
        in_specs += [row(B_VW, 0), row(B_VW, OFF_BR), full(gain)]
        args += [o_back, y, gain]
    kern = functools.partial(_gla_kernel, reverse=reverse, final=final, tc=tc, nblk=nblk, t1=g.t1, s1=g.s1, s2=g.s2)
    return pl.pallas_call(
        kern,
        grid=(nblk,),
        in_specs=in_specs,
        out_specs=row(B_VW, 0),
        out_shape=jax.ShapeDtypeStruct((t, B_VW), BF16 if final else F32),
        scratch_shapes=[pltpu.VMEM((B_HEADS, B_VAL_DIM, B_KEY_DIM), F32)],
        compiler_params=_cparams(("arbitrary",)),
        name="gla_forward_final" if final else "gla_backward",
    )(*args)


def _mla_prep_kernel(ckv_ref, cq_ref, kr_ref, qn_ref, kvn_ref, wq_ref, wk_ref, wv_ref, ct_ref, s1_ref, s2_ref,
                     q_out, k_out, v_out):
    def norm(ref, gain_ref):
        xf = ref[...].astype(F32)
        return (xf * lax.rsqrt(jnp.mean(xf * xf, axis=-1, keepdims=True) + EPS) * gain_ref[...]).astype(BF16)

    ckv_n = norm(ckv_ref, kvn_ref)
    cq_n = norm(cq_ref, qn_ref)
    q = _dot(cq_n, wq_ref[...])
    kn = _dot(ckv_n, wk_ref[...])
    v = _dot(ckv_n, wv_ref[...])
    ct, s1, s2 = ct_ref[...], s1_ref[...], s2_ref[...]
    half = C_ROPE_DIM // 2
    kr_rot = _rope(kr_ref[...].astype(F32), ct, s1, s2, half)
    ones_lane = lax.broadcasted_iota(jnp.int32, kr_rot.shape, 1) == C_V_DIM
    scale = C_QK_DIM ** -0.5 * LOG2E
    for h in range(C_HEADS):
        sl = slice(h * C_HEAD_PAD, (h + 1) * C_HEAD_PAD)
        q_out[:, sl] = (_rope(q[:, sl], ct, s1, s2, half) * scale).astype(BF16)
        k_out[:, sl] = (kn[:, sl] + kr_rot).astype(BF16)
        v_out[:, sl] = jnp.where(ones_lane, 1.0, v[:, sl]).astype(BF16)


def mla_prep(g, y, qn, kvn, wq, wk, wv, tabs, tm):
    t = g.t
    ct, s1, s2 = tabs
    pmap = lambda i: (g.pos_block(i, tm), 0)
    full = lambda a: pl.BlockSpec(a.shape, lambda i: (0,) * a.ndim)
    out = jax.ShapeDtypeStruct((t, C_PAD_WIDTH), BF16)
    ospec = pl.BlockSpec((tm, C_PAD_WIDTH), lambda i: (i, 0))
    return pl.pallas_call(
        _mla_prep_kernel,
        grid=(t // tm,),
        in_specs=[pl.BlockSpec((tm, C_KV_RANK), lambda i: (i, OFF_CKV // C_KV_RANK)),
                  pl.BlockSpec((tm, C_Q_RANK), lambda i: (i, OFF_CQ // C_Q_RANK)),
                  pl.BlockSpec((tm, LANES), lambda i: (i, OFF_KR // LANES)),
                  full(qn), full(kvn), full(wq), full(wk), full(wv),
                  pl.BlockSpec((tm, LANES), pmap), pl.BlockSpec((tm, LANES), pmap), pl.BlockSpec((tm, LANES), pmap)],
        out_specs=[ospec, ospec, ospec],
        out_shape=[out, out, out],
        compiler_params=_cparams(("arbitrary",)),
        name="mla_prep",
    )(y, y, y, qn, kvn, wq, wk, wv, ct, s1, s2)


def _flash_kernel(q_ref, k_ref, v_ref, o_ref, s0_scr, s1_scr, p_scr, m_scr, a_scr, acc_scr, *, tk, nk, rc, unroll):
    tq = q_ref.shape[0]

    def logits(t, scr):
        off = pl.multiple_of(t * tk, tk)
        scr[...] = _dot_nt(q_ref[...], k_ref[pl.ds(off, tk), :])

    def softmax_pv(t, scr):
        off = pl.multiple_of(t * tk, tk)
        for r in range(tq // rc):
            rows = slice(r * rc, (r + 1) * rc)
            s = scr[rows, :]
            m_old = m_scr[rows, :]
            m_new = jnp.maximum(m_old, jnp.max(s, axis=1, keepdims=True))
            p_scr[rows, :] = jnp.exp2(s - jnp.concatenate([m_new] * (tk // LANES), axis=1)).astype(BF16)
            a_scr[rows, :] = jnp.exp2(m_old - m_new)
            m_scr[rows, :] = m_new
        acc_scr[...] = acc_scr[...] * a_scr[...] + _dot(p_scr[...], v_ref[pl.ds(off, tk), :])

    bufs = (s0_scr, s1_scr)

    def group(t0, last):
        for u in range(unroll):
            if not (last and u == unroll - 1):
                logits(t0 + u + 1, bufs[(u + 1) % 2])
            softmax_pv(t0 + u, bufs[u % 2])

    def body(jj, carry):
        group(unroll * jj, False)
        return carry

    m_scr[...] = jnp.full(m_scr.shape, NEG_INF, F32)
    acc_scr[...] = jnp.zeros(acc_scr.shape, F32)
    logits(0, s0_scr)
    lax.fori_loop(0, nk // unroll - 1, body, 0)
    group(nk - unroll, True)
    acc = acc_scr[...]
    o_ref[...] = (acc / acc[:, C_V_DIM:C_V_DIM + 1]).astype(BF16)


def mla_flash(q, k, v, row_off, nseq, s, tq, tk):
    assert row_off % s == 0 and row_off % tq == 0
    qb0 = row_off // tq
    sb0 = row_off // s
    nq = s // tq
    nk = s // tk
    unroll = next(u for u in (16, 8, 4, 2) if nk % u == 0)
    assert nk % unroll == 0
    kern = functools.partial(_flash_kernel, tk=tk, nk=nk, rc=64, unroll=unroll)
    return pl.pallas_call(
        kern,
        grid=(nseq, C_HEADS, nq),
        scratch_shapes=[pltpu.VMEM((tq, tk), F32), pltpu.VMEM((tq, tk), F32), pltpu.VMEM((tq, tk), BF16),
                        pltpu.VMEM((tq, LANES), F32), pltpu.VMEM((tq, LANES), F32),
                        pltpu.VMEM((tq, C_HEAD_PAD), F32)],
        in_specs=[pl.BlockSpec((tq, C_HEAD_PAD), lambda b, h, i: (qb0 + b * nq + i, h)),
                  pl.BlockSpec((s, C_HEAD_PAD), lambda b, h, i: (sb0 + b, h)),
                  pl.BlockSpec((s, C_HEAD_PAD), lambda b, h, i: (sb0 + b, h))],
        out_specs=pl.BlockSpec((tq, C_HEAD_PAD), lambda b, h, i: (b * nq + i, h)),
        out_shape=jax.ShapeDtypeStruct((nseq * s, C_PAD_WIDTH), BF16),
        compiler_params=_cparams(("arbitrary", "arbitrary", "arbitrary")),
        name="mla_flash_s%d" % s,
    )(q, k, v)


def _store_row_tiles(ref, x, rows):
    for c in range(ROW_TILE):
        ref[pl.ds(c, rows, stride=ROW_TILE), :] = x[:, c * LANES:(c + 1) * LANES]


def _load_row_tiles(ref, rows):
    return jnp.concatenate([ref[pl.ds(c, rows, stride=ROW_TILE), :] for c in range(ROW_TILE)], axis=1)


def _router_scores(h, wrh_ref, wrm_ref):
    hh, hm, _ = _split3(h)
    return _sigmoid(_dot(hh, wrh_ref[...]) + _dot(hm, wrh_ref[...]) + _dot(hh, wrm_ref[...]))


def _rank_in_group(sel, pos):
    rank = jnp.zeros(sel.shape, jnp.int32)
    for k in range(1, EXPERTS_PER_GROUP):
        below = pltpu.roll(sel, k, 1)
        above = pltpu.roll(sel, LANES - k, 1)
        rank += jnp.where((pos >= k) & (below >= sel), 1, 0)
        rank += jnp.where((pos + k < EXPERTS_PER_GROUP) & (above > sel), 1, 0)
    return rank


def _route_in_group(s, bias, g):
    lane = lax.broadcasted_iota(jnp.int32, s.shape, 1)
    rank = _rank_in_group(s + bias, lane % EXPERTS_PER_GROUP)
    chosen = (lane // EXPERTS_PER_GROUP == g) & (rank < TOP_K)
    total = jnp.sum(jnp.where(chosen, s, 0.0), axis=1, keepdims=True)
    return jnp.where(chosen, s / total, 0.0)


def _route(s, bias):
    lane = lax.broadcasted_iota(jnp.int32, s.shape, 1)
    pos = lane % EXPERTS_PER_GROUP
    grp = lane // EXPERTS_PER_GROUP
    sel = s + bias
    rank = _rank_in_group(sel, pos)
    top = rank < GROUP_SCORE_K
    contrib = jnp.where(top, sel, 0.0)
    score = contrib
    for k in range(1, EXPERTS_PER_GROUP):
        score += jnp.where(pos >= k, pltpu.roll(contrib, k, 1), 0.0)
        score += jnp.where(pos + k < EXPERTS_PER_GROUP, pltpu.roll(contrib, LANES - k, 1), 0.0)
    best = lane < N_EXPERTS
    for k in range(1, N_GROUPS):
        earlier = pltpu.roll(score, k * EXPERTS_PER_GROUP, 1)
        later = pltpu.roll(score, LANES - k * EXPERTS_PER_GROUP, 1)
        best &= jnp.logical_not((grp >= k) & (earlier >= score))
        best &= jnp.logical_not((grp + k < N_GROUPS) & (later > score))
    chosen = best & (rank < TOP_K)
    total = jnp.sum(jnp.where(chosen, s, 0.0), axis=1, keepdims=True)
    return jnp.where(chosen, s / total, 0.0)


def _proj_kernel(o1_ref, o2_ref, o3_ref, l1_ref, l2_ref, l3_ref, yb_ref, yc1_ref, yc2_ref, ga_ref, gb_ref, gc_ref,
                 x_ref, g1_ref, mul2_ref, sh2_ref, wpa_ref, wpb_ref, wpc_ref, wout_ref, wrh_ref, wrm_ref, br_ref,
                 x_out, h_out, comb_out, o2_scr, l2_scr, o3_scr, l3_scr, *, tm, n1):
    in_group1 = pl.program_id(0) < n1
    ncol = A_WIDTH // LANES
    hm = tm // 2
    for half in range(2):
        rows = slice(half * hm, (half + 1) * hm)
        for dil, o_ref, l_ref, o_scr, l_scr in ((A_PATTERNS[1][1], o2_ref, l2_ref, o2_scr, l2_scr),
                                               (A_PATTERNS[2][1], o3_ref, l3_ref, o3_scr, l3_scr)):
            drows = slice(half * hm // dil, (half + 1) * hm // dil)
            for r in range(dil):
                for c in range(ncol):
                    sl = slice(r * A_WIDTH + c * LANES, r * A_WIDTH + (c + 1) * LANES)
                    o_scr[c, pl.ds(half * hm + r, hm // dil, stride=dil), :] = o_ref[drows, sl].astype(F32)
                    l_scr[c, pl.ds(half * hm + r, hm // dil, stride=dil), :] = l_ref[drows, sl]
        slabs = lambda scr: jnp.concatenate([scr[c, rows, :] for c in range(ncol)], axis=1)
        l1, l2, l3 = l1_ref[rows, :], slabs(l2_scr), slabs(l3_scr)
        m = jnp.maximum(jnp.maximum(l1, l2), l3)
        e1, e2, e3 = jnp.exp(l1 - m), jnp.exp(l2 - m), jnp.exp(l3 - m)
        ya = (e1 * o1_ref[rows, :].astype(F32) + e2 * slabs(o2_scr) + e3 * slabs(o3_scr)) / (e1 + e2 + e3)
        sig = lambda ref: _sigmoid(ref[rows, :].astype(F32))
        yc = jnp.where(in_group1, yc1_ref[rows, :], yc2_ref[rows, :])
        merged = (sig(ga_ref) * _dot(ya.astype(BF16), wpa_ref[...])
                  + sig(gb_ref) * _dot(yb_ref[rows, :], wpb_ref[...])
                  + sig(gc_ref) * _dot(yc, wpc_ref[...]))
        out = _dot(merged.astype(BF16), wout_ref[...])
        x = x_ref[rows, :] + g1_ref[0] * out
        x_out[rows, :] = x
        h = x * lax.rsqrt(jnp.mean(x * x, axis=-1, keepdims=True) + EPS) * mul2_ref[0] + sh2_ref[0]
        _store_row_tiles(h_out.at[pl.ds(half * hm * ROW_TILE, hm * ROW_TILE), :], h, hm)
        comb_out[rows, :] = _route(_router_scores(h, wrh_ref, wrm_ref), br_ref[...])


def proj_merge(g, o_list, lse_list, yb, yc1, yc2, y, x, gate1, mul2, sh2, wpa, wpb, wpc, wout, wrh, wrm, br, tm):
    t, d = x.shape
    n1 = g.t1 // tm
    n2 = g.t2 // tm
    bmap = lambda i: (g.batch_of_block(i, tm), 0, 0)
    row = lambda w, col=0: pl.BlockSpec((tm, w), lambda i: (i, col))
    dil_row = lambda dil: pl.BlockSpec((tm // dil, dil * A_WIDTH), lambda i: (i, 0))
    full = lambda a: pl.BlockSpec(a.shape, lambda i: (0,) * a.ndim)
    gcol = OFF_GATES // d
    d2, d3 = A_PATTERNS[1][1], A_PATTERNS[2][1]
    attn_specs = [row(A_WIDTH), dil_row(d2), dil_row(d3)]
    return pl.pallas_call(
        functools.partial(_proj_kernel, tm=tm, n1=n1),
        grid=(t // tm,),
        in_specs=attn_specs + attn_specs + [
            row(B_VW),
            pl.BlockSpec((tm, C_PAD_WIDTH), lambda i: (jnp.minimum(i, n1 - 1), 0)),
            pl.BlockSpec((tm, C_PAD_WIDTH), lambda i: (jnp.clip(i - n1, 0, n2 - 1), 0)),
            row(d, gcol), row(d, gcol + 1), row(d, gcol + 2),
            row(d), pl.BlockSpec((1, 1, d), bmap), pl.BlockSpec((1, 1, d), bmap), pl.BlockSpec((1, 1, d), bmap),
            full(wpa), full(wpb), full(wpc), full(wout), full(wrh), full(wrm), full(br)],
        out_specs=[row(d), pl.BlockSpec((tm * ROW_TILE, LANES), lambda i: (i, 0)), row(LANES)],
        out_shape=[jax.ShapeDtypeStruct((t, d), F32), jax.ShapeDtypeStruct((t * ROW_TILE, LANES), F32),
                   jax.ShapeDtypeStruct((t, LANES), F32)],
        scratch_shapes=[pltpu.VMEM((A_WIDTH // LANES, tm, LANES), F32)] * 4,
        compiler_params=_cparams(("arbitrary",)),
        name="proj_merge_route",
    )(*o_list, *lse_list, yb, yc1, yc2, y, y, y, x, gate1, mul2, sh2, wpa, wpb, wpc, wout, wrh, wrm, br)


def _sorted_layout(comb, tmoe):
    t = comb.shape[0]
    gw = comb[:, :N_EXPERTS].reshape(t, N_GROUPS, EXPERTS_PER_GROUP).sum(-1)
    gid = jnp.argmax(gw, axis=-1).astype(jnp.int32)
    onehot = (gid[:, None] == jnp.arange(N_GROUPS, dtype=jnp.int32)[None, :]).astype(jnp.int32)
    incl = jnp.cumsum(onehot, axis=0)
    counts = incl[-1]
    padded = (counts + tmoe - 1) // tmoe * tmoe
    ends = jnp.cumsum(padded)
    dest = jnp.sum(onehot * (incl - onehot + (ends - padded)[None, :]), axis=1).astype(jnp.int32)
    ntiles = t // tmoe + N_GROUPS
    tile_group = jnp.sum((jnp.arange(ntiles, dtype=jnp.int32) * tmoe)[:, None] >= ends[None, :], axis=1)
    tile_group = jnp.where(tile_group < N_GROUPS, tile_group, -1).astype(jnp.int32)
    return dest, tile_group, ntiles


def _row_copy(src, src_row, dst, dst_row, sem):
    return pltpu.make_async_copy(src.at[pl.ds(src_row * ROW_TILE, ROW_TILE), :],
                                 dst.at[pl.ds(dst_row * ROW_TILE, ROW_TILE), :], sem)


def _scatter_rows_kernel(dest_ref, h_ref, init_hbm, hs_hbm, sem, *, tm):
    del init_hbm
    i = pl.program_id(0)

    def issue(r, c):
        _row_copy(h_ref, r, hs_hbm, dest_ref[i * tm + r], sem).start()
        return c

    def drain(r, c):
        _row_copy(h_ref, 0, hs_hbm, 0, sem).wait()
        return c

    lax.fori_loop(0, tm, issue, 0, unroll=8)
    lax.fori_loop(0, tm, drain, 0, unroll=8)


def scatter_rows(dest, h_tiles, rows_out, tm):
    t = dest.shape[0]
    init = jnp.zeros((rows_out * ROW_TILE, LANES), F32)
    return pl.pallas_call(
        functools.partial(_scatter_rows_kernel, tm=tm),
        grid_spec=pltpu.PrefetchScalarGridSpec(
            num_scalar_prefetch=1,
            grid=(t // tm,),
            in_specs=[pl.BlockSpec((tm * ROW_TILE, LANES), lambda i, dst: (i, 0)),
                      pl.BlockSpec(memory_space=pl.ANY)],
            out_specs=pl.BlockSpec(memory_space=pl.ANY),
            scratch_shapes=[pltpu.SemaphoreType.DMA(())]),
        out_shape=jax.ShapeDtypeStruct(init.shape, F32),
        input_output_aliases={2: 0},
        compiler_params=_cparams(("arbitrary",)),
        name="moe_scatter_rows",
    )(dest, h_tiles, init)


def _moe_group_kernel(tg_ref, hs_ref, wrh_ref, wrm_ref, br_ref, wg_ref, wu_ref, wd_ref, y_ref,
                      h_scr, comb_scr, acc_scr, *, tm):
    j = pl.program_id(0)
    k = pl.program_id(1)
    g = tg_ref[j]

    @pl.when(k == 0)
    def _():
        h = _load_row_tiles(hs_ref, tm)
        h_scr[...] = h.astype(BF16)
        comb_scr[...] = _route_in_group(_router_scores(h, wrh_ref, wrm_ref), br_ref[...], g)
        acc_scr[...] = jnp.zeros_like(acc_scr)

    @pl.when(g >= 0)
    def _():
        h = h_scr[...]
        comb = comb_scr[...]
        lane = lax.broadcasted_iota(jnp.int32, comb.shape, 1)
        total = acc_scr[...]
        for kk in range(MOE_EXPERTS_PER_STEP):
            gate = _dot(h, wg_ref[kk])
            a = gate * _sigmoid(gate) * _dot(h, wu_ref[kk])
            yv = _dot(a.astype(BF16), wd_ref[kk])
            e = g * EXPERTS_PER_GROUP + k * MOE_EXPERTS_PER_STEP + kk
            total = total + jnp.sum(jnp.where(lane == e, comb, 0.0), axis=1, keepdims=True) * yv
        acc_scr[...] = total

    @pl.when(k == EXPERTS_PER_GROUP // MOE_EXPERTS_PER_STEP - 1)
    def _():
        _store_row_tiles(y_ref, acc_scr[...], tm)


def moe_grouped(tile_group, ntiles, hs, wrh, wrm, br, wg, wu, wd, tm):
    d = D_MODEL
    full = lambda a: pl.BlockSpec(a.shape, lambda j, k, tg: (0,) * a.ndim)
    ksteps = EXPERTS_PER_GROUP // MOE_EXPERTS_PER_STEP
    emap = lambda j, k, tg: (jnp.maximum(tg[j], 0) * ksteps + k, 0, 0)
    return pl.pallas_call(
        functools.partial(_moe_group_kernel, tm=tm),
        grid_spec=pltpu.PrefetchScalarGridSpec(
            num_scalar_prefetch=1,
            grid=(ntiles, ksteps),
            in_specs=[pl.BlockSpec((tm * ROW_TILE, LANES), lambda j, k, tg: (j, 0)),
                      full(wrh), full(wrm), full(br),
                      pl.BlockSpec((MOE_EXPERTS_PER_STEP, d, D_EXPERT), emap),
                      pl.BlockSpec((MOE_EXPERTS_PER_STEP, d, D_EXPERT), emap),
                      pl.BlockSpec((MOE_EXPERTS_PER_STEP, D_EXPERT, d), emap)],
            out_specs=pl.BlockSpec((tm * ROW_TILE, LANES), lambda j, k, tg: (j, 0)),
            scratch_shapes=[pltpu.VMEM((tm, d), BF16), pltpu.VMEM((tm, LANES), F32), pltpu.VMEM((tm, d), F32)]),
        out_shape=jax.ShapeDtypeStruct(hs.shape, F32),
        compiler_params=_cparams(("arbitrary", "arbitrary")),
        name="moe_grouped",
    )(tile_group, hs, wrh, wrm, br, wg, wu, wd)


def _gather_residual_kernel(dest_ref, y_hbm, x_ref, g2_ref, fn_ref, o_ref, buf, sem, *, tm, nsteps, final):
    i = pl.program_id(0)

    def fetch(step, slot):
        def issue(r, c):
            _row_copy(y_hbm, dest_ref[step * tm + r], buf.at[slot], r, sem.at[slot]).start()
            return c
        lax.fori_loop(0, tm, issue, 0, unroll=8)

    @pl.when(i == 0)
    def _():
        fetch(0, 0)

    @pl.when(i + 1 < nsteps)
    def _():
        fetch(i + 1, (i + 1) % 2)

    slot = i % 2

    def drain(r, c):
        _row_copy(y_hbm, 0, buf.at[slot], 0, sem.at[slot]).wait()
        return c

    lax.fori_loop(0, tm, drain, 0, unroll=8)
    x = x_ref[...] + g2_ref[0] * _load_row_tiles(buf.at[slot], tm)
    if final:
        x = x * lax.rsqrt(jnp.mean(x * x, axis=-1, keepdims=True) + EPS) * fn_ref[...]
    o_ref[...] = x


def gather_residual(g, dest, y_tiles, x, gate2, fnorm, final, tm):
    t, d = x.shape
    nsteps = t // tm
    kern = functools.partial(_gather_residual_kernel, tm=tm, nsteps=nsteps, final=final)
    return pl.pallas_call(
        kern,
        grid_spec=pltpu.PrefetchScalarGridSpec(
            num_scalar_prefetch=1,
            grid=(nsteps,),
            in_specs=[pl.BlockSpec(memory_space=pl.ANY),
                      pl.BlockSpec((tm, d), lambda i, dst: (i, 0)),
                      pl.BlockSpec((1, 1, d), lambda i, dst: (g.batch_of_block(i, tm), 0, 0)),
                      pl.BlockSpec((1, d), lambda i, dst: (0, 0))],
            out_specs=pl.BlockSpec((tm, d), lambda i, dst: (i, 0)),
            scratch_shapes=[pltpu.VMEM((2, tm * ROW_TILE, LANES), F32), pltpu.SemaphoreType.DMA((2,))]),
        out_shape=jax.ShapeDtypeStruct((t, d), F32),
        compiler_params=_cparams(("arbitrary",)),
        name="moe_gather_residual",
    )(dest, y_tiles, x, gate2, fnorm)


def _rope_tables(smax, rot_dim, period, lane_off):
    half = rot_dim // 2
    inv_freq = ROPE_THETA ** (-jnp.arange(half, dtype=F32) / half)
    ang = jnp.arange(smax).astype(F32)[:, None] * inv_freq[None, :]
    cos, sin = jnp.cos(ang), jnp.sin(ang)
    gl = (jnp.arange(LANES) % period) - lane_off
    first = (gl >= 0) & (gl < half)
    second = (gl >= half) & (gl < rot_dim)
    j = jnp.clip(jnp.where(first, gl, gl - half), 0, half - 1)
    cl, sn = cos[:, j], sin[:, j]
    ct = jnp.where((first | second)[None, :], cl, 1.0)
    s1 = jnp.where(first[None, :], -sn, 0.0)
    s2 = jnp.where(second[None, :], sn, 0.0)
    return ct, s1, s2


def _pad_w_in(w):
    d = w.shape[0]
    parts, off = [], 0
    for sz in IN_SIZES:
        parts.append(w[:, off:off + sz])
        off += sz
    a_q, a_k, a_v, b_q, b_k, b_v, b_r, b_zf, b_zb, c_cq, c_ckv, c_kr, gates = parts
    z = lambda n: jnp.zeros((d, n), w.dtype)
    bz = jnp.concatenate([b_zf, b_zb, z(LANES - 2 * B_GATE_RANK)], axis=1)
    kr = jnp.concatenate([z(KR_LANE), c_kr, z(LANES - KR_LANE - C_ROPE_DIM)], axis=1)
    used = OFF_KR + LANES
    out = jnp.concatenate([gates, b_v, b_r, a_q, a_k, a_v, b_q, b_k, c_ckv, bz, c_cq, kr, z(NPAD - used)], axis=1)
    assert out.shape[1] == NPAD
    return out.astype(BF16)


def _pad_heads_cols(w, real, take_lo, take_hi):
    kdim = w.shape[0]
    wh = w.reshape(kdim, C_HEADS, real)[:, :, take_lo:take_hi]
    wh = jnp.pad(wh, ((0, 0), (0, 0), (0, C_HEAD_PAD - (take_hi - take_lo))))
    return wh.reshape(kdim, C_PAD_WIDTH).astype(BF16)


def _gate_weights(w_a, b_a, row_off):
    wa = jnp.zeros((LANES, B_KW), F32).at[row_off:row_off + B_GATE_RANK].set(w_a)
    return wa.astype(BF16), b_a.reshape(1, B_KW).astype(F32)


def kernel(x_prompt, x_sample, c_prompt, c_sample, norm_mix, norm_moe, w_mod, b_mod, w_in, w_gla_af, b_gla_af,
           w_gla_ab, b_gla_ab, gla_norm, mla_q_norm, w_mla_uq, mla_kv_norm, w_mla_ukv, w_proj_a, w_proj_b, w_proj_c,
           w_out, w_router, b_router, w_exp_gate, w_exp_up, w_exp_down, final_norm):
    b1, s1, d = x_prompt.shape
    b2, s2, _ = x_sample.shape
    g = Groups(b1, s1, b2, s2)
    depth = w_in.shape[0]
    tm_in = min(1024, s1, s2)
    tm_prep = min(1024, s1, s2)
    tm_proj = min(512, s1, s2)
    tm_moe = min(1024, s1, s2)

    x = jnp.concatenate([x_prompt.reshape(g.t1, d), x_sample.reshape(g.t2, d)], axis=0)
    c = jnp.concatenate([c_prompt, c_sample], axis=0)
    nbp = -(-g.nb // 8) * 8
    c_pad = jnp.pad(c, ((0, nbp - g.nb), (0, 0)))
    mod = modulation(c_pad, w_mod, b_mod)[:, :g.nb]

    smax = max(s1, s2)
    tabs_a = _rope_tables(smax, A_ROPE_DIM, A_HEAD_DIM, 0)
    tabs_c = _rope_tables(smax, C_ROPE_DIM, C_HEAD_PAD, KR_LANE)

    wr_hi = jnp.pad(w_router, ((0, 0), (0, LANES - N_EXPERTS)))
    wrh = wr_hi.astype(BF16)
    wrm = (wr_hi - wrh.astype(F32)).astype(BF16)
    fnorm = final_norm.reshape(1, d)
    br = jnp.pad(b_router.astype(F32), (0, LANES - N_EXPERTS)).reshape(1, LANES)

    for l in range(depth):
        sh1, sc1, gt1, sh2, sc2, gt2 = [m.reshape(g.nb, 1, d) for m in jnp.split(mod[l], N_MOD, axis=-1)]
        mul1 = norm_mix[l][None, None, :] * (1.0 + sc1)
        mul2 = norm_moe[l][None, None, :] * (1.0 + sc2)

        y, yd4, yd16 = in_projection(g, x, mul1, sh1, _pad_w_in(w_in[l]), tabs_a, tm_in)

        nat_cols = tuple((0, off // A_WIDTH) for off in (OFF_AQ, OFF_AK, OFF_AV))
        dil_cols = ((0, 0), (1, 0), (2, 0))
        o_list, lse_list = [], []
        for (_, dil), src in zip(A_PATTERNS, (y.reshape(1, g.t, NPAD), yd4, yd16)):
            o, lse = band_attention(g, src, dil, nat_cols if dil == 1 else dil_cols, NPAD // A_WIDTH if dil == 1 else 1)
            o_list.append(o)
            lse_list.append(lse)

        waf, baf = _gate_weights(w_gla_af[l], b_gla_af[l], 0)
        wab, bab = _gate_weights(w_gla_ab[l], b_gla_ab[l], B_GATE_RANK)
        o_back = gla_pass(g, y, wab, bab, reverse=True)
        yb = gla_pass(g, y, waf, baf, reverse=False, o_back=o_back, gain=gla_norm[l].reshape(1, B_VAL_DIM))

        wq = jnp.concatenate(
            [w_mla_uq[l].reshape(C_Q_RANK, C_HEADS, C_QK_DIM),
             jnp.zeros((C_Q_RANK, C_HEADS, C_HEAD_PAD - C_QK_DIM), F32)], axis=2
        ).reshape(C_Q_RANK, C_PAD_WIDTH).astype(BF16)
        wk = _pad_heads_cols(w_mla_ukv[l], C_NOPE_DIM + C_V_DIM, 0, C_NOPE_DIM)
        wv = _pad_heads_cols(w_mla_ukv[l], C_NOPE_DIM + C_V_DIM, C_NOPE_DIM, C_NOPE_DIM + C_V_DIM)
        qm, km, vm = mla_prep(g, y, mla_q_norm[l].reshape(1, C_Q_RANK), mla_kv_norm[l].reshape(1, C_KV_RANK),
                              wq, wk, wv, tabs_c, tm_prep)
        yc1 = mla_flash(qm, km, vm, 0, b1, s1, min(1024, s1), min(512, s1))
        yc2 = mla_flash(qm, km, vm, g.t1, b2, s2, min(1024, s2), min(512, s2))

        wpc = jnp.pad(w_proj_c[l].reshape(C_HEADS, C_V_DIM, d), ((0, 0), (0, C_HEAD_PAD - C_V_DIM), (0, 0)))
        wpc = wpc.reshape(C_PAD_WIDTH, d).astype(BF16)
        x, h2, comb = proj_merge(g, o_list, lse_list, yb, yc1, yc2, y, x, gt1, mul2, sh2,
                                 w_proj_a[l].astype(BF16), w_proj_b[l].astype(BF16), wpc, w_out[l].astype(BF16),
                                 wrh, wrm, br, tm_proj)

        dest, tile_group, ntiles = _sorted_layout(comb, tm_moe)
        hs = scatter_rows(dest, h2, ntiles * tm_moe, tm_moe)
        ys = moe_grouped(tile_group, ntiles, hs, wrh, wrm, br, w_exp_gate[l].astype(BF16),
                         w_exp_up[l].astype(BF16), w_exp_down[l].astype(BF16), tm_moe)
        x = gather_residual(g, dest, ys, x, gt2, fnorm, l == depth - 1, tm_proj)

    return x[:g.t1].reshape(b1, s1, d), x[g.t1:].reshape(b2, s2, d)
```

```python
import functools

import jax
import jax.numpy as jnp
from jax import lax
from jax.experimental import pallas as pl
from jax.experimental.pallas import tpu as pltpu

F32 = jnp.float32
BF16 = jnp.bfloat16

D_MODEL = 1024
DEPTH = 2
EPS = 1e-6
ROPE_THETA = 500000.0
NEG_INF = -1e30

A_HEADS = 8
A_HEAD_DIM = 64
A_ROPE_DIM = A_HEAD_DIM // 4
A_PATTERNS = ((128, 1), (512, 4), (2048, 16))
A_WIDTH = A_HEADS * A_HEAD_DIM
A_HALF = 64

B_HEADS = 4
B_KEY_DIM = 128
B_VAL_DIM = 256
B_GATE_RANK = 16
B_GATE_TAU = 16.0
B_CHUNK = 64
B_KW = B_HEADS * B_KEY_DIM
B_VW = B_HEADS * B_VAL_DIM

C_HEADS = 8
C_NOPE_DIM = 64
C_ROPE_DIM = 32
C_V_DIM = 64
C_QK_DIM = C_NOPE_DIM + C_ROPE_DIM
C_Q_RANK = 384
C_KV_RANK = 256
C_HEAD_PAD = 128
C_PAD_WIDTH = C_HEADS * C_HEAD_PAD

N_EXPERTS = 16
N_GROUPS = 4
EXPERTS_PER_GROUP = N_EXPERTS // N_GROUPS
TOP_K = 2
GROUP_SCORE_K = 2
D_EXPERT = 512
N_MOD = 6

IN_SIZES = (A_WIDTH, A_WIDTH, A_WIDTH, B_KW, B_KW, B_VW, B_VW, B_GATE_RANK, B_GATE_RANK,
            C_Q_RANK, C_KV_RANK, C_ROPE_DIM, 3 * D_MODEL)

LANES = 128
ROW_TILE = D_MODEL // LANES
MOE_EXPERTS_PER_STEP = 2
LOG2E = 1.4426950408889634

OFF_GATES = 0
OFF_BV = 3072
OFF_BR = 4096
OFF_AQ = 5120
OFF_AK = 5632
OFF_AV = 6144
OFF_BQ = 6656
OFF_BK = 7168
OFF_CKV = 7680
OFF_BZ = 7936
OFF_CQ = 8064
OFF_KR = 8448
NPAD = 9216
KR_LANE = 64

VMEM_LIMIT = 56 * 1024 * 1024


def _cparams(sem):
    return pltpu.CompilerParams(dimension_semantics=sem, vmem_limit_bytes=VMEM_LIMIT)


class Groups:
    def __init__(self, b1, s1, b2, s2):
        self.b1, self.s1, self.b2, self.s2 = b1, s1, b2, s2
        self.t1, self.t2 = b1 * s1, b2 * s2
        self.t = self.t1 + self.t2
        self.nb = b1 + b2

    def batch_of_block(self, i, tm):
        n1 = self.t1 // tm
        return jnp.where(i < n1, i // (self.s1 // tm), self.b1 + (i - n1) // (self.s2 // tm))

    def pos_block(self, i, tm):
        n1 = self.t1 // tm
        return jnp.where(i < n1, i % (self.s1 // tm), (i - n1) % (self.s2 // tm))


def _dot(a, b):
    return jnp.dot(a, b, preferred_element_type=F32)


def _dot_nt(a, b):
    return lax.dot_general(a, b, (((1,), (1,)), ((), ())), preferred_element_type=F32)


def _dot_tn(a, b):
    return lax.dot_general(a, b, (((0,), (0,)), ((), ())), preferred_element_type=F32)


def _split3(a):
    hi = a.astype(BF16)
    r1 = a - hi.astype(F32)
    mid = r1.astype(BF16)
    lo = (r1 - mid.astype(F32)).astype(BF16)
    return hi, mid, lo


def _rope(y, ct, s1, s2, half):
    return y * ct + pltpu.roll(y, LANES - half, 1) * s1 + pltpu.roll(y, half, 1) * s2


def _sigmoid(x):
    return 0.5 * jnp.tanh(0.5 * x) + 0.5


def _mod_kernel(c_ref, w_ref, b_ref, o_ref):
    c = c_ref[...]
    ca = c * _sigmoid(c)
    ch, cm, _ = _split3(ca)
    wh, wm, _ = _split3(w_ref[0])
    o_ref[0] = _dot(ch, wh) + _dot(cm, wh) + _dot(ch, wm) + b_ref[0]


def modulation(c_pad, w_mod, b_mod):
    nbp, d = c_pad.shape
    depth, _, n = w_mod.shape
    tn = 512
    return pl.pallas_call(
        _mod_kernel,
        grid=(depth, n // tn),
        in_specs=[pl.BlockSpec((nbp, d), lambda l, j: (0, 0)),
                  pl.BlockSpec((1, d, tn), lambda l, j: (l, 0, j)),
                  pl.BlockSpec((1, 1, tn), lambda l, j: (l, 0, j))],
        out_specs=pl.BlockSpec((1, nbp, tn), lambda l, j: (l, 0, j)),
        out_shape=jax.ShapeDtypeStruct((depth, nbp, n), F32),
        compiler_params=_cparams(("arbitrary", "arbitrary")),
        name="modulation",
    )(c_pad, w_mod, b_mod.reshape(depth, 1, n))


def _inproj_kernel(x_ref, mul_ref, sh_ref, w_ref, ct_ref, s1_ref, s2_ref, o_ref, od4_ref, od16_ref,
                   h_scr, y_scr, y4_scr, *, tm, j_qk, j_v):
    j = pl.program_id(1)
    aw = A_WIDTH
    ncol = aw // LANES

    @pl.when(j == 0)
    def _():
        x = x_ref[...]
        ms = jnp.mean(x * x, axis=-1, keepdims=True)
        h = x * lax.rsqrt(ms + EPS) * mul_ref[0] + sh_ref[0]
        h_scr[...] = h.astype(BF16)

    def attention_block(kind, half, rope):
        y = _dot(h_scr[...], w_ref[j, :, half * aw:(half + 1) * aw])
        if rope:
            ct, s1, s2 = ct_ref[...], s1_ref[...], s2_ref[...]
        for c in range(ncol):
            yc = y[:, c * LANES:(c + 1) * LANES]
            y_scr[c] = _rope(yc, ct, s1, s2, A_ROPE_DIM // 2) if rope else yc
        n4, n16 = tm // 4, tm // 16
        for c in range(ncol):
            o_ref[:, half * aw + c * LANES:half * aw + (c + 1) * LANES] = y_scr[c].astype(BF16)
            for r in range(4):
                y4 = y_scr[c, pl.ds(r, n4, stride=4), :]
                y4_scr[c, r * n4:(r + 1) * n4, :] = y4
                od4_ref[kind, :, r * aw + c * LANES:r * aw + (c + 1) * LANES] = y4.astype(BF16)
            for r in range(4):
                for r2 in range(4):
                    col = (r + 4 * r2) * aw + c * LANES
                    od16_ref[kind, :, col:col + LANES] = y4_scr[c, pl.ds(r * n4 + r2, n16, stride=4), :].astype(BF16)

    @pl.when((j != j_qk) & (j != j_v))
    def _():
        o_ref[...] = _dot(h_scr[...], w_ref[j]).astype(BF16)

    @pl.when(j == j_qk)
    def _():
        attention_block(0, 0, True)
        attention_block(1, 1, True)

    @pl.when(j == j_v)
    def _():
        attention_block(2, 0, False)
        o_ref[:, aw:] = _dot(h_scr[...], w_ref[j, :, aw:]).astype(BF16)


def in_projection(g, x, mul, shift, w_pad, tabs, tm):
    t, d = x.shape
    tn = 2 * A_WIDTH
    aw = A_WIDTH
    ct, s1, s2 = tabs
    assert OFF_AQ % tn == 0 and OFF_AK == OFF_AQ + aw and OFF_AV == OFF_AQ + tn and NPAD % tn == 0
    bmap = lambda i, j: (g.batch_of_block(i, tm), 0, 0)
    pmap = lambda i, j: (g.pos_block(i, tm), 0)
    kern = functools.partial(_inproj_kernel, tm=tm, j_qk=OFF_AQ // tn, j_v=OFF_AV // tn)
    return pl.pallas_call(
        kern,
        grid=(t // tm, NPAD // tn),
        in_specs=[pl.BlockSpec((tm, d), lambda i, j: (i, 0)),
                  pl.BlockSpec((1, 1, d), bmap),
                  pl.BlockSpec((1, 1, d), bmap),
                  pl.BlockSpec((NPAD // tn, d, tn), lambda i, j: (0, 0, 0), pipeline_mode=pl.Buffered(1)),
                  pl.BlockSpec((tm, LANES), pmap),
                  pl.BlockSpec((tm, LANES), pmap),
                  pl.BlockSpec((tm, LANES), pmap)],
        out_specs=[pl.BlockSpec((tm, tn), lambda i, j: (i, j)),
                   pl.BlockSpec((3, tm // 4, 4 * aw), lambda i, j: (0, i, 0)),
                   pl.BlockSpec((3, tm // 16, 16 * aw), lambda i, j: (0, i, 0))],
        out_shape=[jax.ShapeDtypeStruct((t, NPAD), BF16),
                   jax.ShapeDtypeStruct((3, t // 4, 4 * aw), BF16),
                   jax.ShapeDtypeStruct((3, t // 16, 16 * aw), BF16)],
        scratch_shapes=[pltpu.VMEM((tm, d), BF16), pltpu.VMEM((aw // LANES, tm, LANES), F32),
                        pltpu.VMEM((aw // LANES, tm, LANES), F32)],
        compiler_params=_cparams(("arbitrary", "arbitrary")),
        name="in_projection",
    )(x, mul, shift, w_pad.reshape(d, NPAD // tn, tn).transpose(1, 0, 2), ct, s1, s2)


def _band_kernel(q_ref, kp_ref, km_ref, kn_ref, vp_ref, vm_ref, vn_ref, o_ref, lse_ref, *, tb, tq, rows1, l1, l2):
    i = pl.program_id(1)
    tk = tq + 2 * A_HALF
    k = jnp.concatenate([kp_ref[...], km_ref[...], kn_ref[...]], axis=0)
    v = jnp.concatenate([vp_ref[...], vm_ref[...], vn_ref[...]], axis=0)
    qi = lax.broadcasted_iota(jnp.int32, (tq, tk), 0)
    kj = lax.broadcasted_iota(jnp.int32, (tq, tk), 1)
    band = jnp.abs(kj - A_HALF - qi) <= A_HALF
    lo = lax.broadcasted_iota(jnp.int32, (tq, LANES), 1) < A_HEAD_DIM
    scale = A_HEAD_DIM ** -0.5
    for u in range(tb // tq):
        row0 = i * tb + u * tq
        in1 = row0 < rows1
        seq_len = jnp.where(in1, l1, l2)
        pos0 = jnp.where(in1, row0 % l1, (row0 - rows1) % l2)
        kpos = pos0 - A_HALF + kj
        valid = band & (kpos >= 0) & (kpos < seq_len)
        valid2 = jnp.concatenate([valid, valid], axis=0)
        rows = slice(u * tq, (u + 1) * tq)
        krows = slice(u * tq, u * tq + tk)
        for p in range(A_WIDTH // LANES):
            sl = slice(p * LANES, (p + 1) * LANES)
            qp, kp, vp = q_ref[rows, sl] * scale, k[krows, sl], v[krows, sl]
            zero = jnp.zeros_like(qp)
            qm = jnp.concatenate([jnp.where(lo, qp, zero), jnp.where(lo, zero, qp)], axis=0)
            s = jnp.where(valid2, _dot_nt(qm, kp), NEG_INF)
            m = jnp.max(s, axis=1, keepdims=True)
            e = jnp.exp(s - m)
            l = jnp.sum(e, axis=1, keepdims=True)
            o2 = _dot(e.astype(BF16), vp) / l
            lse2 = m + jnp.log(l)
            o_ref[rows, sl] = jnp.where(lo, o2[:tq], o2[tq:]).astype(BF16)
            lse_ref[rows, sl] = jnp.where(lo, lse2[:tq], lse2[tq:])


def band_attention(g, qkv, dil, cols, cpb):
    rows = g.t // dil
    tq = 128
    tb = 1024
    sub = tb // A_HALF
    nsub = rows // A_HALF

    def main(c):
        ld, col = cols[c]
        return pl.BlockSpec((None, tb, A_WIDTH), lambda r, i: (ld, i, r * cpb + col))

    def prev(c):
        ld, col = cols[c]
        return pl.BlockSpec((None, A_HALF, A_WIDTH), lambda r, i: (ld, jnp.maximum(i * sub - 1, 0), r * cpb + col))

    def nxt(c):
        ld, col = cols[c]
        return pl.BlockSpec((None, A_HALF, A_WIDTH),
                            lambda r, i: (ld, jnp.minimum((i + 1) * sub, nsub - 1), r * cpb + col))

    assert (g.s1 // dil) % tq == 0 and (g.s2 // dil) % tq == 0 and rows % tb == 0
    kern = functools.partial(_band_kernel, tb=tb, tq=tq, rows1=g.t1 // dil, l1=g.s1 // dil, l2=g.s2 // dil)
    return pl.pallas_call(
        kern,
        grid=(dil, rows // tb),
        in_specs=[main(0), prev(1), main(1), nxt(1), prev(2), main(2), nxt(2)],
        out_specs=[pl.BlockSpec((tb, A_WIDTH), lambda r, i: (i, r)),
                   pl.BlockSpec((tb, A_WIDTH), lambda r, i: (i, r))],
        out_shape=[jax.ShapeDtypeStruct((rows, dil * A_WIDTH), BF16),
                   jax.ShapeDtypeStruct((rows, dil * A_WIDTH), F32)],
        compiler_params=_cparams(("arbitrary", "arbitrary")),
        name="band_attention_d%d" % dil,
    )(qkv, qkv, qkv, qkv, qkv, qkv, qkv)


def _gla_kernel(*refs, reverse, final, tc, nblk, t1, s1, s2):
    if final:
        q_ref, k_ref, v_ref, z_ref, wa_ref, ba_ref, tri_ref, ob_ref, r_ref, gain_ref, o_ref, st_scr = refs
    else:
        q_ref, k_ref, v_ref, z_ref, wa_ref, ba_ref, tri_ref, o_ref, st_scr = refs
    i = pl.program_id(0)
    blk = (nblk - 1 - i) if reverse else i
    row0 = blk * tc
    in1 = row0 < t1
    pos0 = jnp.where(in1, row0 % s1, (row0 - t1) % s2)
    slen = jnp.where(in1, s1, s2)
    start = (pos0 + tc == slen) if reverse else (pos0 == 0)

    @pl.when(start)
    def _():
        st_scr[...] = jnp.zeros_like(st_scr)

    zl = _dot(z_ref[...], wa_ref[...]) + ba_ref[...]
    la = (jnp.minimum(zl, 0.0) - jnp.log(1.0 + jnp.exp(-jnp.abs(zl)))) * (1.0 / B_GATE_TAU)
    hi, mid, _ = _split3(la)
    tri = tri_ref[...]
    tb = tri.shape[0]
    bc_all = jnp.concatenate(
        [_dot(tri, hi[b * tb:(b + 1) * tb]) + _dot(tri, mid[b * tb:(b + 1) * tb]) for b in range(tc // tb)], axis=0)

    qi = lax.broadcasted_iota(jnp.int32, (B_CHUNK, B_CHUNK), 0)
    si = lax.broadcasted_iota(jnp.int32, (B_CHUNK, B_CHUNK), 1)
    mask = (si > qi) if reverse else (si <= qi)
    nch = tc // B_CHUNK
    q_all = q_ref[...].astype(F32) * (B_KEY_DIM ** -0.5)
    k_all = k_ref[...].astype(F32)
    q_dec = (q_all * jnp.exp(bc_all)).astype(BF16)
    k_inv = (k_all * jnp.exp(-bc_all)).astype(BF16)
    st = [st_scr[h] for h in range(B_HEADS)]
    for c in (range(nch - 1, -1, -1) if reverse else range(nch)):
        sl = slice(c * B_CHUNK, (c + 1) * B_CHUNK)
        edge = c * B_CHUNK if reverse else (c + 1) * B_CHUNK - 1
        tot = bc_all[edge:edge + 1]
        k_end = (k_all[sl] * jnp.exp(tot - bc_all[sl])).astype(BF16)
        dec = jnp.exp(tot)
        for h in range(B_HEADS):
            ks = slice(h * B_KEY_DIM, (h + 1) * B_KEY_DIM)
            vs = slice(h * B_VAL_DIM, (h + 1) * B_VAL_DIM)
            v = v_ref[sl, vs]
            att = jnp.where(mask, _dot_nt(q_dec[sl, ks], k_inv[sl, ks]), 0.0).astype(BF16)
            o = _dot(att, v) + _dot_nt(q_dec[sl, ks], st[h].astype(BF16))
            st[h] = st[h] * dec[:, ks] + _dot_tn(v, k_end[:, ks])
            if final:
                o = o + ob_ref[sl, vs]
                on = o * lax.rsqrt(jnp.mean(o * o, axis=-1, keepdims=True) + EPS) * gain_ref[...]
                r = r_ref[sl, vs].astype(F32)
                o_ref[sl, vs] = (on * (r * _sigmoid(r))).astype(BF16)
            else:
                o_ref[sl, vs] = o
    for h in range(B_HEADS):
        st_scr[h] = st[h]


def gla_pass(g, y, wa, ba, reverse, o_back=None, gain=None):
    t = g.t
    tc = min(1024, g.s1, g.s2)
    nblk = t // tc
    final = o_back is not None
    rowmap = (lambda i: nblk - 1 - i) if reverse else (lambda i: i)
    row = lambda w, off: pl.BlockSpec((tc, w), lambda i: (rowmap(i), off // w))
    full = lambda a: pl.BlockSpec(a.shape, lambda i: (0,) * a.ndim)
    tb = min(tc, 256)
    ri = jnp.arange(tb)[:, None]
    ci = jnp.arange(tb)[None, :]
    tri = ((ri // B_CHUNK == ci // B_CHUNK) & ((ci >= ri) if reverse else (ci <= ri))).astype(BF16)
    in_specs = [row(B_KW, OFF_BQ), row(B_KW, OFF_BK), row(B_VW, OFF_BV), row(LANES, OFF_BZ),
                full(wa), full(ba), full(tri)]
    args = [y, y, y, y, wa, ba, tri]
    if final:
        in_specs += [row(B_VW, 0), row(B_VW, OFF_BR), full(gain)]
        args += [o_back, y, gain]
    kern = functools.partial(_gla_kernel, reverse=reverse, final=final, tc=tc, nblk=nblk, t1=g.t1, s1=g.s1, s2=g.s2)
    return pl.pallas_call(
        kern,
        grid=(nblk,),
        in_specs=in_specs,
        out_specs=row(B_VW, 0),
        out_shape=jax.ShapeDtypeStruct((t, B_VW), BF16 if final else F32),
        scratch_shapes=[pltpu.VMEM((B_HEADS, B_VAL_DIM, B_KEY_DIM), F32)],
        compiler_params=_cparams(("arbitrary",)),
        name="gla_forward_final" if final else "gla_backward",
    )(*args)


def _mla_prep_kernel(ckv_ref, cq_ref, kr_ref, qn_ref, kvn_ref, wq_ref, wk_ref, wv_ref, ct_ref, s1_ref, s2_ref,
                     q_out, k_out, v_out):
    def norm(ref, gain_ref):
        xf = ref[...].astype(F32)
        return (xf * lax.rsqrt(jnp.mean(xf * xf, axis=-1, keepdims=True) + EPS) * gain_ref[...]).astype(BF16)

    ckv_n = norm(ckv_ref, kvn_ref)
    cq_n = norm(cq_ref, qn_ref)
    q = _dot(cq_n, wq_ref[...])
    kn = _dot(ckv_n, wk_ref[...])
    v = _dot(ckv_n, wv_ref[...])
    ct, s1, s2 = ct_ref[...], s1_ref[...], s2_ref[...]
    half = C_ROPE_DIM // 2
    kr_rot = _rope(kr_ref[...].astype(F32), ct, s1, s2, half)
    ones_lane = lax.broadcasted_iota(jnp.int32, kr_rot.shape, 1) == C_V_DIM
    scale = C_QK_DIM ** -0.5 * LOG2E
    for h in range(C_HEADS):
        sl = slice(h * C_HEAD_PAD, (h + 1) * C_HEAD_PAD)
        q_out[:, sl] = (_rope(q[:, sl], ct, s1, s2, half) * scale).astype(BF16)
        k_out[:, sl] = (kn[:, sl] + kr_rot).astype(BF16)
        v_out[:, sl] = jnp.where(ones_lane, 1.0, v[:, sl]).astype(BF16)


def mla_prep(g, y, qn, kvn, wq, wk, wv, tabs, tm):
    t = g.t
    ct, s1, s2 = tabs
    pmap = lambda i: (g.pos_block(i, tm), 0)
    full = lambda a: pl.BlockSpec(a.shape, lambda i: (0,) * a.ndim)
    out = jax.ShapeDtypeStruct((t, C_PAD_WIDTH), BF16)
    ospec = pl.BlockSpec((tm, C_PAD_WIDTH), lambda i: (i, 0))
    return pl.pallas_call(
        _mla_prep_kernel,
        grid=(t // tm,),
        in_specs=[pl.BlockSpec((tm, C_KV_RANK), lambda i: (i, OFF_CKV // C_KV_RANK)),
                  pl.BlockSpec((tm, C_Q_RANK), lambda i: (i, OFF_CQ // C_Q_RANK)),
                  pl.BlockSpec((tm, LANES), lambda i: (i, OFF_KR // LANES)),
                  full(qn), full(kvn), full(wq), full(wk), full(wv),
                  pl.BlockSpec((tm, LANES), pmap), pl.BlockSpec((tm, LANES), pmap), pl.BlockSpec((tm, LANES), pmap)],
        out_specs=[ospec, ospec, ospec],
        out_shape=[out, out, out],
        compiler_params=_cparams(("arbitrary",)),
        name="mla_prep",
    )(y, y, y, qn, kvn, wq, wk, wv, ct, s1, s2)


def _flash_kernel(q_ref, k_ref, v_ref, o_ref, s0_scr, s1_scr, p_scr, m_scr, a_scr, acc_scr, *, tk, nk, rc, unroll):
    tq = q_ref.shape[0]

    def logits(t, scr):
        off = pl.multiple_of(t * tk, tk)
        scr[...] = _dot_nt(q_ref[...], k_ref[pl.ds(off, tk), :])

    def softmax_pv(t, scr):
        off = pl.multiple_of(t * tk, tk)
        for r in range(tq // rc):
            rows = slice(r * rc, (r + 1) * rc)
            s = scr[rows, :]
            m_old = m_scr[rows, :]
            m_new = jnp.maximum(m_old, jnp.max(s, axis=1, keepdims=True))
            p_scr[rows, :] = jnp.exp2(s - jnp.concatenate([m_new] * (tk // LANES), axis=1)).astype(BF16)
            a_scr[rows, :] = jnp.exp2(m_old - m_new)
            m_scr[rows, :] = m_new
        acc_scr[...] = acc_scr[...] * a_scr[...] + _dot(p_scr[...], v_ref[pl.ds(off, tk), :])

    bufs = (s0_scr, s1_scr)

    def group(t0, last):
        for u in range(unroll):
            if not (last and u == unroll - 1):
                logits(t0 + u + 1, bufs[(u + 1) % 2])
            softmax_pv(t0 + u, bufs[u % 2])

    def body(jj, carry):
        group(unroll * jj, False)
        return carry

    m_scr[...] = jnp.full(m_scr.shape, NEG_INF, F32)
    acc_scr[...] = jnp.zeros(acc_scr.shape, F32)
    logits(0, s0_scr)
    lax.fori_loop(0, nk // unroll - 1, body, 0)
    group(nk - unroll, True)
    acc = acc_scr[...]
    o_ref[...] = (acc / acc[:, C_V_DIM:C_V_DIM + 1]).astype(BF16)


def mla_flash(q, k, v, row_off, nseq, s, tq, tk):
    assert row_off % s == 0 and row_off % tq == 0
    qb0 = row_off // tq
    sb0 = row_off // s
    nq = s // tq
    nk = s // tk
    unroll = next(u for u in (16, 8, 4, 2) if nk % u == 0)
    assert nk % unroll == 0
    kern = functools.partial(_flash_kernel, tk=tk, nk=nk, rc=64, unroll=unroll)
    return pl.pallas_call(
        kern,
        grid=(nseq, C_HEADS, nq),
        scratch_shapes=[pltpu.VMEM((tq, tk), F32), pltpu.VMEM((tq, tk), F32), pltpu.VMEM((tq, tk), BF16),
                        pltpu.VMEM((tq, LANES), F32), pltpu.VMEM((tq, LANES), F32),
                        pltpu.VMEM((tq, C_HEAD_PAD), F32)],
        in_specs=[pl.BlockSpec((tq, C_HEAD_PAD), lambda b, h, i: (qb0 + b * nq + i, h)),
                  pl.BlockSpec((s, C_HEAD_PAD), lambda b, h, i: (sb0 + b, h)),
                  pl.BlockSpec((s, C_HEAD_PAD), lambda b, h, i: (sb0 + b, h))],
        out_specs=pl.BlockSpec((tq, C_HEAD_PAD), lambda b, h, i: (b * nq + i, h)),
        out_shape=jax.ShapeDtypeStruct((nseq * s, C_PAD_WIDTH), BF16),
        compiler_params=_cparams(("arbitrary", "arbitrary", "arbitrary")),
        name="mla_flash_s%d" % s,
    )(q, k, v)


def _store_row_tiles(ref, x, rows):
    for c in range(ROW_TILE):
        ref[pl.ds(c, rows, stride=ROW_TILE), :] = x[:, c * LANES:(c + 1) * LANES]


def _load_row_tiles(ref, rows):
    return jnp.concatenate([ref[pl.ds(c, rows, stride=ROW_TILE), :] for c in range(ROW_TILE)], axis=1)


def _router_scores(h, wrh_ref, wrm_ref):
    hh, hm, _ = _split3(h)
    return _sigmoid(_dot(hh, wrh_ref[...]) + _dot(hm, wrh_ref[...]) + _dot(hh, wrm_ref[...]))


def _rank_in_group(sel, pos):
    rank = jnp.zeros(sel.shape, jnp.int32)
    for k in range(1, EXPERTS_PER_GROUP):
        below = pltpu.roll(sel, k, 1)
        above = pltpu.roll(sel, LANES - k, 1)
        rank += jnp.where((pos >= k) & (below >= sel), 1, 0)
        rank += jnp.where((pos + k < EXPERTS_PER_GROUP) & (above > sel), 1, 0)
    return rank


def _route_in_group(s, bias, g):
    lane = lax.broadcasted_iota(jnp.int32, s.shape, 1)
    rank = _rank_in_group(s + bias, lane % EXPERTS_PER_GROUP)
    chosen = (lane // EXPERTS_PER_GROUP == g) & (rank < TOP_K)
    total = jnp.sum(jnp.where(chosen, s, 0.0), axis=1, keepdims=True)
    return jnp.where(chosen, s / total, 0.0)


def _route(s, bias):
    lane = lax.broadcasted_iota(jnp.int32, s.shape, 1)
    pos = lane % EXPERTS_PER_GROUP
    grp = lane // EXPERTS_PER_GROUP
    sel = s + bias
    rank = _rank_in_group(sel, pos)
    top = rank < GROUP_SCORE_K
    contrib = jnp.where(top, sel, 0.0)
    score = contrib
    for k in range(1, EXPERTS_PER_GROUP):
        score += jnp.where(pos >= k, pltpu.roll(contrib, k, 1), 0.0)
        score += jnp.where(pos + k < EXPERTS_PER_GROUP, pltpu.roll(contrib, LANES - k, 1), 0.0)
    best = lane < N_EXPERTS
    for k in range(1, N_GROUPS):
        earlier = pltpu.roll(score, k * EXPERTS_PER_GROUP, 1)
        later = pltpu.roll(score, LANES - k * EXPERTS_PER_GROUP, 1)
        best &= jnp.logical_not((grp >= k) & (earlier >= score))
        best &= jnp.logical_not((grp + k < N_GROUPS) & (later > score))
    chosen = best & (rank < TOP_K)
    total = jnp.sum(jnp.where(chosen, s, 0.0), axis=1, keepdims=True)
    return jnp.where(chosen, s / total, 0.0)


def _proj_kernel(o1_ref, o2_ref, o3_ref, l1_ref, l2_ref, l3_ref, yb_ref, yc1_ref, yc2_ref, ga_ref, gb_ref, gc_ref,
                 x_ref, g1_ref, mul2_ref, sh2_ref, wpa_ref, wpb_ref, wpc_ref, wout_ref, wrh_ref, wrm_ref, br_ref,
                 x_out, h_out, comb_out, o2_scr, l2_scr, o3_scr, l3_scr, *, tm, n1):
    in_group1 = pl.program_id(0) < n1
    ncol = A_WIDTH // LANES
    hm = tm // 2
    for half in range(2):
        rows = slice(half * hm, (half + 1) * hm)
        for dil, o_ref, l_ref, o_scr, l_scr in ((A_PATTERNS[1][1], o2_ref, l2_ref, o2_scr, l2_scr),
                                               (A_PATTERNS[2][1], o3_ref, l3_ref, o3_scr, l3_scr)):
            drows = slice(half * hm // dil, (half + 1) * hm // dil)
            for r in range(dil):
                for c in range(ncol):
                    sl = slice(r * A_WIDTH + c * LANES, r * A_WIDTH + (c + 1) * LANES)
                    o_scr[c, pl.ds(half * hm + r, hm // dil, stride=dil), :] = o_ref[drows, sl].astype(F32)
                    l_scr[c, pl.ds(half * hm + r, hm // dil, stride=dil), :] = l_ref[drows, sl]
        slabs = lambda scr: jnp.concatenate([scr[c, rows, :] for c in range(ncol)], axis=1)
        l1, l2, l3 = l1_ref[rows, :], slabs(l2_scr), slabs(l3_scr)
        m = jnp.maximum(jnp.maximum(l1, l2), l3)
        e1, e2, e3 = jnp.exp(l1 - m), jnp.exp(l2 - m), jnp.exp(l3 - m)
        ya = (e1 * o1_ref[rows, :].astype(F32) + e2 * slabs(o2_scr) + e3 * slabs(o3_scr)) / (e1 + e2 + e3)
        sig = lambda ref: _sigmoid(ref[rows, :].astype(F32))
        yc = jnp.where(in_group1, yc1_ref[rows, :], yc2_ref[rows, :])
        merged = (sig(ga_ref) * _dot(ya.astype(BF16), wpa_ref[...])
                  + sig(gb_ref) * _dot(yb_ref[rows, :], wpb_ref[...])
                  + sig(gc_ref) * _dot(yc, wpc_ref[...]))
        out = _dot(merged.astype(BF16), wout_ref[...])
        x = x_ref[rows, :] + g1_ref[0] * out
        x_out[rows, :] = x
        h = x * lax.rsqrt(jnp.mean(x * x, axis=-1, keepdims=True) + EPS) * mul2_ref[0] + sh2_ref[0]
        _store_row_tiles(h_out.at[pl.ds(half * hm * ROW_TILE, hm * ROW_TILE), :], h, hm)
        comb_out[rows, :] = _route(_router_scores(h, wrh_ref, wrm_ref), br_ref[...])


def proj_merge(g, o_list, lse_list, yb, yc1, yc2, y, x, gate1, mul2, sh2, wpa, wpb, wpc, wout, wrh, wrm, br, tm):
    t, d = x.shape
    n1 = g.t1 // tm
    n2 = g.t2 // tm
    bmap = lambda i: (g.batch_of_block(i, tm), 0, 0)
    row = lambda w, col=0: pl.BlockSpec((tm, w), lambda i: (i, col))
    dil_row = lambda dil: pl.BlockSpec((tm // dil, dil * A_WIDTH), lambda i: (i, 0))
    full = lambda a: pl.BlockSpec(a.shape, lambda i: (0,) * a.ndim)
    gcol = OFF_GATES // d
    d2, d3 = A_PATTERNS[1][1], A_PATTERNS[2][1]
    attn_specs = [row(A_WIDTH), dil_row(d2), dil_row(d3)]
    return pl.pallas_call(
        functools.partial(_proj_kernel, tm=tm, n1=n1),
        grid=(t // tm,),
        in_specs=attn_specs + attn_specs + [
            row(B_VW),
            pl.BlockSpec((tm, C_PAD_WIDTH), lambda i: (jnp.minimum(i, n1 - 1), 0)),
            pl.BlockSpec((tm, C_PAD_WIDTH), lambda i: (jnp.clip(i - n1, 0, n2 - 1), 0)),
            row(d, gcol), row(d, gcol + 1), row(d, gcol + 2),
            row(d), pl.BlockSpec((1, 1, d), bmap), pl.BlockSpec((1, 1, d), bmap), pl.BlockSpec((1, 1, d), bmap),
            full(wpa), full(wpb), full(wpc), full(wout), full(wrh), full(wrm), full(br)],
        out_specs=[row(d), pl.BlockSpec((tm * ROW_TILE, LANES), lambda i: (i, 0)), row(LANES)],
        out_shape=[jax.ShapeDtypeStruct((t, d), F32), jax.ShapeDtypeStruct((t * ROW_TILE, LANES), F32),
                   jax.ShapeDtypeStruct((t, LANES), F32)],
        scratch_shapes=[pltpu.VMEM((A_WIDTH // LANES, tm, LANES), F32)] * 4,
        compiler_params=_cparams(("arbitrary",)),
        name="proj_merge_route",
    )(*o_list, *lse_list, yb, yc1, yc2, y, y, y, x, gate1, mul2, sh2, wpa, wpb, wpc, wout, wrh, wrm, br)


def _sorted_layout(comb, tmoe):
    t = comb.shape[0]
    gw = comb[:, :N_EXPERTS].reshape(t, N_GROUPS, EXPERTS_PER_GROUP).sum(-1)
    gid = jnp.argmax(gw, axis=-1).astype(jnp.int32)
    onehot = (gid[:, None] == jnp.arange(N_GROUPS, dtype=jnp.int32)[None, :]).astype(jnp.int32)
    incl = jnp.cumsum(onehot, axis=0)
    counts = incl[-1]
    padded = (counts + tmoe - 1) // tmoe * tmoe
    ends = jnp.cumsum(padded)
    dest = jnp.sum(onehot * (incl - onehot + (ends - padded)[None, :]), axis=1).astype(jnp.int32)
    ntiles = t // tmoe + N_GROUPS
    tile_group = jnp.sum((jnp.arange(ntiles, dtype=jnp.int32) * tmoe)[:, None] >= ends[None, :], axis=1)
    tile_group = jnp.where(tile_group < N_GROUPS, tile_group, -1).astype(jnp.int32)
    return dest, tile_group, ntiles


def _row_copy(src, src_row, dst, dst_row, sem):
    return pltpu.make_async_copy(src.at[pl.ds(src_row * ROW_TILE, ROW_TILE), :],
                                 dst.at[pl.ds(dst_row * ROW_TILE, ROW_TILE), :], sem)


def _scatter_rows_kernel(dest_ref, h_ref, init_hbm, hs_hbm, sem, *, tm):
    del init_hbm
    i = pl.program_id(0)

    def issue(r, c):
        _row_copy(h_ref, r, hs_hbm, dest_ref[i * tm + r], sem).start()
        return c

    def drain(r, c):
        _row_copy(h_ref, 0, hs_hbm, 0, sem).wait()
        return c

    lax.fori_loop(0, tm, issue, 0, unroll=8)
    lax.fori_loop(0, tm, drain, 0, unroll=8)


def scatter_rows(dest, h_tiles, rows_out, tm):
    t = dest.shape[0]
    init = jnp.zeros((rows_out * ROW_TILE, LANES), F32)
    return pl.pallas_call(
        functools.partial(_scatter_rows_kernel, tm=tm),
        grid_spec=pltpu.PrefetchScalarGridSpec(
            num_scalar_prefetch=1,
            grid=(t // tm,),
            in_specs=[pl.BlockSpec((tm * ROW_TILE, LANES), lambda i, dst: (i, 0)),
                      pl.BlockSpec(memory_space=pl.ANY)],
            out_specs=pl.BlockSpec(memory_space=pl.ANY),
            scratch_shapes=[pltpu.SemaphoreType.DMA(())]),
        out_shape=jax.ShapeDtypeStruct(init.shape, F32),
        input_output_aliases={2: 0},
        compiler_params=_cparams(("arbitrary",)),
        name="moe_scatter_rows",
    )(dest, h_tiles, init)


def _moe_group_kernel(tg_ref, hs_ref, wrh_ref, wrm_ref, br_ref, wg_ref, wu_ref, wd_ref, y_ref,
                      h_scr, comb_scr, acc_scr, *, tm):
    j = pl.program_id(0)
    k = pl.program_id(1)
    g = tg_ref[j]

    @pl.when(k == 0)
    def _():
        h = _load_row_tiles(hs_ref, tm)
        h_scr[...] = h.astype(BF16)
        comb_scr[...] = _route_in_group(_router_scores(h, wrh_ref, wrm_ref), br_ref[...], g)
        acc_scr[...] = jnp.zeros_like(acc_scr)

    @pl.when(g >= 0)
    def _():
        h = h_scr[...]
        comb = comb_scr[...]
        lane = lax.broadcasted_iota(jnp.int32, comb.shape, 1)
        total = acc_scr[...]
        for kk in range(MOE_EXPERTS_PER_STEP):
            gate = _dot(h, wg_ref[kk])
            a = gate * _sigmoid(gate) * _dot(h, wu_ref[kk])
            yv = _dot(a.astype(BF16), wd_ref[kk])
            e = g * EXPERTS_PER_GROUP + k * MOE_EXPERTS_PER_STEP + kk
            total = total + jnp.sum(jnp.where(lane == e, comb, 0.0), axis=1, keepdims=True) * yv
        acc_scr[...] = total

    @pl.when(k == EXPERTS_PER_GROUP // MOE_EXPERTS_PER_STEP - 1)
    def _():
        _store_row_tiles(y_ref, acc_scr[...], tm)


def moe_grouped(tile_group, ntiles, hs, wrh, wrm, br, wg, wu, wd, tm):
    d = D_MODEL
    full = lambda a: pl.BlockSpec(a.shape, lambda j, k, tg: (0,) * a.ndim)
    ksteps = EXPERTS_PER_GROUP // MOE_EXPERTS_PER_STEP
    emap = lambda j, k, tg: (jnp.maximum(tg[j], 0) * ksteps + k, 0, 0)
    return pl.pallas_call(
        functools.partial(_moe_group_kernel, tm=tm),
        grid_spec=pltpu.PrefetchScalarGridSpec(
            num_scalar_prefetch=1,
            grid=(ntiles, ksteps),
            in_specs=[pl.BlockSpec((tm * ROW_TILE, LANES), lambda j, k, tg: (j, 0)),
                      full(wrh), full(wrm), full(br),
                      pl.BlockSpec((MOE_EXPERTS_PER_STEP, d, D_EXPERT), emap),
                      pl.BlockSpec((MOE_EXPERTS_PER_STEP, d, D_EXPERT), emap),
                      pl.BlockSpec((MOE_EXPERTS_PER_STEP, D_EXPERT, d), emap)],
            out_specs=pl.BlockSpec((tm * ROW_TILE, LANES), lambda j, k, tg: (j, 0)),
            scratch_shapes=[pltpu.VMEM((tm, d), BF16), pltpu.VMEM((tm, LANES), F32), pltpu.VMEM((tm, d), F32)]),
        out_shape=jax.ShapeDtypeStruct(hs.shape, F32),
        compiler_params=_cparams(("arbitrary", "arbitrary")),
        name="moe_grouped",
    )(tile_group, hs, wrh, wrm, br, wg, wu, wd)


def _gather_residual_kernel(dest_ref, y_hbm, x_ref, g2_ref, fn_ref, o_ref, buf, sem, *, tm, nsteps, final):
    i = pl.program_id(0)

    def fetch(step, slot):
        def issue(r, c):
            _row_copy(y_hbm, dest_ref[step * tm + r], buf.at[slot], r, sem.at[slot]).start()
            return c
        lax.fori_loop(0, tm, issue, 0, unroll=8)

    @pl.when(i == 0)
    def _():
        fetch(0, 0)

    @pl.when(i + 1 < nsteps)
    def _():
        fetch(i + 1, (i + 1) % 2)

    slot = i % 2

    def drain(r, c):
        _row_copy(y_hbm, 0, buf.at[slot], 0, sem.at[slot]).wait()
        return c

    lax.fori_loop(0, tm, drain, 0, unroll=8)
    x = x_ref[...] + g2_ref[0] * _load_row_tiles(buf.at[slot], tm)
    if final:
        x = x * lax.rsqrt(jnp.mean(x * x, axis=-1, keepdims=True) + EPS) * fn_ref[...]
    o_ref[...] = x


def gather_residual(g, dest, y_tiles, x, gate2, fnorm, final, tm):
    t, d = x.shape
    nsteps = t // tm
    kern = functools.partial(_gather_residual_kernel, tm=tm, nsteps=nsteps, final=final)
    return pl.pallas_call(
        kern,
        grid_spec=pltpu.PrefetchScalarGridSpec(
            num_scalar_prefetch=1,
            grid=(nsteps,),
            in_specs=[pl.BlockSpec(memory_space=pl.ANY),
                      pl.BlockSpec((tm, d), lambda i, dst: (i, 0)),
                      pl.BlockSpec((1, 1, d), lambda i, dst: (g.batch_of_block(i, tm), 0, 0)),
                      pl.BlockSpec((1, d), lambda i, dst: (0, 0))],
            out_specs=pl.BlockSpec((tm, d), lambda i, dst: (i, 0)),
            scratch_shapes=[pltpu.VMEM((2, tm * ROW_TILE, LANES), F32), pltpu.SemaphoreType.DMA((2,))]),
        out_shape=jax.ShapeDtypeStruct((t, d), F32),
        compiler_params=_cparams(("arbitrary",)),
        name="moe_gather_residual",
    )(dest, y_tiles, x, gate2, fnorm)


def _rope_tables(smax, rot_dim, period, lane_off):
    half = rot_dim // 2
    inv_freq = ROPE_THETA ** (-jnp.arange(half, dtype=F32) / half)
    ang = jnp.arange(smax).astype(F32)[:, None] * inv_freq[None, :]
    cos, sin = jnp.cos(ang), jnp.sin(ang)
    gl = (jnp.arange(LANES) % period) - lane_off
    first = (gl >= 0) & (gl < half)
    second = (gl >= half) & (gl < rot_dim)
    j = jnp.clip(jnp.where(first, gl, gl - half), 0, half - 1)
    cl, sn = cos[:, j], sin[:, j]
    ct = jnp.where((first | second)[None, :], cl, 1.0)
    s1 = jnp.where(first[None, :], -sn, 0.0)
    s2 = jnp.where(second[None, :], sn, 0.0)
    return ct, s1, s2


def _pad_w_in(w):
    d = w.shape[0]
    parts, off = [], 0
    for sz in IN_SIZES:
        parts.append(w[:, off:off + sz])
        off += sz
    a_q, a_k, a_v, b_q, b_k, b_v, b_r, b_zf, b_zb, c_cq, c_ckv, c_kr, gates = parts
    z = lambda n: jnp.zeros((d, n), w.dtype)
    bz = jnp.concatenate([b_zf, b_zb, z(LANES - 2 * B_GATE_RANK)], axis=1)
    kr = jnp.concatenate([z(KR_LANE), c_kr, z(LANES - KR_LANE - C_ROPE_DIM)], axis=1)
    used = OFF_KR + LANES
    out = jnp.concatenate([gates, b_v, b_r, a_q, a_k, a_v, b_q, b_k, c_ckv, bz, c_cq, kr, z(NPAD - used)], axis=1)
    assert out.shape[1] == NPAD
    return out.astype(BF16)


def _pad_heads_cols(w, real, take_lo, take_hi):
    kdim = w.shape[0]
    wh = w.reshape(kdim, C_HEADS, real)[:, :, take_lo:take_hi]
    wh = jnp.pad(wh, ((0, 0), (0, 0), (0, C_HEAD_PAD - (take_hi - take_lo))))
    return wh.reshape(kdim, C_PAD_WIDTH).astype(BF16)


def _gate_weights(w_a, b_a, row_off):
    wa = jnp.zeros((LANES, B_KW), F32).at[row_off:row_off + B_GATE_RANK].set(w_a)
    return wa.astype(BF16), b_a.reshape(1, B_KW).astype(F32)


def kernel(x_prompt, x_sample, c_prompt, c_sample, norm_mix, norm_moe, w_mod, b_mod, w_in, w_gla_af, b_gla_af,
           w_gla_ab, b_gla_ab, gla_norm, mla_q_norm, w_mla_uq, mla_kv_norm, w_mla_ukv, w_proj_a, w_proj_b, w_proj_c,
           w_out, w_router, b_router, w_exp_gate, w_exp_up, w_exp_down, final_norm):
    b1, s1, d = x_prompt.shape
    b2, s2, _ = x_sample.shape
    g = Groups(b1, s1, b2, s2)
    depth = w_in.shape[0]
    tm_in = min(1024, s1, s2)
    tm_prep = min(1024, s1, s2)
    tm_proj = min(512, s1, s2)
    tm_moe = min(1024, s1, s2)

    x = jnp.concatenate([x_prompt.reshape(g.t1, d), x_sample.reshape(g.t2, d)], axis=0)
    c = jnp.concatenate([c_prompt, c_sample], axis=0)
    nbp = -(-g.nb // 8) * 8
    c_pad = jnp.pad(c, ((0, nbp - g.nb), (0, 0)))
    mod = modulation(c_pad, w_mod, b_mod)[:, :g.nb]

    smax = max(s1, s2)
    tabs_a = _rope_tables(smax, A_ROPE_DIM, A_HEAD_DIM, 0)
    tabs_c = _rope_tables(smax, C_ROPE_DIM, C_HEAD_PAD, KR_LANE)

    wr_hi = jnp.pad(w_router, ((0, 0), (0, LANES - N_EXPERTS)))
    wrh = wr_hi.astype(BF16)
    wrm = (wr_hi - wrh.astype(F32)).astype(BF16)
    fnorm = final_norm.reshape(1, d)
    br = jnp.pad(b_router.astype(F32), (0, LANES - N_EXPERTS)).reshape(1, LANES)

    for l in range(depth):
        sh1, sc1, gt1, sh2, sc2, gt2 = [m.reshape(g.nb, 1, d) for m in jnp.split(mod[l], N_MOD, axis=-1)]
        mul1 = norm_mix[l][None, None, :] * (1.0 + sc1)
        mul2 = norm_moe[l][None, None, :] * (1.0 + sc2)

        y, yd4, yd16 = in_projection(g, x, mul1, sh1, _pad_w_in(w_in[l]), tabs_a, tm_in)

        nat_cols = tuple((0, off // A_WIDTH) for off in (OFF_AQ, OFF_AK, OFF_AV))
        dil_cols = ((0, 0), (1, 0), (2, 0))
        o_list, lse_list = [], []
        for (_, dil), src in zip(A_PATTERNS, (y.reshape(1, g.t, NPAD), yd4, yd16)):
            o, lse = band_attention(g, src, dil, nat_cols if dil == 1 else dil_cols, NPAD // A_WIDTH if dil == 1 else 1)
            o_list.append(o)
            lse_list.append(lse)

        waf, baf = _gate_weights(w_gla_af[l], b_gla_af[l], 0)
        wab, bab = _gate_weights(w_gla_ab[l], b_gla_ab[l], B_GATE_RANK)
        o_back = gla_pass(g, y, wab, bab, reverse=True)
        yb = gla_pass(g, y, waf, baf, reverse=False, o_back=o_back, gain=gla_norm[l].reshape(1, B_VAL_DIM))

        wq = jnp.concatenate(
            [w_mla_uq[l].reshape(C_Q_RANK, C_HEADS, C_QK_DIM),
             jnp.zeros((C_Q_RANK, C_HEADS, C_HEAD_PAD - C_QK_DIM), F32)], axis=2
        ).reshape(C_Q_RANK, C_PAD_WIDTH).astype(BF16)
        wk = _pad_heads_cols(w_mla_ukv[l], C_NOPE_DIM + C_V_DIM, 0, C_NOPE_DIM)
        wv = _pad_heads_cols(w_mla_ukv[l], C_NOPE_DIM + C_V_DIM, C_NOPE_DIM, C_NOPE_DIM + C_V_DIM)
        qm, km, vm = mla_prep(g, y, mla_q_norm[l].reshape(1, C_Q_RANK), mla_kv_norm[l].reshape(1, C_KV_RANK),
                              wq, wk, wv, tabs_c, tm_prep)
        yc1 = mla_flash(qm, km, vm, 0, b1, s1, min(1024, s1), min(512, s1))
        yc2 = mla_flash(qm, km, vm, g.t1, b2, s2, min(512, s2), min(512, s2))

        wpc = jnp.pad(w_proj_c[l].reshape(C_HEADS, C_V_DIM, d), ((0, 0), (0, C_HEAD_PAD - C_V_DIM), (0, 0)))
        wpc = wpc.reshape(C_PAD_WIDTH, d).astype(BF16)
        x, h2, comb = proj_merge(g, o_list, lse_list, yb, yc1, yc2, y, x, gt1, mul2, sh2,
                                 w_proj_a[l].astype(BF16), w_proj_b[l].astype(BF16), wpc, w_out[l].astype(BF16),
                                 wrh, wrm, br, tm_proj)

        dest, tile_group, ntiles = _sorted_layout(comb, tm_moe)
        hs = scatter_rows(dest, h2, ntiles * tm_moe, tm_moe)
        ys = moe_grouped(tile_group, ntiles, hs, wrh, wrm, br, w_exp_gate[l].astype(BF16),
                         w_exp_up[l].astype(BF16), w_exp_down[l].astype(BF16), tm_moe)
        x = gather_residual(g, dest, ys, x, gt2, fnorm, l == depth - 1, tm_proj)

    return x[:g.t1].reshape(b1, s1, d), x[g.t1:].reshape(b2, s2, d)
```

```python
import functools

import jax
import jax.numpy as jnp
from jax import lax
from jax.experimental import pallas as pl
from jax.experimental.pallas import tpu as pltpu

F32 = jnp.float32
BF16 = jnp.bfloat16

D_MODEL = 1024
DEPTH = 2
EPS = 1e-6
ROPE_THETA = 500000.0
NEG_INF = -1e30

A_HEADS = 8
A_HEAD_DIM = 64
A_ROPE_DIM = A_HEAD_DIM // 4
A_PATTERNS = ((128, 1), (512, 4), (2048, 16))
A_WIDTH = A_HEADS * A_HEAD_DIM
A_HALF = 64

B_HEADS = 4
B_KEY_DIM = 128
B_VAL_DIM = 256
B_GATE_RANK = 16
B_GATE_TAU = 16.0
B_CHUNK = 64
B_KW = B_HEADS * B_KEY_DIM
B_VW = B_HEADS * B_VAL_DIM

C_HEADS = 8
C_NOPE_DIM = 64
C_ROPE_DIM = 32
C_V_DIM = 64
C_QK_DIM = C_NOPE_DIM + C_ROPE_DIM
C_Q_RANK = 384
C_KV_RANK = 256
C_HEAD_PAD = 128
C_PAD_WIDTH = C_HEADS * C_HEAD_PAD

N_EXPERTS = 16
N_GROUPS = 4
EXPERTS_PER_GROUP = N_EXPERTS // N_GROUPS
TOP_K = 2
GROUP_SCORE_K = 2
D_EXPERT = 512
N_MOD = 6

IN_SIZES = (A_WIDTH, A_WIDTH, A_WIDTH, B_KW, B_KW, B_VW, B_VW, B_GATE_RANK, B_GATE_RANK,
            C_Q_RANK, C_KV_RANK, C_ROPE_DIM, 3 * D_MODEL)

LANES = 128
ROW_TILE = D_MODEL // LANES
MOE_EXPERTS_PER_STEP = 2
LOG2E = 1.4426950408889634

OFF_GATES = 0
OFF_BV = 3072
OFF_BR = 4096
OFF_AQ = 5120
OFF_AK = 5632
OFF_AV = 6144
OFF_BQ = 6656
OFF_BK = 7168
OFF_CKV = 7680
OFF_BZ = 7936
OFF_CQ = 8064
OFF_KR = 8448
NPAD = 9216
KR_LANE = 64

VMEM_LIMIT = 56 * 1024 * 1024


def _cparams(sem):
    return pltpu.CompilerParams(dimension_semantics=sem, vmem_limit_bytes=VMEM_LIMIT)


class Groups:
    def __init__(self, b1, s1, b2, s2):
        self.b1, self.s1, self.b2, self.s2 = b1, s1, b2, s2
        self.t1, self.t2 = b1 * s1, b2 * s2
        self.t = self.t1 + self.t2
        self.nb = b1 + b2

    def batch_of_block(self, i, tm):
        n1 = self.t1 // tm
        return jnp.where(i < n1, i // (self.s1 // tm), self.b1 + (i - n1) // (self.s2 // tm))

    def pos_block(self, i, tm):
        n1 = self.t1 // tm
        return jnp.where(i < n1, i % (self.s1 // tm), (i - n1) % (self.s2 // tm))


def _dot(a, b):
    return jnp.dot(a, b, preferred_element_type=F32)


def _dot_nt(a, b):
    return lax.dot_general(a, b, (((1,), (1,)), ((), ())), preferred_element_type=F32)


def _dot_tn(a, b):
    return lax.dot_general(a, b, (((0,), (0,)), ((), ())), preferred_element_type=F32)


def _split3(a):
    hi = a.astype(BF16)
    r1 = a - hi.astype(F32)
    mid = r1.astype(BF16)
    lo = (r1 - mid.astype(F32)).astype(BF16)
    return hi, mid, lo


def _rope(y, ct, s1, s2, half):
    return y * ct + pltpu.roll(y, LANES - half, 1) * s1 + pltpu.roll(y, half, 1) * s2


def _sigmoid(x):
    return 0.5 * jnp.tanh(0.5 * x) + 0.5


def _mod_kernel(c_ref, w_ref, b_ref, o_ref):
    c = c_ref[...]
    ca = c * _sigmoid(c)
    ch, cm, _ = _split3(ca)
    wh, wm, _ = _split3(w_ref[0])
    o_ref[0] = _dot(ch, wh) + _dot(cm, wh) + _dot(ch, wm) + b_ref[0]


def modulation(c_pad, w_mod, b_mod):
    nbp, d = c_pad.shape
    depth, _, n = w_mod.shape
    tn = 512
    return pl.pallas_call(
        _mod_kernel,
        grid=(depth, n // tn),
        in_specs=[pl.BlockSpec((nbp, d), lambda l, j: (0, 0)),
                  pl.BlockSpec((1, d, tn), lambda l, j: (l, 0, j)),
                  pl.BlockSpec((1, 1, tn), lambda l, j: (l, 0, j))],
        out_specs=pl.BlockSpec((1, nbp, tn), lambda l, j: (l, 0, j)),
        out_shape=jax.ShapeDtypeStruct((depth, nbp, n), F32),
        compiler_params=_cparams(("arbitrary", "arbitrary")),
        name="modulation",
    )(c_pad, w_mod, b_mod.reshape(depth, 1, n))


def _inproj_kernel(x_ref, mul_ref, sh_ref, w_ref, ct_ref, s1_ref, s2_ref, o_ref, od4_ref, od16_ref,
                   h_scr, y_scr, y4_scr, *, tm, j_qk, j_v):
    j = pl.program_id(1)
    aw = A_WIDTH
    ncol = aw // LANES

    @pl.when(j == 0)
    def _():
        x = x_ref[...]
        ms = jnp.mean(x * x, axis=-1, keepdims=True)
        h = x * lax.rsqrt(ms + EPS) * mul_ref[0] + sh_ref[0]
        h_scr[...] = h.astype(BF16)

    def attention_block(kind, half, rope):
        y = _dot(h_scr[...], w_ref[j, :, half * aw:(half + 1) * aw])
        if rope:
            ct, s1, s2 = ct_ref[...], s1_ref[...], s2_ref[...]
        for c in range(ncol):
            yc = y[:, c * LANES:(c + 1) * LANES]
            y_scr[c] = _rope(yc, ct, s1, s2, A_ROPE_DIM // 2) if rope else yc
        n4, n16 = tm // 4, tm // 16
        for c in range(ncol):
            o_ref[:, half * aw + c * LANES:half * aw + (c + 1) * LANES] = y_scr[c].astype(BF16)
            for r in range(4):
                y4 = y_scr[c, pl.ds(r, n4, stride=4), :]
                y4_scr[c, r * n4:(r + 1) * n4, :] = y4
                od4_ref[kind, :, r * aw + c * LANES:r * aw + (c + 1) * LANES] = y4.astype(BF16)
            for r in range(4):
                for r2 in range(4):
                    col = (r + 4 * r2) * aw + c * LANES
                    od16_ref[kind, :, col:col + LANES] = y4_scr[c, pl.ds(r * n4 + r2, n16, stride=4), :].astype(BF16)

    @pl.when((j != j_qk) & (j != j_v))
    def _():
        o_ref[...] = _dot(h_scr[...], w_ref[j]).astype(BF16)

    @pl.when(j == j_qk)
    def _():
        attention_block(0, 0, True)
        attention_block(1, 1, True)

    @pl.when(j == j_v)
    def _():
        attention_block(2, 0, False)
        o_ref[:, aw:] = _dot(h_scr[...], w_ref[j, :, aw:]).astype(BF16)


def in_projection(g, x, mul, shift, w_pad, tabs, tm):
    t, d = x.shape
    tn = 2 * A_WIDTH
    aw = A_WIDTH
    ct, s1, s2 = tabs
    assert OFF_AQ % tn == 0 and OFF_AK == OFF_AQ + aw and OFF_AV == OFF_AQ + tn and NPAD % tn == 0
    bmap = lambda i, j: (g.batch_of_block(i, tm), 0, 0)
    pmap = lambda i, j: (g.pos_block(i, tm), 0)
    kern = functools.partial(_inproj_kernel, tm=tm, j_qk=OFF_AQ // tn, j_v=OFF_AV // tn)
    return pl.pallas_call(
        kern,
        grid=(t // tm, NPAD // tn),
        in_specs=[pl.BlockSpec((tm, d), lambda i, j: (i, 0)),
                  pl.BlockSpec((1, 1, d), bmap),
                  pl.BlockSpec((1, 1, d), bmap),
                  pl.BlockSpec((NPAD // tn, d, tn), lambda i, j: (0, 0, 0), pipeline_mode=pl.Buffered(1)),
                  pl.BlockSpec((tm, LANES), pmap),
                  pl.BlockSpec((tm, LANES), pmap),
                  pl.BlockSpec((tm, LANES), pmap)],
        out_specs=[pl.BlockSpec((tm, tn), lambda i, j: (i, j)),
                   pl.BlockSpec((3, tm // 4, 4 * aw), lambda i, j: (0, i, 0)),
                   pl.BlockSpec((3, tm // 16, 16 * aw), lambda i, j: (0, i, 0))],
        out_shape=[jax.ShapeDtypeStruct((t, NPAD), BF16),
                   jax.ShapeDtypeStruct((3, t // 4, 4 * aw), BF16),
                   jax.ShapeDtypeStruct((3, t // 16, 16 * aw), BF16)],
        scratch_shapes=[pltpu.VMEM((tm, d), BF16), pltpu.VMEM((aw // LANES, tm, LANES), F32),
                        pltpu.VMEM((aw // LANES, tm, LANES), F32)],
        compiler_params=_cparams(("arbitrary", "arbitrary")),
        name="in_projection",
    )(x, mul, shift, w_pad.reshape(d, NPAD // tn, tn).transpose(1, 0, 2), ct, s1, s2)


def _band_kernel(q_ref, kp_ref, km_ref, kn_ref, vp_ref, vm_ref, vn_ref, o_ref, lse_ref, *, tb, tq, rows1, l1, l2):
    i = pl.program_id(1)
    tk = tq + 2 * A_HALF
    k = jnp.concatenate([kp_ref[...], km_ref[...], kn_ref[...]], axis=0)
    v = jnp.concatenate([vp_ref[...], vm_ref[...], vn_ref[...]], axis=0)
    qi = lax.broadcasted_iota(jnp.int32, (tq, tk), 0)
    kj = lax.broadcasted_iota(jnp.int32, (tq, tk), 1)
    band = jnp.abs(kj - A_HALF - qi) <= A_HALF
    lo = lax.broadcasted_iota(jnp.int32, (tq, LANES), 1) < A_HEAD_DIM
    scale = A_HEAD_DIM ** -0.5
    for u in range(tb // tq):
        row0 = i * tb + u * tq
        in1 = row0 < rows1
        seq_len = jnp.where(in1, l1, l2)
        pos0 = jnp.where(in1, row0 % l1, (row0 - rows1) % l2)
        kpos = pos0 - A_HALF + kj
        valid = band & (kpos >= 0) & (kpos < seq_len)
        valid2 = jnp.concatenate([valid, valid], axis=0)
        rows = slice(u * tq, (u + 1) * tq)
        krows = slice(u * tq, u * tq + tk)
        for p in range(A_WIDTH // LANES):
            sl = slice(p * LANES, (p + 1) * LANES)
            qp, kp, vp = q_ref[rows, sl] * scale, k[krows, sl], v[krows, sl]
            zero = jnp.zeros_like(qp)
            qm = jnp.concatenate([jnp.where(lo, qp, zero), jnp.where(lo, zero, qp)], axis=0)
            s = jnp.where(valid2, _dot_nt(qm, kp), NEG_INF)
            m = jnp.max(s, axis=1, keepdims=True)
            e = jnp.exp(s - m)
            l = jnp.sum(e, axis=1, keepdims=True)
            o2 = _dot(e.astype(BF16), vp) / l
            lse2 = m + jnp.log(l)
            o_ref[rows, sl] = jnp.where(lo, o2[:tq], o2[tq:]).astype(BF16)
            lse_ref[rows, sl] = jnp.where(lo, lse2[:tq], lse2[tq:])


def band_attention(g, qkv, dil, cols, cpb):
    rows = g.t // dil
    tq = 128
    tb = next(b for b in (1024, 512, 256, tq) if rows % b == 0)
    sub = tb // A_HALF
    nsub = rows // A_HALF

    def main(c):
        ld, col = cols[c]
        return pl.BlockSpec((None, tb, A_WIDTH), lambda r, i: (ld, i, r * cpb + col))

    def prev(c):
        ld, col = cols[c]
        return pl.BlockSpec((None, A_HALF, A_WIDTH), lambda r, i: (ld, jnp.maximum(i * sub - 1, 0), r * cpb + col))

    def nxt(c):
        ld, col = cols[c]
        return pl.BlockSpec((None, A_HALF, A_WIDTH),
                            lambda r, i: (ld, jnp.minimum((i + 1) * sub, nsub - 1), r * cpb + col))

    assert (g.s1 // dil) % tq == 0 and (g.s2 // dil) % tq == 0 and rows % tb == 0
    kern = functools.partial(_band_kernel, tb=tb, tq=tq, rows1=g.t1 // dil, l1=g.s1 // dil, l2=g.s2 // dil)
    return pl.pallas_call(
        kern,
        grid=(dil, rows // tb),
        in_specs=[main(0), prev(1), main(1), nxt(1), prev(2), main(2), nxt(2)],
        out_specs=[pl.BlockSpec((tb, A_WIDTH), lambda r, i: (i, r)),
                   pl.BlockSpec((tb, A_WIDTH), lambda r, i: (i, r))],
        out_shape=[jax.ShapeDtypeStruct((rows, dil * A_WIDTH), BF16),
                   jax.ShapeDtypeStruct((rows, dil * A_WIDTH), F32)],
        compiler_params=_cparams(("arbitrary", "arbitrary")),
        name="band_attention_d%d" % dil,
    )(qkv, qkv, qkv, qkv, qkv, qkv, qkv)


def _gla_kernel(*refs, reverse, final, tc, nblk, t1, s1, s2):
    if final:
        q_ref, k_ref, v_ref, z_ref, wa_ref, ba_ref, tri_ref, ob_ref, r_ref, gain_ref, o_ref, st_scr = refs
    else:
        q_ref, k_ref, v_ref, z_ref, wa_ref, ba_ref, tri_ref, o_ref, st_scr = refs
    i = pl.program_id(0)
    blk = (nblk - 1 - i) if reverse else i
    row0 = blk * tc
    in1 = row0 < t1
    pos0 = jnp.where(in1, row0 % s1, (row0 - t1) % s2)
    slen = jnp.where(in1, s1, s2)
    start = (pos0 + tc == slen) if reverse else (pos0 == 0)

    @pl.when(start)
    def _():
        st_scr[...] = jnp.zeros_like(st_scr)

    zl = _dot(z_ref[...], wa_ref[...]) + ba_ref[...]
    la = (jnp.minimum(zl, 0.0) - jnp.log(1.0 + jnp.exp(-jnp.abs(zl)))) * (1.0 / B_GATE_TAU)
    hi, mid, _ = _split3(la)
    tri = tri_ref[...]
    tb = tri.shape[0]
    bc_all = jnp.concatenate(
        [_dot(tri, hi[b * tb:(b + 1) * tb]) + _dot(tri, mid[b * tb:(b + 1) * tb]) for b in range(tc // tb)], axis=0)

    qi = lax.broadcasted_iota(jnp.int32, (B_CHUNK, B_CHUNK), 0)
    si = lax.broadcasted_iota(jnp.int32, (B_CHUNK, B_CHUNK), 1)
    mask = (si > qi) if reverse else (si <= qi)
    nch = tc // B_CHUNK
    q_all = q_ref[...].astype(F32) * (B_KEY_DIM ** -0.5)
    k_all = k_ref[...].astype(F32)
    q_dec = (q_all * jnp.exp(bc_all)).astype(BF16)
    k_inv = (k_all * jnp.exp(-bc_all)).astype(BF16)
    st = [st_scr[h] for h in range(B_HEADS)]
    for c in (range(nch - 1, -1, -1) if reverse else range(nch)):
        sl = slice(c * B_CHUNK, (c + 1) * B_CHUNK)
        edge = c * B_CHUNK if reverse else (c + 1) * B_CHUNK - 1
        tot = bc_all[edge:edge + 1]
        k_end = (k_all[sl] * jnp.exp(tot - bc_all[sl])).astype(BF16)
        dec = jnp.exp(tot)
        for h in range(B_HEADS):
            ks = slice(h * B_KEY_DIM, (h + 1) * B_KEY_DIM)
            vs = slice(h * B_VAL_DIM, (h + 1) * B_VAL_DIM)
            v = v_ref[sl, vs]
            att = jnp.where(mask, _dot_nt(q_dec[sl, ks], k_inv[sl, ks]), 0.0).astype(BF16)
            o = _dot(att, v) + _dot_nt(q_dec[sl, ks], st[h].astype(BF16))
            st[h] = st[h] * dec[:, ks] + _dot_tn(v, k_end[:, ks])
            if final:
                o = o + ob_ref[sl, vs]
                on = o * lax.rsqrt(jnp.mean(o * o, axis=-1, keepdims=True) + EPS) * gain_ref[...]
                r = r_ref[sl, vs].astype(F32)
                o_ref[sl, vs] = (on * (r * _sigmoid(r))).astype(BF16)
            else:
                o_ref[sl, vs] = o
    for h in range(B_HEADS):
        st_scr[h] = st[h]


def gla_pass(g, y, wa, ba, reverse, o_back=None, gain=None):
    t = g.t
    tc = min(1024, g.s1, g.s2)
    nblk = t // tc
    final = o_back is not None
    rowmap = (lambda i: nblk - 1 - i) if reverse else (lambda i: i)
    row = lambda w, off: pl.BlockSpec((tc, w), lambda i: (rowmap(i), off // w))
    full = lambda a: pl.BlockSpec(a.shape, lambda i: (0,) * a.ndim)
    tb = min(tc, 256)
    ri = jnp.arange(tb)[:, None]
    ci = jnp.arange(tb)[None, :]
    tri = ((ri // B_CHUNK == ci // B_CHUNK) & ((ci >= ri) if reverse else (ci <= ri))).astype(BF16)
    in_specs = [row(B_KW, OFF_BQ), row(B_KW, OFF_BK), row(B_VW, OFF_BV), row(LANES, OFF_BZ),
                full(wa), full(ba), full(tri)]
    args = [y, y, y, y, wa, ba, tri]
    if final:
        in_specs += [row(B_VW, 0), row(B_VW, OFF_BR), full(gain)]
        args += [o_back, y, gain]
    kern = functools.partial(_gla_kernel, reverse=reverse, final=final, tc=tc, nblk=nblk, t1=g.t1, s1=g.s1, s2=g.s2)
    return pl.pallas_call(
        kern,
        grid=(nblk,),
        in_specs=in_specs,
        out_specs=row(B_VW, 0),
        out_shape=jax.ShapeDtypeStruct((t, B_VW), BF16 if final else F32),
        scratch_shapes=[pltpu.VMEM((B_HEADS, B_VAL_DIM, B_KEY_DIM), F32)],
        compiler_params=_cparams(("arbitrary",)),
        name="gla_forward_final" if final else "gla_backward",
    )(*args)


def _mla_prep_kernel(ckv_ref, cq_ref, kr_ref, qn_ref, kvn_ref, wq_ref, wk_ref, wv_ref, ct_ref, s1_ref, s2_ref,
                     q_out, k_out, v_out):
    def norm(ref, gain_ref):
        xf = ref[...].astype(F32)
        return (xf * lax.rsqrt(jnp.mean(xf * xf, axis=-1, keepdims=True) + EPS) * gain_ref[...]).astype(BF16)

    ckv_n = norm(ckv_ref, kvn_ref)
    cq_n = norm(cq_ref, qn_ref)
    q = _dot(cq_n, wq_ref[...])
    kn = _dot(ckv_n, wk_ref[...])
    v = _dot(ckv_n, wv_ref[...])
    ct, s1, s2 = ct_ref[...], s1_ref[...], s2_ref[...]
    half = C_ROPE_DIM // 2
    kr_rot = _rope(kr_ref[...].astype(F32), ct, s1, s2, half)
    ones_lane = lax.broadcasted_iota(jnp.int32, kr_rot.shape, 1) == C_V_DIM
    scale = C_QK_DIM ** -0.5 * LOG2E
    for h in range(C_HEADS):
        sl = slice(h * C_HEAD_PAD, (h + 1) * C_HEAD_PAD)
        q_out[:, sl] = (_rope(q[:, sl], ct, s1, s2, half) * scale).astype(BF16)
        k_out[:, sl] = (kn[:, sl] + kr_rot).astype(BF16)
        v_out[:, sl] = jnp.where(ones_lane, 1.0, v[:, sl]).astype(BF16)


def mla_prep(g, y, qn, kvn, wq, wk, wv, tabs, tm):
    t = g.t
    ct, s1, s2 = tabs
    pmap = lambda i: (g.pos_block(i, tm), 0)
    full = lambda a: pl.BlockSpec(a.shape, lambda i: (0,) * a.ndim)
    out = jax.ShapeDtypeStruct((t, C_PAD_WIDTH), BF16)
    ospec = pl.BlockSpec((tm, C_PAD_WIDTH), lambda i: (i, 0))
    return pl.pallas_call(
        _mla_prep_kernel,
        grid=(t // tm,),
        in_specs=[pl.BlockSpec((tm, C_KV_RANK), lambda i: (i, OFF_CKV // C_KV_RANK)),
                  pl.BlockSpec((tm, C_Q_RANK), lambda i: (i, OFF_CQ // C_Q_RANK)),
                  pl.BlockSpec((tm, LANES), lambda i: (i, OFF_KR // LANES)),
                  full(qn), full(kvn), full(wq), full(wk), full(wv),
                  pl.BlockSpec((tm, LANES), pmap), pl.BlockSpec((tm, LANES), pmap), pl.BlockSpec((tm, LANES), pmap)],
        out_specs=[ospec, ospec, ospec],
        out_shape=[out, out, out],
        compiler_params=_cparams(("arbitrary",)),
        name="mla_prep",
    )(y, y, y, qn, kvn, wq, wk, wv, ct, s1, s2)


def _flash_kernel(q_ref, k_ref, v_ref, o_ref, s0_scr, s1_scr, p_scr, m_scr, a_scr, acc_scr, *, tk, nk, rc, unroll):
    tq = q_ref.shape[0]

    def logits(t, scr):
        off = pl.multiple_of(t * tk, tk)
        scr[...] = _dot_nt(q_ref[...], k_ref[pl.ds(off, tk), :])

    def softmax_pv(t, scr):
        off = pl.multiple_of(t * tk, tk)
        for r in range(tq // rc):
            rows = slice(r * rc, (r + 1) * rc)
            s = scr[rows, :]
            m_old = m_scr[rows, :]
            m_new = jnp.maximum(m_old, jnp.max(s, axis=1, keepdims=True))
            p_scr[rows, :] = jnp.exp2(s - jnp.concatenate([m_new] * (tk // LANES), axis=1)).astype(BF16)
            a_scr[rows, :] = jnp.exp2(m_old - m_new)
            m_scr[rows, :] = m_new
        acc_scr[...] = acc_scr[...] * a_scr[...] + _dot(p_scr[...], v_ref[pl.ds(off, tk), :])

    bufs = (s0_scr, s1_scr)

    def group(t0, last):
        for u in range(unroll):
            if not (last and u == unroll - 1):
                logits(t0 + u + 1, bufs[(u + 1) % 2])
            softmax_pv(t0 + u, bufs[u % 2])

    def body(jj, carry):
        group(unroll * jj, False)
        return carry

    m_scr[...] = jnp.full(m_scr.shape, NEG_INF, F32)
    acc_scr[...] = jnp.zeros(acc_scr.shape, F32)
    logits(0, s0_scr)
    lax.fori_loop(0, nk // unroll - 1, body, 0)
    group(nk - unroll, True)
    acc = acc_scr[...]
    o_ref[...] = (acc / acc[:, C_V_DIM:C_V_DIM + 1]).astype(BF16)


def mla_flash(q, k, v, row_off, nseq, s, tq, tk):
    assert row_off % s == 0 and row_off % tq == 0
    qb0 = row_off // tq
    sb0 = row_off // s
    nq = s // tq
    nk = s // tk
    unroll = next(u for u in (16, 8, 4, 2) if nk % u == 0)
    assert nk % unroll == 0
    kern = functools.partial(_flash_kernel, tk=tk, nk=nk, rc=64, unroll=unroll)
    return pl.pallas_call(
        kern,
        grid=(nseq, C_HEADS, nq),
        scratch_shapes=[pltpu.VMEM((tq, tk), F32), pltpu.VMEM((tq, tk), F32), pltpu.VMEM((tq, tk), BF16),
                        pltpu.VMEM((tq, LANES), F32), pltpu.VMEM((tq, LANES), F32),
                        pltpu.VMEM((tq, C_HEAD_PAD), F32)],
        in_specs=[pl.BlockSpec((tq, C_HEAD_PAD), lambda b, h, i: (qb0 + b * nq + i, h)),
                  pl.BlockSpec((s, C_HEAD_PAD), lambda b, h, i: (sb0 + b, h)),
                  pl.BlockSpec((s, C_HEAD_PAD), lambda b, h, i: (sb0 + b, h))],
        out_specs=pl.BlockSpec((tq, C_HEAD_PAD), lambda b, h, i: (b * nq + i, h)),
        out_shape=jax.ShapeDtypeStruct((nseq * s, C_PAD_WIDTH), BF16),
        compiler_params=_cparams(("arbitrary", "arbitrary", "arbitrary")),
        name="mla_flash_s%d" % s,
    )(q, k, v)


def _store_row_tiles(ref, x, rows):
    for c in range(ROW_TILE):
        ref[pl.ds(c, rows, stride=ROW_TILE), :] = x[:, c * LANES:(c + 1) * LANES]


def _load_row_tiles(ref, rows):
    return jnp.concatenate([ref[pl.ds(c, rows, stride=ROW_TILE), :] for c in range(ROW_TILE)], axis=1)


def _router_scores(h, wrh_ref, wrm_ref):
    hh, hm, _ = _split3(h)
    return _sigmoid(_dot(hh, wrh_ref[...]) + _dot(hm, wrh_ref[...]) + _dot(hh, wrm_ref[...]))


def _rank_in_group(sel, pos):
    rank = jnp.zeros(sel.shape, jnp.int32)
    for k in range(1, EXPERTS_PER_GROUP):
        below = pltpu.roll(sel, k, 1)
        above = pltpu.roll(sel, LANES - k, 1)
        rank += jnp.where((pos >= k) & (below >= sel), 1, 0)
        rank += jnp.where((pos + k < EXPERTS_PER_GROUP) & (above > sel), 1, 0)
    return rank


def _route_in_group(s, bias, g):
    lane = lax.broadcasted_iota(jnp.int32, s.shape, 1)
    rank = _rank_in_group(s + bias, lane % EXPERTS_PER_GROUP)
    chosen = (lane // EXPERTS_PER_GROUP == g) & (rank < TOP_K)
    total = jnp.sum(jnp.where(chosen, s, 0.0), axis=1, keepdims=True)
    return jnp.where(chosen, s / total, 0.0)


def _route(s, bias):
    lane = lax.broadcasted_iota(jnp.int32, s.shape, 1)
    pos = lane % EXPERTS_PER_GROUP
    grp = lane // EXPERTS_PER_GROUP
    sel = s + bias
    rank = _rank_in_group(sel, pos)
    top = rank < GROUP_SCORE_K
    contrib = jnp.where(top, sel, 0.0)
    score = contrib
    for k in range(1, EXPERTS_PER_GROUP):
        score += jnp.where(pos >= k, pltpu.roll(contrib, k, 1), 0.0)
        score += jnp.where(pos + k < EXPERTS_PER_GROUP, pltpu.roll(contrib, LANES - k, 1), 0.0)
    best = lane < N_EXPERTS
    for k in range(1, N_GROUPS):
        earlier = pltpu.roll(score, k * EXPERTS_PER_GROUP, 1)
        later = pltpu.roll(score, LANES - k * EXPERTS_PER_GROUP, 1)
        best &= jnp.logical_not((grp >= k) & (earlier >= score))
        best &= jnp.logical_not((grp + k < N_GROUPS) & (later > score))
    chosen = best & (rank < TOP_K)
    total = jnp.sum(jnp.where(chosen, s, 0.0), axis=1, keepdims=True)
    return jnp.where(chosen, s / total, 0.0)


def _proj_kernel(o1_ref, o2_ref, o3_ref, l1_ref, l2_ref, l3_ref, yb_ref, yc1_ref, yc2_ref, ga_ref, gb_ref, gc_ref,
                 x_ref, g1_ref, mul2_ref, sh2_ref, wpa_ref, wpb_ref, wpc_ref, wout_ref, wrh_ref, wrm_ref, br_ref,
                 x_out, h_out, comb_out, o2_scr, l2_scr, o3_scr, l3_scr, *, tm, n1):
    in_group1 = pl.program_id(0) < n1
    ncol = A_WIDTH // LANES
    hm = tm // 2
    for half in range(2):
        rows = slice(half * hm, (half + 1) * hm)
        for dil, o_ref, l_ref, o_scr, l_scr in ((A_PATTERNS[1][1], o2_ref, l2_ref, o2_scr, l2_scr),
                                               (A_PATTERNS[2][1], o3_ref, l3_ref, o3_scr, l3_scr)):
            drows = slice(half * hm // dil, (half + 1) * hm // dil)
            for r in range(dil):
                for c in range(ncol):
                    sl = slice(r * A_WIDTH + c * LANES, r * A_WIDTH + (c + 1) * LANES)
                    o_scr[c, pl.ds(half * hm + r, hm // dil, stride=dil), :] = o_ref[drows, sl].astype(F32)
                    l_scr[c, pl.ds(half * hm + r, hm // dil, stride=dil), :] = l_ref[drows, sl]
        slabs = lambda scr: jnp.concatenate([scr[c, rows, :] for c in range(ncol)], axis=1)
        l1, l2, l3 = l1_ref[rows, :], slabs(l2_scr), slabs(l3_scr)
        m = jnp.maximum(jnp.maximum(l1, l2), l3)
        e1, e2, e3 = jnp.exp(l1 - m), jnp.exp(l2 - m), jnp.exp(l3 - m)
        ya = (e1 * o1_ref[rows, :].astype(F32) + e2 * slabs(o2_scr) + e3 * slabs(o3_scr)) / (e1 + e2 + e3)
        sig = lambda ref: _sigmoid(ref[rows, :].astype(F32))
        yc = jnp.where(in_group1, yc1_ref[rows, :], yc2_ref[rows, :])
        merged = (sig(ga_ref) * _dot(ya.astype(BF16), wpa_ref[...])
                  + sig(gb_ref) * _dot(yb_ref[rows, :], wpb_ref[...])
                  + sig(gc_ref) * _dot(yc, wpc_ref[...]))
        out = _dot(merged.astype(BF16), wout_ref[...])
        x = x_ref[rows, :] + g1_ref[0] * out
        x_out[rows, :] = x
        h = x * lax.rsqrt(jnp.mean(x * x, axis=-1, keepdims=True) + EPS) * mul2_ref[0] + sh2_ref[0]
        _store_row_tiles(h_out.at[pl.ds(half * hm * ROW_TILE, hm * ROW_TILE), :], h, hm)
        comb_out[rows, :] = _route(_router_scores(h, wrh_ref, wrm_ref), br_ref[...])


def proj_merge(g, o_list, lse_list, yb, yc1, yc2, y, x, gate1, mul2, sh2, wpa, wpb, wpc, wout, wrh, wrm, br, tm):
    t, d = x.shape
    n1 = g.t1 // tm
    n2 = g.t2 // tm
    bmap = lambda i: (g.batch_of_block(i, tm), 0, 0)
    row = lambda w, col=0: pl.BlockSpec((tm, w), lambda i: (i, col))
    dil_row = lambda dil: pl.BlockSpec((tm // dil, dil * A_WIDTH), lambda i: (i, 0))
    full = lambda a: pl.BlockSpec(a.shape, lambda i: (0,) * a.ndim)
    gcol = OFF_GATES // d
    d2, d3 = A_PATTERNS[1][1], A_PATTERNS[2][1]
    attn_specs = [row(A_WIDTH), dil_row(d2), dil_row(d3)]
    return pl.pallas_call(
        functools.partial(_proj_kernel, tm=tm, n1=n1),
        grid=(t // tm,),
        in_specs=attn_specs + attn_specs + [
            row(B_VW),
            pl.BlockSpec((tm, C_PAD_WIDTH), lambda i: (jnp.minimum(i, n1 - 1), 0)),
            pl.BlockSpec((tm, C_PAD_WIDTH), lambda i: (jnp.clip(i - n1, 0, n2 - 1), 0)),
            row(d, gcol), row(d, gcol + 1), row(d, gcol + 2),
            row(d), pl.BlockSpec((1, 1, d), bmap), pl.BlockSpec((1, 1, d), bmap), pl.BlockSpec((1, 1, d), bmap),
            full(wpa), full(wpb), full(wpc), full(wout), full(wrh), full(wrm), full(br)],
        out_specs=[row(d), pl.BlockSpec((tm * ROW_TILE, LANES), lambda i: (i, 0)), row(LANES)],
        out_shape=[jax.ShapeDtypeStruct((t, d), F32), jax.ShapeDtypeStruct((t * ROW_TILE, LANES), F32),
                   jax.ShapeDtypeStruct((t, LANES), F32)],
        scratch_shapes=[pltpu.VMEM((A_WIDTH // LANES, tm, LANES), F32)] * 4,
        compiler_params=_cparams(("arbitrary",)),
        name="proj_merge_route",
    )(*o_list, *lse_list, yb, yc1, yc2, y, y, y, x, gate1, mul2, sh2, wpa, wpb, wpc, wout, wrh, wrm, br)


def _sorted_layout(comb, tmoe):
    t = comb.shape[0]
    gw = comb[:, :N_EXPERTS].reshape(t, N_GROUPS, EXPERTS_PER_GROUP).sum(-1)
    gid = jnp.argmax(gw, axis=-1).astype(jnp.int32)
    onehot = (gid[:, None] == jnp.arange(N_GROUPS, dtype=jnp.int32)[None, :]).astype(jnp.int32)
    incl = jnp.cumsum(onehot, axis=0)
    counts = incl[-1]
    padded = (counts + tmoe - 1) // tmoe * tmoe
    ends = jnp.cumsum(padded)
    dest = jnp.sum(onehot * (incl - onehot + (ends - padded)[None, :]), axis=1).astype(jnp.int32)
    ntiles = t // tmoe + N_GROUPS
    tile_group = jnp.sum((jnp.arange(ntiles, dtype=jnp.int32) * tmoe)[:, None] >= ends[None, :], axis=1)
    tile_group = jnp.where(tile_group < N_GROUPS, tile_group, -1).astype(jnp.int32)
    return dest, tile_group, ntiles


def _row_copy(src, src_row, dst, dst_row, sem):
    return pltpu.make_async_copy(src.at[pl.ds(src_row * ROW_TILE, ROW_TILE), :],
                                 dst.at[pl.ds(dst_row * ROW_TILE, ROW_TILE), :], sem)


def _scatter_rows_kernel(dest_ref, h_ref, init_hbm, hs_hbm, sem, *, tm):
    del init_hbm
    i = pl.program_id(0)

    def issue(r, c):
        _row_copy(h_ref, r, hs_hbm, dest_ref[i * tm + r], sem).start()
        return c

    def drain(r, c):
        _row_copy(h_ref, 0, hs_hbm, 0, sem).wait()
        return c

    lax.fori_loop(0, tm, issue, 0, unroll=8)
    lax.fori_loop(0, tm, drain, 0, unroll=8)


def scatter_rows(dest, h_tiles, rows_out, tm):
    t = dest.shape[0]
    init = jnp.zeros((rows_out * ROW_TILE, LANES), F32)
    return pl.pallas_call(
        functools.partial(_scatter_rows_kernel, tm=tm),
        grid_spec=pltpu.PrefetchScalarGridSpec(
            num_scalar_prefetch=1,
            grid=(t // tm,),
            in_specs=[pl.BlockSpec((tm * ROW_TILE, LANES), lambda i, dst: (i, 0)),
                      pl.BlockSpec(memory_space=pl.ANY)],
            out_specs=pl.BlockSpec(memory_space=pl.ANY),
            scratch_shapes=[pltpu.SemaphoreType.DMA(())]),
        out_shape=jax.ShapeDtypeStruct(init.shape, F32),
        input_output_aliases={2: 0},
        compiler_params=_cparams(("arbitrary",)),
        name="moe_scatter_rows",
    )(dest, h_tiles, init)


def _moe_group_kernel(tg_ref, hs_ref, wrh_ref, wrm_ref, br_ref, wg_ref, wu_ref, wd_ref, y_ref,
                      h_scr, comb_scr, acc_scr, *, tm):
    j = pl.program_id(0)
    k = pl.program_id(1)
    g = tg_ref[j]

    @pl.when(k == 0)
    def _():
        h = _load_row_tiles(hs_ref, tm)
        h_scr[...] = h.astype(BF16)
        comb_scr[...] = _route_in_group(_router_scores(h, wrh_ref, wrm_ref), br_ref[...], g)
        acc_scr[...] = jnp.zeros_like(acc_scr)

    @pl.when(g >= 0)
    def _():
        h = h_scr[...]
        comb = comb_scr[...]
        lane = lax.broadcasted_iota(jnp.int32, comb.shape, 1)
        total = acc_scr[...]
        for kk in range(MOE_EXPERTS_PER_STEP):
            gate = _dot(h, wg_ref[kk])
            a = gate * _sigmoid(gate) * _dot(h, wu_ref[kk])
            yv = _dot(a.astype(BF16), wd_ref[kk])
            e = g * EXPERTS_PER_GROUP + k * MOE_EXPERTS_PER_STEP + kk
            total = total + jnp.sum(jnp.where(lane == e, comb, 0.0), axis=1, keepdims=True) * yv
        acc_scr[...] = total

    @pl.when(k == EXPERTS_PER_GROUP // MOE_EXPERTS_PER_STEP - 1)
    def _():
        _store_row_tiles(y_ref, acc_scr[...], tm)


def moe_grouped(tile_group, ntiles, hs, wrh, wrm, br, wg, wu, wd, tm):
    d = D_MODEL
    full = lambda a: pl.BlockSpec(a.shape, lambda j, k, tg: (0,) * a.ndim)
    ksteps = EXPERTS_PER_GROUP // MOE_EXPERTS_PER_STEP
    emap = lambda j, k, tg: (jnp.maximum(tg[j], 0) * ksteps + k, 0, 0)
    return pl.pallas_call(
        functools.partial(_moe_group_kernel, tm=tm),
        grid_spec=pltpu.PrefetchScalarGridSpec(
            num_scalar_prefetch=1,
            grid=(ntiles, ksteps),
            in_specs=[pl.BlockSpec((tm * ROW_TILE, LANES), lambda j, k, tg: (j, 0)),
                      full(wrh), full(wrm), full(br),
                      pl.BlockSpec((MOE_EXPERTS_PER_STEP, d, D_EXPERT), emap),
                      pl.BlockSpec((MOE_EXPERTS_PER_STEP, d, D_EXPERT), emap),
                      pl.BlockSpec((MOE_EXPERTS_PER_STEP, D_EXPERT, d), emap)],
            out_specs=pl.BlockSpec((tm * ROW_TILE, LANES), lambda j, k, tg: (j, 0)),
            scratch_shapes=[pltpu.VMEM((tm, d), BF16), pltpu.VMEM((tm, LANES), F32), pltpu.VMEM((tm, d), F32)]),
        out_shape=jax.ShapeDtypeStruct(hs.shape, F32),
        compiler_params=_cparams(("arbitrary", "arbitrary")),
        name="moe_grouped",
    )(tile_group, hs, wrh, wrm, br, wg, wu, wd)


def _gather_residual_kernel(dest_ref, y_hbm, x_ref, g2_ref, fn_ref, o_ref, buf, sem, *, tm, nsteps, final):
    i = pl.program_id(0)

    def fetch(step, slot):
        def issue(r, c):
            _row_copy(y_hbm, dest_ref[step * tm + r], buf.at[slot], r, sem.at[slot]).start()
            return c
        lax.fori_loop(0, tm, issue, 0, unroll=8)

    @pl.when(i == 0)
    def _():
        fetch(0, 0)

    @pl.when(i + 1 < nsteps)
    def _():
        fetch(i + 1, (i + 1) % 2)

    slot = i % 2

    def drain(r, c):
        _row_copy(y_hbm, 0, buf.at[slot], 0, sem.at[slot]).wait()
        return c

    lax.fori_loop(0, tm, drain, 0, unroll=8)
    x = x_ref[...] + g2_ref[0] * _load_row_tiles(buf.at[slot], tm)
    if final:
        x = x * lax.rsqrt(jnp.mean(x * x, axis=-1, keepdims=True) + EPS) * fn_ref[...]
    o_ref[...] = x


def gather_residual(g, dest, y_tiles, x, gate2, fnorm, final, tm):
    t, d = x.shape
    nsteps = t // tm
    kern = functools.partial(_gather_residual_kernel, tm=tm, nsteps=nsteps, final=final)
    return pl.pallas_call(
        kern,
        grid_spec=pltpu.PrefetchScalarGridSpec(
            num_scalar_prefetch=1,
            grid=(nsteps,),
            in_specs=[pl.BlockSpec(memory_space=pl.ANY),
                      pl.BlockSpec((tm, d), lambda i, dst: (i, 0)),
                      pl.BlockSpec((1, 1, d), lambda i, dst: (g.batch_of_block(i, tm), 0, 0)),
                      pl.BlockSpec((1, d), lambda i, dst: (0, 0))],
            out_specs=pl.BlockSpec((tm, d), lambda i, dst: (i, 0)),
            scratch_shapes=[pltpu.VMEM((2, tm * ROW_TILE, LANES), F32), pltpu.SemaphoreType.DMA((2,))]),
        out_shape=jax.ShapeDtypeStruct((t, d), F32),
        compiler_params=_cparams(("arbitrary",)),
        name="moe_gather_residual",
    )(dest, y_tiles, x, gate2, fnorm)


def _rope_tables(smax, rot_dim, period, lane_off):
    half = rot_dim // 2
    inv_freq = ROPE_THETA ** (-jnp.arange(half, dtype=F32) / half)
    ang = jnp.arange(smax).astype(F32)[:, None] * inv_freq[None, :]
    cos, sin = jnp.cos(ang), jnp.sin(ang)
    gl = (jnp.arange(LANES) % period) - lane_off
    first = (gl >= 0) & (gl < half)
    second = (gl >= half) & (gl < rot_dim)
    j = jnp.clip(jnp.where(first, gl, gl - half), 0, half - 1)
    cl, sn = cos[:, j], sin[:, j]
    ct = jnp.where((first | second)[None, :], cl, 1.0)
    s1 = jnp.where(first[None, :], -sn, 0.0)
    s2 = jnp.where(second[None, :], sn, 0.0)
    return ct, s1, s2


def _pad_w_in(w):
    d = w.shape[0]
    parts, off = [], 0
    for sz in IN_SIZES:
        parts.append(w[:, off:off + sz])
        off += sz
    a_q, a_k, a_v, b_q, b_k, b_v, b_r, b_zf, b_zb, c_cq, c_ckv, c_kr, gates = parts
    z = lambda n: jnp.zeros((d, n), w.dtype)
    bz = jnp.concatenate([b_zf, b_zb, z(LANES - 2 * B_GATE_RANK)], axis=1)
    kr = jnp.concatenate([z(KR_LANE), c_kr, z(LANES - KR_LANE - C_ROPE_DIM)], axis=1)
    used = OFF_KR + LANES
    out = jnp.concatenate([gates, b_v, b_r, a_q, a_k, a_v, b_q, b_k, c_ckv, bz, c_cq, kr, z(NPAD - used)], axis=1)
    assert out.shape[1] == NPAD
    return out.astype(BF16)


def _pad_heads_cols(w, real, take_lo, take_hi):
    kdim = w.shape[0]
    wh = w.reshape(kdim, C_HEADS, real)[:, :, take_lo:take_hi]
    wh = jnp.pad(wh, ((0, 0), (0, 0), (0, C_HEAD_PAD - (take_hi - take_lo))))
    return wh.reshape(kdim, C_PAD_WIDTH).astype(BF16)


def _gate_weights(w_a, b_a, row_off):
    wa = jnp.zeros((LANES, B_KW), F32).at[row_off:row_off + B_GATE_RANK].set(w_a)
    return wa.astype(BF16), b_a.reshape(1, B_KW).astype(F32)


def kernel(x_prompt, x_sample, c_prompt, c_sample, norm_mix, norm_moe, w_mod, b_mod, w_in, w_gla_af, b_gla_af,
           w_gla_ab, b_gla_ab, gla_norm, mla_q_norm, w_mla_uq, mla_kv_norm, w_mla_ukv, w_proj_a, w_proj_b, w_proj_c,
           w_out, w_router, b_router, w_exp_gate, w_exp_up, w_exp_down, final_norm):
    b1, s1, d = x_prompt.shape
    b2, s2, _ = x_sample.shape
    g = Groups(b1, s1, b2, s2)
    depth = w_in.shape[0]
    tm_in = min(1024, s1, s2)
    tm_prep = min(1024, s1, s2)
    tm_proj = min(512, s1, s2)
    tm_moe = min(1024, s1, s2)

    x = jnp.concatenate([x_prompt.reshape(g.t1, d), x_sample.reshape(g.t2, d)], axis=0)
    c = jnp.concatenate([c_prompt, c_sample], axis=0)
    nbp = -(-g.nb // 8) * 8
    c_pad = jnp.pad(c, ((0, nbp - g.nb), (0, 0)))
    mod = modulation(c_pad, w_mod, b_mod)[:, :g.nb]

    smax = max(s1, s2)
    tabs_a = _rope_tables(smax, A_ROPE_DIM, A_HEAD_DIM, 0)
    tabs_c = _rope_tables(smax, C_ROPE_DIM, C_HEAD_PAD, KR_LANE)

    wr_hi = jnp.pad(w_router, ((0, 0), (0, LANES - N_EXPERTS)))
    wrh = wr_hi.astype(BF16)
    wrm = (wr_hi - wrh.astype(F32)).astype(BF16)
    fnorm = final_norm.reshape(1, d)
    br = jnp.pad(b_router.astype(F32), (0, LANES - N_EXPERTS)).reshape(1, LANES)

    for l in range(depth):
        sh1, sc1, gt1, sh2, sc2, gt2 = [m.reshape(g.nb, 1, d) for m in jnp.split(mod[l], N_MOD, axis=-1)]
        mul1 = norm_mix[l][None, None, :] * (1.0 + sc1)
        mul2 = norm_moe[l][None, None, :] * (1.0 + sc2)

        y, yd4, yd16 = in_projection(g, x, mul1, sh1, _pad_w_in(w_in[l]), tabs_a, tm_in)

        nat_cols = tuple((0, off // A_WIDTH) for off in (OFF_AQ, OFF_AK, OFF_AV))
        dil_cols = ((0, 0), (1, 0), (2, 0))
        o_list, lse_list = [], []
        for (_, dil), src in zip(A_PATTERNS, (y.reshape(1, g.t, NPAD), yd4, yd16)):
            o, lse = band_attention(g, src, dil, nat_cols if dil == 1 else dil_cols, NPAD // A_WIDTH if dil == 1 else 1)
            o_list.append(o)
            lse_list.append(lse)

        waf, baf = _gate_weights(w_gla_af[l], b_gla_af[l], 0)
        wab, bab = _gate_weights(w_gla_ab[l], b_gla_ab[l], B_GATE_RANK)
        o_back = gla_pass(g, y, wab, bab, reverse=True)
        yb = gla_pass(g, y, waf, baf, reverse=False, o_back=o_back, gain=gla_norm[l].reshape(1, B_VAL_DIM))

        wq = jnp.concatenate(
            [w_mla_uq[l].reshape(C_Q_RANK, C_HEADS, C_QK_DIM),
             jnp.zeros((C_Q_RANK, C_HEADS, C_HEAD_PAD - C_QK_DIM), F32)], axis=2
        ).reshape(C_Q_RANK, C_PAD_WIDTH).astype(BF16)
        wk = _pad_heads_cols(w_mla_ukv[l], C_NOPE_DIM + C_V_DIM, 0, C_NOPE_DIM)
        wv = _pad_heads_cols(w_mla_ukv[l], C_NOPE_DIM + C_V_DIM, C_NOPE_DIM, C_NOPE_DIM + C_V_DIM)
        qm, km, vm = mla_prep(g, y, mla_q_norm[l].reshape(1, C_Q_RANK), mla_kv_norm[l].reshape(1, C_KV_RANK),
                              wq, wk, wv, tabs_c, tm_prep)
        yc1 = mla_flash(qm, km, vm, 0, b1, s1, min(2048, s1), min(512, s1))
        yc2 = mla_flash(qm, km, vm, g.t1, b2, s2, min(512, s2), min(512, s2))

        wpc = jnp.pad(w_proj_c[l].reshape(C_HEADS, C_V_DIM, d), ((0, 0), (0, C_HEAD_PAD - C_V_DIM), (0, 0)))
        wpc = wpc.reshape(C_PAD_WIDTH, d).astype(BF16)
        x, h2, comb = proj_merge(g, o_list, lse_list, yb, yc1, yc2, y, x, gt1, mul2, sh2,
                                 w_proj_a[l].astype(BF16), w_proj_b[l].astype(BF16), wpc, w_out[l].astype(BF16),
                                 wrh, wrm, br, tm_proj)

        dest, tile_group, ntiles = _sorted_layout(comb, tm_moe)
        hs = scatter_rows(dest, h2, ntiles * tm_moe, tm_moe)
        ys = moe_grouped(tile_group, ntiles, hs, wrh, wrm, br, w_exp_gate[l].astype(BF16),
                         w_exp_up[l].astype(BF16), w_exp_down[l].astype(BF16), tm_moe)
        x = gather_residual(g, dest, ys, x, gt2, fnorm, l == depth - 1, tm_proj)

    return x[:g.t1].reshape(b1, s1, d), x[g.t1:].reshape(b2, s2, d)
```

```python
import functools

import jax
import jax.numpy as jnp
from jax import lax
from jax.experimental import pallas as pl
from jax.experimental.pallas import tpu as pltpu

F32 = jnp.float32
BF16 = jnp.bfloat16

D_MODEL = 1024
DEPTH = 2
EPS = 1e-6
ROPE_THETA = 500000.0
NEG_INF = -1e30

A_HEADS = 8
A_HEAD_DIM = 64
A_ROPE_DIM = A_HEAD_DIM // 4
A_PATTERNS = ((128, 1), (512, 4), (2048, 16))
A_WIDTH = A_HEADS * A_HEAD_DIM
A_HALF = 64

B_HEADS = 4
B_KEY_DIM = 128
B_VAL_DIM = 256
B_GATE_RANK = 16
B_GATE_TAU = 16.0
B_CHUNK = 64
B_KW = B_HEADS * B_KEY_DIM
B_VW = B_HEADS * B_VAL_DIM

C_HEADS = 8
C_NOPE_DIM = 64
C_ROPE_DIM = 32
C_V_DIM = 64
C_QK_DIM = C_NOPE_DIM + C_ROPE_DIM
C_Q_RANK = 384
C_KV_RANK = 256
C_HEAD_PAD = 128
C_PAD_WIDTH = C_HEADS * C_HEAD_PAD

N_EXPERTS = 16
N_GROUPS = 4
EXPERTS_PER_GROUP = N_EXPERTS // N_GROUPS
TOP_K = 2
GROUP_SCORE_K = 2
D_EXPERT = 512
N_MOD = 6

IN_SIZES = (A_WIDTH, A_WIDTH, A_WIDTH, B_KW, B_KW, B_VW, B_VW, B_GATE_RANK, B_GATE_RANK,
            C_Q_RANK, C_KV_RANK, C_ROPE_DIM, 3 * D_MODEL)

LANES = 128
ROW_TILE = D_MODEL // LANES
MOE_EXPERTS_PER_STEP = 2
LOG2E = 1.4426950408889634

OFF_GATES = 0
OFF_BV = 3072
OFF_BR = 4096
OFF_AQ = 5120
OFF_AK = 5632
OFF_AV = 6144
OFF_BQ = 6656
OFF_BK = 7168
OFF_CKV = 7680
OFF_BZ = 7936
OFF_CQ = 8064
OFF_KR = 8448
NPAD = 9216
KR_LANE = 64

VMEM_LIMIT = 56 * 1024 * 1024


def _cparams(sem):
    return pltpu.CompilerParams(dimension_semantics=sem, vmem_limit_bytes=VMEM_LIMIT)


class Groups:
    def __init__(self, b1, s1, b2, s2):
        self.b1, self.s1, self.b2, self.s2 = b1, s1, b2, s2
        self.t1, self.t2 = b1 * s1, b2 * s2
        self.t = self.t1 + self.t2
        self.nb = b1 + b2

    def batch_of_block(self, i, tm):
        n1 = self.t1 // tm
        return jnp.where(i < n1, i // (self.s1 // tm), self.b1 + (i - n1) // (self.s2 // tm))

    def pos_block(self, i, tm):
        n1 = self.t1 // tm
        return jnp.where(i < n1, i % (self.s1 // tm), (i - n1) % (self.s2 // tm))


def _dot(a, b):
    return jnp.dot(a, b, preferred_element_type=F32)


def _dot_nt(a, b):
    return lax.dot_general(a, b, (((1,), (1,)), ((), ())), preferred_element_type=F32)


def _dot_tn(a, b):
    return lax.dot_general(a, b, (((0,), (0,)), ((), ())), preferred_element_type=F32)


def _split3(a):
    hi = a.astype(BF16)
    r1 = a - hi.astype(F32)
    mid = r1.astype(BF16)
    lo = (r1 - mid.astype(F32)).astype(BF16)
    return hi, mid, lo


def _rope(y, ct, s1, s2, half):
    return y * ct + pltpu.roll(y, LANES - half, 1) * s1 + pltpu.roll(y, half, 1) * s2


def _sigmoid(x):
    return 0.5 * jnp.tanh(0.5 * x) + 0.5


def _mod_kernel(c_ref, w_ref, b_ref, o_ref):
    c = c_ref[...]
    ca = c * _sigmoid(c)
    ch, cm, _ = _split3(ca)
    wh, wm, _ = _split3(w_ref[0])
    o_ref[0] = _dot(ch, wh) + _dot(cm, wh) + _dot(ch, wm) + b_ref[0]


def modulation(c_pad, w_mod, b_mod):
    nbp, d = c_pad.shape
    depth, _, n = w_mod.shape
    tn = 512
    return pl.pallas_call(
        _mod_kernel,
        grid=(depth, n // tn),
        in_specs=[pl.BlockSpec((nbp, d), lambda l, j: (0, 0)),
                  pl.BlockSpec((1, d, tn), lambda l, j: (l, 0, j)),
                  pl.BlockSpec((1, 1, tn), lambda l, j: (l, 0, j))],
        out_specs=pl.BlockSpec((1, nbp, tn), lambda l, j: (l, 0, j)),
        out_shape=jax.ShapeDtypeStruct((depth, nbp, n), F32),
        compiler_params=_cparams(("arbitrary", "arbitrary")),
        name="modulation",
    )(c_pad, w_mod, b_mod.reshape(depth, 1, n))


def _inproj_kernel(x_ref, mul_ref, sh_ref, w_ref, ct_ref, s1_ref, s2_ref, o_ref, od4_ref, od16_ref,
                   h_scr, y_scr, y4_scr, *, tm, j_qk, j_v):
    j = pl.program_id(1)
    aw = A_WIDTH
    ncol = aw // LANES

    @pl.when(j == 0)
    def _():
        x = x_ref[...]
        ms = jnp.mean(x * x, axis=-1, keepdims=True)
        h = x * lax.rsqrt(ms + EPS) * mul_ref[0] + sh_ref[0]
        h_scr[...] = h.astype(BF16)

    def attention_block(kind, half, rope):
        y = _dot(h_scr[...], w_ref[j, :, half * aw:(half + 1) * aw])
        if rope:
            ct, s1, s2 = ct_ref[...], s1_ref[...], s2_ref[...]
        for c in range(ncol):
            yc = y[:, c * LANES:(c + 1) * LANES]
            y_scr[c] = _rope(yc, ct, s1, s2, A_ROPE_DIM // 2) if rope else yc
        n4, n16 = tm // 4, tm // 16
        for c in range(ncol):
            o_ref[:, half * aw + c * LANES:half * aw + (c + 1) * LANES] = y_scr[c].astype(BF16)
            for r in range(4):
                y4 = y_scr[c, pl.ds(r, n4, stride=4), :]
                y4_scr[c, r * n4:(r + 1) * n4, :] = y4
                od4_ref[kind, :, r * aw + c * LANES:r * aw + (c + 1) * LANES] = y4.astype(BF16)
            for r in range(4):
                for r2 in range(4):
                    col = (r + 4 * r2) * aw + c * LANES
                    od16_ref[kind, :, col:col + LANES] = y4_scr[c, pl.ds(r * n4 + r2, n16, stride=4), :].astype(BF16)

    @pl.when((j != j_qk) & (j != j_v))
    def _():
        o_ref[...] = _dot(h_scr[...], w_ref[j]).astype(BF16)

    @pl.when(j == j_qk)
    def _():
        attention_block(0, 0, True)
        attention_block(1, 1, True)

    @pl.when(j == j_v)
    def _():
        attention_block(2, 0, False)
        o_ref[:, aw:] = _dot(h_scr[...], w_ref[j, :, aw:]).astype(BF16)


def in_projection(g, x, mul, shift, w_pad, tabs, tm):
    t, d = x.shape
    tn = 2 * A_WIDTH
    aw = A_WIDTH
    ct, s1, s2 = tabs
    assert OFF_AQ % tn == 0 and OFF_AK == OFF_AQ + aw and OFF_AV == OFF_AQ + tn and NPAD % tn == 0
    bmap = lambda i, j: (g.batch_of_block(i, tm), 0, 0)
    pmap = lambda i, j: (g.pos_block(i, tm), 0)
    kern = functools.partial(_inproj_kernel, tm=tm, j_qk=OFF_AQ // tn, j_v=OFF_AV // tn)
    return pl.pallas_call(
        kern,
        grid=(t // tm, NPAD // tn),
        in_specs=[pl.BlockSpec((tm, d), lambda i, j: (i, 0)),
                  pl.BlockSpec((1, 1, d), bmap),
                  pl.BlockSpec((1, 1, d), bmap),
                  pl.BlockSpec((NPAD // tn, d, tn), lambda i, j: (0, 0, 0), pipeline_mode=pl.Buffered(1)),
                  pl.BlockSpec((tm, LANES), pmap),
                  pl.BlockSpec((tm, LANES), pmap),
                  pl.BlockSpec((tm, LANES), pmap)],
        out_specs=[pl.BlockSpec((tm, tn), lambda i, j: (i, j)),
                   pl.BlockSpec((3, tm // 4, 4 * aw), lambda i, j: (0, i, 0)),
                   pl.BlockSpec((3, tm // 16, 16 * aw), lambda i, j: (0, i, 0))],
        out_shape=[jax.ShapeDtypeStruct((t, NPAD), BF16),
                   jax.ShapeDtypeStruct((3, t // 4, 4 * aw), BF16),
                   jax.ShapeDtypeStruct((3, t // 16, 16 * aw), BF16)],
        scratch_shapes=[pltpu.VMEM((tm, d), BF16), pltpu.VMEM((aw // LANES, tm, LANES), F32),
                        pltpu.VMEM((aw // LANES, tm, LANES), F32)],
        compiler_params=_cparams(("arbitrary", "arbitrary")),
        name="in_projection",
    )(x, mul, shift, w_pad.reshape(d, NPAD // tn, tn).transpose(1, 0, 2), ct, s1, s2)


def _band_kernel(q_ref, kp_ref, km_ref, kn_ref, vp_ref, vm_ref, vn_ref, o_ref, lse_ref, *, tb, tq, rows1, l1, l2):
    i = pl.program_id(1)
    tk = tq + 2 * A_HALF
    k = jnp.concatenate([kp_ref[...], km_ref[...], kn_ref[...]], axis=0)
    v = jnp.concatenate([vp_ref[...], vm_ref[...], vn_ref[...]], axis=0)
    qi = lax.broadcasted_iota(jnp.int32, (tq, tk), 0)
    kj = lax.broadcasted_iota(jnp.int32, (tq, tk), 1)
    band = jnp.abs(kj - A_HALF - qi) <= A_HALF
    lo = lax.broadcasted_iota(jnp.int32, (tq, LANES), 1) < A_HEAD_DIM
    scale = A_HEAD_DIM ** -0.5
    for u in range(tb // tq):
        row0 = i * tb + u * tq
        in1 = row0 < rows1
        seq_len = jnp.where(in1, l1, l2)
        pos0 = jnp.where(in1, row0 % l1, (row0 - rows1) % l2)
        kpos = pos0 - A_HALF + kj
        valid = band & (kpos >= 0) & (kpos < seq_len)
        valid2 = jnp.concatenate([valid, valid], axis=0)
        rows = slice(u * tq, (u + 1) * tq)
        krows = slice(u * tq, u * tq + tk)
        for p in range(A_WIDTH // LANES):
            sl = slice(p * LANES, (p + 1) * LANES)
            qp, kp, vp = q_ref[rows, sl] * scale, k[krows, sl], v[krows, sl]
            zero = jnp.zeros_like(qp)
            qm = jnp.concatenate([jnp.where(lo, qp, zero), jnp.where(lo, zero, qp)], axis=0)
            s = jnp.where(valid2, _dot_nt(qm, kp), NEG_INF)
            m = jnp.max(s, axis=1, keepdims=True)
            e = jnp.exp(s - m)
            l = jnp.sum(e, axis=1, keepdims=True)
            o2 = _dot(e.astype(BF16), vp) / l
            lse2 = m + jnp.log(l)
            o_ref[rows, sl] = jnp.where(lo, o2[:tq], o2[tq:]).astype(BF16)
            lse_ref[rows, sl] = jnp.where(lo, lse2[:tq], lse2[tq:])


def band_attention(g, qkv, dil, cols, cpb):
    rows = g.t // dil
    tq = 128
    tb = next(b for b in (1024, 512, 256, tq) if rows % b == 0)
    sub = tb // A_HALF
    nsub = rows // A_HALF

    def main(c):
        ld, col = cols[c]
        return pl.BlockSpec((None, tb, A_WIDTH), lambda r, i: (ld, i, r * cpb + col))

    def prev(c):
        ld, col = cols[c]
        return pl.BlockSpec((None, A_HALF, A_WIDTH), lambda r, i: (ld, jnp.maximum(i * sub - 1, 0), r * cpb + col))

    def nxt(c):
        ld, col = cols[c]
        return pl.BlockSpec((None, A_HALF, A_WIDTH),
                            lambda r, i: (ld, jnp.minimum((i + 1) * sub, nsub - 1), r * cpb + col))

    assert (g.s1 // dil) % tq == 0 and (g.s2 // dil) % tq == 0 and rows % tb == 0
    kern = functools.partial(_band_kernel, tb=tb, tq=tq, rows1=g.t1 // dil, l1=g.s1 // dil, l2=g.s2 // dil)
    return pl.pallas_call(
        kern,
        grid=(dil, rows // tb),
        in_specs=[main(0), prev(1), main(1), nxt(1), prev(2), main(2), nxt(2)],
        out_specs=[pl.BlockSpec((tb, A_WIDTH), lambda r, i: (i, r)),
                   pl.BlockSpec((tb, A_WIDTH), lambda r, i: (i, r))],
        out_shape=[jax.ShapeDtypeStruct((rows, dil * A_WIDTH), BF16),
                   jax.ShapeDtypeStruct((rows, dil * A_WIDTH), F32)],
        compiler_params=_cparams(("arbitrary", "arbitrary")),
        name="band_attention_d%d" % dil,
    )(qkv, qkv, qkv, qkv, qkv, qkv, qkv)


def _gla_kernel(*refs, reverse, final, tc, nblk, t1, s1, s2):
    if final:
        q_ref, k_ref, v_ref, z_ref, wa_ref, ba_ref, tri_ref, ob_ref, r_ref, gain_ref, o_ref, st_scr = refs
    else:
        q_ref, k_ref, v_ref, z_ref, wa_ref, ba_ref, tri_ref, o_ref, st_scr = refs
    i = pl.program_id(0)
    blk = (nblk - 1 - i) if reverse else i
    row0 = blk * tc
    in1 = row0 < t1
    pos0 = jnp.where(in1, row0 % s1, (row0 - t1) % s2)
    slen = jnp.where(in1, s1, s2)
    start = (pos0 + tc == slen) if reverse else (pos0 == 0)

    @pl.when(start)
    def _():
        st_scr[...] = jnp.zeros_like(st_scr)

    zl = _dot(z_ref[...], wa_ref[...]) + ba_ref[...]
    la = (jnp.minimum(zl, 0.0) - jnp.log(1.0 + jnp.exp(-jnp.abs(zl)))) * (1.0 / B_GATE_TAU)
    hi, mid, _ = _split3(la)
    tri = tri_ref[...]
    tb = tri.shape[0]
    bc_all = jnp.concatenate(
        [_dot(tri, hi[b * tb:(b + 1) * tb]) + _dot(tri, mid[b * tb:(b + 1) * tb]) for b in range(tc // tb)], axis=0)

    qi = lax.broadcasted_iota(jnp.int32, (B_CHUNK, B_CHUNK), 0)
    si = lax.broadcasted_iota(jnp.int32, (B_CHUNK, B_CHUNK), 1)
    mask = (si > qi) if reverse else (si <= qi)
    nch = tc // B_CHUNK
    q_all = q_ref[...].astype(F32) * (B_KEY_DIM ** -0.5)
    k_all = k_ref[...].astype(F32)
    q_dec = (q_all * jnp.exp(bc_all)).astype(BF16)
    k_inv = (k_all * jnp.exp(-bc_all)).astype(BF16)
    st = [st_scr[h] for h in range(B_HEADS)]
    for c in (range(nch - 1, -1, -1) if reverse else range(nch)):
        sl = slice(c * B_CHUNK, (c + 1) * B_CHUNK)
        edge = c * B_CHUNK if reverse else (c + 1) * B_CHUNK - 1
        tot = bc_all[edge:edge + 1]
        k_end = (k_all[sl] * jnp.exp(tot - bc_all[sl])).astype(BF16)
        dec = jnp.exp(tot)
        for h in range(B_HEADS):
            ks = slice(h * B_KEY_DIM, (h + 1) * B_KEY_DIM)
            vs = slice(h * B_VAL_DIM, (h + 1) * B_VAL_DIM)
            v = v_ref[sl, vs]
            att = jnp.where(mask, _dot_nt(q_dec[sl, ks], k_inv[sl, ks]), 0.0).astype(BF16)
            o = _dot(att, v) + _dot_nt(q_dec[sl, ks], st[h].astype(BF16))
            st[h] = st[h] * dec[:, ks] + _dot_tn(v, k_end[:, ks])
            if final:
                o = o + ob_ref[sl, vs]
                on = o * lax.rsqrt(jnp.mean(o * o, axis=-1, keepdims=True) + EPS) * gain_ref[...]
                r = r_ref[sl, vs].astype(F32)
                o_ref[sl, vs] = (on * (r * _sigmoid(r))).astype(BF16)
            else:
                o_ref[sl, vs] = o
    for h in range(B_HEADS):
        st_scr[h] = st[h]


def gla_pass(g, y, wa, ba, reverse, o_back=None, gain=None):
    t = g.t
    tc = min(1024, g.s1, g.s2)
    nblk = t // tc
    final = o_back is not None
    rowmap = (lambda i: nblk - 1 - i) if reverse else (lambda i: i)
    row = lambda w, off: pl.BlockSpec((tc, w), lambda i: (rowmap(i), off // w))
    full = lambda a: pl.BlockSpec(a.shape, lambda i: (0,) * a.ndim)
    tb = min(tc, 256)
    ri = jnp.arange(tb)[:, None]
    ci = jnp.arange(tb)[None, :]
    tri = ((ri // B_CHUNK == ci // B_CHUNK) & ((ci >= ri) if reverse else (ci <= ri))).astype(BF16)
    in_specs = [row(B_KW, OFF_BQ), row(B_KW, OFF_BK), row(B_VW, OFF_BV), row(LANES, OFF_BZ),
                full(wa), full(ba), full(tri)]
    args = [y, y, y, y, wa, ba, tri]
    if final:
        in_specs += [row(B_VW, 0), row(B_VW, OFF_BR), full(gain)]
        args += [o_back, y, gain]
    kern = functools.partial(_gla_kernel, reverse=reverse, final=final, tc=tc, nblk=nblk, t1=g.t1, s1=g.s1, s2=g.s2)
    return pl.pallas_call(
        kern,
        grid=(nblk,),
        in_specs=in_specs,
        out_specs=row(B_VW, 0),
        out_shape=jax.ShapeDtypeStruct((t, B_VW), BF16 if final else F32),
        scratch_shapes=[pltpu.VMEM((B_HEADS, B_VAL_DIM, B_KEY_DIM), F32)],
        compiler_params=_cparams(("arbitrary",)),
        name="gla_forward_final" if final else "gla_backward",
    )(*args)


def _mla_prep_kernel(ckv_ref, cq_ref, kr_ref, qn_ref, kvn_ref, wq_ref, wk_ref, wv_ref, ct_ref, s1_ref, s2_ref,
                     q_out, k_out, v_out):
    def norm(ref, gain_ref):
        xf = ref[...].astype(F32)
        return (xf * lax.rsqrt(jnp.mean(xf * xf, axis=-1, keepdims=True) + EPS) * gain_ref[...]).astype(BF16)

    ckv_n = norm(ckv_ref, kvn_ref)
    cq_n = norm(cq_ref, qn_ref)
    q = _dot(cq_n, wq_ref[...])
    kn = _dot(ckv_n, wk_ref[...])
    v = _dot(ckv_n, wv_ref[...])
    ct, s1, s2 = ct_ref[...], s1_ref[...], s2_ref[...]
    half = C_ROPE_DIM // 2
    kr_rot = _rope(kr_ref[...].astype(F32), ct, s1, s2, half)
    ones_lane = lax.broadcasted_iota(jnp.int32, kr_rot.shape, 1) == C_V_DIM
    scale = C_QK_DIM ** -0.5 * LOG2E
    for h in range(C_HEADS):
        sl = slice(h * C_HEAD_PAD, (h + 1) * C_HEAD_PAD)
        q_out[:, sl] = (_rope(q[:, sl], ct, s1, s2, half) * scale).astype(BF16)
        k_out[:, sl] = (kn[:, sl] + kr_rot).astype(BF16)
        v_out[:, sl] = jnp.where(ones_lane, 1.0, v[:, sl]).astype(BF16)


def mla_prep(g, y, qn, kvn, wq, wk, wv, tabs, tm):
    t = g.t
    ct, s1, s2 = tabs
    pmap = lambda i: (g.pos_block(i, tm), 0)
    full = lambda a: pl.BlockSpec(a.shape, lambda i: (0,) * a.ndim)
    out = jax.ShapeDtypeStruct((t, C_PAD_WIDTH), BF16)
    ospec = pl.BlockSpec((tm, C_PAD_WIDTH), lambda i: (i, 0))
    return pl.pallas_call(
        _mla_prep_kernel,
        grid=(t // tm,),
        in_specs=[pl.BlockSpec((tm, C_KV_RANK), lambda i: (i, OFF_CKV // C_KV_RANK)),
                  pl.BlockSpec((tm, C_Q_RANK), lambda i: (i, OFF_CQ // C_Q_RANK)),
                  pl.BlockSpec((tm, LANES), lambda i: (i, OFF_KR // LANES)),
                  full(qn), full(kvn), full(wq), full(wk), full(wv),
                  pl.BlockSpec((tm, LANES), pmap), pl.BlockSpec((tm, LANES), pmap), pl.BlockSpec((tm, LANES), pmap)],
        out_specs=[ospec, ospec, ospec],
        out_shape=[out, out, out],
        compiler_params=_cparams(("arbitrary",)),
        name="mla_prep",
    )(y, y, y, qn, kvn, wq, wk, wv, ct, s1, s2)


def _flash_kernel(q_ref, k_ref, v_ref, o_ref, s0_scr, s1_scr, p_scr, m_scr, a_scr, acc_scr, *, tk, nk, rc, unroll):
    tq = q_ref.shape[0]

    def logits(t, scr):
        off = pl.multiple_of(t * tk, tk)
        scr[...] = _dot_nt(q_ref[...], k_ref[pl.ds(off, tk), :])

    def softmax_pv(t, scr):
        off = pl.multiple_of(t * tk, tk)
        for r in range(tq // rc):
            rows = slice(r * rc, (r + 1) * rc)
            s = scr[rows, :]
            m_old = m_scr[rows, :]
            m_new = jnp.maximum(m_old, jnp.max(s, axis=1, keepdims=True))
            p_scr[rows, :] = jnp.exp2(s - jnp.concatenate([m_new] * (tk // LANES), axis=1)).astype(BF16)
            a_scr[rows, :] = jnp.exp2(m_old - m_new)
            m_scr[rows, :] = m_new
        acc_scr[...] = acc_scr[...] * a_scr[...] + _dot(p_scr[...], v_ref[pl.ds(off, tk), :])

    bufs = (s0_scr, s1_scr)

    def group(t0, last):
        for u in range(unroll):
            if not (last and u == unroll - 1):
                logits(t0 + u + 1, bufs[(u + 1) % 2])
            softmax_pv(t0 + u, bufs[u % 2])

    def body(jj, carry):
        group(unroll * jj, False)
        return carry

    m_scr[...] = jnp.full(m_scr.shape, NEG_INF, F32)
    acc_scr[...] = jnp.zeros(acc_scr.shape, F32)
    logits(0, s0_scr)
    lax.fori_loop(0, nk // unroll - 1, body, 0)
    group(nk - unroll, True)
    acc = acc_scr[...]
    o_ref[...] = (acc / acc[:, C_V_DIM:C_V_DIM + 1]).astype(BF16)


def mla_flash(q, k, v, row_off, nseq, s, tq, tk):
    assert row_off % s == 0 and row_off % tq == 0
    qb0 = row_off // tq
    sb0 = row_off // s
    nq = s // tq
    nk = s // tk
    unroll = next(u for u in (16, 8, 4, 2) if nk % u == 0)
    assert nk % unroll == 0
    kern = functools.partial(_flash_kernel, tk=tk, nk=nk, rc=64, unroll=unroll)
    return pl.pallas_call(
        kern,
        grid=(nseq, C_HEADS, nq),
        scratch_shapes=[pltpu.VMEM((tq, tk), F32), pltpu.VMEM((tq, tk), F32), pltpu.VMEM((tq, tk), BF16),
                        pltpu.VMEM((tq, LANES), F32), pltpu.VMEM((tq, LANES), F32),
                        pltpu.VMEM((tq, C_HEAD_PAD), F32)],
        in_specs=[pl.BlockSpec((tq, C_HEAD_PAD), lambda b, h, i: (qb0 + b * nq + i, h)),
                  pl.BlockSpec((s, C_HEAD_PAD), lambda b, h, i: (sb0 + b, h)),
                  pl.BlockSpec((s, C_HEAD_PAD), lambda b, h, i: (sb0 + b, h))],
        out_specs=pl.BlockSpec((tq, C_HEAD_PAD), lambda b, h, i: (b * nq + i, h)),
        out_shape=jax.ShapeDtypeStruct((nseq * s, C_PAD_WIDTH), BF16),
        compiler_params=_cparams(("arbitrary", "arbitrary", "arbitrary")),
        name="mla_flash_s%d" % s,
    )(q, k, v)


def _store_row_tiles(ref, x, rows):
    for c in range(ROW_TILE):
        ref[pl.ds(c, rows, stride=ROW_TILE), :] = x[:, c * LANES:(c + 1) * LANES]


def _load_row_tiles(ref, rows):
    return jnp.concatenate([ref[pl.ds(c, rows, stride=ROW_TILE), :] for c in range(ROW_TILE)], axis=1)


def _router_scores(h, wrh_ref, wrm_ref):
    hh, hm, _ = _split3(h)
    return _sigmoid(_dot(hh, wrh_ref[...]) + _dot(hm, wrh_ref[...]) + _dot(hh, wrm_ref[...]))


def _rank_in_group(sel, pos):
    rank = jnp.zeros(sel.shape, jnp.int32)
    for k in range(1, EXPERTS_PER_GROUP):
        below = pltpu.roll(sel, k, 1)
        above = pltpu.roll(sel, LANES - k, 1)
        rank += jnp.where((pos >= k) & (below >= sel), 1, 0)
        rank += jnp.where((pos + k < EXPERTS_PER_GROUP) & (above > sel), 1, 0)
    return rank


def _route_in_group(s, bias, g):
    lane = lax.broadcasted_iota(jnp.int32, s.shape, 1)
    rank = _rank_in_group(s + bias, lane % EXPERTS_PER_GROUP)
    chosen = (lane // EXPERTS_PER_GROUP == g) & (rank < TOP_K)
    total = jnp.sum(jnp.where(chosen, s, 0.0), axis=1, keepdims=True)
    return jnp.where(chosen, s / total, 0.0)


def _route(s, bias):
    lane = lax.broadcasted_iota(jnp.int32, s.shape, 1)
    pos = lane % EXPERTS_PER_GROUP
    grp = lane // EXPERTS_PER_GROUP
    sel = s + bias
    rank = _rank_in_group(sel, pos)
    top = rank < GROUP_SCORE_K
    contrib = jnp.where(top, sel, 0.0)
    score = contrib
    for k in range(1, EXPERTS_PER_GROUP):
        score += jnp.where(pos >= k, pltpu.roll(contrib, k, 1), 0.0)
        score += jnp.where(pos + k < EXPERTS_PER_GROUP, pltpu.roll(contrib, LANES - k, 1), 0.0)
    best = lane < N_EXPERTS
    for k in range(1, N_GROUPS):
        earlier = pltpu.roll(score, k * EXPERTS_PER_GROUP, 1)
        later = pltpu.roll(score, LANES - k * EXPERTS_PER_GROUP, 1)
        best &= jnp.logical_not((grp >= k) & (earlier >= score))
        best &= jnp.logical_not((grp + k < N_GROUPS) & (later > score))
    chosen = best & (rank < TOP_K)
    total = jnp.sum(jnp.where(chosen, s, 0.0), axis=1, keepdims=True)
    return jnp.where(chosen, s / total, 0.0)


def _proj_kernel(o1_ref, o2_ref, o3_ref, l1_ref, l2_ref, l3_ref, yb_ref, yc1_ref, yc2_ref, ga_ref, gb_ref, gc_ref,
                 x_ref, g1_ref, mul2_ref, sh2_ref, wpa_ref, wpb_ref, wpc_ref, wout_ref, wrh_ref, wrm_ref, br_ref,
                 x_out, h_out, comb_out, o2_scr, l2_scr, o3_scr, l3_scr, *, tm, n1):
    in_group1 = pl.program_id(0) < n1
    ncol = A_WIDTH // LANES
    hm = tm // 2
    for half in range(2):
        rows = slice(half * hm, (half + 1) * hm)
        for dil, o_ref, l_ref, o_scr, l_scr in ((A_PATTERNS[1][1], o2_ref, l2_ref, o2_scr, l2_scr),
                                               (A_PATTERNS[2][1], o3_ref, l3_ref, o3_scr, l3_scr)):
            drows = slice(half * hm // dil, (half + 1) * hm // dil)
            for r in range(dil):
                for c in range(ncol):
                    sl = slice(r * A_WIDTH + c * LANES, r * A_WIDTH + (c + 1) * LANES)
                    o_scr[c, pl.ds(half * hm + r, hm // dil, stride=dil), :] = o_ref[drows, sl].astype(F32)
                    l_scr[c, pl.ds(half * hm + r, hm // dil, stride=dil), :] = l_ref[drows, sl]
        slabs = lambda scr: jnp.concatenate([scr[c, rows, :] for c in range(ncol)], axis=1)
        l1, l2, l3 = l1_ref[rows, :], slabs(l2_scr), slabs(l3_scr)
        m = jnp.maximum(jnp.maximum(l1, l2), l3)
        e1, e2, e3 = jnp.exp(l1 - m), jnp.exp(l2 - m), jnp.exp(l3 - m)
        ya = (e1 * o1_ref[rows, :].astype(F32) + e2 * slabs(o2_scr) + e3 * slabs(o3_scr)) / (e1 + e2 + e3)
        sig = lambda ref: _sigmoid(ref[rows, :].astype(F32))
        yc = jnp.where(in_group1, yc1_ref[rows, :], yc2_ref[rows, :])
        merged = (sig(ga_ref) * _dot(ya.astype(BF16), wpa_ref[...])
                  + sig(gb_ref) * _dot(yb_ref[rows, :], wpb_ref[...])
                  + sig(gc_ref) * _dot(yc, wpc_ref[...]))
        out = _dot(merged.astype(BF16), wout_ref[...])
        x = x_ref[rows, :] + g1_ref[0] * out
        x_out[rows, :] = x
        h = x * lax.rsqrt(jnp.mean(x * x, axis=-1, keepdims=True) + EPS) * mul2_ref[0] + sh2_ref[0]
        _store_row_tiles(h_out.at[pl.ds(half * hm * ROW_TILE, hm * ROW_TILE), :], h, hm)
        comb_out[rows, :] = _route(_router_scores(h, wrh_ref, wrm_ref), br_ref[...])


def proj_merge(g, o_list, lse_list, yb, yc1, yc2, y, x, gate1, mul2, sh2, wpa, wpb, wpc, wout, wrh, wrm, br, tm):
    t, d = x.shape
    n1 = g.t1 // tm
    n2 = g.t2 // tm
    bmap = lambda i: (g.batch_of_block(i, tm), 0, 0)
    row = lambda w, col=0: pl.BlockSpec((tm, w), lambda i: (i, col))
    dil_row = lambda dil: pl.BlockSpec((tm // dil, dil * A_WIDTH), lambda i: (i, 0))
    full = lambda a: pl.BlockSpec(a.shape, lambda i: (0,) * a.ndim)
    gcol = OFF_GATES // d
    d2, d3 = A_PATTERNS[1][1], A_PATTERNS[2][1]
    attn_specs = [row(A_WIDTH), dil_row(d2), dil_row(d3)]
    return pl.pallas_call(
        functools.partial(_proj_kernel, tm=tm, n1=n1),
        grid=(t // tm,),
        in_specs=attn_specs + attn_specs + [
            row(B_VW),
            pl.BlockSpec((tm, C_PAD_WIDTH), lambda i: (jnp.minimum(i, n1 - 1), 0)),
            pl.BlockSpec((tm, C_PAD_WIDTH), lambda i: (jnp.clip(i - n1, 0, n2 - 1), 0)),
            row(d, gcol), row(d, gcol + 1), row(d, gcol + 2),
            row(d), pl.BlockSpec((1, 1, d), bmap), pl.BlockSpec((1, 1, d), bmap), pl.BlockSpec((1, 1, d), bmap),
            full(wpa), full(wpb), full(wpc), full(wout), full(wrh), full(wrm), full(br)],
        out_specs=[row(d), pl.BlockSpec((tm * ROW_TILE, LANES), lambda i: (i, 0)), row(LANES)],
        out_shape=[jax.ShapeDtypeStruct((t, d), F32), jax.ShapeDtypeStruct((t * ROW_TILE, LANES), F32),
                   jax.ShapeDtypeStruct((t, LANES), F32)],
        scratch_shapes=[pltpu.VMEM((A_WIDTH // LANES, tm, LANES), F32)] * 4,
        compiler_params=_cparams(("arbitrary",)),
        name="proj_merge_route",
    )(*o_list, *lse_list, yb, yc1, yc2, y, y, y, x, gate1, mul2, sh2, wpa, wpb, wpc, wout, wrh, wrm, br)


def _sorted_layout(comb, tmoe):
    t = comb.shape[0]
    gw = comb[:, :N_EXPERTS].reshape(t, N_GROUPS, EXPERTS_PER_GROUP).sum(-1)
    gid = jnp.argmax(gw, axis=-1).astype(jnp.int32)
    onehot = (gid[:, None] == jnp.arange(N_GROUPS, dtype=jnp.int32)[None, :]).astype(jnp.int32)
    incl = jnp.cumsum(onehot, axis=0)
    counts = incl[-1]
    padded = (counts + tmoe - 1) // tmoe * tmoe
    ends = jnp.cumsum(padded)
    dest = jnp.sum(onehot * (incl - onehot + (ends - padded)[None, :]), axis=1).astype(jnp.int32)
    ntiles = t // tmoe + N_GROUPS
    tile_group = jnp.sum((jnp.arange(ntiles, dtype=jnp.int32) * tmoe)[:, None] >= ends[None, :], axis=1)
    tile_group = jnp.where(tile_group < N_GROUPS, tile_group, -1).astype(jnp.int32)
    return dest, tile_group, ntiles


def _row_copy(src, src_row, dst, dst_row, sem):
    return pltpu.make_async_copy(src.at[pl.ds(src_row * ROW_TILE, ROW_TILE), :],
                                 dst.at[pl.ds(dst_row * ROW_TILE, ROW_TILE), :], sem)


def _scatter_rows_kernel(dest_ref, h_ref, init_hbm, hs_hbm, sem, *, tm):
    del init_hbm
    i = pl.program_id(0)

    def issue(r2, c):
        for p in range(2):
            r = 2 * r2 + p
            _row_copy(h_ref, r, hs_hbm, dest_ref[i * tm + r], sem).start(priority=p)
        return c

    def drain(r, c):
        _row_copy(h_ref, 0, hs_hbm, 0, sem).wait()
        return c

    lax.fori_loop(0, tm // 2, issue, 0, unroll=4)
    lax.fori_loop(0, tm, drain, 0, unroll=8)


def scatter_rows(dest, h_tiles, rows_out, tm):
    t = dest.shape[0]
    init = jnp.zeros((rows_out * ROW_TILE, LANES), F32)
    return pl.pallas_call(
        functools.partial(_scatter_rows_kernel, tm=tm),
        grid_spec=pltpu.PrefetchScalarGridSpec(
            num_scalar_prefetch=1,
            grid=(t // tm,),
            in_specs=[pl.BlockSpec((tm * ROW_TILE, LANES), lambda i, dst: (i, 0)),
                      pl.BlockSpec(memory_space=pl.ANY)],
            out_specs=pl.BlockSpec(memory_space=pl.ANY),
            scratch_shapes=[pltpu.SemaphoreType.DMA(())]),
        out_shape=jax.ShapeDtypeStruct(init.shape, F32),
        input_output_aliases={2: 0},
        compiler_params=_cparams(("arbitrary",)),
        name="moe_scatter_rows",
    )(dest, h_tiles, init)


def _moe_group_kernel(tg_ref, hs_ref, wrh_ref, wrm_ref, br_ref, wg_ref, wu_ref, wd_ref, y_ref,
                      h_scr, comb_scr, acc_scr, *, tm):
    j = pl.program_id(0)
    k = pl.program_id(1)
    g = tg_ref[j]

    @pl.when(k == 0)
    def _():
        h = _load_row_tiles(hs_ref, tm)
        h_scr[...] = h.astype(BF16)
        comb_scr[...] = _route_in_group(_router_scores(h, wrh_ref, wrm_ref), br_ref[...], g)
        acc_scr[...] = jnp.zeros_like(acc_scr)

    @pl.when(g >= 0)
    def _():
        h = h_scr[...]
        comb = comb_scr[...]
        lane = lax.broadcasted_iota(jnp.int32, comb.shape, 1)
        total = acc_scr[...]
        for kk in range(MOE_EXPERTS_PER_STEP):
            gate = _dot(h, wg_ref[kk])
            a = gate * _sigmoid(gate) * _dot(h, wu_ref[kk])
            yv = _dot(a.astype(BF16), wd_ref[kk])
            e = g * EXPERTS_PER_GROUP + k * MOE_EXPERTS_PER_STEP + kk
            total = total + jnp.sum(jnp.where(lane == e, comb, 0.0), axis=1, keepdims=True) * yv
        acc_scr[...] = total

    @pl.when(k == EXPERTS_PER_GROUP // MOE_EXPERTS_PER_STEP - 1)
    def _():
        _store_row_tiles(y_ref, acc_scr[...], tm)


def moe_grouped(tile_group, ntiles, hs, wrh, wrm, br, wg, wu, wd, tm):
    d = D_MODEL
    full = lambda a: pl.BlockSpec(a.shape, lambda j, k, tg: (0,) * a.ndim)
    ksteps = EXPERTS_PER_GROUP // MOE_EXPERTS_PER_STEP
    emap = lambda j, k, tg: (jnp.maximum(tg[j], 0) * ksteps + k, 0, 0)
    return pl.pallas_call(
        functools.partial(_moe_group_kernel, tm=tm),
        grid_spec=pltpu.PrefetchScalarGridSpec(
            num_scalar_prefetch=1,
            grid=(ntiles, ksteps),
            in_specs=[pl.BlockSpec((tm * ROW_TILE, LANES), lambda j, k, tg: (j, 0)),
                      full(wrh), full(wrm), full(br),
                      pl.BlockSpec((MOE_EXPERTS_PER_STEP, d, D_EXPERT), emap),
                      pl.BlockSpec((MOE_EXPERTS_PER_STEP, d, D_EXPERT), emap),
                      pl.BlockSpec((MOE_EXPERTS_PER_STEP, D_EXPERT, d), emap)],
            out_specs=pl.BlockSpec((tm * ROW_TILE, LANES), lambda j, k, tg: (j, 0)),
            scratch_shapes=[pltpu.VMEM((tm, d), BF16), pltpu.VMEM((tm, LANES), F32), pltpu.VMEM((tm, d), F32)]),
        out_shape=jax.ShapeDtypeStruct(hs.shape, F32),
        compiler_params=_cparams(("arbitrary", "arbitrary")),
        name="moe_grouped",
    )(tile_group, hs, wrh, wrm, br, wg, wu, wd)


def _gather_residual_kernel(dest_ref, y_hbm, x_ref, g2_ref, fn_ref, o_ref, buf, sem, *, tm, nsteps, final):
    i = pl.program_id(0)

    def fetch(step, slot):
        def issue(r2, c):
            for p in range(2):
                r = 2 * r2 + p
                _row_copy(y_hbm, dest_ref[step * tm + r], buf.at[slot], r, sem.at[slot]).start(priority=p)
            return c
        lax.fori_loop(0, tm // 2, issue, 0, unroll=4)

    @pl.when(i == 0)
    def _():
        fetch(0, 0)

    @pl.when(i + 1 < nsteps)
    def _():
        fetch(i + 1, (i + 1) % 2)

    slot = i % 2

    def drain(r, c):
        _row_copy(y_hbm, 0, buf.at[slot], 0, sem.at[slot]).wait()
        return c

    lax.fori_loop(0, tm, drain, 0, unroll=8)
    x = x_ref[...] + g2_ref[0] * _load_row_tiles(buf.at[slot], tm)
    if final:
        x = x * lax.rsqrt(jnp.mean(x * x, axis=-1, keepdims=True) + EPS) * fn_ref[...]
    o_ref[...] = x


def gather_residual(g, dest, y_tiles, x, gate2, fnorm, final, tm):
    t, d = x.shape
    nsteps = t // tm
    kern = functools.partial(_gather_residual_kernel, tm=tm, nsteps=nsteps, final=final)
    return pl.pallas_call(
        kern,
        grid_spec=pltpu.PrefetchScalarGridSpec(
            num_scalar_prefetch=1,
            grid=(nsteps,),
            in_specs=[pl.BlockSpec(memory_space=pl.ANY),
                      pl.BlockSpec((tm, d), lambda i, dst: (i, 0)),
                      pl.BlockSpec((1, 1, d), lambda i, dst: (g.batch_of_block(i, tm), 0, 0)),
                      pl.BlockSpec((1, d), lambda i, dst: (0, 0))],
            out_specs=pl.BlockSpec((tm, d), lambda i, dst: (i, 0)),
            scratch_shapes=[pltpu.VMEM((2, tm * ROW_TILE, LANES), F32), pltpu.SemaphoreType.DMA((2,))]),
        out_shape=jax.ShapeDtypeStruct((t, d), F32),
        compiler_params=_cparams(("arbitrary",)),
        name="moe_gather_residual",
    )(dest, y_tiles, x, gate2, fnorm)


def _rope_tables(smax, rot_dim, period, lane_off):
    half = rot_dim // 2
    inv_freq = ROPE_THETA ** (-jnp.arange(half, dtype=F32) / half)
    ang = jnp.arange(smax).astype(F32)[:, None] * inv_freq[None, :]
    cos, sin = jnp.cos(ang), jnp.sin(ang)
    gl = (jnp.arange(LANES) % period) - lane_off
    first = (gl >= 0) & (gl < half)
    second = (gl >= half) & (gl < rot_dim)
    j = jnp.clip(jnp.where(first, gl, gl - half), 0, half - 1)
    cl, sn = cos[:, j], sin[:, j]
    ct = jnp.where((first | second)[None, :], cl, 1.0)
    s1 = jnp.where(first[None, :], -sn, 0.0)
    s2 = jnp.where(second[None, :], sn, 0.0)
    return ct, s1, s2


def _pad_w_in(w):
    d = w.shape[0]
    parts, off = [], 0
    for sz in IN_SIZES:
        parts.append(w[:, off:off + sz])
        off += sz
    a_q, a_k, a_v, b_q, b_k, b_v, b_r, b_zf, b_zb, c_cq, c_ckv, c_kr, gates = parts
    z = lambda n: jnp.zeros((d, n), w.dtype)
    bz = jnp.concatenate([b_zf, b_zb, z(LANES - 2 * B_GATE_RANK)], axis=1)
    kr = jnp.concatenate([z(KR_LANE), c_kr, z(LANES - KR_LANE - C_ROPE_DIM)], axis=1)
    used = OFF_KR + LANES
    out = jnp.concatenate([gates, b_v, b_r, a_q, a_k, a_v, b_q, b_k, c_ckv, bz, c_cq, kr, z(NPAD - used)], axis=1)
    assert out.shape[1] == NPAD
    return out.astype(BF16)


def _pad_heads_cols(w, real, take_lo, take_hi):
    kdim = w.shape[0]
    wh = w.reshape(kdim, C_HEADS, real)[:, :, take_lo:take_hi]
    wh = jnp.pad(wh, ((0, 0), (0, 0), (0, C_HEAD_PAD - (take_hi - take_lo))))
    return wh.reshape(kdim, C_PAD_WIDTH).astype(BF16)


def _gate_weights(w_a, b_a, row_off):
    wa = jnp.zeros((LANES, B_KW), F32).at[row_off:row_off + B_GATE_RANK].set(w_a)
    return wa.astype(BF16), b_a.reshape(1, B_KW).astype(F32)


def kernel(x_prompt, x_sample, c_prompt, c_sample, norm_mix, norm_moe, w_mod, b_mod, w_in, w_gla_af, b_gla_af,
           w_gla_ab, b_gla_ab, gla_norm, mla_q_norm, w_mla_uq, mla_kv_norm, w_mla_ukv, w_proj_a, w_proj_b, w_proj_c,
           w_out, w_router, b_router, w_exp_gate, w_exp_up, w_exp_down, final_norm):
    b1, s1, d = x_prompt.shape
    b2, s2, _ = x_sample.shape
    g = Groups(b1, s1, b2, s2)
    depth = w_in.shape[0]
    tm_in = min(1024, s1, s2)
    tm_prep = min(1024, s1, s2)
    tm_proj = min(512, s1, s2)
    tm_moe = min(1024, s1, s2)

    x = jnp.concatenate([x_prompt.reshape(g.t1, d), x_sample.reshape(g.t2, d)], axis=0)
    c = jnp.concatenate([c_prompt, c_sample], axis=0)
    nbp = -(-g.nb // 8) * 8
    c_pad = jnp.pad(c, ((0, nbp - g.nb), (0, 0)))
    mod = modulation(c_pad, w_mod, b_mod)[:, :g.nb]

    smax = max(s1, s2)
    tabs_a = _rope_tables(smax, A_ROPE_DIM, A_HEAD_DIM, 0)
    tabs_c = _rope_tables(smax, C_ROPE_DIM, C_HEAD_PAD, KR_LANE)

    wr_hi = jnp.pad(w_router, ((0, 0), (0, LANES - N_EXPERTS)))
    wrh = wr_hi.astype(BF16)
    wrm = (wr_hi - wrh.astype(F32)).astype(BF16)
    fnorm = final_norm.reshape(1, d)
    br = jnp.pad(b_router.astype(F32), (0, LANES - N_EXPERTS)).reshape(1, LANES)

    for l in range(depth):
        sh1, sc1, gt1, sh2, sc2, gt2 = [m.reshape(g.nb, 1, d) for m in jnp.split(mod[l], N_MOD, axis=-1)]
        mul1 = norm_mix[l][None, None, :] * (1.0 + sc1)
        mul2 = norm_moe[l][None, None, :] * (1.0 + sc2)

        y, yd4, yd16 = in_projection(g, x, mul1, sh1, _pad_w_in(w_in[l]), tabs_a, tm_in)

        nat_cols = tuple((0, off // A_WIDTH) for off in (OFF_AQ, OFF_AK, OFF_AV))
        dil_cols = ((0, 0), (1, 0), (2, 0))
        o_list, lse_list = [], []
        for (_, dil), src in zip(A_PATTERNS, (y.reshape(1, g.t, NPAD), yd4, yd16)):
            o, lse = band_attention(g, src, dil, nat_cols if dil == 1 else dil_cols, NPAD // A_WIDTH if dil == 1 else 1)
            o_list.append(o)
            lse_list.append(lse)

        waf, baf = _gate_weights(w_gla_af[l], b_gla_af[l], 0)
        wab, bab = _gate_weights(w_gla_ab[l], b_gla_ab[l], B_GATE_RANK)
        o_back = gla_pass(g, y, wab, bab, reverse=True)
        yb = gla_pass(g, y, waf, baf, reverse=False, o_back=o_back, gain=gla_norm[l].reshape(1, B_VAL_DIM))

        wq = jnp.concatenate(
            [w_mla_uq[l].reshape(C_Q_RANK, C_HEADS, C_QK_DIM),
             jnp.zeros((C_Q_RANK, C_HEADS, C_HEAD_PAD - C_QK_DIM), F32)], axis=2
        ).reshape(C_Q_RANK, C_PAD_WIDTH).astype(BF16)
        wk = _pad_heads_cols(w_mla_ukv[l], C_NOPE_DIM + C_V_DIM, 0, C_NOPE_DIM)
        wv = _pad_heads_cols(w_mla_ukv[l], C_NOPE_DIM + C_V_DIM, C_NOPE_DIM, C_NOPE_DIM + C_V_DIM)
        qm, km, vm = mla_prep(g, y, mla_q_norm[l].reshape(1, C_Q_RANK), mla_kv_norm[l].reshape(1, C_KV_RANK),
                              wq, wk, wv, tabs_c, tm_prep)
        yc1 = mla_flash(qm, km, vm, 0, b1, s1, min(2048, s1), min(512, s1))
        yc2 = mla_flash(qm, km, vm, g.t1, b2, s2, min(512, s2), min(512, s2))

        wpc = jnp.pad(w_proj_c[l].reshape(C_HEADS, C_V_DIM, d), ((0, 0), (0, C_HEAD_PAD - C_V_DIM), (0, 0)))
        wpc = wpc.reshape(C_PAD_WIDTH, d).astype(BF16)
        x, h2, comb = proj_merge(g, o_list, lse_list, yb, yc1, yc2, y, x, gt1, mul2, sh2,
                                 w_proj_a[l].astype(BF16), w_proj_b[l].astype(BF16), wpc, w_out[l].astype(BF16),
                                 wrh, wrm, br, tm_proj)

        dest, tile_group, ntiles = _sorted_layout(comb, tm_moe)
        hs = scatter_rows(dest, h2, ntiles * tm_moe, tm_moe)
        ys = moe_grouped(tile_group, ntiles, hs, wrh, wrm, br, w_exp_gate[l].astype(BF16),
                         w_exp_up[l].astype(BF16), w_exp_down[l].astype(BF16), tm_moe)
        x = gather_residual(g, dest, ys, x, gt2, fnorm, l == depth - 1, tm_proj)

    return x[:g.t1].reshape(b1, s1, d), x[g.t1:].reshape(b2, s2, d)
```
